```python
import math
import jax, jax.numpy as jnp
from jax import lax
import numpy as np

D_MODEL = 1024
BATCH = 8
SEQ = 2048
DEPTH = 2
DEC_BATCH = 128
DEC_SEQ = 4
PAST_LEN = 16384
PAGE_SIZE = 128

HEAD_DIM = 64
D_FF = 2816
PLE_DIM = 256
NORM_EPS = 1e-6
CHUNK = 64

RWKV_W = D_MODEL // 4
RWKV_HEADS = RWKV_W // HEAD_DIM
RWKV_DECAY_LORA = 32
RWKV_A_LORA = 32
RWKV_GATE_LORA = 64
RWKV_GN_EPS = 64e-5
RWKV_PROJ = 3 * RWKV_W + RWKV_DECAY_LORA + RWKV_A_LORA + RWKV_GATE_LORA

GLA_W = D_MODEL // 4
GLA_HEADS = GLA_W // HEAD_DIM
GLA_DK = HEAD_DIM // 2
GLA_DV = HEAD_DIM
GLA_KEY_W = GLA_HEADS * GLA_DK
GLA_GATE_LORA = 16
GLA_TAU = 16.0
GLA_PROJ = 2 * GLA_KEY_W + GLA_W + GLA_GATE_LORA + GLA_W

M2_W = D_MODEL // 2
M2_HEADS = M2_W // HEAD_DIM
M2_STATE = 64
M2_GROUPS = 2
M2_CONV = 4
M2_CONV_DIM = M2_W + 2 * M2_GROUPS * M2_STATE
M2_PROJ = M2_W + M2_CONV_DIM + M2_HEADS

IN_PROJ = RWKV_PROJ + GLA_PROJ + M2_PROJ
MIX_W = RWKV_W + GLA_W + M2_W

kernel_name = 'hybrid_rwkv7_gla_ssd_decode_step'


def _f32(t):
    return t.astype(jnp.float32)


def _rmsnorm(x, w):
    xf = _f32(x)
    y = xf * lax.rsqrt(jnp.mean(xf * xf, axis=-1, keepdims=True) + NORM_EPS)
    return (y * _f32(w)).astype(x.dtype)


def _swiglu(h, w_gate, w_up, w_down):
    return (jax.nn.silu(h @ w_gate) * (h @ w_up)) @ w_down


def _to_chunks(t, c):
    pad = (-t.shape[1]) % c
    t = jnp.pad(t, [(0, 0), (0, pad)] + [(0, 0)] * (t.ndim - 2))
    n = t.shape[1] // c
    return jnp.moveaxis(t.reshape((t.shape[0], n, c) + t.shape[2:]), 1, 0)


def _from_chunks(t, L):
    t = jnp.moveaxis(t, 0, 1)
    return t.reshape((t.shape[0], -1) + t.shape[3:])[:, :L]


def _rwkv7_scan(r, w, k, v, a, b, S0):
    def step(S, inp):
        r_t, w_t, k_t, v_t, a_t, b_t = inp
        sa = jnp.einsum('bhvk,bhk->bhv', S, a_t)
        S = S * w_t[:, :, None, :] + sa[..., None] * b_t[:, :, None, :] + v_t[..., None] * k_t[:, :, None, :]
        return S, jnp.einsum('bhvk,bhk->bhv', S, r_t)
    xs = tuple(jnp.moveaxis(t, 1, 0) for t in (r, w, k, v, a, b))
    S, ys = lax.scan(step, S0, xs)
    return jnp.moveaxis(ys, 0, 1), S


def _rwkv7_mixer(z, shift_prev, S0, mu, w0, w2, a0, a2, g2, k_k, k_a, r_k, ln_w, ln_b):
    Bsz, L, _ = z.shape
    prev = jnp.concatenate([shift_prev[:, None, :].astype(z.dtype), z[:, :-1]], axis=1)
    zs = _f32(z + mu * (prev - z))
    i0 = 3 * RWKV_W
    r, k, v, wl, al, gl = jnp.split(zs, [RWKV_W, 2 * RWKV_W, i0, i0 + RWKV_DECAY_LORA,
                                        i0 + RWKV_DECAY_LORA + RWKV_A_LORA], axis=-1)
    w_raw = -jax.nn.softplus(-(_f32(w0) + jnp.tanh(wl) @ _f32(w2))) - 0.5
    decay = jnp.exp(-jnp.exp(w_raw))
    a = jax.nn.sigmoid(_f32(a0) + al @ _f32(a2))
    g = jax.nn.sigmoid(gl) @ _f32(g2)
    heads = lambda t: t.reshape(Bsz, L, RWKV_HEADS, HEAD_DIM)
    kk = heads(k * _f32(k_k))
    kk = kk / jnp.maximum(jnp.sqrt(jnp.sum(kk * kk, axis=-1, keepdims=True)), 1e-12)
    k = k * (1.0 + (a - 1.0) * _f32(k_a))
    rh, kh, vh, ah = heads(r), heads(k), heads(v), heads(a)
    y, S = _rwkv7_scan(rh, heads(decay), kh, vh, -kk, kk * ah, _f32(S0))
    mean = jnp.mean(y, axis=-1, keepdims=True)
    var = jnp.mean(jnp.square(y - mean), axis=-1, keepdims=True)
    y = ((y - mean) * lax.rsqrt(var + RWKV_GN_EPS)).reshape(Bsz, L, RWKV_W) * _f32(ln_w) + _f32(ln_b)
    bonus = jnp.sum(rh * kh * _f32(r_k), axis=-1, keepdims=True) * vh
    y = (y + bonus.reshape(Bsz, L, RWKV_W)) * g
    return y.astype(z.dtype), z[:, -1], S.astype(z.dtype)


def _gla_chunked(q, k, v, log_a, S0):
    L = q.shape[1]
    c = min(CHUNK, L)
    tri = jnp.tril(jnp.ones((c, c), dtype=bool))[None, :, :, None, None]

    def step(S, inp):
        qc, kc, vc, gc = inp
        b = jnp.cumsum(gc, axis=1)
        inter = jnp.einsum('bthk,bhkv->bthv', qc * jnp.exp(b), S)
        decay = jnp.exp(jnp.where(tri, b[:, :, None] - b[:, None, :], -jnp.inf))
        att = jnp.einsum('bthk,bshk,btshk->bhts', qc, kc, decay)
        intra = jnp.einsum('bhts,bshv->bthv', att, vc)
        b_last = b[:, -1]
        S = S * jnp.exp(b_last)[..., None] + jnp.einsum('bshk,bshv->bhkv', kc * jnp.exp(b_last[:, None] - b), vc)
        return S, inter + intra

    xs = tuple(_to_chunks(t, c) for t in (q, k, v, log_a))
    S, ys = lax.scan(step, S0, xs)
    return _from_chunks(ys, L), S


def _gla_mixer(z, S0, gate_w2, gate_b, norm_w):
    Bsz, L, _ = z.shape
    zf = _f32(z)
    q, k, v, gl, g = jnp.split(zf, [GLA_KEY_W, 2 * GLA_KEY_W, 2 * GLA_KEY_W + GLA_W,
                                    2 * GLA_KEY_W + GLA_W + GLA_GATE_LORA], axis=-1)
    log_a = jax.nn.log_sigmoid(gl @ _f32(gate_w2) + _f32(gate_b)) / GLA_TAU
    kh = lambda t: t.reshape(Bsz, L, GLA_HEADS, GLA_DK)
    o, S = _gla_chunked(kh(q) * (GLA_DK ** -0.5), kh(k), v.reshape(Bsz, L, GLA_HEADS, GLA_DV), kh(log_a), _f32(S0))
    o = _rmsnorm(o, norm_w).reshape(Bsz, L, GLA_W) * jax.nn.silu(g)
    return o.astype(z.dtype), S.astype(z.dtype)


def _ssd_chunked(x, dt, A, Bm, Cm, S0):
    L = x.shape[1]
    c = min(CHUNK, L)
    tri = jnp.tril(jnp.ones((c, c), dtype=bool))[None, :, :, None]

    def step(S, inp):
        xc, dtc, Bc, Cc = inp
        cum = jnp.cumsum(dtc * A, axis=1)
        seg = jnp.exp(jnp.where(tri, cum[:, :, None] - cum[:, None], -jnp.inf))
        scores = jnp.einsum('bthn,bshn->btsh', Cc, Bc) * seg
        y_intra = jnp.einsum('btsh,bshp->bthp', scores, xc * dtc[..., None])
        y_inter = jnp.einsum('bthn,bhpn->bthp', Cc, S) * jnp.exp(cum)[..., None]
        last = cum[:, -1]
        wgt = jnp.exp(last[:, None] - cum) * dtc
        S = S * jnp.exp(last)[..., None, None] + jnp.einsum('bshn,bshp->bhpn', Bc * wgt[..., None], xc)
        return S, y_intra + y_inter

    xs = tuple(_to_chunks(t, c) for t in (x, dt, Bm, Cm))
    S, ys = lax.scan(step, S0, xs)
    return _from_chunks(ys, L), S


def _mamba2_mixer(z, conv_prev, S0, conv_w, conv_b, dt_bias, A_log, D_skip, norm_w):
    Bsz, L, _ = z.shape
    zg, xBC, dtr = jnp.split(z, [M2_W, M2_W + M2_CONV_DIM], axis=-1)
    buf = jnp.concatenate([conv_prev.astype(z.dtype), xBC], axis=1)
    conv = conv_b + buf[:, 0:L] * conv_w[0]
    for j in range(1, M2_CONV):
        conv = conv + buf[:, j:j + L] * conv_w[j]
    xBC_c = _f32(jax.nn.silu(conv))
    xs, Bm, Cm = jnp.split(xBC_c, [M2_W, M2_W + M2_GROUPS * M2_STATE], axis=-1)
    dt = jax.nn.softplus(_f32(dtr) + _f32(dt_bias))
    A = -jnp.exp(_f32(A_log))
    rep = M2_HEADS // M2_GROUPS
    grp = lambda t: jnp.repeat(t.reshape(Bsz, L, M2_GROUPS, M2_STATE), rep, axis=2)
    xh = xs.reshape(Bsz, L, M2_HEADS, HEAD_DIM)
    y, S = _ssd_chunked(xh, dt, A, grp(Bm), grp(Cm), _f32(S0))
    y = (y + _f32(D_skip)[:, None] * xh).reshape(Bsz, L, M2_W)
    y = _rmsnorm(y * jax.nn.silu(_f32(zg)), norm_w)
    return y.astype(z.dtype), buf[:, -(M2_CONV - 1):], S.astype(z.dtype)


def _layer(x, p, state, W, i):
    shift0, wkv0, gla0, conv0, ssm0 = state
    x = x + 0.5 * _swiglu(_rmsnorm(x, W['ffn1_norm'][i]), W['ffn1_w_gate'][i], W['ffn1_w_up'][i], W['ffn1_w_down'][i])
    h = _rmsnorm(x, W['mix_norm'][i])
    zin = h @ W['w_in'][i]
    z_rwkv, z_gla, z_m2 = jnp.split(zin, [RWKV_PROJ, RWKV_PROJ + GLA_PROJ], axis=-1)
    y_rwkv, shift1, wkv1 = _rwkv7_mixer(z_rwkv, shift0, wkv0, W['rwkv_mu'][i], W['rwkv_w0'][i], W['rwkv_w2'][i],
                                        W['rwkv_a0'][i], W['rwkv_a2'][i], W['rwkv_g2'][i], W['rwkv_k_k'][i],
                                        W['rwkv_k_a'][i], W['rwkv_r_k'][i], W['rwkv_ln_w'][i], W['rwkv_ln_b'][i])
    y_gla, gla1 = _gla_mixer(z_gla, gla0, W['gla_gate_w2'][i], W['gla_gate_b'][i], W['gla_norm'][i])
    y_m2, conv1, ssm1 = _mamba2_mixer(z_m2, conv0, ssm0, W['mamba_conv_w'][i], W['mamba_conv_b'][i],
                                      W['mamba_dt_bias'][i], W['mamba_A_log'][i], W['mamba_D'][i], W['mamba_norm'][i])
    x = x + jnp.concatenate([y_rwkv, y_gla, y_m2], axis=-1) @ W['w_out'][i]
    x = x + 0.5 * _swiglu(_rmsnorm(x, W['ffn2_norm'][i]), W['ffn2_w_gate'][i], W['ffn2_w_up'][i], W['ffn2_w_down'][i])
    gate = jax.nn.sigmoid(_rmsnorm(x, W['ple_norm'][i]) @ W['ple_w_gate'][i])
    x = x + gate * (p @ W['ple_w_proj'][i])
    return x, (shift1, wkv1, gla1, conv1, ssm1)


def setup_inputs(seed: int = 0) -> dict:
    key = jax.random.key(seed)
    ks = iter(jax.random.split(key, 64))
    nrm = lambda shape, scale: scale * jax.random.normal(next(ks), shape, jnp.float32)
    gain = lambda shape: 1.0 + 0.05 * jax.random.normal(next(ks), shape, jnp.float32)
    unif = lambda shape, lo, hi: jax.random.uniform(next(ks), shape, jnp.float32, lo, hi)
    Dd = DEPTH
    dt0 = jnp.exp(unif((Dd, M2_HEADS), math.log(1e-3), math.log(1e-1)))
    return {
        'x_prompt': nrm((BATCH, SEQ, D_MODEL), 1.0),
        'x_sample': nrm((DEC_BATCH, DEC_SEQ, D_MODEL), 1.0),
        'p_prompt': nrm((DEPTH, BATCH, SEQ, PLE_DIM), 1.0),
        'p_sample': nrm((DEPTH, DEC_BATCH, DEC_SEQ, PLE_DIM), 1.0),
        'state_rwkv_shift': nrm((Dd, DEC_BATCH, RWKV_PROJ), 1.0),
        'state_rwkv_wkv': nrm((Dd, DEC_BATCH, RWKV_HEADS, HEAD_DIM, HEAD_DIM), 0.1),
        'state_gla': nrm((Dd, DEC_BATCH, GLA_HEADS, GLA_DK, GLA_DV), 0.1),
        'state_mamba_conv': nrm((Dd, DEC_BATCH, M2_CONV - 1, M2_CONV_DIM), 1.0),
        'state_mamba_ssm': nrm((Dd, DEC_BATCH, M2_HEADS, HEAD_DIM, M2_STATE), 0.1),
        'ffn1_norm': gain((Dd, D_MODEL)),
        'ffn1_w_gate': nrm((Dd, D_MODEL, D_FF), D_MODEL ** -0.5),
        'ffn1_w_up': nrm((Dd, D_MODEL, D_FF), D_MODEL ** -0.5),
        'ffn1_w_down': nrm((Dd, D_FF, D_MODEL), D_FF ** -0.5),
        'mix_norm': gain((Dd, D_MODEL)),
        'w_in': nrm((Dd, D_MODEL, IN_PROJ), D_MODEL ** -0.5),
        'rwkv_mu': unif((Dd, RWKV_PROJ), 0.0, 1.0),
        'rwkv_w0': unif((Dd, RWKV_W), -6.0, -1.0),
        'rwkv_w2': nrm((Dd, RWKV_DECAY_LORA, RWKV_W), 0.1 * RWKV_DECAY_LORA ** -0.5),
        'rwkv_a0': nrm((Dd, RWKV_W), 0.1),
        'rwkv_a2': nrm((Dd, RWKV_A_LORA, RWKV_W), 0.1 * RWKV_A_LORA ** -0.5),
        'rwkv_g2': nrm((Dd, RWKV_GATE_LORA, RWKV_W), RWKV_GATE_LORA ** -0.5),
        'rwkv_k_k': 0.85 + nrm((Dd, RWKV_W), 0.05),
        'rwkv_k_a': gain((Dd, RWKV_W)),
        'rwkv_r_k': nrm((Dd, RWKV_HEADS, HEAD_DIM), 0.1),
        'rwkv_ln_w': gain((Dd, RWKV_W)),
        'rwkv_ln_b': nrm((Dd, RWKV_W), 0.02),
        'gla_gate_w2': nrm((Dd, GLA_GATE_LORA, GLA_KEY_W), GLA_GATE_LORA ** -0.5),
        'gla_gate_b': nrm((Dd, GLA_KEY_W), 0.1),
        'gla_norm': gain((Dd, GLA_DV)),
        'mamba_conv_w': nrm((Dd, M2_CONV, M2_CONV_DIM), M2_CONV ** -0.5),
        'mamba_conv_b': nrm((Dd, M2_CONV_DIM), 0.02),
        'mamba_dt_bias': dt0 + jnp.log(-jnp.expm1(-dt0)),
        'mamba_A_log': jnp.log(unif((Dd, M2_HEADS), 1.0, 16.0)),
        'mamba_D': gain((Dd, M2_HEADS)),
        'mamba_norm': gain((Dd, M2_W)),
        'w_out': nrm((Dd, MIX_W, D_MODEL), MIX_W ** -0.5),
        'ffn2_norm': gain((Dd, D_MODEL)),
        'ffn2_w_gate': nrm((Dd, D_MODEL, D_FF), D_MODEL ** -0.5),
        'ffn2_w_up': nrm((Dd, D_MODEL, D_FF), D_MODEL ** -0.5),
        'ffn2_w_down': nrm((Dd, D_FF, D_MODEL), D_FF ** -0.5),
        'ple_norm': gain((Dd, D_MODEL)),
        'ple_w_gate': nrm((Dd, D_MODEL, D_MODEL), D_MODEL ** -0.5),
        'ple_w_proj': nrm((Dd, PLE_DIM, D_MODEL), PLE_DIM ** -0.5),
        'final_norm': gain((D_MODEL,)),
    }


def reference(x_prompt, x_sample, p_prompt, p_sample, state_rwkv_shift, state_rwkv_wkv, state_gla,
              state_mamba_conv, state_mamba_ssm, ffn1_norm, ffn1_w_gate, ffn1_w_up, ffn1_w_down, mix_norm, w_in,
              rwkv_mu, rwkv_w0, rwkv_w2, rwkv_a0, rwkv_a2, rwkv_g2, rwkv_k_k, rwkv_k_a, rwkv_r_k, rwkv_ln_w, rwkv_ln_b,
              gla_gate_w2, gla_gate_b, gla_norm, mamba_conv_w, mamba_conv_b, mamba_dt_bias, mamba_A_log, mamba_D,
              mamba_norm, w_out, ffn2_norm, ffn2_w_gate, ffn2_w_up, ffn2_w_down, ple_norm, ple_w_gate, ple_w_proj,
              final_norm):
    W = {
        'ffn1_norm': ffn1_norm, 'ffn1_w_gate': ffn1_w_gate, 'ffn1_w_up': ffn1_w_up, 'ffn1_w_down': ffn1_w_down,
        'mix_norm': mix_norm, 'w_in': w_in,
        'rwkv_mu': rwkv_mu, 'rwkv_w0': rwkv_w0, 'rwkv_w2': rwkv_w2, 'rwkv_a0': rwkv_a0, 'rwkv_a2': rwkv_a2,
        'rwkv_g2': rwkv_g2, 'rwkv_k_k': rwkv_k_k, 'rwkv_k_a': rwkv_k_a, 'rwkv_r_k': rwkv_r_k,
        'rwkv_ln_w': rwkv_ln_w, 'rwkv_ln_b': rwkv_ln_b,
        'gla_gate_w2': gla_gate_w2, 'gla_gate_b': gla_gate_b, 'gla_norm': gla_norm,
        'mamba_conv_w': mamba_conv_w, 'mamba_conv_b': mamba_conv_b, 'mamba_dt_bias': mamba_dt_bias,
        'mamba_A_log': mamba_A_log, 'mamba_D': mamba_D, 'mamba_norm': mamba_norm,
        'w_out': w_out,
        'ffn2_norm': ffn2_norm, 'ffn2_w_gate': ffn2_w_gate, 'ffn2_w_up': ffn2_w_up, 'ffn2_w_down': ffn2_w_down,
        'ple_norm': ple_norm, 'ple_w_gate': ple_w_gate, 'ple_w_proj': ple_w_proj,
    }
    nb = x_prompt.shape[0]
    dt_ = x_prompt.dtype
    xp, xs = x_prompt, x_sample
    p_states = ([], [], [], [], [])
    s_states = ([], [], [], [], [])
    for i in range(DEPTH):
        fresh = (jnp.zeros((nb, RWKV_PROJ), dt_),
                 jnp.zeros((nb, RWKV_HEADS, HEAD_DIM, HEAD_DIM), dt_),
                 jnp.zeros((nb, GLA_HEADS, GLA_DK, GLA_DV), dt_),
                 jnp.zeros((nb, M2_CONV - 1, M2_CONV_DIM), dt_),
                 jnp.zeros((nb, M2_HEADS, HEAD_DIM, M2_STATE), dt_))
        xp, st_p = _layer(xp, p_prompt[i], fresh, W, i)
        past = (state_rwkv_shift[i], state_rwkv_wkv[i], state_gla[i], state_mamba_conv[i], state_mamba_ssm[i])
        xs, st_s = _layer(xs, p_sample[i], past, W, i)
        for j in range(5):
            p_states[j].append(st_p[j])
            s_states[j].append(st_s[j])
    y_prompt = _rmsnorm(xp, final_norm)
    y_sample = _rmsnorm(xs, final_norm)
    prompt_rwkv_shift = jnp.stack(p_states[0])
    prompt_rwkv_wkv = jnp.stack(p_states[1])
    prompt_gla = jnp.stack(p_states[2])
    prompt_mamba_conv = jnp.stack(p_states[3])
    prompt_mamba_ssm = jnp.stack(p_states[4])
    sample_rwkv_shift = jnp.stack(s_states[0])
    sample_rwkv_wkv = jnp.stack(s_states[1])
    sample_gla = jnp.stack(s_states[2])
    sample_mamba_conv = jnp.stack(s_states[3])
    sample_mamba_ssm = jnp.stack(s_states[4])
    return (y_prompt, y_sample, prompt_rwkv_shift, prompt_rwkv_wkv, prompt_gla, prompt_mamba_conv, prompt_mamba_ssm,
            sample_rwkv_shift, sample_rwkv_wkv, sample_gla, sample_mamba_conv, sample_mamba_ssm)
```

```python
import functools

import jax
import jax.numpy as jnp
from jax import lax
from jax.experimental import pallas as pl
from jax.experimental.pallas import tpu as pltpu

F32 = jnp.float32
BF16 = jnp.bfloat16

D_MODEL = 1024
D_FF = 2816
PLE_DIM = 256
HEAD_DIM = 64
NORM_EPS = 1e-6

RWKV_W = 256
RWKV_HEADS = 4
RWKV_GN_EPS = 64e-5
RWKV_PROJ = 896
RWKV_LORA = 128
RWKV_DECAY_LORA = 32
RWKV_A_LORA = 32

GLA_W = 256
GLA_HEADS = 4
GLA_DK = 32
GLA_KEY_W = 128
GLA_GATE_LORA = 16
GLA_TAU = 16.0
GLA_PROJ = 784
GLA_PAD = 896

M2_W = 512
M2_HEADS = 8
M2_STATE = 64
M2_GROUPS = 2
M2_CONV = 4
M2_CONV_DIM = 768
M2_PROJ = 1288
M2_PAD = 1408

LANE = 128
SUBLANE = 8
FF_CHUNK = 256
VMEM_LIMIT = 56 * 1024 * 1024

RWKV_CHUNK = 64
GLA_CHUNK = 64
SSD_CHUNK = 128


def _bdot(a, b):
    return jnp.dot(a.astype(BF16), b.astype(BF16), preferred_element_type=F32)


def _bdot_nt(a, b):
    return lax.dot_general(a.astype(BF16), b.astype(BF16), (((1,), (1,)), ((), ())),
                           preferred_element_type=F32)


def _bdot_tn(a, b):
    return lax.dot_general(a.astype(BF16), b.astype(BF16), (((0,), (0,)), ((), ())),
                           preferred_element_type=F32)


def _split3(x):
    hi = x.astype(BF16)
    r1 = x - hi.astype(F32)
    mid = r1.astype(BF16)
    lo = (r1 - mid.astype(F32)).astype(BF16)
    return hi, mid, lo


def _ldot3(mat, x):
    return sum(jnp.dot(mat, p, preferred_element_type=F32) for p in _split3(x))


def _rdot3(x, mat):
    return sum(jnp.dot(p, mat, preferred_element_type=F32) for p in _split3(x))


def _nt3(mat, x):
    return sum(lax.dot_general(mat, p, (((1,), (1,)), ((), ())), preferred_element_type=F32)
               for p in _split3(x))


def _softplus(x):
    return jnp.maximum(x, 0.0) + jnp.log1p(jnp.exp(-jnp.abs(x)))


def _sigmoid(x):
    return 1.0 / (1.0 + jnp.exp(-x))


def _silu(x):
    return x * _sigmoid(x)


def _rms(x, w):
    return x * lax.rsqrt(jnp.mean(x * x, axis=-1, keepdims=True) + NORM_EPS) * w


def _const_spec(shape):
    nd = len(shape)
    return pl.BlockSpec(shape, lambda *_: (0,) * nd, pipeline_mode=pl.Buffered(1))


def _ffn(x, nw, wg_ref, wu_ref, wd_ref):
    xn = _rms(x, nw).astype(BF16)
    acc = jnp.zeros_like(x)
    for c in range(D_FF // FF_CHUNK):
        sl = slice(c * FF_CHUNK, (c + 1) * FF_CHUNK)
        hg = jnp.dot(xn, wg_ref[:, sl], preferred_element_type=F32)
        hu = jnp.dot(xn, wu_ref[:, sl], preferred_element_type=F32)
        act = (_silu(hg) * hu).astype(BF16)
        acc = acc + jnp.dot(act, wd_ref[sl, :], preferred_element_type=F32)
    return x + 0.5 * acc


def _pre_kernel(x_ref, n1_ref, wg_ref, wu_ref, wd_ref, nm_ref, win_ref,
                x1_ref, zr_ref, zg_ref, zm_ref):
    x1 = _ffn(x_ref[...], n1_ref[...], wg_ref, wu_ref, wd_ref)
    x1_ref[...] = x1
    h = _rms(x1, nm_ref[...]).astype(BF16)
    zr_ref[...] = jnp.dot(h, win_ref[:, 0:RWKV_PROJ], preferred_element_type=F32)
    zg_ref[...] = jnp.dot(h, win_ref[:, RWKV_PROJ:RWKV_PROJ + GLA_PAD], preferred_element_type=F32)
    zm_ref[...] = jnp.dot(h, win_ref[:, RWKV_PROJ + GLA_PAD:], preferred_element_type=F32)


def _pre_call(x, n1, wg, wu, wd, nm, win, tm):
    n = x.shape[0]
    tok = lambda w: pl.BlockSpec((tm, w), lambda i: (i, 0))
    return pl.pallas_call(
        _pre_kernel,
        grid=(n // tm,),
        in_specs=[tok(D_MODEL), _const_spec(n1.shape), _const_spec(wg.shape), _const_spec(wu.shape),
                  _const_spec(wd.shape), _const_spec(nm.shape), _const_spec(win.shape)],
        out_specs=[tok(D_MODEL), tok(RWKV_PROJ), tok(GLA_PAD), tok(M2_PAD)],
        out_shape=[jax.ShapeDtypeStruct((n, D_MODEL), F32), jax.ShapeDtypeStruct((n, RWKV_PROJ), F32),
                   jax.ShapeDtypeStruct((n, GLA_PAD), F32), jax.ShapeDtypeStruct((n, M2_PAD), F32)],
        compiler_params=pltpu.CompilerParams(dimension_semantics=("parallel",), vmem_limit_bytes=VMEM_LIMIT),
        name="pre_ffn_inproj",
    )(x, n1, wg, wu, wd, nm, win)


def _post_kernel(x_ref, yr_ref, yg_ref, ym_ref, p_ref, wo_ref, n2_ref, wg_ref, wu_ref, wd_ref,
                 np_ref, pg_ref, pp_ref, nf_ref, o_ref, *, final):
    x = x_ref[...]
    x = x + jnp.dot(yr_ref[...].astype(BF16), wo_ref[0:RWKV_W, :], preferred_element_type=F32)
    x = x + jnp.dot(yg_ref[...].astype(BF16), wo_ref[RWKV_W:RWKV_W + GLA_W, :], preferred_element_type=F32)
    x = x + jnp.dot(ym_ref[...].astype(BF16), wo_ref[RWKV_W + GLA_W:, :], preferred_element_type=F32)
    x = _ffn(x, n2_ref[...], wg_ref, wu_ref, wd_ref)
    gate = _sigmoid(jnp.dot(_rms(x, np_ref[...]).astype(BF16), pg_ref[...], preferred_element_type=F32))
    x = x + gate * jnp.dot(p_ref[...].astype(BF16), pp_ref[...], preferred_element_type=F32)
    if final:
        x = _rms(x, nf_ref[...])
    o_ref[...] = x


def _post_call(x, yr, yg, ym, p, wo, n2, wg, wu, wd, npn, pg, pp, nf, tm, final):
    n = x.shape[0]
    tok = lambda w: pl.BlockSpec((tm, w), lambda i: (i, 0))
    consts = (wo, n2, wg, wu, wd, npn, pg, pp, nf)
    return pl.pallas_call(
        functools.partial(_post_kernel, final=final),
        grid=(n // tm,),
        in_specs=[tok(D_MODEL), tok(RWKV_W), tok(GLA_W), tok(M2_W), tok(PLE_DIM)]
                 + [_const_spec(a.shape) for a in consts],
        out_specs=tok(D_MODEL),
        out_shape=jax.ShapeDtypeStruct((n, D_MODEL), F32),
        compiler_params=pltpu.CompilerParams(dimension_semantics=("parallel",), vmem_limit_bytes=VMEM_LIMIT),
        name="post_outproj_ffn_ple",
    )(x, yr, yg, ym, p, *consts)


def _tri_masks(n):
    ri = lax.broadcasted_iota(jnp.int32, (n, n), 0)
    ci = lax.broadcasted_iota(jnp.int32, (n, n), 1)
    return ri >= ci, ri > ci, (ri == ci).astype(F32)


def _unit_lower_inverse(a, eye, n):
    t = eye + a
    x = a
    p = 1
    while 2 * p < n:
        x = _bdot(x, x)
        p *= 2
        t = t + _bdot(t, x)
    return t


def _rwkv_kernel(z_ref, sh_ref, s0_ref, mu_ref, w0_ref, w2_ref, a0_ref, a2_ref, g2_ref, kk_ref, ka_ref,
                 rk_ref, lnw_ref, lnb_ref, tri_ref, ones_ref,
                 y_ref, s_ref, zbuf_ref, ybuf_ref, *, bb, chunk, valid):
    c = pl.program_id(1)

    @pl.when(c == 0)
    def _():
        s_ref[...] = s0_ref[...]
        zbuf_ref[:, SUBLANE - 1:SUBLANE, :] = sh_ref[...]

    incl, strict, eye = _tri_masks(chunk)
    tri = tri_ref[...]
    ones = ones_ref[...]
    row = lax.broadcasted_iota(jnp.int32, (chunk, 1), 0)
    mu = mu_ref[...]
    for i in range(bb):
        z = z_ref[i]
        zbuf_ref[i, SUBLANE:SUBLANE + chunk, :] = z
        prev = zbuf_ref[i, SUBLANE - 1:SUBLANE - 1 + chunk, :]
        zbuf_ref[i, SUBLANE - 1:SUBLANE, :] = z[chunk - 1:chunk, :]
        zs = z + mu * (prev - z)
        r = zs[:, 0:RWKV_W]
        k = zs[:, RWKV_W:2 * RWKV_W]
        v = zs[:, 2 * RWKV_W:3 * RWKV_W]
        lora = zs[:, 3 * RWKV_W:]
        w_raw = -_softplus(-(w0_ref[...] + _bdot(jnp.tanh(lora), w2_ref[...]))) - 0.5
        logw = -jnp.exp(w_raw)
        a = _sigmoid(a0_ref[...] + _bdot(lora, a2_ref[...]))
        gate = _bdot(_sigmoid(lora), g2_ref[...])
        kk = k * kk_ref[...]
        kk = kk / jnp.maximum(jnp.sqrt(_rdot3(kk * kk, ones)), 1e-12)
        k = k * (1.0 + (a - 1.0) * ka_ref[...])
        ahat = -kk
        bhat = kk * a
        if valid < chunk:
            vm = row < valid
            logw = jnp.where(vm, logw, 0.0)
            ahat = jnp.where(vm, ahat, 0.0)
            bhat = jnp.where(vm, bhat, 0.0)
            k = jnp.where(vm, k, 0.0)
            v = jnp.where(vm, v, 0.0)
        g = _ldot3(tri, logw)
        gprev = g - logw
        glast = g[chunk - 1:chunk, :]
        e_g = jnp.exp(g)
        e_ng = jnp.exp(-g)
        e_gl = jnp.exp(glast - g)
        a_l = ahat * jnp.exp(gprev)
        r_l = r * e_g
        b_r = bhat * e_ng
        k_r = k * e_ng
        k_p = k * e_gl
        b_p = bhat * e_gl
        dec = jnp.exp(glast)
        for h in range(RWKV_HEADS):
            hs = slice(h * HEAD_DIM, (h + 1) * HEAD_DIM)
            a_ab = jnp.where(strict, _bdot_nt(a_l[:, hs], b_r[:, hs]), 0.0)
            a_ak = jnp.where(strict, _bdot_nt(a_l[:, hs], k_r[:, hs]), 0.0)
            a_rb = jnp.where(incl, _bdot_nt(r_l[:, hs], b_r[:, hs]), 0.0)
            a_rk = jnp.where(incl, _bdot_nt(r_l[:, hs], k_r[:, hs]), 0.0)
            t = _unit_lower_inverse(a_ab, eye, chunk)
            vh = v[:, hs]
            w_t = _bdot(t, a_l[:, hs])
            u_t = _bdot(t, _bdot(a_ak, vh))
            r_t = r_l[:, hs] + _bdot(a_rb, w_t)
            y_t = _bdot(a_rk, vh) + _bdot(a_rb, u_t)
            m = _bdot_tn(w_t, b_p[:, hs])
            q = _bdot_tn(vh, k_p[:, hs]) + _bdot_tn(u_t, b_p[:, hs])
            s = s_ref[i, h]
            ybuf_ref[:, hs] = _bdot_nt(r_t, s) + y_t
            s_ref[i, h] = s * dec[:, hs] + _bdot(s, m) + q
        y = ybuf_ref[...]
        mean = _rdot3(y, ones) * (1.0 / HEAD_DIM)
        yc = y - mean
        var = _rdot3(yc * yc, ones) * (1.0 / HEAD_DIM)
        yn = yc * lax.rsqrt(var + RWKV_GN_EPS) * lnw_ref[...] + lnb_ref[...]
        bonus = _rdot3(r * k * rk_ref[...], ones) * v
        y_ref[i] = ((yn + bonus) * gate).astype(y_ref.dtype)


def _rwkv_call(z, shift, s0, params, bb, chunk, valid, out_dtype):
    b, l, _ = z.shape
    grid = (b // bb, l // chunk)
    tri = jnp.tril(jnp.ones((chunk, chunk), F32)).astype(BF16)
    consts = tuple(params) + (tri, _HEAD_ONES())
    return pl.pallas_call(
        functools.partial(_rwkv_kernel, bb=bb, chunk=chunk, valid=valid),
        grid=grid,
        in_specs=[pl.BlockSpec((bb, chunk, RWKV_PROJ), lambda i, c: (i, c, 0)),
                  pl.BlockSpec((bb, 1, RWKV_PROJ), lambda i, c: (i, 0, 0)),
                  pl.BlockSpec((bb, RWKV_HEADS, HEAD_DIM, HEAD_DIM), lambda i, c: (i, 0, 0, 0))]
                 + [_const_spec(a.shape) for a in consts],
        out_specs=[pl.BlockSpec((bb, chunk, RWKV_W), lambda i, c: (i, c, 0)),
                   pl.BlockSpec((bb, RWKV_HEADS, HEAD_DIM, HEAD_DIM), lambda i, c: (i, 0, 0, 0))],
        out_shape=[jax.ShapeDtypeStruct((b, l, RWKV_W), out_dtype),
                   jax.ShapeDtypeStruct((b, RWKV_HEADS, HEAD_DIM, HEAD_DIM), F32)],
        scratch_shapes=[pltpu.VMEM((bb, SUBLANE + chunk, RWKV_PROJ), F32),
                        pltpu.VMEM((chunk, RWKV_W), F32)],
        compiler_params=pltpu.CompilerParams(dimension_semantics=("arbitrary", "arbitrary"),
                                             vmem_limit_bytes=VMEM_LIMIT),
        name="rwkv7_mixer",
    )(z, shift, s0, *consts)


def _gla_kernel(z_ref, s0_ref, gw_ref, gb_ref, nw_ref, tri_ref, ones_ref, eye_ref,
                y_ref, s_ref, obuf_ref, *, bb, chunk, valid):
    c = pl.program_id(1)

    @pl.when(c == 0)
    def _():
        s_ref[...] = s0_ref[...]

    incl, _, _ = _tri_masks(chunk)
    tri = tri_ref[...]
    ones = ones_ref[...]
    row = lax.broadcasted_iota(jnp.int32, (chunk, 1), 0)
    for i in range(bb):
        z = z_ref[i]
        q = z[:, 0:GLA_KEY_W] * (GLA_DK ** -0.5)
        k = z[:, GLA_KEY_W:2 * GLA_KEY_W]
        v = z[:, 2 * GLA_KEY_W:2 * GLA_KEY_W + GLA_W]
        og = z[:, 2 * GLA_KEY_W + GLA_W:2 * GLA_KEY_W + 2 * GLA_W]
        lora = z[:, 2 * GLA_KEY_W + 2 * GLA_W:]
        log_a = -_softplus(-(_bdot(lora, gw_ref[...]) + gb_ref[...])) * (1.0 / GLA_TAU)
        if valid < chunk:
            vm = row < valid
            log_a = jnp.where(vm, log_a, 0.0)
            k = jnp.where(vm, k, 0.0)
        b = _ldot3(tri, log_a)
        blast = b[chunk - 1:chunk, :]
        qe = q * jnp.exp(b)
        ke = k * jnp.exp(-b)
        kl = k * jnp.exp(blast - b)
        dec_rows = jnp.broadcast_to(jnp.exp(blast), (HEAD_DIM, GLA_KEY_W))
        for h in range(GLA_HEADS):
            ks = slice(h * GLA_DK, (h + 1) * GLA_DK)
            vs = slice(h * HEAD_DIM, (h + 1) * HEAD_DIM)
            att = jnp.where(incl, _bdot_nt(qe[:, ks], ke[:, ks]), 0.0)
            s = s_ref[i, h]
            obuf_ref[:, vs] = _bdot(att, v[:, vs]) + _bdot(qe[:, ks], s)
            dec = _nt3(eye_ref[h * GLA_DK:(h + 1) * GLA_DK, :], dec_rows)
            s_ref[i, h] = s * dec + _bdot_tn(kl[:, ks], v[:, vs])
        o = obuf_ref[...]
        ms = _rdot3(o * o, ones) * (1.0 / HEAD_DIM)
        y_ref[i] = (o * lax.rsqrt(ms + NORM_EPS) * nw_ref[...] * _silu(og)).astype(y_ref.dtype)


def _gla_call(z, s0, params, bb, chunk, valid, out_dtype):
    b, l, _ = z.shape
    tri = jnp.tril(jnp.ones((chunk, chunk), F32)).astype(BF16)
    consts = tuple(params) + (tri, _HEAD_ONES(), jnp.eye(LANE, dtype=BF16))
    return pl.pallas_call(
        functools.partial(_gla_kernel, bb=bb, chunk=chunk, valid=valid),
        grid=(b // bb, l // chunk),
        in_specs=[pl.BlockSpec((bb, chunk, GLA_PAD), lambda i, c: (i, c, 0)),
                  pl.BlockSpec((bb, GLA_HEADS, GLA_DK, HEAD_DIM), lambda i, c: (i, 0, 0, 0))]
                 + [_const_spec(a.shape) for a in consts],
        out_specs=[pl.BlockSpec((bb, chunk, GLA_W), lambda i, c: (i, c, 0)),
                   pl.BlockSpec((bb, GLA_HEADS, GLA_DK, HEAD_DIM), lambda i, c: (i, 0, 0, 0))],
        out_shape=[jax.ShapeDtypeStruct((b, l, GLA_W), out_dtype),
                   jax.ShapeDtypeStruct((b, GLA_HEADS, GLA_DK, HEAD_DIM), F32)],
        scratch_shapes=[pltpu.VMEM((chunk, GLA_W), F32)],
        compiler_params=pltpu.CompilerParams(dimension_semantics=("arbitrary", "arbitrary"),
                                             vmem_limit_bytes=VMEM_LIMIT),
        name="gla_mixer",
    )(z, s0, *consts)


def _ssd_kernel(z_ref, cp_ref, s0_ref, cw_ref, cb_ref, dtb_ref, alog_ref, dsk_ref, nw_ref, tri_ref, eye_ref,
                y_ref, s_ref, xbuf_ref, ybuf_ref, *, bb, chunk, valid):
    c = pl.program_id(1)

    @pl.when(c == 0)
    def _():
        s_ref[...] = s0_ref[...]
        xbuf_ref[:, 0:SUBLANE, :] = cp_ref[...]

    incl, _, _ = _tri_masks(chunk)
    tri = tri_ref[...]
    row = lax.broadcasted_iota(jnp.int32, (chunk, 1), 0)
    lane = lax.broadcasted_iota(jnp.int32, (1, LANE), 1)
    rep = M2_HEADS // M2_GROUPS
    for i in range(bb):
        z = z_ref[i]
        zg = z[:, 0:M2_W]
        x = z[:, M2_W:M2_W + M2_CONV_DIM]
        dtr = z[:, M2_W + M2_CONV_DIM:]
        xbuf_ref[i, SUBLANE:SUBLANE + chunk, :] = x
        conv = cb_ref[...] + x * cw_ref[M2_CONV - 1:M2_CONV, :]
        for j in range(1, M2_CONV):
            conv = conv + xbuf_ref[i, SUBLANE - j:SUBLANE - j + chunk, :] * cw_ref[M2_CONV - 1 - j:M2_CONV - j, :]
        xbuf_ref[i, 0:SUBLANE, :] = xbuf_ref[i, chunk:chunk + SUBLANE, :]
        xc = _silu(conv)
        xs = xc[:, 0:M2_W]
        bm = xc[:, M2_W:M2_W + M2_GROUPS * M2_STATE]
        cm = xc[:, M2_W + M2_GROUPS * M2_STATE:]
        live = lane < M2_HEADS
        if valid < chunk:
            live = jnp.logical_and(live, row < valid)
        dt = jnp.where(live, _softplus(dtr + dtb_ref[...]), 0.0)
        cum = _ldot3(tri, dt * (-jnp.exp(alog_ref[...])))
        cum_t = _nt3(eye_ref[0:SUBLANE, :], cum)
        last = cum[chunk - 1:chunk, :]
        for g in range(M2_GROUPS):
            gs = slice(g * M2_STATE, (g + 1) * M2_STATE)
            bg = bm[:, gs]
            cg = cm[:, gs]
            cb = _bdot_nt(cg, bg)
            for hh in range(rep):
                h = g * rep + hh
                hs = slice(h * HEAD_DIM, (h + 1) * HEAD_DIM)
                cc = cum[:, h:h + 1]
                seg = jnp.where(incl, jnp.exp(jnp.minimum(cc - cum_t[h:h + 1, :], 0.0)), 0.0)
                dth = dt[:, h:h + 1]
                xh = xs[:, hs]
                s = s_ref[i, h]
                lh = last[:, h:h + 1]
                ybuf_ref[:, hs] = (_bdot(cb * seg, xh * dth) + _bdot_nt(cg, s) * jnp.exp(cc)
                                   + dsk_ref[:, h:h + 1] * xh)
                s_ref[i, h] = s * jnp.exp(lh) + _bdot_tn(xh, bg * (jnp.exp(lh - cc) * dth))
        y = ybuf_ref[...] * _silu(zg)
        y_ref[i] = _rms(y, nw_ref[...]).astype(y_ref.dtype)


def _ssd_call(z, conv_prev, s0, params, bb, chunk, valid, out_dtype):
    b, l, _ = z.shape
    tri = jnp.tril(jnp.ones((chunk, chunk), F32)).astype(BF16)
    consts = tuple(params) + (tri, jnp.eye(LANE, dtype=BF16))
    return pl.pallas_call(
        functools.partial(_ssd_kernel, bb=bb, chunk=chunk, valid=valid),
        grid=(b // bb, l // chunk),
        in_specs=[pl.BlockSpec((bb, chunk, M2_PAD), lambda i, c: (i, c, 0)),
                  pl.BlockSpec((bb, SUBLANE, M2_CONV_DIM), lambda i, c: (i, 0, 0)),
                  pl.BlockSpec((bb, M2_HEADS, HEAD_DIM, M2_STATE), lambda i, c: (i, 0, 0, 0))]
                 + [_const_spec(a.shape) for a in consts],
        out_specs=[pl.BlockSpec((bb, chunk, M2_W), lambda i, c: (i, c, 0)),
                   pl.BlockSpec((bb, M2_HEADS, HEAD_DIM, M2_STATE), lambda i, c: (i, 0, 0, 0))],
        out_shape=[jax.ShapeDtypeStruct((b, l, M2_W), out_dtype),
                   jax.ShapeDtypeStruct((b, M2_HEADS, HEAD_DIM, M2_STATE), F32)],
        scratch_shapes=[pltpu.VMEM((bb, SUBLANE + chunk, M2_CONV_DIM), F32),
                        pltpu.VMEM((chunk, M2_W), F32)],
        compiler_params=pltpu.CompilerParams(dimension_semantics=("arbitrary", "arbitrary"),
                                             vmem_limit_bytes=VMEM_LIMIT),
        name="ssd_mixer",
    )(z, conv_prev, s0, *consts)


def _HEAD_ONES():
    idx = jnp.arange(RWKV_W) // HEAD_DIM
    return (idx[:, None] == idx[None, :]).astype(BF16)


def _row(v):
    return v.reshape(1, -1).astype(F32)


def _pad_rows(m, top, total):
    return jnp.zeros((total, m.shape[1]), F32).at[top:top + m.shape[0]].set(m)


def _pad_lanes(v, total=LANE):
    return jnp.zeros((1, total), F32).at[0, :v.shape[0]].set(v)


def _layer_params(i, ffn1_norm, ffn1_w_gate, ffn1_w_up, ffn1_w_down, mix_norm, w_in, rwkv_mu, rwkv_w0, rwkv_w2,
                  rwkv_a0, rwkv_a2, rwkv_g2, rwkv_k_k, rwkv_k_a, rwkv_r_k, rwkv_ln_w, rwkv_ln_b, gla_gate_w2,
                  gla_gate_b, gla_norm, mamba_conv_w, mamba_conv_b, mamba_dt_bias, mamba_A_log, mamba_D, mamba_norm,
                  w_out, ffn2_norm, ffn2_w_gate, ffn2_w_up, ffn2_w_down, ple_norm, ple_w_gate, ple_w_proj):
    w = w_in[i]
    g0 = RWKV_PROJ
    m0 = RWKV_PROJ + GLA_PROJ
    qkv_end = 2 * GLA_KEY_W + GLA_W
    gla_cols = jnp.concatenate([
        w[:, g0:g0 + qkv_end], w[:, g0 + qkv_end + GLA_GATE_LORA:m0],
        w[:, g0 + qkv_end:g0 + qkv_end + GLA_GATE_LORA],
        jnp.zeros((D_MODEL, LANE - GLA_GATE_LORA), F32)], axis=1)
    m2_cols = jnp.concatenate([w[:, m0:], jnp.zeros((D_MODEL, LANE - M2_HEADS), F32)], axis=1)
    win = jnp.concatenate([w[:, :g0], gla_cols, m2_cols], axis=1).astype(BF16)
    pre = (_row(ffn1_norm[i]), ffn1_w_gate[i].astype(BF16), ffn1_w_up[i].astype(BF16),
           ffn1_w_down[i].astype(BF16), _row(mix_norm[i]), win)
    rwkv = (_row(rwkv_mu[i]), _row(rwkv_w0[i]),
            _pad_rows(rwkv_w2[i], 0, RWKV_LORA).astype(BF16), _row(rwkv_a0[i]),
            _pad_rows(rwkv_a2[i], RWKV_DECAY_LORA, RWKV_LORA).astype(BF16),
            _pad_rows(rwkv_g2[i], RWKV_DECAY_LORA + RWKV_A_LORA, RWKV_LORA).astype(BF16),
            _row(rwkv_k_k[i]), _row(rwkv_k_a[i]), _row(rwkv_r_k[i]), _row(rwkv_ln_w[i]), _row(rwkv_ln_b[i]))
    gla = (_pad_rows(gla_gate_w2[i], 0, LANE).astype(BF16), _row(gla_gate_b[i]),
           _row(jnp.tile(gla_norm[i], GLA_HEADS)))
    ssd = (mamba_conv_w[i].astype(F32), _row(mamba_conv_b[i]), _pad_lanes(mamba_dt_bias[i]),
           _pad_lanes(mamba_A_log[i]), _pad_lanes(mamba_D[i]), _row(mamba_norm[i]))
    post = (w_out[i].astype(BF16), _row(ffn2_norm[i]), ffn2_w_gate[i].astype(BF16), ffn2_w_up[i].astype(BF16),
            ffn2_w_down[i].astype(BF16), _row(ple_norm[i]), ple_w_gate[i].astype(BF16),
            ple_w_proj[i].astype(BF16))
    return pre, rwkv, gla, ssd, post


def _pad_time(a, total):
    return jnp.pad(a, ((0, 0), (0, total - a.shape[1]), (0, 0)))


def _group_layer(x, p, state, lp, final_norm, *, batch, seq, chunks, bb, tm, y_dtype, final):
    pre, rwkv, gla, ssd, post = lp
    shift0, wkv0, gla0, conv0, ssm0 = state
    x1, zr, zg, zm = _pre_call(x, *pre, tm)
    zr = zr.reshape(batch, seq, RWKV_PROJ)
    zg = zg.reshape(batch, seq, GLA_PAD)
    zm = zm.reshape(batch, seq, M2_PAD)
    xbc = zm[:, :, M2_W:M2_W + M2_CONV_DIM]
    shift1 = zr[:, -1]
    conv1 = jnp.concatenate([conv0, xbc[:, -(M2_CONV - 1):]], axis=1)[:, -(M2_CONV - 1):]
    conv_hist = jnp.pad(conv0, ((0, 0), (SUBLANE - (M2_CONV - 1), 0), (0, 0)))
    c_r, c_g, c_m = chunks
    pad = max(c_r, c_g, c_m, seq)
    if pad != seq:
        zr, zg, zm = _pad_time(zr, pad), _pad_time(zg, pad), _pad_time(zm, pad)
    valid = lambda ch: seq if seq < ch else ch
    yr, wkv1 = _rwkv_call(zr, shift0[:, None, :], wkv0, rwkv, bb, c_r, valid(c_r), y_dtype)
    yg, gla1 = _gla_call(zg, gla0, gla, bb, c_g, valid(c_g), y_dtype)
    ym, ssm1 = _ssd_call(zm, conv_hist, ssm0, ssd, bb, c_m, valid(c_m), y_dtype)
    n = batch * seq
    flat = lambda y: y[:, :seq].reshape(n, y.shape[-1])
    x2 = _post_call(x1, flat(yr), flat(yg), flat(ym), p, *post, _row(final_norm), tm, final)
    return x2, (shift1, wkv1, gla1, conv1, ssm1)


def kernel(x_prompt, x_sample, p_prompt, p_sample, state_rwkv_shift, state_rwkv_wkv, state_gla, state_mamba_conv, state_mamba_ssm, ffn1_norm, ffn1_w_gate, ffn1_w_up, ffn1_w_down, mix_norm, w_in, rwkv_mu, rwkv_w0, rwkv_w2, rwkv_a0, rwkv_a2, rwkv_g2, rwkv_k_k, rwkv_k_a, rwkv_r_k, rwkv_ln_w, rwkv_ln_b, gla_gate_w2, gla_gate_b, gla_norm, mamba_conv_w, mamba_conv_b, mamba_dt_bias, mamba_A_log, mamba_D, mamba_norm, w_out, ffn2_norm, ffn2_w_gate, ffn2_w_up, ffn2_w_down, ple_norm, ple_w_gate, ple_w_proj, final_norm):
    depth = w_in.shape[0]
    nb, seq, _ = x_prompt.shape
    db, dseq, _ = x_sample.shape
    weights = (ffn1_norm, ffn1_w_gate, ffn1_w_up, ffn1_w_down, mix_norm, w_in, rwkv_mu, rwkv_w0, rwkv_w2, rwkv_a0,
               rwkv_a2, rwkv_g2, rwkv_k_k, rwkv_k_a, rwkv_r_k, rwkv_ln_w, rwkv_ln_b, gla_gate_w2, gla_gate_b,
               gla_norm, mamba_conv_w, mamba_conv_b, mamba_dt_bias, mamba_A_log, mamba_D, mamba_norm, w_out,
               ffn2_norm, ffn2_w_gate, ffn2_w_up, ffn2_w_down, ple_norm, ple_w_gate, ple_w_proj)
    xp = x_prompt.reshape(nb * seq, D_MODEL)
    xs = x_sample.reshape(db * dseq, D_MODEL)
    p_states, s_states = [], []
    for i in range(depth):
        lp = _layer_params(i, *weights)
        final = i == depth - 1
        fresh = (jnp.zeros((nb, RWKV_PROJ), F32), jnp.zeros((nb, RWKV_HEADS, HEAD_DIM, HEAD_DIM), F32),
                 jnp.zeros((nb, GLA_HEADS, GLA_DK, HEAD_DIM), F32),
                 jnp.zeros((nb, M2_CONV - 1, M2_CONV_DIM), F32),
                 jnp.zeros((nb, M2_HEADS, HEAD_DIM, M2_STATE), F32))
        xp, st_p = _group_layer(xp, p_prompt[i].reshape(nb * seq, PLE_DIM), fresh, lp, final_norm,
                                batch=nb, seq=seq, chunks=(RWKV_CHUNK, GLA_CHUNK, SSD_CHUNK), bb=1, tm=256,
                                y_dtype=BF16, final=final)
        past = (state_rwkv_shift[i], state_rwkv_wkv[i], state_gla[i], state_mamba_conv[i], state_mamba_ssm[i])
        xs, st_s = _group_layer(xs, p_sample[i].reshape(db * dseq, PLE_DIM), past, lp, final_norm,
                                batch=db, seq=dseq, chunks=(SUBLANE, SUBLANE, SUBLANE), bb=8, tm=256,
                                y_dtype=F32, final=final)
        p_states.append(st_p)
        s_states.append(st_s)
    stack = lambda sts, j: jnp.stack([s[j] for s in sts])
    return ((xp.reshape(nb, seq, D_MODEL), xs.reshape(db, dseq, D_MODEL))
            + tuple(stack(p_states, j) for j in range(5)) + tuple(stack(s_states, j) for j in range(5)))
```

```python
import functools

import jax
import jax.numpy as jnp
from jax import lax
from jax.experimental import pallas as pl
from jax.experimental.pallas import tpu as pltpu

F32 = jnp.float32
BF16 = jnp.bfloat16

D_MODEL = 1024
D_FF = 2816
PLE_DIM = 256
HEAD_DIM = 64
NORM_EPS = 1e-6

RWKV_W = 256
RWKV_HEADS = 4
RWKV_GN_EPS = 64e-5
RWKV_PROJ = 896
RWKV_LORA = 128
RWKV_DECAY_LORA = 32
RWKV_A_LORA = 32

GLA_W = 256
GLA_HEADS = 4
GLA_DK = 32
GLA_KEY_W = 128
GLA_GATE_LORA = 16
GLA_TAU = 16.0
GLA_PROJ = 784
GLA_PAD = 896

M2_W = 512
M2_HEADS = 8
M2_STATE = 64
M2_GROUPS = 2
M2_CONV = 4
M2_CONV_DIM = 768
M2_PROJ = 1288
M2_PAD = 1408

LANE = 128
SUBLANE = 8
FF_CHUNK = 256
VMEM_LIMIT = 56 * 1024 * 1024

RWKV_CHUNK = 64
GLA_CHUNK = 64
SSD_CHUNK = 128


def _bdot(a, b):
    return jnp.dot(a.astype(BF16), b.astype(BF16), preferred_element_type=F32)


def _bdot_nt(a, b):
    return lax.dot_general(a.astype(BF16), b.astype(BF16), (((1,), (1,)), ((), ())),
                           preferred_element_type=F32)


def _bdot_tn(a, b):
    return lax.dot_general(a.astype(BF16), b.astype(BF16), (((0,), (0,)), ((), ())),
                           preferred_element_type=F32)


def _split3(x):
    hi = x.astype(BF16)
    r1 = x - hi.astype(F32)
    mid = r1.astype(BF16)
    lo = (r1 - mid.astype(F32)).astype(BF16)
    return hi, mid, lo


def _ldot3(mat, x):
    return sum(jnp.dot(mat, p, preferred_element_type=F32) for p in _split3(x))


def _rdot3(x, mat):
    return sum(jnp.dot(p, mat, preferred_element_type=F32) for p in _split3(x))


def _nt3(mat, x):
    return sum(lax.dot_general(mat, p, (((1,), (1,)), ((), ())), preferred_element_type=F32)
               for p in _split3(x))


def _softplus(x):
    return jnp.maximum(x, 0.0) + jnp.log1p(jnp.exp(-jnp.abs(x)))


def _sigmoid(x):
    return 1.0 / (1.0 + jnp.exp(-x))


def _silu(x):
    return x * _sigmoid(x)


def _rms(x, w):
    return x * lax.rsqrt(jnp.mean(x * x, axis=-1, keepdims=True) + NORM_EPS) * w


def _const_spec(shape):
    nd = len(shape)
    return pl.BlockSpec(shape, lambda *_: (0,) * nd, pipeline_mode=pl.Buffered(1))


def _ffn(x, nw, wg_ref, wu_ref, wd_ref):
    xn = _rms(x, nw).astype(BF16)
    acc = jnp.zeros_like(x)
    for c in range(D_FF // FF_CHUNK):
        sl = slice(c * FF_CHUNK, (c + 1) * FF_CHUNK)
        hg = jnp.dot(xn, wg_ref[:, sl], preferred_element_type=F32)
        hu = jnp.dot(xn, wu_ref[:, sl], preferred_element_type=F32)
        act = (_silu(hg) * hu).astype(BF16)
        acc = acc + jnp.dot(act, wd_ref[sl, :], preferred_element_type=F32)
    return x + 0.5 * acc


def _pre_kernel(x_ref, n1_ref, wg_ref, wu_ref, wd_ref, nm_ref, win_ref,
                x1_ref, zr_ref, zg_ref, zm_ref):
    x1 = _ffn(x_ref[...], n1_ref[...], wg_ref, wu_ref, wd_ref)
    x1_ref[...] = x1
    h = _rms(x1, nm_ref[...]).astype(BF16)
    zr_ref[...] = jnp.dot(h, win_ref[:, 0:RWKV_PROJ], preferred_element_type=F32)
    zg_ref[...] = jnp.dot(h, win_ref[:, RWKV_PROJ:RWKV_PROJ + GLA_PAD], preferred_element_type=F32)
    zm_ref[...] = jnp.dot(h, win_ref[:, RWKV_PROJ + GLA_PAD:], preferred_element_type=F32)


def _pre_call(x, n1, wg, wu, wd, nm, win, tm):
    n = x.shape[0]
    tok = lambda w: pl.BlockSpec((tm, w), lambda i: (i, 0))
    return pl.pallas_call(
        _pre_kernel,
        grid=(n // tm,),
        in_specs=[tok(D_MODEL), _const_spec(n1.shape), _const_spec(wg.shape), _const_spec(wu.shape),
                  _const_spec(wd.shape), _const_spec(nm.shape), _const_spec(win.shape)],
        out_specs=[tok(D_MODEL), tok(RWKV_PROJ), tok(GLA_PAD), tok(M2_PAD)],
        out_shape=[jax.ShapeDtypeStruct((n, D_MODEL), F32), jax.ShapeDtypeStruct((n, RWKV_PROJ), F32),
                   jax.ShapeDtypeStruct((n, GLA_PAD), F32), jax.ShapeDtypeStruct((n, M2_PAD), F32)],
        compiler_params=pltpu.CompilerParams(dimension_semantics=("parallel",), vmem_limit_bytes=VMEM_LIMIT),
        name="pre_ffn_inproj",
    )(x, n1, wg, wu, wd, nm, win)


def _post_kernel(x_ref, yr_ref, yg_ref, ym_ref, p_ref, wo_ref, n2_ref, wg_ref, wu_ref, wd_ref,
                 np_ref, pg_ref, pp_ref, nf_ref, o_ref, *, final):
    x = x_ref[...]
    x = x + jnp.dot(yr_ref[...].astype(BF16), wo_ref[0:RWKV_W, :], preferred_element_type=F32)
    x = x + jnp.dot(yg_ref[...].astype(BF16), wo_ref[RWKV_W:RWKV_W + GLA_W, :], preferred_element_type=F32)
    x = x + jnp.dot(ym_ref[...].astype(BF16), wo_ref[RWKV_W + GLA_W:, :], preferred_element_type=F32)
    x = _ffn(x, n2_ref[...], wg_ref, wu_ref, wd_ref)
    gate = _sigmoid(jnp.dot(_rms(x, np_ref[...]).astype(BF16), pg_ref[...], preferred_element_type=F32))
    x = x + gate * jnp.dot(p_ref[...].astype(BF16), pp_ref[...], preferred_element_type=F32)
    if final:
        x = _rms(x, nf_ref[...])
    o_ref[...] = x


def _post_call(x, yr, yg, ym, p, wo, n2, wg, wu, wd, npn, pg, pp, nf, tm, final):
    n = x.shape[0]
    tok = lambda w: pl.BlockSpec((tm, w), lambda i: (i, 0))
    consts = (wo, n2, wg, wu, wd, npn, pg, pp, nf)
    return pl.pallas_call(
        functools.partial(_post_kernel, final=final),
        grid=(n // tm,),
        in_specs=[tok(D_MODEL), tok(RWKV_W), tok(GLA_W), tok(M2_W), tok(PLE_DIM)]
                 + [_const_spec(a.shape) for a in consts],
        out_specs=tok(D_MODEL),
        out_shape=jax.ShapeDtypeStruct((n, D_MODEL), F32),
        compiler_params=pltpu.CompilerParams(dimension_semantics=("parallel",), vmem_limit_bytes=VMEM_LIMIT),
        name="post_outproj_ffn_ple",
    )(x, yr, yg, ym, p, *consts)


def _tri_masks(n):
    ri = lax.broadcasted_iota(jnp.int32, (n, n), 0)
    ci = lax.broadcasted_iota(jnp.int32, (n, n), 1)
    return ri >= ci, ri > ci, (ri == ci).astype(F32)


def _rwkv_kernel(z_ref, sh_ref, s0_ref, mu_ref, w0_ref, w2_ref, a0_ref, a2_ref, g2_ref, kk_ref, ka_ref,
                 rk_ref, lnw_ref, lnb_ref, tri_ref, ones_ref,
                 y_ref, s_ref, zbuf_ref, ybuf_ref, *, bb, chunk, valid):
    c = pl.program_id(1)

    @pl.when(c == 0)
    def _():
        s_ref[...] = s0_ref[...]
        zbuf_ref[:, SUBLANE - 1:SUBLANE, :] = sh_ref[...]

    incl, strict, eye = _tri_masks(chunk)
    tri = tri_ref[...]
    ones = ones_ref[...]
    row = lax.broadcasted_iota(jnp.int32, (chunk, 1), 0)
    mu = mu_ref[...]
    seqs = range(bb)
    zs = []
    for i in seqs:
        z = z_ref[i]
        zbuf_ref[i, SUBLANE:SUBLANE + chunk, :] = z
        prev = zbuf_ref[i, SUBLANE - 1:SUBLANE - 1 + chunk, :]
        zbuf_ref[i, SUBLANE - 1:SUBLANE, :] = z[chunk - 1:chunk, :]
        zs.append(z + mu * (prev - z))
    r = [x[:, 0:RWKV_W] for x in zs]
    k = [x[:, RWKV_W:2 * RWKV_W] for x in zs]
    v = [x[:, 2 * RWKV_W:3 * RWKV_W] for x in zs]
    lora = [x[:, 3 * RWKV_W:] for x in zs]
    w_lin = [_bdot(jnp.tanh(x), w2_ref[...]) for x in lora]
    a_lin = [_bdot(x, a2_ref[...]) for x in lora]
    gate = [_bdot(_sigmoid(x), g2_ref[...]) for x in lora]
    kk = [x * kk_ref[...] for x in k]
    kk_ss = [_rdot3(x * x, ones) for x in kk]
    logw = [-jnp.exp(-_softplus(-(w0_ref[...] + x)) - 0.5) for x in w_lin]
    a = [_sigmoid(a0_ref[...] + x) for x in a_lin]
    kk = [x / jnp.maximum(jnp.sqrt(ss), 1e-12) for x, ss in zip(kk, kk_ss)]
    k = [x * (1.0 + (ai - 1.0) * ka_ref[...]) for x, ai in zip(k, a)]
    ahat = [-x for x in kk]
    bhat = [x * ai for x, ai in zip(kk, a)]
    if valid < chunk:
        vm = row < valid
        mask = lambda xs: [jnp.where(vm, x, 0.0) for x in xs]
        logw, ahat, bhat, k, v = mask(logw), mask(ahat), mask(bhat), mask(k), mask(v)
    g = [_ldot3(tri, x) for x in logw]
    glast = [x[chunk - 1:chunk, :] for x in g]
    e_ng = [jnp.exp(-x) for x in g]
    e_gl = [jnp.exp(gl - x) for gl, x in zip(glast, g)]
    a_l = [ah * jnp.exp(x - lw) for ah, x, lw in zip(ahat, g, logw)]
    r_l = [ri * jnp.exp(x) for ri, x in zip(r, g)]
    b_r = [x * e for x, e in zip(bhat, e_ng)]
    k_r = [x * e for x, e in zip(k, e_ng)]
    k_p = [x * e for x, e in zip(k, e_gl)]
    b_p = [x * e for x, e in zip(bhat, e_gl)]
    dec = [jnp.exp(x) for x in glast]

    probs = [(i, h) for i in seqs for h in range(RWKV_HEADS)]
    hd = lambda xs: [xs[i][:, h * HEAD_DIM:(h + 1) * HEAD_DIM] for i, h in probs]
    a_lh, r_lh, b_rh, k_rh, k_ph, b_ph, vh = hd(a_l), hd(r_l), hd(b_r), hd(k_r), hd(k_p), hd(b_p), hd(v)
    ar = [jnp.concatenate([x, y], axis=0) for x, y in zip(a_lh, r_lh)]
    gb = [_bdot_nt(x, y) for x, y in zip(ar, b_rh)]
    gk = [_bdot_nt(x, y) for x, y in zip(ar, k_rh)]
    a_ab = [jnp.where(strict, x[:chunk], 0.0) for x in gb]
    a_rb = [jnp.where(incl, x[chunk:], 0.0) for x in gb]
    a_ak = [jnp.where(strict, x[:chunk], 0.0) for x in gk]
    a_rk = [jnp.where(incl, x[chunk:], 0.0) for x in gk]
    t = [eye + m for m in a_ab]
    x = [_bdot(m, m) for m in a_ab]
    p = 2
    while p < chunk:
        if 2 * p < chunk:
            xt = [_bdot(jnp.concatenate([xi, ti], axis=0), xi) for xi, ti in zip(x, t)]
            x = [m[:chunk] for m in xt]
            t = [ti + m[chunk:] for ti, m in zip(t, xt)]
        else:
            t = [ti + _bdot(ti, xi) for xi, ti in zip(x, t)]
        p *= 2
    akv = [_bdot(m, vi) for m, vi in zip(a_ak, vh)]
    w_t = [_bdot(ti, m) for ti, m in zip(t, a_lh)]
    u_t = [_bdot(ti, m) for ti, m in zip(t, akv)]
    r_t = [ri + _bdot(m, wi) for ri, m, wi in zip(r_lh, a_rb, w_t)]
    y_t = [_bdot(m, vi) + _bdot(n, ui) for m, vi, n, ui in zip(a_rk, vh, a_rb, u_t)]
    m_s = [_bdot_tn(wi, bi) for wi, bi in zip(w_t, b_ph)]
    q_s = [_bdot_tn(jnp.concatenate([vi, ui], axis=0), jnp.concatenate([ki, bi], axis=0))
           for vi, ui, ki, bi in zip(vh, u_t, k_ph, b_ph)]
    s_old = [s_ref[i, h] for i, h in probs]
    y_h = [_bdot_nt(ri, si) + yi for ri, si, yi in zip(r_t, s_old, y_t)]
    s_m = [_bdot(si, mi) for si, mi in zip(s_old, m_s)]
    for j, (i, h) in enumerate(probs):
        hs = slice(h * HEAD_DIM, (h + 1) * HEAD_DIM)
        ybuf_ref[i, :, hs] = y_h[j]
        s_ref[i, h] = s_old[j] * dec[i][:, hs] + s_m[j] + q_s[j]
    y = [ybuf_ref[i] for i in seqs]
    mean = [_rdot3(x, ones) * (1.0 / HEAD_DIM) for x in y]
    yc = [x - m for x, m in zip(y, mean)]
    var = [_rdot3(x * x, ones) * (1.0 / HEAD_DIM) for x in yc]
    bonus = [_rdot3(ri * ki * rk_ref[...], ones) * vi for ri, ki, vi in zip(r, k, v)]
    for i in seqs:
        yn = yc[i] * lax.rsqrt(var[i] + RWKV_GN_EPS) * lnw_ref[...] + lnb_ref[...]
        y_ref[i] = ((yn + bonus[i]) * gate[i]).astype(y_ref.dtype)


def _rwkv_call(z, shift, s0, params, bb, chunk, valid, out_dtype):
    b, l, _ = z.shape
    grid = (b // bb, l // chunk)
    tri = jnp.tril(jnp.ones((chunk, chunk), F32)).astype(BF16)
    consts = tuple(params) + (tri, _HEAD_ONES())
    return pl.pallas_call(
        functools.partial(_rwkv_kernel, bb=bb, chunk=chunk, valid=valid),
        grid=grid,
        in_specs=[pl.BlockSpec((bb, chunk, RWKV_PROJ), lambda i, c: (i, c, 0)),
                  pl.BlockSpec((bb, 1, RWKV_PROJ), lambda i, c: (i, 0, 0)),
                  pl.BlockSpec((bb, RWKV_HEADS, HEAD_DIM, HEAD_DIM), lambda i, c: (i, 0, 0, 0))]
                 + [_const_spec(a.shape) for a in consts],
        out_specs=[pl.BlockSpec((bb, chunk, RWKV_W), lambda i, c: (i, c, 0)),
                   pl.BlockSpec((bb, RWKV_HEADS, HEAD_DIM, HEAD_DIM), lambda i, c: (i, 0, 0, 0))],
        out_shape=[jax.ShapeDtypeStruct((b, l, RWKV_W), out_dtype),
                   jax.ShapeDtypeStruct((b, RWKV_HEADS, HEAD_DIM, HEAD_DIM), F32)],
        scratch_shapes=[pltpu.VMEM((bb, SUBLANE + chunk, RWKV_PROJ), F32),
                        pltpu.VMEM((bb, chunk, RWKV_W), F32)],
        compiler_params=pltpu.CompilerParams(dimension_semantics=("arbitrary", "arbitrary"),
                                             vmem_limit_bytes=VMEM_LIMIT),
        name="rwkv7_mixer",
    )(z, shift, s0, *consts)


def _gla_kernel(z_ref, s0_ref, gw_ref, gb_ref, nw_ref, tri_ref, ones_ref, eye_ref,
                y_ref, s_ref, obuf_ref, *, bb, chunk, valid):
    c = pl.program_id(1)

    @pl.when(c == 0)
    def _():
        s_ref[...] = s0_ref[...]

    incl, _, _ = _tri_masks(chunk)
    tri = tri_ref[...]
    ones = ones_ref[...]
    row = lax.broadcasted_iota(jnp.int32, (chunk, 1), 0)
    seqs = range(bb)
    z = [z_ref[i] for i in seqs]
    q = [x[:, 0:GLA_KEY_W] * (GLA_DK ** -0.5) for x in z]
    k = [x[:, GLA_KEY_W:2 * GLA_KEY_W] for x in z]
    v = [x[:, 2 * GLA_KEY_W:2 * GLA_KEY_W + GLA_W] for x in z]
    og = [x[:, 2 * GLA_KEY_W + GLA_W:2 * GLA_KEY_W + 2 * GLA_W] for x in z]
    lin = [_bdot(x[:, 2 * GLA_KEY_W + 2 * GLA_W:], gw_ref[...]) for x in z]
    log_a = [-_softplus(-(x + gb_ref[...])) * (1.0 / GLA_TAU) for x in lin]
    if valid < chunk:
        vm = row < valid
        log_a = [jnp.where(vm, x, 0.0) for x in log_a]
        k = [jnp.where(vm, x, 0.0) for x in k]
    b = [_ldot3(tri, x) for x in log_a]
    blast = [x[chunk - 1:chunk, :] for x in b]
    qe = [x * jnp.exp(bi) for x, bi in zip(q, b)]
    ke = [x * jnp.exp(-bi) for x, bi in zip(k, b)]
    kl = [x * jnp.exp(bl - bi) for x, bl, bi in zip(k, blast, b)]
    dec = [_nt3(eye_ref[...], jnp.broadcast_to(jnp.exp(bl), (HEAD_DIM, GLA_KEY_W))) for bl in blast]

    probs = [(i, h) for i in seqs for h in range(GLA_HEADS)]
    kd = lambda xs: [xs[i][:, h * GLA_DK:(h + 1) * GLA_DK] for i, h in probs]
    qeh, keh, klh = kd(qe), kd(ke), kd(kl)
    vh = [v[i][:, h * HEAD_DIM:(h + 1) * HEAD_DIM] for i, h in probs]
    att = [jnp.where(incl, _bdot_nt(x, y), 0.0) for x, y in zip(qeh, keh)]
    s_old = [s_ref[i, h] for i, h in probs]
    o_h = [_bdot(ai, vi) + _bdot(qi, si) for ai, vi, qi, si in zip(att, vh, qeh, s_old)]
    kv = [_bdot_tn(ki, vi) for ki, vi in zip(klh, vh)]
    for j, (i, h) in enumerate(probs):
        obuf_ref[i, :, h * HEAD_DIM:(h + 1) * HEAD_DIM] = o_h[j]
        s_ref[i, h] = s_old[j] * dec[i][h * GLA_DK:(h + 1) * GLA_DK, :] + kv[j]
    o = [obuf_ref[i] for i in seqs]
    ms = [_rdot3(x * x, ones) * (1.0 / HEAD_DIM) for x in o]
    for i in seqs:
        y_ref[i] = (o[i] * lax.rsqrt(ms[i] + NORM_EPS) * nw_ref[...] * _silu(og[i])).astype(y_ref.dtype)


def _gla_call(z, s0, params, bb, chunk, valid, out_dtype):
    b, l, _ = z.shape
    tri = jnp.tril(jnp.ones((chunk, chunk), F32)).astype(BF16)
    consts = tuple(params) + (tri, _HEAD_ONES(), jnp.eye(LANE, dtype=BF16))
    return pl.pallas_call(
        functools.partial(_gla_kernel, bb=bb, chunk=chunk, valid=valid),
        grid=(b // bb, l // chunk),
        in_specs=[pl.BlockSpec((bb, chunk, GLA_PAD), lambda i, c: (i, c, 0)),
                  pl.BlockSpec((bb, GLA_HEADS, GLA_DK, HEAD_DIM), lambda i, c: (i, 0, 0, 0))]
                 + [_const_spec(a.shape) for a in consts],
        out_specs=[pl.BlockSpec((bb, chunk, GLA_W), lambda i, c: (i, c, 0)),
                   pl.BlockSpec((bb, GLA_HEADS, GLA_DK, HEAD_DIM), lambda i, c: (i, 0, 0, 0))],
        out_shape=[jax.ShapeDtypeStruct((b, l, GLA_W), out_dtype),
                   jax.ShapeDtypeStruct((b, GLA_HEADS, GLA_DK, HEAD_DIM), F32)],
        scratch_shapes=[pltpu.VMEM((bb, chunk, GLA_W), F32)],
        compiler_params=pltpu.CompilerParams(dimension_semantics=("arbitrary", "arbitrary"),
                                             vmem_limit_bytes=VMEM_LIMIT),
        name="gla_mixer",
    )(z, s0, *consts)


def _ssd_kernel(z_ref, cp_ref, s0_ref, cw_ref, cb_ref, dtb_ref, alog_ref, dsk_ref, nw_ref, tri_ref, eye_ref,
                y_ref, s_ref, xbuf_ref, ybuf_ref, *, bb, chunk, valid):
    c = pl.program_id(1)

    @pl.when(c == 0)
    def _():
        s_ref[...] = s0_ref[...]
        xbuf_ref[:, 0:SUBLANE, :] = cp_ref[...]

    incl, _, _ = _tri_masks(chunk)
    tri = tri_ref[...]
    row = lax.broadcasted_iota(jnp.int32, (chunk, 1), 0)
    lane = lax.broadcasted_iota(jnp.int32, (1, LANE), 1)
    rep = M2_HEADS // M2_GROUPS
    seqs = range(bb)
    zg, xc, dt = [], [], []
    live = lane < M2_HEADS
    if valid < chunk:
        live = jnp.logical_and(live, row < valid)
    for i in seqs:
        z = z_ref[i]
        x = z[:, M2_W:M2_W + M2_CONV_DIM]
        xbuf_ref[i, SUBLANE:SUBLANE + chunk, :] = x
        conv = cb_ref[...] + x * cw_ref[M2_CONV - 1:M2_CONV, :]
        for j in range(1, M2_CONV):
            conv = conv + xbuf_ref[i, SUBLANE - j:SUBLANE - j + chunk, :] * cw_ref[M2_CONV - 1 - j:M2_CONV - j, :]
        xbuf_ref[i, 0:SUBLANE, :] = xbuf_ref[i, chunk:chunk + SUBLANE, :]
        zg.append(z[:, 0:M2_W])
        xc.append(_silu(conv))
        dt.append(jnp.where(live, _softplus(z[:, M2_W + M2_CONV_DIM:] + dtb_ref[...]), 0.0))
    a_neg = -jnp.exp(alog_ref[...])
    cum = [_ldot3(tri, x * a_neg) for x in dt]
    cum_t = [_nt3(eye_ref[0:SUBLANE, :], x) for x in cum]
    grp = [(i, g) for i in seqs for g in range(M2_GROUPS)]
    bg = [xc[i][:, M2_W + g * M2_STATE:M2_W + (g + 1) * M2_STATE] for i, g in grp]
    cg = [xc[i][:, M2_W + (M2_GROUPS + g) * M2_STATE:M2_W + (M2_GROUPS + g + 1) * M2_STATE] for i, g in grp]
    cb = [_bdot_nt(x, y) for x, y in zip(cg, bg)]
    probs = [(i, h) for i in seqs for h in range(M2_HEADS)]
    gi = lambda i, h: i * M2_GROUPS + h // rep
    cc = [cum[i][:, h:h + 1] for i, h in probs]
    lh = [cum[i][chunk - 1:chunk, h:h + 1] for i, h in probs]
    dth = [dt[i][:, h:h + 1] for i, h in probs]
    xh = [xc[i][:, h * HEAD_DIM:(h + 1) * HEAD_DIM] for i, h in probs]
    seg = [jnp.where(incl, jnp.exp(jnp.minimum(cc[j] - cum_t[i][h:h + 1, :], 0.0)), 0.0)
           for j, (i, h) in enumerate(probs)]
    s_old = [s_ref[i, h] for i, h in probs]
    y_in = [_bdot(cb[gi(i, h)] * seg[j], xh[j] * dth[j]) for j, (i, h) in enumerate(probs)]
    y_st = [_bdot_nt(cg[gi(i, h)], s_old[j]) for j, (i, h) in enumerate(probs)]
    s_in = [_bdot_tn(xh[j], bg[gi(i, h)] * (jnp.exp(lh[j] - cc[j]) * dth[j])) for j, (i, h) in enumerate(probs)]
    for j, (i, h) in enumerate(probs):
        hs = slice(h * HEAD_DIM, (h + 1) * HEAD_DIM)
        ybuf_ref[i, :, hs] = y_in[j] + y_st[j] * jnp.exp(cc[j]) + dsk_ref[:, h:h + 1] * xh[j]
        s_ref[i, h] = s_old[j] * jnp.exp(lh[j]) + s_in[j]
    for i in seqs:
        y_ref[i] = _rms(ybuf_ref[i] * _silu(zg[i]), nw_ref[...]).astype(y_ref.dtype)


def _ssd_call(z, conv_prev, s0, params, bb, chunk, valid, out_dtype):
    b, l, _ = z.shape
    tri = jnp.tril(jnp.ones((chunk, chunk), F32)).astype(BF16)
    consts = tuple(params) + (tri, jnp.eye(LANE, dtype=BF16))
    return pl.pallas_call(
        functools.partial(_ssd_kernel, bb=bb, chunk=chunk, valid=valid),
        grid=(b // bb, l // chunk),
        in_specs=[pl.BlockSpec((bb, chunk, M2_PAD), lambda i, c: (i, c, 0)),
                  pl.BlockSpec((bb, SUBLANE, M2_CONV_DIM), lambda i, c: (i, 0, 0)),
                  pl.BlockSpec((bb, M2_HEADS, HEAD_DIM, M2_STATE), lambda i, c: (i, 0, 0, 0))]
                 + [_const_spec(a.shape) for a in consts],
        out_specs=[pl.BlockSpec((bb, chunk, M2_W), lambda i, c: (i, c, 0)),
                   pl.BlockSpec((bb, M2_HEADS, HEAD_DIM, M2_STATE), lambda i, c: (i, 0, 0, 0))],
        out_shape=[jax.ShapeDtypeStruct((b, l, M2_W), out_dtype),
                   jax.ShapeDtypeStruct((b, M2_HEADS, HEAD_DIM, M2_STATE), F32)],
        scratch_shapes=[pltpu.VMEM((bb, SUBLANE + chunk, M2_CONV_DIM), F32),
                        pltpu.VMEM((bb, chunk, M2_W), F32)],
        compiler_params=pltpu.CompilerParams(dimension_semantics=("arbitrary", "arbitrary"),
                                             vmem_limit_bytes=VMEM_LIMIT),
        name="ssd_mixer",
    )(z, conv_prev, s0, *consts)


def _HEAD_ONES():
    idx = jnp.arange(RWKV_W) // HEAD_DIM
    return (idx[:, None] == idx[None, :]).astype(BF16)


def _row(v):
    return v.reshape(1, -1).astype(F32)


def _pad_rows(m, top, total):
    return jnp.zeros((total, m.shape[1]), F32).at[top:top + m.shape[0]].set(m)


def _pad_lanes(v, total=LANE):
    return jnp.zeros((1, total), F32).at[0, :v.shape[0]].set(v)


def _layer_params(i, ffn1_norm, ffn1_w_gate, ffn1_w_up, ffn1_w_down, mix_norm, w_in, rwkv_mu, rwkv_w0, rwkv_w2,
                  rwkv_a0, rwkv_a2, rwkv_g2, rwkv_k_k, rwkv_k_a, rwkv_r_k, rwkv_ln_w, rwkv_ln_b, gla_gate_w2,
                  gla_gate_b, gla_norm, mamba_conv_w, mamba_conv_b, mamba_dt_bias, mamba_A_log, mamba_D, mamba_norm,
                  w_out, ffn2_norm, ffn2_w_gate, ffn2_w_up, ffn2_w_down, ple_norm, ple_w_gate, ple_w_proj):
    w = w_in[i]
    g0 = RWKV_PROJ
    m0 = RWKV_PROJ + GLA_PROJ
    qkv_end = 2 * GLA_KEY_W + GLA_W
    gla_cols = jnp.concatenate([
        w[:, g0:g0 + qkv_end], w[:, g0 + qkv_end + GLA_GATE_LORA:m0],
        w[:, g0 + qkv_end:g0 + qkv_end + GLA_GATE_LORA],
        jnp.zeros((D_MODEL, LANE - GLA_GATE_LORA), F32)], axis=1)
    m2_cols = jnp.concatenate([w[:, m0:], jnp.zeros((D_MODEL, LANE - M2_HEADS), F32)], axis=1)
    win = jnp.concatenate([w[:, :g0], gla_cols, m2_cols], axis=1).astype(BF16)
    pre = (_row(ffn1_norm[i]), ffn1_w_gate[i].astype(BF16), ffn1_w_up[i].astype(BF16),
           ffn1_w_down[i].astype(BF16), _row(mix_norm[i]), win)
    rwkv = (_row(rwkv_mu[i]), _row(rwkv_w0[i]),
            _pad_rows(rwkv_w2[i], 0, RWKV_LORA).astype(BF16), _row(rwkv_a0[i]),
            _pad_rows(rwkv_a2[i], RWKV_DECAY_LORA, RWKV_LORA).astype(BF16),
            _pad_rows(rwkv_g2[i], RWKV_DECAY_LORA + RWKV_A_LORA, RWKV_LORA).astype(BF16),
            _row(rwkv_k_k[i]), _row(rwkv_k_a[i]), _row(rwkv_r_k[i]), _row(rwkv_ln_w[i]), _row(rwkv_ln_b[i]))
    gla = (_pad_rows(gla_gate_w2[i], 0, LANE).astype(BF16), _row(gla_gate_b[i]),
           _row(jnp.tile(gla_norm[i], GLA_HEADS)))
    ssd = (mamba_conv_w[i].astype(F32), _row(mamba_conv_b[i]), _pad_lanes(mamba_dt_bias[i]),
           _pad_lanes(mamba_A_log[i]), _pad_lanes(mamba_D[i]), _row(mamba_norm[i]))
    post = (w_out[i].astype(BF16), _row(ffn2_norm[i]), ffn2_w_gate[i].astype(BF16), ffn2_w_up[i].astype(BF16),
            ffn2_w_down[i].astype(BF16), _row(ple_norm[i]), ple_w_gate[i].astype(BF16),
            ple_w_proj[i].astype(BF16))
    return pre, rwkv, gla, ssd, post


def _pad_time(a, total):
    return jnp.pad(a, ((0, 0), (0, total - a.shape[1]), (0, 0)))


def _group_layer(x, p, state, lp, final_norm, *, batch, seq, chunks, bb, tm, y_dtype, final):
    pre, rwkv, gla, ssd, post = lp
    shift0, wkv0, gla0, conv0, ssm0 = state
    x1, zr, zg, zm = _pre_call(x, *pre, tm)
    zr = zr.reshape(batch, seq, RWKV_PROJ)
    zg = zg.reshape(batch, seq, GLA_PAD)
    zm = zm.reshape(batch, seq, M2_PAD)
    xbc = zm[:, :, M2_W:M2_W + M2_CONV_DIM]
    shift1 = zr[:, -1]
    conv1 = jnp.concatenate([conv0, xbc[:, -(M2_CONV - 1):]], axis=1)[:, -(M2_CONV - 1):]
    conv_hist = jnp.pad(conv0, ((0, 0), (SUBLANE - (M2_CONV - 1), 0), (0, 0)))
    c_r, c_g, c_m = chunks
    pad = max(c_r, c_g, c_m, seq)
    if pad != seq:
        zr, zg, zm = _pad_time(zr, pad), _pad_time(zg, pad), _pad_time(zm, pad)
    valid = lambda ch: seq if seq < ch else ch
    yr, wkv1 = _rwkv_call(zr, shift0[:, None, :], wkv0, rwkv, bb, c_r, valid(c_r), y_dtype)
    yg, gla1 = _gla_call(zg, gla0, gla, bb, c_g, valid(c_g), y_dtype)
    ym, ssm1 = _ssd_call(zm, conv_hist, ssm0, ssd, bb, c_m, valid(c_m), y_dtype)
    n = batch * seq
    flat = lambda y: y[:, :seq].reshape(n, y.shape[-1])
    x2 = _post_call(x1, flat(yr), flat(yg), flat(ym), p, *post, _row(final_norm), tm, final)
    return x2, (shift1, wkv1, gla1, conv1, ssm1)


def kernel(x_prompt, x_sample, p_prompt, p_sample, state_rwkv_shift, state_rwkv_wkv, state_gla, state_mamba_conv, state_mamba_ssm, ffn1_norm, ffn1_w_gate, ffn1_w_up, ffn1_w_down, mix_norm, w_in, rwkv_mu, rwkv_w0, rwkv_w2, rwkv_a0, rwkv_a2, rwkv_g2, rwkv_k_k, rwkv_k_a, rwkv_r_k, rwkv_ln_w, rwkv_ln_b, gla_gate_w2, gla_gate_b, gla_norm, mamba_conv_w, mamba_conv_b, mamba_dt_bias, mamba_A_log, mamba_D, mamba_norm, w_out, ffn2_norm, ffn2_w_gate, ffn2_w_up, ffn2_w_down, ple_norm, ple_w_gate, ple_w_proj, final_norm):
    depth = w_in.shape[0]
    nb, seq, _ = x_prompt.shape
    db, dseq, _ = x_sample.shape
    weights = (ffn1_norm, ffn1_w_gate, ffn1_w_up, ffn1_w_down, mix_norm, w_in, rwkv_mu, rwkv_w0, rwkv_w2, rwkv_a0,
               rwkv_a2, rwkv_g2, rwkv_k_k, rwkv_k_a, rwkv_r_k, rwkv_ln_w, rwkv_ln_b, gla_gate_w2, gla_gate_b,
               gla_norm, mamba_conv_w, mamba_conv_b, mamba_dt_bias, mamba_A_log, mamba_D, mamba_norm, w_out,
               ffn2_norm, ffn2_w_gate, ffn2_w_up, ffn2_w_down, ple_norm, ple_w_gate, ple_w_proj)
    xp = x_prompt.reshape(nb * seq, D_MODEL)
    xs = x_sample.reshape(db * dseq, D_MODEL)
    p_states, s_states = [], []
    for i in range(depth):
        lp = _layer_params(i, *weights)
        final = i == depth - 1
        fresh = (jnp.zeros((nb, RWKV_PROJ), F32), jnp.zeros((nb, RWKV_HEADS, HEAD_DIM, HEAD_DIM), F32),
                 jnp.zeros((nb, GLA_HEADS, GLA_DK, HEAD_DIM), F32),
                 jnp.zeros((nb, M2_CONV - 1, M2_CONV_DIM), F32),
                 jnp.zeros((nb, M2_HEADS, HEAD_DIM, M2_STATE), F32))
        xp, st_p = _group_layer(xp, p_prompt[i].reshape(nb * seq, PLE_DIM), fresh, lp, final_norm,
                                batch=nb, seq=seq, chunks=(RWKV_CHUNK, GLA_CHUNK, SSD_CHUNK), bb=8, tm=256,
                                y_dtype=BF16, final=final)
        past = (state_rwkv_shift[i], state_rwkv_wkv[i], state_gla[i], state_mamba_conv[i], state_mamba_ssm[i])
        xs, st_s = _group_layer(xs, p_sample[i].reshape(db * dseq, PLE_DIM), past, lp, final_norm,
                                batch=db, seq=dseq, chunks=(SUBLANE, SUBLANE, SUBLANE), bb=8, tm=256,
                                y_dtype=F32, final=final)
        p_states.append(st_p)
        s_states.append(st_s)
    stack = lambda sts, j: jnp.stack([s[j] for s in sts])
    return ((xp.reshape(nb, seq, D_MODEL), xs.reshape(db, dseq, D_MODEL))
            + tuple(stack(p_states, j) for j in range(5)) + tuple(stack(s_states, j) for j in range(5)))
```

```python
import functools

import jax
import jax.numpy as jnp
from jax import lax
from jax.experimental import pallas as pl
from jax.experimental.pallas import tpu as pltpu

F32 = jnp.float32
BF16 = jnp.bfloat16

D_MODEL = 1024
D_FF = 2816
PLE_DIM = 256
HEAD_DIM = 64
NORM_EPS = 1e-6

RWKV_W = 256
RWKV_HEADS = 4
RWKV_GN_EPS = 64e-5
RWKV_PROJ = 896
RWKV_LORA = 128
RWKV_DECAY_LORA = 32
RWKV_A_LORA = 32

GLA_W = 256
GLA_HEADS = 4
GLA_DK = 32
GLA_KEY_W = 128
GLA_GATE_LORA = 16
GLA_TAU = 16.0
GLA_PROJ = 784
GLA_PAD = 896

M2_W = 512
M2_HEADS = 8
M2_STATE = 64
M2_GROUPS = 2
M2_CONV = 4
M2_CONV_DIM = 768
M2_PROJ = 1288
M2_PAD = 1408

LANE = 128
SUBLANE = 8
FF_CHUNK = 256
VMEM_LIMIT = 56 * 1024 * 1024

RWKV_CHUNK = 64
GLA_CHUNK = 64
SSD_CHUNK = 128


def _bdot(a, b):
    return jnp.dot(a.astype(BF16), b.astype(BF16), preferred_element_type=F32)


def _bdot_nt(a, b):
    return lax.dot_general(a.astype(BF16), b.astype(BF16), (((1,), (1,)), ((), ())),
                           preferred_element_type=F32)


def _bdot_tn(a, b):
    return lax.dot_general(a.astype(BF16), b.astype(BF16), (((0,), (0,)), ((), ())),
                           preferred_element_type=F32)


def _split3(x):
    hi = x.astype(BF16)
    r1 = x - hi.astype(F32)
    mid = r1.astype(BF16)
    lo = (r1 - mid.astype(F32)).astype(BF16)
    return hi, mid, lo


def _ldot3(mat, x):
    return sum(jnp.dot(mat, p, preferred_element_type=F32) for p in _split3(x))


def _rdot3(x, mat):
    return sum(jnp.dot(p, mat, preferred_element_type=F32) for p in _split3(x))


def _rdot2(x, mat):
    return sum(jnp.dot(p, mat, preferred_element_type=F32) for p in _split3(x)[:2])


def _nt3(mat, x):
    return sum(lax.dot_general(mat, p, (((1,), (1,)), ((), ())), preferred_element_type=F32)
               for p in _split3(x))


def _softplus(x):
    return jnp.maximum(x, 0.0) + jnp.log1p(jnp.exp(-jnp.abs(x)))


def _sigmoid(x):
    return 1.0 / (1.0 + jnp.exp(-x))


def _silu(x):
    return x * _sigmoid(x)


def _rms(x, w):
    return x * lax.rsqrt(jnp.mean(x * x, axis=-1, keepdims=True) + NORM_EPS) * w


def _const_spec(shape):
    nd = len(shape)
    return pl.BlockSpec(shape, lambda *_: (0,) * nd, pipeline_mode=pl.Buffered(1))


def _ffn(x, nw, wg_ref, wu_ref, wd_ref, act_ref):
    xn = _rms(x, nw).astype(BF16)
    for c in range(D_FF // FF_CHUNK):
        sl = slice(c * FF_CHUNK, (c + 1) * FF_CHUNK)
        hg = jnp.dot(xn, wg_ref[:, sl], preferred_element_type=F32)
        hu = jnp.dot(xn, wu_ref[:, sl], preferred_element_type=F32)
        act_ref[:, sl] = (_silu(hg) * hu).astype(BF16)
    return x + 0.5 * jnp.dot(act_ref[...], wd_ref[...], preferred_element_type=F32)


def _pre_kernel(x_ref, n1_ref, wg_ref, wu_ref, wd_ref, nm_ref, win_ref,
                x1_ref, zr_ref, zg_ref, zm_ref, act_ref):
    x1 = _ffn(x_ref[...], n1_ref[...], wg_ref, wu_ref, wd_ref, act_ref)
    x1_ref[...] = x1
    h = _rms(x1, nm_ref[...]).astype(BF16)
    zr_ref[...] = jnp.dot(h, win_ref[:, 0:RWKV_PROJ], preferred_element_type=F32)
    zg_ref[...] = jnp.dot(h, win_ref[:, RWKV_PROJ:RWKV_PROJ + GLA_PAD], preferred_element_type=F32)
    zm_ref[...] = jnp.dot(h, win_ref[:, RWKV_PROJ + GLA_PAD:], preferred_element_type=F32)


def _pre_call(x, n1, wg, wu, wd, nm, win, tm):
    n = x.shape[0]
    tok = lambda w: pl.BlockSpec((tm, w), lambda i: (i, 0))
    return pl.pallas_call(
        _pre_kernel,
        grid=(n // tm,),
        in_specs=[tok(D_MODEL), _const_spec(n1.shape), _const_spec(wg.shape), _const_spec(wu.shape),
                  _const_spec(wd.shape), _const_spec(nm.shape), _const_spec(win.shape)],
        out_specs=[tok(D_MODEL), tok(RWKV_PROJ), tok(GLA_PAD), tok(M2_PAD)],
        out_shape=[jax.ShapeDtypeStruct((n, D_MODEL), F32), jax.ShapeDtypeStruct((n, RWKV_PROJ), F32),
                   jax.ShapeDtypeStruct((n, GLA_PAD), F32), jax.ShapeDtypeStruct((n, M2_PAD), F32)],
        scratch_shapes=[pltpu.VMEM((tm, D_FF), BF16)],
        compiler_params=pltpu.CompilerParams(dimension_semantics=("parallel",), vmem_limit_bytes=VMEM_LIMIT),
        name="pre_ffn_inproj",
    )(x, n1, wg, wu, wd, nm, win)


def _post_kernel(x_ref, yr_ref, yg_ref, ym_ref, p_ref, wo_ref, n2_ref, wg_ref, wu_ref, wd_ref,
                 np_ref, pg_ref, pp_ref, nf_ref, o_ref, act_ref, *, final):
    x = x_ref[...]
    x = x + jnp.dot(yr_ref[...].astype(BF16), wo_ref[0:RWKV_W, :], preferred_element_type=F32)
    x = x + jnp.dot(yg_ref[...].astype(BF16), wo_ref[RWKV_W:RWKV_W + GLA_W, :], preferred_element_type=F32)
    x = x + jnp.dot(ym_ref[...].astype(BF16), wo_ref[RWKV_W + GLA_W:, :], preferred_element_type=F32)
    x = _ffn(x, n2_ref[...], wg_ref, wu_ref, wd_ref, act_ref)
    gate = _sigmoid(jnp.dot(_rms(x, np_ref[...]).astype(BF16), pg_ref[...], preferred_element_type=F32))
    x = x + gate * jnp.dot(p_ref[...].astype(BF16), pp_ref[...], preferred_element_type=F32)
    if final:
        x = _rms(x, nf_ref[...])
    o_ref[...] = x


def _post_call(x, yr, yg, ym, p, wo, n2, wg, wu, wd, npn, pg, pp, nf, tm, final):
    n = x.shape[0]
    tok = lambda w: pl.BlockSpec((tm, w), lambda i: (i, 0))
    consts = (wo, n2, wg, wu, wd, npn, pg, pp, nf)
    return pl.pallas_call(
        functools.partial(_post_kernel, final=final),
        grid=(n // tm,),
        in_specs=[tok(D_MODEL), tok(RWKV_W), tok(GLA_W), tok(M2_W), tok(PLE_DIM)]
                 + [_const_spec(a.shape) for a in consts],
        out_specs=tok(D_MODEL),
        out_shape=jax.ShapeDtypeStruct((n, D_MODEL), F32),
        scratch_shapes=[pltpu.VMEM((tm, D_FF), BF16)],
        compiler_params=pltpu.CompilerParams(dimension_semantics=("parallel",), vmem_limit_bytes=VMEM_LIMIT),
        name="post_outproj_ffn_ple",
    )(x, yr, yg, ym, p, *consts)


def _tri_masks(n):
    ri = lax.broadcasted_iota(jnp.int32, (n, n), 0)
    ci = lax.broadcasted_iota(jnp.int32, (n, n), 1)
    return ri >= ci, ri > ci, (ri == ci).astype(F32)


def _rwkv_kernel(z_ref, sh_ref, s0_ref, mu_ref, w0_ref, w2_ref, a0_ref, a2_ref, g2_ref, kk_ref, ka_ref,
                 rk_ref, lnw_ref, lnb_ref, tri_ref, ones_ref,
                 y_ref, s_ref, zbuf_ref, ybuf_ref, *, bb, chunk, valid):
    c = pl.program_id(1)

    @pl.when(c == 0)
    def _():
        s_ref[...] = s0_ref[...]
        zbuf_ref[:, SUBLANE - 1:SUBLANE, :] = sh_ref[...]

    incl, strict, eye = _tri_masks(chunk)
    tri = tri_ref[...]
    ones = ones_ref[...]
    row = lax.broadcasted_iota(jnp.int32, (chunk, 1), 0)
    mu = mu_ref[...]
    seqs = range(bb)
    zs = []
    for i in seqs:
        z = z_ref[i]
        zbuf_ref[i, SUBLANE:SUBLANE + chunk, :] = z
        prev = zbuf_ref[i, SUBLANE - 1:SUBLANE - 1 + chunk, :]
        zbuf_ref[i, SUBLANE - 1:SUBLANE, :] = z[chunk - 1:chunk, :]
        zs.append(z + mu * (prev - z))
    r = [x[:, 0:RWKV_W] for x in zs]
    k = [x[:, RWKV_W:2 * RWKV_W] for x in zs]
    v = [x[:, 2 * RWKV_W:3 * RWKV_W] for x in zs]
    lora = [x[:, 3 * RWKV_W:] for x in zs]
    w_lin = [_bdot(jnp.tanh(x), w2_ref[...]) for x in lora]
    a_lin = [_bdot(x, a2_ref[...]) for x in lora]
    gate = [_bdot(_sigmoid(x), g2_ref[...]) for x in lora]
    kk = [x * kk_ref[...] for x in k]
    kk_ss = [_rdot3(x * x, ones) for x in kk]
    logw = [-jnp.exp(-_softplus(-(w0_ref[...] + x)) - 0.5) for x in w_lin]
    a = [_sigmoid(a0_ref[...] + x) for x in a_lin]
    kk = [x / jnp.maximum(jnp.sqrt(ss), 1e-12) for x, ss in zip(kk, kk_ss)]
    k = [x * (1.0 + (ai - 1.0) * ka_ref[...]) for x, ai in zip(k, a)]
    ahat = [-x for x in kk]
    bhat = [x * ai for x, ai in zip(kk, a)]
    if valid < chunk:
        vm = row < valid
        mask = lambda xs: [jnp.where(vm, x, 0.0) for x in xs]
        logw, ahat, bhat, k, v = mask(logw), mask(ahat), mask(bhat), mask(k), mask(v)
    g = [_ldot3(tri, x) for x in logw]
    glast = [x[chunk - 1:chunk, :] for x in g]
    e_ng = [jnp.exp(-x) for x in g]
    e_gl = [jnp.exp(gl - x) for gl, x in zip(glast, g)]
    a_l = [ah * jnp.exp(x - lw) for ah, x, lw in zip(ahat, g, logw)]
    r_l = [ri * jnp.exp(x) for ri, x in zip(r, g)]
    b_r = [x * e for x, e in zip(bhat, e_ng)]
    k_r = [x * e for x, e in zip(k, e_ng)]
    k_p = [x * e for x, e in zip(k, e_gl)]
    b_p = [x * e for x, e in zip(bhat, e_gl)]
    dec = [jnp.exp(x) for x in glast]

    probs = [(i, h) for i in seqs for h in range(RWKV_HEADS)]
    hd = lambda xs: [xs[i][:, h * HEAD_DIM:(h + 1) * HEAD_DIM] for i, h in probs]
    a_lh, r_lh, b_rh, k_rh, k_ph, b_ph, vh = hd(a_l), hd(r_l), hd(b_r), hd(k_r), hd(k_p), hd(b_p), hd(v)
    ar = [jnp.concatenate([x, y], axis=0) for x, y in zip(a_lh, r_lh)]
    gb = [_bdot_nt(x, y) for x, y in zip(ar, b_rh)]
    gk = [_bdot_nt(x, y) for x, y in zip(ar, k_rh)]
    a_ab = [jnp.where(strict, x[:chunk], 0.0) for x in gb]
    a_rb = [jnp.where(incl, x[chunk:], 0.0) for x in gb]
    a_ak = [jnp.where(strict, x[:chunk], 0.0) for x in gk]
    a_rk = [jnp.where(incl, x[chunk:], 0.0) for x in gk]
    t = [eye + m for m in a_ab]
    x = [_bdot(m, m) for m in a_ab]
    p = 2
    while p < chunk:
        if 2 * p < chunk:
            xt = [_bdot(jnp.concatenate([xi, ti], axis=0), xi) for xi, ti in zip(x, t)]
            x = [m[:chunk] for m in xt]
            t = [ti + m[chunk:] for ti, m in zip(t, xt)]
        else:
            t = [ti + _bdot(ti, xi) for xi, ti in zip(x, t)]
        p *= 2
    akv = [_bdot(m, vi) for m, vi in zip(a_ak, vh)]
    w_t = [_bdot(ti, m) for ti, m in zip(t, a_lh)]
    u_t = [_bdot(ti, m) for ti, m in zip(t, akv)]
    r_t = [ri + _bdot(m, wi) for ri, m, wi in zip(r_lh, a_rb, w_t)]
    y_t = [_bdot(m, vi) + _bdot(n, ui) for m, vi, n, ui in zip(a_rk, vh, a_rb, u_t)]
    m_s = [_bdot_tn(wi, bi) for wi, bi in zip(w_t, b_ph)]
    q_s = [_bdot_tn(jnp.concatenate([vi, ui], axis=0), jnp.concatenate([ki, bi], axis=0))
           for vi, ui, ki, bi in zip(vh, u_t, k_ph, b_ph)]
    s_old = [s_ref[i, h] for i, h in probs]
    y_h = [_bdot_nt(ri, si) + yi for ri, si, yi in zip(r_t, s_old, y_t)]
    s_m = [_bdot(si, mi) for si, mi in zip(s_old, m_s)]
    for j, (i, h) in enumerate(probs):
        hs = slice(h * HEAD_DIM, (h + 1) * HEAD_DIM)
        ybuf_ref[i, :, hs] = y_h[j]
        s_ref[i, h] = s_old[j] * dec[i][:, hs] + s_m[j] + q_s[j]
    y = [ybuf_ref[i] for i in seqs]
    mean = [_rdot3(x, ones) * (1.0 / HEAD_DIM) for x in y]
    yc = [x - m for x, m in zip(y, mean)]
    var = [_rdot3(x * x, ones) * (1.0 / HEAD_DIM) for x in yc]
    bonus = [_rdot3(ri * ki * rk_ref[...], ones) * vi for ri, ki, vi in zip(r, k, v)]
    for i in seqs:
        yn = yc[i] * lax.rsqrt(var[i] + RWKV_GN_EPS) * lnw_ref[...] + lnb_ref[...]
        y_ref[i] = ((yn + bonus[i]) * gate[i]).astype(y_ref.dtype)


def _rwkv_call(z, shift, s0, params, bb, chunk, valid, out_dtype):
    b, l, _ = z.shape
    grid = (b // bb, l // chunk)
    tri = jnp.tril(jnp.ones((chunk, chunk), F32)).astype(BF16)
    consts = tuple(params) + (tri, _HEAD_ONES())
    return pl.pallas_call(
        functools.partial(_rwkv_kernel, bb=bb, chunk=chunk, valid=valid),
        grid=grid,
        in_specs=[pl.BlockSpec((bb, chunk, RWKV_PROJ), lambda i, c: (i, c, 0)),
                  pl.BlockSpec((bb, 1, RWKV_PROJ), lambda i, c: (i, 0, 0)),
                  pl.BlockSpec((bb, RWKV_HEADS, HEAD_DIM, HEAD_DIM), lambda i, c: (i, 0, 0, 0))]
                 + [_const_spec(a.shape) for a in consts],
        out_specs=[pl.BlockSpec((bb, chunk, RWKV_W), lambda i, c: (i, c, 0)),
                   pl.BlockSpec((bb, RWKV_HEADS, HEAD_DIM, HEAD_DIM), lambda i, c: (i, 0, 0, 0))],
        out_shape=[jax.ShapeDtypeStruct((b, l, RWKV_W), out_dtype),
                   jax.ShapeDtypeStruct((b, RWKV_HEADS, HEAD_DIM, HEAD_DIM), F32)],
        scratch_shapes=[pltpu.VMEM((bb, SUBLANE + chunk, RWKV_PROJ), F32),
                        pltpu.VMEM((bb, chunk, RWKV_W), F32)],
        compiler_params=pltpu.CompilerParams(dimension_semantics=("arbitrary", "arbitrary"),
                                             vmem_limit_bytes=VMEM_LIMIT),
        name="rwkv7_mixer",
    )(z, shift, s0, *consts)


def _gla_kernel(z_ref, s0_ref, gw_ref, gb_ref, nw_ref, tri_ref, ones_ref, eye_ref,
                y_ref, s_ref, obuf_ref, *, bb, chunk, valid):
    c = pl.program_id(1)

    @pl.when(c == 0)
    def _():
        s_ref[...] = s0_ref[...]

    incl, _, _ = _tri_masks(chunk)
    tri = tri_ref[...]
    ones = ones_ref[...]
    row = lax.broadcasted_iota(jnp.int32, (chunk, 1), 0)
    seqs = range(bb)
    z = [z_ref[i] for i in seqs]
    q = [x[:, 0:GLA_KEY_W] * (GLA_DK ** -0.5) for x in z]
    k = [x[:, GLA_KEY_W:2 * GLA_KEY_W] for x in z]
    v = [x[:, 2 * GLA_KEY_W:2 * GLA_KEY_W + GLA_W] for x in z]
    og = [x[:, 2 * GLA_KEY_W + GLA_W:2 * GLA_KEY_W + 2 * GLA_W] for x in z]
    lin = [_bdot(x[:, 2 * GLA_KEY_W + 2 * GLA_W:], gw_ref[...]) for x in z]
    log_a = [-_softplus(-(x + gb_ref[...])) * (1.0 / GLA_TAU) for x in lin]
    if valid < chunk:
        vm = row < valid
        log_a = [jnp.where(vm, x, 0.0) for x in log_a]
        k = [jnp.where(vm, x, 0.0) for x in k]
    b = [_ldot3(tri, x) for x in log_a]
    blast = [x[chunk - 1:chunk, :] for x in b]
    qe = [x * jnp.exp(bi) for x, bi in zip(q, b)]
    ke = [x * jnp.exp(-bi) for x, bi in zip(k, b)]
    kl = [x * jnp.exp(bl - bi) for x, bl, bi in zip(k, blast, b)]
    dec = [_nt3(eye_ref[...], jnp.broadcast_to(jnp.exp(bl), (HEAD_DIM, GLA_KEY_W))) for bl in blast]

    probs = [(i, h) for i in seqs for h in range(GLA_HEADS)]
    kd = lambda xs: [xs[i][:, h * GLA_DK:(h + 1) * GLA_DK] for i, h in probs]
    qeh, keh, klh = kd(qe), kd(ke), kd(kl)
    vh = [v[i][:, h * HEAD_DIM:(h + 1) * HEAD_DIM] for i, h in probs]
    att = [jnp.where(incl, _bdot_nt(x, y), 0.0) for x, y in zip(qeh, keh)]
    s_old = [s_ref[i, h] for i, h in probs]
    o_h = [_bdot(ai, vi) + _bdot(qi, si) for ai, vi, qi, si in zip(att, vh, qeh, s_old)]
    kv = [_bdot_tn(ki, vi) for ki, vi in zip(klh, vh)]
    for j, (i, h) in enumerate(probs):
        obuf_ref[i, :, h * HEAD_DIM:(h + 1) * HEAD_DIM] = o_h[j]
        s_ref[i, h] = s_old[j] * dec[i][h * GLA_DK:(h + 1) * GLA_DK, :] + kv[j]
    o = [obuf_ref[i] for i in seqs]
    ms = [_rdot3(x * x, ones) * (1.0 / HEAD_DIM) for x in o]
    for i in seqs:
        y_ref[i] = (o[i] * lax.rsqrt(ms[i] + NORM_EPS) * nw_ref[...] * _silu(og[i])).astype(y_ref.dtype)


def _gla_call(z, s0, params, bb, chunk, valid, out_dtype):
    b, l, _ = z.shape
    tri = jnp.tril(jnp.ones((chunk, chunk), F32)).astype(BF16)
    consts = tuple(params) + (tri, _HEAD_ONES(), jnp.eye(LANE, dtype=BF16))
    return pl.pallas_call(
        functools.partial(_gla_kernel, bb=bb, chunk=chunk, valid=valid),
        grid=(b // bb, l // chunk),
        in_specs=[pl.BlockSpec((bb, chunk, GLA_PAD), lambda i, c: (i, c, 0)),
                  pl.BlockSpec((bb, GLA_HEADS, GLA_DK, HEAD_DIM), lambda i, c: (i, 0, 0, 0))]
                 + [_const_spec(a.shape) for a in consts],
        out_specs=[pl.BlockSpec((bb, chunk, GLA_W), lambda i, c: (i, c, 0)),
                   pl.BlockSpec((bb, GLA_HEADS, GLA_DK, HEAD_DIM), lambda i, c: (i, 0, 0, 0))],
        out_shape=[jax.ShapeDtypeStruct((b, l, GLA_W), out_dtype),
                   jax.ShapeDtypeStruct((b, GLA_HEADS, GLA_DK, HEAD_DIM), F32)],
        scratch_shapes=[pltpu.VMEM((bb, chunk, GLA_W), F32)],
        compiler_params=pltpu.CompilerParams(dimension_semantics=("arbitrary", "arbitrary"),
                                             vmem_limit_bytes=VMEM_LIMIT),
        name="gla_mixer",
    )(z, s0, *consts)


def _ssd_kernel(z_ref, cp_ref, s0_ref, cw_ref, cb_ref, dtb_ref, alog_ref, dsk_ref, nw_ref, tri_ref, eye_ref,
                esel_ref, y_ref, s_ref, xbuf_ref, ybuf_ref, *, bb, chunk, valid):
    c = pl.program_id(1)

    @pl.when(c == 0)
    def _():
        s_ref[...] = s0_ref[...]
        xbuf_ref[:, 0:SUBLANE, :] = cp_ref[...]

    incl, _, _ = _tri_masks(chunk)
    tri = tri_ref[...]
    row = lax.broadcasted_iota(jnp.int32, (chunk, 1), 0)
    lane = lax.broadcasted_iota(jnp.int32, (1, LANE), 1)
    rep = M2_HEADS // M2_GROUPS
    seqs = range(bb)
    zg, xc, dt = [], [], []
    live = lane < M2_HEADS
    if valid < chunk:
        live = jnp.logical_and(live, row < valid)
    for i in seqs:
        z = z_ref[i]
        x = z[:, M2_W:M2_W + M2_CONV_DIM]
        xbuf_ref[i, SUBLANE:SUBLANE + chunk, :] = x
        conv = cb_ref[...] + x * cw_ref[M2_CONV - 1:M2_CONV, :]
        for j in range(1, M2_CONV):
            conv = conv + xbuf_ref[i, SUBLANE - j:SUBLANE - j + chunk, :] * cw_ref[M2_CONV - 1 - j:M2_CONV - j, :]
        xbuf_ref[i, 0:SUBLANE, :] = xbuf_ref[i, chunk:chunk + SUBLANE, :]
        zg.append(z[:, 0:M2_W])
        xc.append(_silu(conv))
        dt.append(jnp.where(live, _softplus(z[:, M2_W + M2_CONV_DIM:] + dtb_ref[...]), 0.0))
    a_neg = -jnp.exp(alog_ref[...])
    cum = [_ldot3(tri, x * a_neg) for x in dt]
    cum_t = [_nt3(eye_ref[0:SUBLANE, :], x) for x in cum]
    dt_b = [_rdot2(x, esel_ref[...]) for x in dt]
    grp = [(i, g) for i in seqs for g in range(M2_GROUPS)]
    bg = [xc[i][:, M2_W + g * M2_STATE:M2_W + (g + 1) * M2_STATE] for i, g in grp]
    cg = [xc[i][:, M2_W + (M2_GROUPS + g) * M2_STATE:M2_W + (M2_GROUPS + g + 1) * M2_STATE] for i, g in grp]
    cb = [_bdot_nt(x, y) for x, y in zip(cg, bg)]
    s_grp = [jnp.concatenate([s_ref[i, g * rep + hh] for hh in range(rep)], axis=0) for i, g in grp]
    y_st = [_bdot_nt(x, s) for x, s in zip(cg, s_grp)]
    probs = [(i, h) for i in seqs for h in range(M2_HEADS)]
    ccb = [jnp.broadcast_to(cum[i][:, h:h + 1], (chunk, LANE)) for i, h in probs]
    low = lane < HEAD_DIM
    cum_b = [jnp.concatenate([jnp.where(low, ccb[i * M2_HEADS + 2 * j], ccb[i * M2_HEADS + 2 * j + 1])
                              for j in range(M2_HEADS // 2)], axis=1) for i in seqs]
    xs = [x[:, 0:M2_W] for x in xc]
    xdt = [x * d for x, d in zip(xs, dt_b)]
    xw = [x * jnp.exp(cb_[chunk - 1:chunk, :] - cb_) for x, cb_ in zip(xdt, cum_b)]
    seg = [jnp.where(incl, jnp.exp(jnp.minimum(ccb[j][:, 0:chunk] - cum_t[i][h:h + 1, :], 0.0)), 0.0)
           for j, (i, h) in enumerate(probs)]
    y_in = [_bdot(cb[i * M2_GROUPS + h // rep] * seg[j], xdt[i][:, h * HEAD_DIM:(h + 1) * HEAD_DIM])
            for j, (i, h) in enumerate(probs)]
    s_in = [_bdot_tn(xw[i][:, g * rep * HEAD_DIM:(g + 1) * rep * HEAD_DIM], bg[j]) for j, (i, g) in enumerate(grp)]
    for j, (i, h) in enumerate(probs):
        ybuf_ref[i, :, h * HEAD_DIM:(h + 1) * HEAD_DIM] = y_in[j]
        gj = i * M2_GROUPS + h // rep
        hh = h % rep
        s_ref[i, h] = (s_grp[gj][hh * HEAD_DIM:(hh + 1) * HEAD_DIM] * jnp.exp(cum[i][chunk - 1:chunk, h:h + 1])
                       + s_in[gj][hh * HEAD_DIM:(hh + 1) * HEAD_DIM])
    for i in seqs:
        y_state = jnp.concatenate([y_st[i * M2_GROUPS + g] for g in range(M2_GROUPS)], axis=1)
        y = ybuf_ref[i] + y_state * jnp.exp(cum_b[i]) + dsk_ref[...] * xs[i]
        y_ref[i] = _rms(y * _silu(zg[i]), nw_ref[...]).astype(y_ref.dtype)


def _ssd_call(z, conv_prev, s0, params, bb, chunk, valid, out_dtype):
    b, l, _ = z.shape
    tri = jnp.tril(jnp.ones((chunk, chunk), F32)).astype(BF16)
    esel = (jnp.arange(LANE)[:, None] == jnp.arange(M2_W)[None, :] // HEAD_DIM).astype(BF16)
    consts = tuple(params) + (tri, jnp.eye(LANE, dtype=BF16), esel)
    return pl.pallas_call(
        functools.partial(_ssd_kernel, bb=bb, chunk=chunk, valid=valid),
        grid=(b // bb, l // chunk),
        in_specs=[pl.BlockSpec((bb, chunk, M2_PAD), lambda i, c: (i, c, 0)),
                  pl.BlockSpec((bb, SUBLANE, M2_CONV_DIM), lambda i, c: (i, 0, 0)),
                  pl.BlockSpec((bb, M2_HEADS, HEAD_DIM, M2_STATE), lambda i, c: (i, 0, 0, 0))]
                 + [_const_spec(a.shape) for a in consts],
        out_specs=[pl.BlockSpec((bb, chunk, M2_W), lambda i, c: (i, c, 0)),
                   pl.BlockSpec((bb, M2_HEADS, HEAD_DIM, M2_STATE), lambda i, c: (i, 0, 0, 0))],
        out_shape=[jax.ShapeDtypeStruct((b, l, M2_W), out_dtype),
                   jax.ShapeDtypeStruct((b, M2_HEADS, HEAD_DIM, M2_STATE), F32)],
        scratch_shapes=[pltpu.VMEM((bb, SUBLANE + chunk, M2_CONV_DIM), F32),
                        pltpu.VMEM((bb, chunk, M2_W), F32)],
        compiler_params=pltpu.CompilerParams(dimension_semantics=("arbitrary", "arbitrary"),
                                             vmem_limit_bytes=VMEM_LIMIT),
        name="ssd_mixer",
    )(z, conv_prev, s0, *consts)


def _HEAD_ONES():
    idx = jnp.arange(RWKV_W) // HEAD_DIM
    return (idx[:, None] == idx[None, :]).astype(BF16)


def _row(v):
    return v.reshape(1, -1).astype(F32)


def _pad_rows(m, top, total):
    return jnp.zeros((total, m.shape[1]), F32).at[top:top + m.shape[0]].set(m)


def _pad_lanes(v, total=LANE):
    return jnp.zeros((1, total), F32).at[0, :v.shape[0]].set(v)


def _layer_params(i, ffn1_norm, ffn1_w_gate, ffn1_w_up, ffn1_w_down, mix_norm, w_in, rwkv_mu, rwkv_w0, rwkv_w2,
                  rwkv_a0, rwkv_a2, rwkv_g2, rwkv_k_k, rwkv_k_a, rwkv_r_k, rwkv_ln_w, rwkv_ln_b, gla_gate_w2,
                  gla_gate_b, gla_norm, mamba_conv_w, mamba_conv_b, mamba_dt_bias, mamba_A_log, mamba_D, mamba_norm,
                  w_out, ffn2_norm, ffn2_w_gate, ffn2_w_up, ffn2_w_down, ple_norm, ple_w_gate, ple_w_proj):
    w = w_in[i]
    g0 = RWKV_PROJ
    m0 = RWKV_PROJ + GLA_PROJ
    qkv_end = 2 * GLA_KEY_W + GLA_W
    gla_cols = jnp.concatenate([
        w[:, g0:g0 + qkv_end], w[:, g0 + qkv_end + GLA_GATE_LORA:m0],
        w[:, g0 + qkv_end:g0 + qkv_end + GLA_GATE_LORA],
        jnp.zeros((D_MODEL, LANE - GLA_GATE_LORA), F32)], axis=1)
    m2_cols = jnp.concatenate([w[:, m0:], jnp.zeros((D_MODEL, LANE - M2_HEADS), F32)], axis=1)
    win = jnp.concatenate([w[:, :g0], gla_cols, m2_cols], axis=1).astype(BF16)
    pre = (_row(ffn1_norm[i]), ffn1_w_gate[i].astype(BF16), ffn1_w_up[i].astype(BF16),
           ffn1_w_down[i].astype(BF16), _row(mix_norm[i]), win)
    rwkv = (_row(rwkv_mu[i]), _row(rwkv_w0[i]),
            _pad_rows(rwkv_w2[i], 0, RWKV_LORA).astype(BF16), _row(rwkv_a0[i]),
            _pad_rows(rwkv_a2[i], RWKV_DECAY_LORA, RWKV_LORA).astype(BF16),
            _pad_rows(rwkv_g2[i], RWKV_DECAY_LORA + RWKV_A_LORA, RWKV_LORA).astype(BF16),
            _row(rwkv_k_k[i]), _row(rwkv_k_a[i]), _row(rwkv_r_k[i]), _row(rwkv_ln_w[i]), _row(rwkv_ln_b[i]))
    gla = (_pad_rows(gla_gate_w2[i], 0, LANE).astype(BF16), _row(gla_gate_b[i]),
           _row(jnp.tile(gla_norm[i], GLA_HEADS)))
    ssd = (mamba_conv_w[i].astype(F32), _row(mamba_conv_b[i]), _pad_lanes(mamba_dt_bias[i]),
           _pad_lanes(mamba_A_log[i]), _row(jnp.repeat(mamba_D[i], HEAD_DIM)), _row(mamba_norm[i]))
    post = (w_out[i].astype(BF16), _row(ffn2_norm[i]), ffn2_w_gate[i].astype(BF16), ffn2_w_up[i].astype(BF16),
            ffn2_w_down[i].astype(BF16), _row(ple_norm[i]), ple_w_gate[i].astype(BF16),
            ple_w_proj[i].astype(BF16))
    return pre, rwkv, gla, ssd, post


def _pad_time(a, total):
    return jnp.pad(a, ((0, 0), (0, total - a.shape[1]), (0, 0)))


def _group_layer(x, p, state, lp, final_norm, *, batch, seq, chunks, bb, tm, y_dtype, final):
    pre, rwkv, gla, ssd, post = lp
    shift0, wkv0, gla0, conv0, ssm0 = state
    x1, zr, zg, zm = _pre_call(x, *pre, tm)
    zr = zr.reshape(batch, seq, RWKV_PROJ)
    zg = zg.reshape(batch, seq, GLA_PAD)
    zm = zm.reshape(batch, seq, M2_PAD)
    xbc = zm[:, :, M2_W:M2_W + M2_CONV_DIM]
    shift1 = zr[:, -1]
    conv1 = jnp.concatenate([conv0, xbc[:, -(M2_CONV - 1):]], axis=1)[:, -(M2_CONV - 1):]
    conv_hist = jnp.pad(conv0, ((0, 0), (SUBLANE - (M2_CONV - 1), 0), (0, 0)))
    c_r, c_g, c_m = chunks
    pad = max(c_r, c_g, c_m, seq)
    if pad != seq:
        zr, zg, zm = _pad_time(zr, pad), _pad_time(zg, pad), _pad_time(zm, pad)
    valid = lambda ch: seq if seq < ch else ch
    yr, wkv1 = _rwkv_call(zr, shift0[:, None, :], wkv0, rwkv, bb, c_r, valid(c_r), y_dtype)
    yg, gla1 = _gla_call(zg, gla0, gla, bb, c_g, valid(c_g), y_dtype)
    ym, ssm1 = _ssd_call(zm, conv_hist, ssm0, ssd, bb, c_m, valid(c_m), y_dtype)
    n = batch * seq
    flat = lambda y: y[:, :seq].reshape(n, y.shape[-1])
    x2 = _post_call(x1, flat(yr), flat(yg), flat(ym), p, *post, _row(final_norm), tm, final)
    return x2, (shift1, wkv1, gla1, conv1, ssm1)


def kernel(x_prompt, x_sample, p_prompt, p_sample, state_rwkv_shift, state_rwkv_wkv, state_gla, state_mamba_conv, state_mamba_ssm, ffn1_norm, ffn1_w_gate, ffn1_w_up, ffn1_w_down, mix_norm, w_in, rwkv_mu, rwkv_w0, rwkv_w2, rwkv_a0, rwkv_a2, rwkv_g2, rwkv_k_k, rwkv_k_a, rwkv_r_k, rwkv_ln_w, rwkv_ln_b, gla_gate_w2, gla_gate_b, gla_norm, mamba_conv_w, mamba_conv_b, mamba_dt_bias, mamba_A_log, mamba_D, mamba_norm, w_out, ffn2_norm, ffn2_w_gate, ffn2_w_up, ffn2_w_down, ple_norm, ple_w_gate, ple_w_proj, final_norm):
    depth = w_in.shape[0]
    nb, seq, _ = x_prompt.shape
    db, dseq, _ = x_sample.shape
    weights = (ffn1_norm, ffn1_w_gate, ffn1_w_up, ffn1_w_down, mix_norm, w_in, rwkv_mu, rwkv_w0, rwkv_w2, rwkv_a0,
               rwkv_a2, rwkv_g2, rwkv_k_k, rwkv_k_a, rwkv_r_k, rwkv_ln_w, rwkv_ln_b, gla_gate_w2, gla_gate_b,
               gla_norm, mamba_conv_w, mamba_conv_b, mamba_dt_bias, mamba_A_log, mamba_D, mamba_norm, w_out,
               ffn2_norm, ffn2_w_gate, ffn2_w_up, ffn2_w_down, ple_norm, ple_w_gate, ple_w_proj)
    xp = x_prompt.reshape(nb * seq, D_MODEL)
    xs = x_sample.reshape(db * dseq, D_MODEL)
    p_states, s_states = [], []
    for i in range(depth):
        lp = _layer_params(i, *weights)
        final = i == depth - 1
        fresh = (jnp.zeros((nb, RWKV_PROJ), F32), jnp.zeros((nb, RWKV_HEADS, HEAD_DIM, HEAD_DIM), F32),
                 jnp.zeros((nb, GLA_HEADS, GLA_DK, HEAD_DIM), F32),
                 jnp.zeros((nb, M2_CONV - 1, M2_CONV_DIM), F32),
                 jnp.zeros((nb, M2_HEADS, HEAD_DIM, M2_STATE), F32))
        xp, st_p = _group_layer(xp, p_prompt[i].reshape(nb * seq, PLE_DIM), fresh, lp, final_norm,
                                batch=nb, seq=seq, chunks=(RWKV_CHUNK, GLA_CHUNK, SSD_CHUNK), bb=8, tm=512,
                                y_dtype=BF16, final=final)
        past = (state_rwkv_shift[i], state_rwkv_wkv[i], state_gla[i], state_mamba_conv[i], state_mamba_ssm[i])
        xs, st_s = _group_layer(xs, p_sample[i].reshape(db * dseq, PLE_DIM), past, lp, final_norm,
                                batch=db, seq=dseq, chunks=(SUBLANE, SUBLANE, SUBLANE), bb=8, tm=256,
                                y_dtype=F32, final=final)
        p_states.append(st_p)
        s_states.append(st_s)
    stack = lambda sts, j: jnp.stack([s[j] for s in sts])
    return ((xp.reshape(nb, seq, D_MODEL), xs.reshape(db, dseq, D_MODEL))
            + tuple(stack(p_states, j) for j in range(5)) + tuple(stack(s_states, j) for j in range(5)))
```

```python
import functools

import jax
import jax.numpy as jnp
from jax import lax
from jax.experimental import pallas as pl
from jax.experimental.pallas import tpu as pltpu

F32 = jnp.float32
BF16 = jnp.bfloat16

D_MODEL = 1024
D_FF = 2816
PLE_DIM = 256
HEAD_DIM = 64
NORM_EPS = 1e-6

RWKV_W = 256
RWKV_HEADS = 4
RWKV_GN_EPS = 64e-5
RWKV_PROJ = 896
RWKV_LORA = 128
RWKV_DECAY_LORA = 32
RWKV_A_LORA = 32

GLA_W = 256
GLA_HEADS = 4
GLA_DK = 32
GLA_KEY_W = 128
GLA_GATE_LORA = 16
GLA_TAU = 16.0
GLA_PROJ = 784
GLA_PAD = 896

M2_W = 512
M2_HEADS = 8
M2_STATE = 64
M2_GROUPS = 2
M2_CONV = 4
M2_CONV_DIM = 768
M2_PROJ = 1288
M2_PAD = 1408

LANE = 128
SUBLANE = 8
FF_CHUNK = 256
VMEM_LIMIT = 56 * 1024 * 1024

RWKV_CHUNK = 128
GLA_CHUNK = 64
SSD_CHUNK = 128


def _bdot(a, b):
    return jnp.dot(a.astype(BF16), b.astype(BF16), preferred_element_type=F32)


def _bdot_nt(a, b):
    return lax.dot_general(a.astype(BF16), b.astype(BF16), (((1,), (1,)), ((), ())),
                           preferred_element_type=F32)


def _bdot_tn(a, b):
    return lax.dot_general(a.astype(BF16), b.astype(BF16), (((0,), (0,)), ((), ())),
                           preferred_element_type=F32)


def _split3(x):
    hi = x.astype(BF16)
    r1 = x - hi.astype(F32)
    mid = r1.astype(BF16)
    lo = (r1 - mid.astype(F32)).astype(BF16)
    return hi, mid, lo


def _ldot3(mat, x):
    return sum(jnp.dot(mat, p, preferred_element_type=F32) for p in _split3(x))


def _rdot3(x, mat):
    return sum(jnp.dot(p, mat, preferred_element_type=F32) for p in _split3(x))


def _rdot2(x, mat):
    return sum(jnp.dot(p, mat, preferred_element_type=F32) for p in _split3(x)[:2])


def _nt3(mat, x):
    return sum(lax.dot_general(mat, p, (((1,), (1,)), ((), ())), preferred_element_type=F32)
               for p in _split3(x))


def _softplus(x):
    return jnp.maximum(x, 0.0) + jnp.log1p(jnp.exp(-jnp.abs(x)))


def _sigmoid(x):
    return 1.0 / (1.0 + jnp.exp(-x))


def _silu(x):
    return x * _sigmoid(x)


def _rms(x, w):
    return x * lax.rsqrt(jnp.mean(x * x, axis=-1, keepdims=True) + NORM_EPS) * w


def _const_spec(shape):
    nd = len(shape)
    return pl.BlockSpec(shape, lambda *_: (0,) * nd, pipeline_mode=pl.Buffered(1))


def _layer_spec(a, layer):
    nd = a.ndim - 1
    return pl.BlockSpec((None,) + a.shape[1:], lambda *_: (layer,) + (0,) * nd, pipeline_mode=pl.Buffered(1))


def _ffn(x, nw, wg_ref, wu_ref, wd_ref, act_ref):
    xn = _rms(x, nw).astype(BF16)
    for c in range(D_FF // FF_CHUNK):
        sl = slice(c * FF_CHUNK, (c + 1) * FF_CHUNK)
        hg = jnp.dot(xn, wg_ref[:, sl], preferred_element_type=F32)
        hu = jnp.dot(xn, wu_ref[:, sl], preferred_element_type=F32)
        act_ref[:, sl] = (_silu(hg) * hu).astype(BF16)
    return x + 0.5 * jnp.dot(act_ref[...], wd_ref[...], preferred_element_type=F32)


def _pre_kernel(x_ref, n1_ref, wg_ref, wu_ref, wd_ref, nm_ref, win_ref,
                x1_ref, zr_ref, zg_ref, zm_ref, act_ref):
    x1 = _ffn(x_ref[...], n1_ref[...], wg_ref, wu_ref, wd_ref, act_ref)
    x1_ref[...] = x1
    h = _rms(x1, nm_ref[...]).astype(BF16)
    zr_ref[...] = jnp.dot(h, win_ref[:, 0:RWKV_PROJ], preferred_element_type=F32)
    zg_ref[...] = jnp.dot(h, win_ref[:, RWKV_PROJ:RWKV_PROJ + GLA_PAD], preferred_element_type=F32)
    zm_ref[...] = jnp.dot(h, win_ref[:, RWKV_PROJ + GLA_PAD:], preferred_element_type=F32)


def _pre_call(x, layer, n1, wg, wu, wd, nm, win, tm):
    n = x.shape[0]
    tok = lambda w: pl.BlockSpec((tm, w), lambda i: (i, 0))
    return pl.pallas_call(
        _pre_kernel,
        grid=(n // tm,),
        in_specs=[tok(D_MODEL)] + [_layer_spec(a, layer) for a in (n1, wg, wu, wd, nm, win)],
        out_specs=[tok(D_MODEL), tok(RWKV_PROJ), tok(GLA_PAD), tok(M2_PAD)],
        out_shape=[jax.ShapeDtypeStruct((n, D_MODEL), F32), jax.ShapeDtypeStruct((n, RWKV_PROJ), F32),
                   jax.ShapeDtypeStruct((n, GLA_PAD), F32), jax.ShapeDtypeStruct((n, M2_PAD), F32)],
        scratch_shapes=[pltpu.VMEM((tm, D_FF), BF16)],
        compiler_params=pltpu.CompilerParams(dimension_semantics=("parallel",), vmem_limit_bytes=VMEM_LIMIT),
        name="pre_ffn_inproj",
    )(x, n1, wg, wu, wd, nm, win)


def _post_kernel(x_ref, yr_ref, yg_ref, ym_ref, p_ref, wo_ref, n2_ref, wg_ref, wu_ref, wd_ref,
                 np_ref, pg_ref, pp_ref, nf_ref, o_ref, act_ref, *, final):
    x = x_ref[...]
    x = x + jnp.dot(yr_ref[...].astype(BF16), wo_ref[0:RWKV_W, :], preferred_element_type=F32)
    x = x + jnp.dot(yg_ref[...].astype(BF16), wo_ref[RWKV_W:RWKV_W + GLA_W, :], preferred_element_type=F32)
    x = x + jnp.dot(ym_ref[...].astype(BF16), wo_ref[RWKV_W + GLA_W:, :], preferred_element_type=F32)
    x = _ffn(x, n2_ref[...], wg_ref, wu_ref, wd_ref, act_ref)
    gate = _sigmoid(jnp.dot(_rms(x, np_ref[...]).astype(BF16), pg_ref[...], preferred_element_type=F32))
    x = x + gate * jnp.dot(p_ref[...].astype(BF16), pp_ref[...], preferred_element_type=F32)
    if final:
        x = _rms(x, nf_ref[...])
    o_ref[...] = x


def _post_call(x, yr, yg, ym, p, layer, wo, n2, wg, wu, wd, npn, pg, pp, nf, tm, final):
    n = x.shape[0]
    tok = lambda w: pl.BlockSpec((tm, w), lambda i: (i, 0))
    params = (wo, n2, wg, wu, wd, npn, pg, pp)
    consts = params + (nf,)
    return pl.pallas_call(
        functools.partial(_post_kernel, final=final),
        grid=(n // tm,),
        in_specs=[tok(D_MODEL), tok(RWKV_W), tok(GLA_W), tok(M2_W),
                  pl.BlockSpec((None, tm, PLE_DIM), lambda i: (layer, i, 0))]
                 + [_layer_spec(a, layer) for a in params] + [_const_spec(nf.shape)],
        out_specs=tok(D_MODEL),
        out_shape=jax.ShapeDtypeStruct((n, D_MODEL), F32),
        scratch_shapes=[pltpu.VMEM((tm, D_FF), BF16)],
        compiler_params=pltpu.CompilerParams(dimension_semantics=("parallel",), vmem_limit_bytes=VMEM_LIMIT),
        name="post_outproj_ffn_ple",
    )(x, yr, yg, ym, p, *consts)


def _tri_masks(n):
    ri = lax.broadcasted_iota(jnp.int32, (n, n), 0)
    ci = lax.broadcasted_iota(jnp.int32, (n, n), 1)
    return ri >= ci, ri > ci, (ri == ci).astype(F32)


def _rwkv_kernel(z_ref, sh_ref, s0_ref, mu_ref, w0_ref, w2_ref, a0_ref, a2_ref, g2_ref, kk_ref, ka_ref,
                 rk_ref, lnw_ref, lnb_ref, tri_ref, ones_ref,
                 y_ref, s_ref, zbuf_ref, ybuf_ref, *, bb, chunk, valid):
    c = pl.program_id(1)

    @pl.when(c == 0)
    def _():
        s_ref[...] = s0_ref[...]
        zbuf_ref[:, SUBLANE - 1:SUBLANE, :] = sh_ref[...]

    incl, strict, eye = _tri_masks(chunk)
    tri = tri_ref[...]
    ones = ones_ref[...]
    row = lax.broadcasted_iota(jnp.int32, (chunk, 1), 0)
    mu = mu_ref[...]
    seqs = range(bb)
    zs = []
    for i in seqs:
        z = z_ref[i]
        zbuf_ref[i, SUBLANE:SUBLANE + chunk, :] = z
        prev = zbuf_ref[i, SUBLANE - 1:SUBLANE - 1 + chunk, :]
        zbuf_ref[i, SUBLANE - 1:SUBLANE, :] = z[chunk - 1:chunk, :]
        zs.append(z + mu * (prev - z))
    r = [x[:, 0:RWKV_W] for x in zs]
    k = [x[:, RWKV_W:2 * RWKV_W] for x in zs]
    v = [x[:, 2 * RWKV_W:3 * RWKV_W] for x in zs]
    lora = [x[:, 3 * RWKV_W:] for x in zs]
    w_lin = [_bdot(jnp.tanh(x), w2_ref[...]) for x in lora]
    a_lin = [_bdot(x, a2_ref[...]) for x in lora]
    gate = [_bdot(_sigmoid(x), g2_ref[...]) for x in lora]
    kk = [x * kk_ref[...] for x in k]
    kk_ss = [_rdot3(x * x, ones) for x in kk]
    logw = [-jnp.exp(-_softplus(-(w0_ref[...] + x)) - 0.5) for x in w_lin]
    a = [_sigmoid(a0_ref[...] + x) for x in a_lin]
    kk = [x / jnp.maximum(jnp.sqrt(ss), 1e-12) for x, ss in zip(kk, kk_ss)]
    k = [x * (1.0 + (ai - 1.0) * ka_ref[...]) for x, ai in zip(k, a)]
    ahat = [-x for x in kk]
    bhat = [x * ai for x, ai in zip(kk, a)]
    if valid < chunk:
        vm = row < valid
        mask = lambda xs: [jnp.where(vm, x, 0.0) for x in xs]
        logw, ahat, bhat, k, v = mask(logw), mask(ahat), mask(bhat), mask(k), mask(v)
    g = [_ldot3(tri, x) for x in logw]
    glast = [x[chunk - 1:chunk, :] for x in g]
    e_ng = [jnp.exp(-x) for x in g]
    e_gl = [jnp.exp(gl - x) for gl, x in zip(glast, g)]
    a_l = [ah * jnp.exp(x - lw) for ah, x, lw in zip(ahat, g, logw)]
    r_l = [ri * jnp.exp(x) for ri, x in zip(r, g)]
    b_r = [x * e for x, e in zip(bhat, e_ng)]
    k_r = [x * e for x, e in zip(k, e_ng)]
    k_p = [x * e for x, e in zip(k, e_gl)]
    b_p = [x * e for x, e in zip(bhat, e_gl)]
    dec = [jnp.exp(x) for x in glast]

    probs = [(i, h) for i in seqs for h in range(RWKV_HEADS)]
    hd = lambda xs: [xs[i][:, h * HEAD_DIM:(h + 1) * HEAD_DIM] for i, h in probs]
    a_lh, r_lh, b_rh, k_rh, k_ph, b_ph, vh = hd(a_l), hd(r_l), hd(b_r), hd(k_r), hd(k_p), hd(b_p), hd(v)
    ar = [jnp.concatenate([x, y], axis=0) for x, y in zip(a_lh, r_lh)]
    gb = [_bdot_nt(x, y) for x, y in zip(ar, b_rh)]
    gk = [_bdot_nt(x, y) for x, y in zip(ar, k_rh)]
    a_ab = [jnp.where(strict, x[:chunk], 0.0) for x in gb]
    a_rb = [jnp.where(incl, x[chunk:], 0.0) for x in gb]
    a_ak = [jnp.where(strict, x[:chunk], 0.0) for x in gk]
    a_rk = [jnp.where(incl, x[chunk:], 0.0) for x in gk]
    t = [eye + m for m in a_ab]
    x = [_bdot(m, m) for m in a_ab]
    p = 2
    while p < chunk:
        if 2 * p < chunk:
            xt = [_bdot(jnp.concatenate([xi, ti], axis=0), xi) for xi, ti in zip(x, t)]
            x = [m[:chunk] for m in xt]
            t = [ti + m[chunk:] for ti, m in zip(t, xt)]
        else:
            t = [ti + _bdot(ti, xi) for xi, ti in zip(x, t)]
        p *= 2
    akv = [_bdot(m, vi) for m, vi in zip(a_ak, vh)]
    w_t = [_bdot(ti, m) for ti, m in zip(t, a_lh)]
    u_t = [_bdot(ti, m) for ti, m in zip(t, akv)]
    r_t = [ri + _bdot(m, wi) for ri, m, wi in zip(r_lh, a_rb, w_t)]
    y_t = [_bdot(m, vi) + _bdot(n, ui) for m, vi, n, ui in zip(a_rk, vh, a_rb, u_t)]
    m_s = [_bdot_tn(wi, bi) for wi, bi in zip(w_t, b_ph)]
    q_s = [_bdot_tn(jnp.concatenate([vi, ui], axis=0), jnp.concatenate([ki, bi], axis=0))
           for vi, ui, ki, bi in zip(vh, u_t, k_ph, b_ph)]
    s_old = [s_ref[i, h] for i, h in probs]
    y_h = [_bdot_nt(ri, si) + yi for ri, si, yi in zip(r_t, s_old, y_t)]
    s_m = [_bdot(si, mi) for si, mi in zip(s_old, m_s)]
    for j, (i, h) in enumerate(probs):
        hs = slice(h * HEAD_DIM, (h + 1) * HEAD_DIM)
        ybuf_ref[i, :, hs] = y_h[j]
        s_ref[i, h] = s_old[j] * dec[i][:, hs] + s_m[j] + q_s[j]
    y = [ybuf_ref[i] for i in seqs]
    mean = [_rdot3(x, ones) * (1.0 / HEAD_DIM) for x in y]
    yc = [x - m for x, m in zip(y, mean)]
    var = [_rdot3(x * x, ones) * (1.0 / HEAD_DIM) for x in yc]
    bonus = [_rdot3(ri * ki * rk_ref[...], ones) * vi for ri, ki, vi in zip(r, k, v)]
    for i in seqs:
        yn = yc[i] * lax.rsqrt(var[i] + RWKV_GN_EPS) * lnw_ref[...] + lnb_ref[...]
        y_ref[i] = ((yn + bonus[i]) * gate[i]).astype(y_ref.dtype)


def _state_spec(a, bb, sl):
    nd = a.ndim - 2
    return pl.BlockSpec((None, bb) + a.shape[2:], lambda i, c: (sl, i) + (0,) * nd)


def _rwkv_call(z, shift, s0, sl, params, layer, bb, chunk, valid, out_dtype):
    b, l, _ = z.shape
    grid = (b // bb, l // chunk)
    tri = jnp.tril(jnp.ones((chunk, chunk), F32)).astype(BF16)
    consts = (tri, _HEAD_ONES())
    return pl.pallas_call(
        functools.partial(_rwkv_kernel, bb=bb, chunk=chunk, valid=valid),
        grid=grid,
        in_specs=[pl.BlockSpec((bb, chunk, RWKV_PROJ), lambda i, c: (i, c, 0)),
                  _state_spec(shift, bb, sl), _state_spec(s0, bb, sl)]
                 + [_layer_spec(a, layer) for a in params] + [_const_spec(a.shape) for a in consts],
        out_specs=[pl.BlockSpec((bb, chunk, RWKV_W), lambda i, c: (i, c, 0)),
                   pl.BlockSpec((bb, RWKV_HEADS, HEAD_DIM, HEAD_DIM), lambda i, c: (i, 0, 0, 0))],
        out_shape=[jax.ShapeDtypeStruct((b, l, RWKV_W), out_dtype),
                   jax.ShapeDtypeStruct((b, RWKV_HEADS, HEAD_DIM, HEAD_DIM), F32)],
        scratch_shapes=[pltpu.VMEM((bb, SUBLANE + chunk, RWKV_PROJ), F32),
                        pltpu.VMEM((bb, chunk, RWKV_W), F32)],
        compiler_params=pltpu.CompilerParams(dimension_semantics=("arbitrary", "arbitrary"),
                                             vmem_limit_bytes=VMEM_LIMIT),
        name="rwkv7_mixer",
    )(z, shift, s0, *params, *consts)


def _gla_kernel(z_ref, s0_ref, gw_ref, gb_ref, nw_ref, tri_ref, ones_ref, eye_ref,
                y_ref, s_ref, obuf_ref, *, bb, chunk, valid):
    c = pl.program_id(1)

    @pl.when(c == 0)
    def _():
        s_ref[...] = s0_ref[...]

    incl, _, _ = _tri_masks(chunk)
    tri = tri_ref[...]
    ones = ones_ref[...]
    row = lax.broadcasted_iota(jnp.int32, (chunk, 1), 0)
    seqs = range(bb)
    z = [z_ref[i] for i in seqs]
    q = [x[:, 0:GLA_KEY_W] * (GLA_DK ** -0.5) for x in z]
    k = [x[:, GLA_KEY_W:2 * GLA_KEY_W] for x in z]
    v = [x[:, 2 * GLA_KEY_W:2 * GLA_KEY_W + GLA_W] for x in z]
    og = [x[:, 2 * GLA_KEY_W + GLA_W:2 * GLA_KEY_W + 2 * GLA_W] for x in z]
    lin = [_bdot(x[:, 2 * GLA_KEY_W + 2 * GLA_W:], gw_ref[...]) for x in z]
    log_a = [-_softplus(-(x + gb_ref[...])) * (1.0 / GLA_TAU) for x in lin]
    if valid < chunk:
        vm = row < valid
        log_a = [jnp.where(vm, x, 0.0) for x in log_a]
        k = [jnp.where(vm, x, 0.0) for x in k]
    b = [_ldot3(tri, x) for x in log_a]
    blast = [x[chunk - 1:chunk, :] for x in b]
    qe = [x * jnp.exp(bi) for x, bi in zip(q, b)]
    ke = [x * jnp.exp(-bi) for x, bi in zip(k, b)]
    kl = [x * jnp.exp(bl - bi) for x, bl, bi in zip(k, blast, b)]
    dec = [_nt3(eye_ref[...], jnp.broadcast_to(jnp.exp(bl), (HEAD_DIM, GLA_KEY_W))) for bl in blast]

    probs = [(i, h) for i in seqs for h in range(GLA_HEADS)]
    kd = lambda xs: [xs[i][:, h * GLA_DK:(h + 1) * GLA_DK] for i, h in probs]
    qeh, keh, klh = kd(qe), kd(ke), kd(kl)
    vh = [v[i][:, h * HEAD_DIM:(h + 1) * HEAD_DIM] for i, h in probs]
    att = [jnp.where(incl, _bdot_nt(x, y), 0.0) for x, y in zip(qeh, keh)]
    s_old = [s_ref[i, h] for i, h in probs]
    o_h = [_bdot(ai, vi) + _bdot(qi, si) for ai, vi, qi, si in zip(att, vh, qeh, s_old)]
    kv = [_bdot_tn(ki, vi) for ki, vi in zip(klh, vh)]
    for j, (i, h) in enumerate(probs):
        obuf_ref[i, :, h * HEAD_DIM:(h + 1) * HEAD_DIM] = o_h[j]
        s_ref[i, h] = s_old[j] * dec[i][h * GLA_DK:(h + 1) * GLA_DK, :] + kv[j]
    o = [obuf_ref[i] for i in seqs]
    ms = [_rdot3(x * x, ones) * (1.0 / HEAD_DIM) for x in o]
    for i in seqs:
        y_ref[i] = (o[i] * lax.rsqrt(ms[i] + NORM_EPS) * nw_ref[...] * _silu(og[i])).astype(y_ref.dtype)


def _gla_call(z, s0, sl, params, layer, bb, chunk, valid, out_dtype):
    b, l, _ = z.shape
    tri = jnp.tril(jnp.ones((chunk, chunk), F32)).astype(BF16)
    consts = (tri, _HEAD_ONES(), jnp.eye(LANE, dtype=BF16))
    return pl.pallas_call(
        functools.partial(_gla_kernel, bb=bb, chunk=chunk, valid=valid),
        grid=(b // bb, l // chunk),
        in_specs=[pl.BlockSpec((bb, chunk, GLA_PAD), lambda i, c: (i, c, 0)), _state_spec(s0, bb, sl)]
                 + [_layer_spec(a, layer) for a in params] + [_const_spec(a.shape) for a in consts],
        out_specs=[pl.BlockSpec((bb, chunk, GLA_W), lambda i, c: (i, c, 0)),
                   pl.BlockSpec((bb, GLA_HEADS, GLA_DK, HEAD_DIM), lambda i, c: (i, 0, 0, 0))],
        out_shape=[jax.ShapeDtypeStruct((b, l, GLA_W), out_dtype),
                   jax.ShapeDtypeStruct((b, GLA_HEADS, GLA_DK, HEAD_DIM), F32)],
        scratch_shapes=[pltpu.VMEM((bb, chunk, GLA_W), F32)],
        compiler_params=pltpu.CompilerParams(dimension_semantics=("arbitrary", "arbitrary"),
                                             vmem_limit_bytes=VMEM_LIMIT),
        name="gla_mixer",
    )(z, s0, *params, *consts)


def _ssd_kernel(z_ref, cp_ref, s0_ref, cw_ref, cb_ref, dtb_ref, alog_ref, dsk_ref, nw_ref, tri_ref, eye_ref,
                esel_ref, y_ref, s_ref, xbuf_ref, ybuf_ref, *, bb, chunk, valid):
    c = pl.program_id(1)

    @pl.when(c == 0)
    def _():
        s_ref[...] = s0_ref[...]
        xbuf_ref[:, SUBLANE - (M2_CONV - 1):SUBLANE, :] = cp_ref[...]

    incl, _, _ = _tri_masks(chunk)
    tri = tri_ref[...]
    row = lax.broadcasted_iota(jnp.int32, (chunk, 1), 0)
    lane = lax.broadcasted_iota(jnp.int32, (1, LANE), 1)
    rep = M2_HEADS // M2_GROUPS
    seqs = range(bb)
    zg, xc, dt = [], [], []
    live = lane < M2_HEADS
    if valid < chunk:
        live = jnp.logical_and(live, row < valid)
    for i in seqs:
        z = z_ref[i]
        x = z[:, M2_W:M2_W + M2_CONV_DIM]
        xbuf_ref[i, SUBLANE:SUBLANE + chunk, :] = x
        conv = cb_ref[...] + x * cw_ref[M2_CONV - 1:M2_CONV, :]
        for j in range(1, M2_CONV):
            conv = conv + xbuf_ref[i, SUBLANE - j:SUBLANE - j + chunk, :] * cw_ref[M2_CONV - 1 - j:M2_CONV - j, :]
        xbuf_ref[i, 0:SUBLANE, :] = xbuf_ref[i, chunk:chunk + SUBLANE, :]
        zg.append(z[:, 0:M2_W])
        xc.append(_silu(conv))
        dt.append(jnp.where(live, _softplus(z[:, M2_W + M2_CONV_DIM:] + dtb_ref[...]), 0.0))
    a_neg = -jnp.exp(alog_ref[...])
    cum = [_ldot3(tri, x * a_neg) for x in dt]
    cum_t = [_nt3(eye_ref[0:SUBLANE, :], x) for x in cum]
    dt_b = [_rdot2(x, esel_ref[...]) for x in dt]
    grp = [(i, g) for i in seqs for g in range(M2_GROUPS)]
    bg = [xc[i][:, M2_W + g * M2_STATE:M2_W + (g + 1) * M2_STATE] for i, g in grp]
    cg = [xc[i][:, M2_W + (M2_GROUPS + g) * M2_STATE:M2_W + (M2_GROUPS + g + 1) * M2_STATE] for i, g in grp]
    cb = [_bdot_nt(x, y) for x, y in zip(cg, bg)]
    s_grp = [jnp.concatenate([s_ref[i, g * rep + hh] for hh in range(rep)], axis=0) for i, g in grp]
    y_st = [_bdot_nt(x, s) for x, s in zip(cg, s_grp)]
    probs = [(i, h) for i in seqs for h in range(M2_HEADS)]
    ccb = [jnp.broadcast_to(cum[i][:, h:h + 1], (chunk, LANE)) for i, h in probs]
    low = lane < HEAD_DIM
    cum_b = [jnp.concatenate([jnp.where(low, ccb[i * M2_HEADS + 2 * j], ccb[i * M2_HEADS + 2 * j + 1])
                              for j in range(M2_HEADS // 2)], axis=1) for i in seqs]
    xs = [x[:, 0:M2_W] for x in xc]
    xdt = [x * d for x, d in zip(xs, dt_b)]
    xw = [x * jnp.exp(cb_[chunk - 1:chunk, :] - cb_) for x, cb_ in zip(xdt, cum_b)]
    seg = [jnp.where(incl, jnp.exp(jnp.minimum(ccb[j][:, 0:chunk] - cum_t[i][h:h + 1, :], 0.0)), 0.0)
           for j, (i, h) in enumerate(probs)]
    y_in = [_bdot(cb[i * M2_GROUPS + h // rep] * seg[j], xdt[i][:, h * HEAD_DIM:(h + 1) * HEAD_DIM])
            for j, (i, h) in enumerate(probs)]
    s_in = [_bdot_tn(xw[i][:, g * rep * HEAD_DIM:(g + 1) * rep * HEAD_DIM], bg[j]) for j, (i, g) in enumerate(grp)]
    for j, (i, h) in enumerate(probs):
        ybuf_ref[i, :, h * HEAD_DIM:(h + 1) * HEAD_DIM] = y_in[j]
        gj = i * M2_GROUPS + h // rep
        hh = h % rep
        s_ref[i, h] = (s_grp[gj][hh * HEAD_DIM:(hh + 1) * HEAD_DIM] * jnp.exp(cum[i][chunk - 1:chunk, h:h + 1])
                       + s_in[gj][hh * HEAD_DIM:(hh + 1) * HEAD_DIM])
    for i in seqs:
        y_state = jnp.concatenate([y_st[i * M2_GROUPS + g] for g in range(M2_GROUPS)], axis=1)
        y = ybuf_ref[i] + y_state * jnp.exp(cum_b[i]) + dsk_ref[...] * xs[i]
        y_ref[i] = _rms(y * _silu(zg[i]), nw_ref[...]).astype(y_ref.dtype)


def _ssd_call(z, conv_prev, s0, sl, params, layer, bb, chunk, valid, out_dtype):
    b, l, _ = z.shape
    tri = jnp.tril(jnp.ones((chunk, chunk), F32)).astype(BF16)
    esel = (jnp.arange(LANE)[:, None] == jnp.arange(M2_W)[None, :] // HEAD_DIM).astype(BF16)
    consts = (tri, jnp.eye(LANE, dtype=BF16), esel)
    return pl.pallas_call(
        functools.partial(_ssd_kernel, bb=bb, chunk=chunk, valid=valid),
        grid=(b // bb, l // chunk),
        in_specs=[pl.BlockSpec((bb, chunk, M2_PAD), lambda i, c: (i, c, 0)),
                  _state_spec(conv_prev, bb, sl), _state_spec(s0, bb, sl)]
                 + [_layer_spec(a, layer) for a in params] + [_const_spec(a.shape) for a in consts],
        out_specs=[pl.BlockSpec((bb, chunk, M2_W), lambda i, c: (i, c, 0)),
                   pl.BlockSpec((bb, M2_HEADS, HEAD_DIM, M2_STATE), lambda i, c: (i, 0, 0, 0))],
        out_shape=[jax.ShapeDtypeStruct((b, l, M2_W), out_dtype),
                   jax.ShapeDtypeStruct((b, M2_HEADS, HEAD_DIM, M2_STATE), F32)],
        scratch_shapes=[pltpu.VMEM((bb, SUBLANE + chunk, M2_CONV_DIM), F32),
                        pltpu.VMEM((bb, chunk, M2_W), F32)],
        compiler_params=pltpu.CompilerParams(dimension_semantics=("arbitrary", "arbitrary"),
                                             vmem_limit_bytes=VMEM_LIMIT),
        name="ssd_mixer",
    )(z, conv_prev, s0, *params, *consts)


def _HEAD_ONES():
    idx = jnp.arange(RWKV_W) // HEAD_DIM
    return (idx[:, None] == idx[None, :]).astype(BF16)


def _rows(v):
    return v.reshape(v.shape[0], 1, -1).astype(F32)


def _pad_rows(m, top, total):
    return jnp.zeros((m.shape[0], total, m.shape[2]), F32).at[:, top:top + m.shape[1]].set(m).astype(BF16)


def _pad_lanes(v):
    return jnp.zeros((v.shape[0], 1, LANE), F32).at[:, 0, :v.shape[1]].set(v)


def _stacked_params(ffn1_norm, ffn1_w_gate, ffn1_w_up, ffn1_w_down, mix_norm, w_in, rwkv_mu, rwkv_w0, rwkv_w2,
                    rwkv_a0, rwkv_a2, rwkv_g2, rwkv_k_k, rwkv_k_a, rwkv_r_k, rwkv_ln_w, rwkv_ln_b, gla_gate_w2,
                    gla_gate_b, gla_norm, mamba_conv_w, mamba_conv_b, mamba_dt_bias, mamba_A_log, mamba_D, mamba_norm,
                    w_out, ffn2_norm, ffn2_w_gate, ffn2_w_up, ffn2_w_down, ple_norm, ple_w_gate, ple_w_proj):
    depth = w_in.shape[0]
    w = w_in.astype(BF16)
    g0 = RWKV_PROJ
    m0 = RWKV_PROJ + GLA_PROJ
    qkv_end = 2 * GLA_KEY_W + GLA_W
    zeros = lambda n: jnp.zeros((depth, D_MODEL, n), BF16)
    win = jnp.concatenate([
        w[:, :, :g0 + qkv_end], w[:, :, g0 + qkv_end + GLA_GATE_LORA:m0],
        w[:, :, g0 + qkv_end:g0 + qkv_end + GLA_GATE_LORA], zeros(LANE - GLA_GATE_LORA),
        w[:, :, m0:], zeros(LANE - M2_HEADS)], axis=2)
    pre = (_rows(ffn1_norm), ffn1_w_gate.astype(BF16), ffn1_w_up.astype(BF16), ffn1_w_down.astype(BF16),
           _rows(mix_norm), win)
    rwkv = (_rows(rwkv_mu), _rows(rwkv_w0), _pad_rows(rwkv_w2, 0, RWKV_LORA), _rows(rwkv_a0),
            _pad_rows(rwkv_a2, RWKV_DECAY_LORA, RWKV_LORA),
            _pad_rows(rwkv_g2, RWKV_DECAY_LORA + RWKV_A_LORA, RWKV_LORA),
            _rows(rwkv_k_k), _rows(rwkv_k_a), _rows(rwkv_r_k), _rows(rwkv_ln_w), _rows(rwkv_ln_b))
    gla = (_pad_rows(gla_gate_w2, 0, LANE), _rows(gla_gate_b), _rows(jnp.tile(gla_norm, (1, GLA_HEADS))))
    ssd = (mamba_conv_w.astype(F32), _rows(mamba_conv_b), _pad_lanes(mamba_dt_bias), _pad_lanes(mamba_A_log),
           _rows(jnp.repeat(mamba_D, HEAD_DIM, axis=1)), _rows(mamba_norm))
    post = (w_out.astype(BF16), _rows(ffn2_norm), ffn2_w_gate.astype(BF16), ffn2_w_up.astype(BF16),
            ffn2_w_down.astype(BF16), _rows(ple_norm), ple_w_gate.astype(BF16), ple_w_proj.astype(BF16))
    return pre, rwkv, gla, ssd, post


def _pad_time(a, total):
    return jnp.pad(a, ((0, 0), (0, total - a.shape[1]), (0, 0)))


def _group_layer(x, p, state, sl, params, layer, final_norm, *, batch, seq, chunks, bb, tm, y_dtype, final):
    pre, rwkv, gla, ssd, post = params
    shift0, wkv0, gla0, conv0, ssm0 = state
    x1, zr, zg, zm = _pre_call(x, layer, *pre, tm)
    zr = zr.reshape(batch, seq, RWKV_PROJ)
    zg = zg.reshape(batch, seq, GLA_PAD)
    zm = zm.reshape(batch, seq, M2_PAD)
    xbc = zm[:, :, M2_W:M2_W + M2_CONV_DIM]
    shift1 = zr[:, -1]
    conv1 = jnp.concatenate([conv0[sl], xbc[:, -(M2_CONV - 1):]], axis=1)[:, -(M2_CONV - 1):]
    c_r, c_g, c_m = chunks
    pad = max(c_r, c_g, c_m, seq)
    if pad != seq:
        zr, zg, zm = _pad_time(zr, pad), _pad_time(zg, pad), _pad_time(zm, pad)
    valid = lambda ch: seq if seq < ch else ch
    yr, wkv1 = _rwkv_call(zr, shift0[:, :, None, :], wkv0, sl, rwkv, layer, bb, c_r, valid(c_r), y_dtype)
    yg, gla1 = _gla_call(zg, gla0, sl, gla, layer, bb, c_g, valid(c_g), y_dtype)
    ym, ssm1 = _ssd_call(zm, conv0, ssm0, sl, ssd, layer, bb, c_m, valid(c_m), y_dtype)
    n = batch * seq
    flat = lambda y: y[:, :seq].reshape(n, y.shape[-1])
    x2 = _post_call(x1, flat(yr), flat(yg), flat(ym), p, layer, *post, final_norm.reshape(1, -1), tm, final)
    return x2, (shift1, wkv1, gla1, conv1, ssm1)


def kernel(x_prompt, x_sample, p_prompt, p_sample, state_rwkv_shift, state_rwkv_wkv, state_gla, state_mamba_conv, state_mamba_ssm, ffn1_norm, ffn1_w_gate, ffn1_w_up, ffn1_w_down, mix_norm, w_in, rwkv_mu, rwkv_w0, rwkv_w2, rwkv_a0, rwkv_a2, rwkv_g2, rwkv_k_k, rwkv_k_a, rwkv_r_k, rwkv_ln_w, rwkv_ln_b, gla_gate_w2, gla_gate_b, gla_norm, mamba_conv_w, mamba_conv_b, mamba_dt_bias, mamba_A_log, mamba_D, mamba_norm, w_out, ffn2_norm, ffn2_w_gate, ffn2_w_up, ffn2_w_down, ple_norm, ple_w_gate, ple_w_proj, final_norm):
    depth = w_in.shape[0]
    nb, seq, _ = x_prompt.shape
    db, dseq, _ = x_sample.shape
    weights = (ffn1_norm, ffn1_w_gate, ffn1_w_up, ffn1_w_down, mix_norm, w_in, rwkv_mu, rwkv_w0, rwkv_w2, rwkv_a0,
               rwkv_a2, rwkv_g2, rwkv_k_k, rwkv_k_a, rwkv_r_k, rwkv_ln_w, rwkv_ln_b, gla_gate_w2, gla_gate_b,
               gla_norm, mamba_conv_w, mamba_conv_b, mamba_dt_bias, mamba_A_log, mamba_D, mamba_norm, w_out,
               ffn2_norm, ffn2_w_gate, ffn2_w_up, ffn2_w_down, ple_norm, ple_w_gate, ple_w_proj)
    xp = x_prompt.reshape(nb * seq, D_MODEL)
    xs = x_sample.reshape(db * dseq, D_MODEL)
    pp = p_prompt.reshape(depth, nb * seq, PLE_DIM)
    ps = p_sample.reshape(depth, db * dseq, PLE_DIM)
    params = _stacked_params(*weights)
    fresh = (jnp.zeros((1, nb, RWKV_PROJ), F32), jnp.zeros((1, nb, RWKV_HEADS, HEAD_DIM, HEAD_DIM), F32),
             jnp.zeros((1, nb, GLA_HEADS, GLA_DK, HEAD_DIM), F32),
             jnp.zeros((1, nb, M2_CONV - 1, M2_CONV_DIM), F32),
             jnp.zeros((1, nb, M2_HEADS, HEAD_DIM, M2_STATE), F32))
    past = (state_rwkv_shift, state_rwkv_wkv, state_gla, state_mamba_conv, state_mamba_ssm)
    p_states, s_states = [], []
    for i in range(depth):
        final = i == depth - 1
        xp, st_p = _group_layer(xp, pp, fresh, 0, params, i, final_norm,
                                batch=nb, seq=seq, chunks=(RWKV_CHUNK, GLA_CHUNK, SSD_CHUNK), bb=8, tm=512,
                                y_dtype=BF16, final=final)
        xs, st_s = _group_layer(xs, ps, past, i, params, i, final_norm,
                                batch=db, seq=dseq, chunks=(SUBLANE, SUBLANE, SUBLANE), bb=8, tm=256,
                                y_dtype=F32, final=final)
        p_states.append(st_p)
        s_states.append(st_s)
    stack = lambda sts, j: jnp.stack([s[j] for s in sts])
    return ((xp.reshape(nb, seq, D_MODEL), xs.reshape(db, dseq, D_MODEL))
            + tuple(stack(p_states, j) for j in range(5)) + tuple(stack(s_states, j) for j in range(5)))
```

```python
import functools

import jax
import jax.numpy as jnp
from jax import lax
from jax.experimental import pallas as pl
from jax.experimental.pallas import tpu as pltpu

F32 = jnp.float32
BF16 = jnp.bfloat16

D_MODEL = 1024
D_FF = 2816
PLE_DIM = 256
HEAD_DIM = 64
NORM_EPS = 1e-6

RWKV_W = 256
RWKV_HEADS = 4
RWKV_GN_EPS = 64e-5
RWKV_PROJ = 896
RWKV_LORA = 128
RWKV_DECAY_LORA = 32
RWKV_A_LORA = 32

GLA_W = 256
GLA_HEADS = 4
GLA_DK = 32
GLA_KEY_W = 128
GLA_GATE_LORA = 16
GLA_TAU = 16.0
GLA_PROJ = 784
GLA_PAD = 896

M2_W = 512
M2_HEADS = 8
M2_STATE = 64
M2_GROUPS = 2
M2_CONV = 4
M2_CONV_DIM = 768
M2_PROJ = 1288
M2_PAD = 1408

LANE = 128
SUBLANE = 8
FF_CHUNK = 256
VMEM_LIMIT = 56 * 1024 * 1024

RWKV_CHUNK = 128
GLA_CHUNK = 64
SSD_CHUNK = 128


def _bdot(a, b):
    return jnp.dot(a.astype(BF16), b.astype(BF16), preferred_element_type=F32)


def _bdot_nt(a, b):
    return lax.dot_general(a.astype(BF16), b.astype(BF16), (((1,), (1,)), ((), ())),
                           preferred_element_type=F32)


def _bdot_tn(a, b):
    return lax.dot_general(a.astype(BF16), b.astype(BF16), (((0,), (0,)), ((), ())),
                           preferred_element_type=F32)


def _split3(x):
    hi = x.astype(BF16)
    r1 = x - hi.astype(F32)
    mid = r1.astype(BF16)
    lo = (r1 - mid.astype(F32)).astype(BF16)
    return hi, mid, lo


def _ldot3(mat, x):
    return sum(jnp.dot(mat, p, preferred_element_type=F32) for p in _split3(x))


def _rdot3(x, mat):
    return sum(jnp.dot(p, mat, preferred_element_type=F32) for p in _split3(x))


def _rdot2(x, mat):
    return sum(jnp.dot(p, mat, preferred_element_type=F32) for p in _split3(x)[:2])


def _nt3(mat, x):
    return sum(lax.dot_general(mat, p, (((1,), (1,)), ((), ())), preferred_element_type=F32)
               for p in _split3(x))


def _softplus(x):
    return jnp.maximum(x, 0.0) + jnp.log1p(jnp.exp(-jnp.abs(x)))


def _sigmoid(x):
    return 1.0 / (1.0 + jnp.exp(-x))


def _silu(x):
    return x * _sigmoid(x)


def _rms(x, w):
    return x * lax.rsqrt(jnp.mean(x * x, axis=-1, keepdims=True) + NORM_EPS) * w


def _const_spec(shape):
    nd = len(shape)
    return pl.BlockSpec(shape, lambda *_: (0,) * nd, pipeline_mode=pl.Buffered(1))


def _layer_spec(a, layer):
    nd = a.ndim - 1
    return pl.BlockSpec((None,) + a.shape[1:], lambda *_: (layer,) + (0,) * nd, pipeline_mode=pl.Buffered(1))


def _ffn(x, nw, wg_ref, wu_ref, wd_ref, act_ref):
    xn = _rms(x, nw).astype(BF16)
    for c in range(D_FF // FF_CHUNK):
        sl = slice(c * FF_CHUNK, (c + 1) * FF_CHUNK)
        hg = jnp.dot(xn, wg_ref[:, sl], preferred_element_type=F32)
        hu = jnp.dot(xn, wu_ref[:, sl], preferred_element_type=F32)
        act_ref[:, sl] = (_silu(hg) * hu).astype(BF16)
    return x + 0.5 * jnp.dot(act_ref[...], wd_ref[...], preferred_element_type=F32)


def _pre_kernel(x_ref, n1_ref, wg_ref, wu_ref, wd_ref, nm_ref, win_ref,
                x1_ref, zr_ref, zg_ref, zm_ref, act_ref):
    x1 = _ffn(x_ref[...], n1_ref[...], wg_ref, wu_ref, wd_ref, act_ref)
    x1_ref[...] = x1
    h = _rms(x1, nm_ref[...]).astype(BF16)
    zr_ref[...] = jnp.dot(h, win_ref[:, 0:RWKV_PROJ], preferred_element_type=F32)
    zg_ref[...] = jnp.dot(h, win_ref[:, RWKV_PROJ:RWKV_PROJ + GLA_PAD], preferred_element_type=F32)
    zm_ref[...] = jnp.dot(h, win_ref[:, RWKV_PROJ + GLA_PAD:], preferred_element_type=F32)


def _pre_call(x, layer, n1, wg, wu, wd, nm, win, tm):
    n = x.shape[0]
    tok = lambda w: pl.BlockSpec((tm, w), lambda i: (i, 0))
    return pl.pallas_call(
        _pre_kernel,
        grid=(n // tm,),
        in_specs=[tok(D_MODEL)] + [_layer_spec(a, layer) for a in (n1, wg, wu, wd, nm, win)],
        out_specs=[tok(D_MODEL), tok(RWKV_PROJ), tok(GLA_PAD), tok(M2_PAD)],
        out_shape=[jax.ShapeDtypeStruct((n, D_MODEL), F32), jax.ShapeDtypeStruct((n, RWKV_PROJ), F32),
                   jax.ShapeDtypeStruct((n, GLA_PAD), F32), jax.ShapeDtypeStruct((n, M2_PAD), F32)],
        scratch_shapes=[pltpu.VMEM((tm, D_FF), BF16)],
        compiler_params=pltpu.CompilerParams(dimension_semantics=("parallel",), vmem_limit_bytes=VMEM_LIMIT),
        name="pre_ffn_inproj",
    )(x, n1, wg, wu, wd, nm, win)


def _post_kernel(x_ref, yr_ref, yg_ref, ym_ref, p_ref, wo_ref, n2_ref, wg_ref, wu_ref, wd_ref,
                 np_ref, pg_ref, pp_ref, nf_ref, o_ref, act_ref, *, final):
    x = x_ref[...]
    x = x + jnp.dot(yr_ref[...].astype(BF16), wo_ref[0:RWKV_W, :], preferred_element_type=F32)
    x = x + jnp.dot(yg_ref[...].astype(BF16), wo_ref[RWKV_W:RWKV_W + GLA_W, :], preferred_element_type=F32)
    x = x + jnp.dot(ym_ref[...].astype(BF16), wo_ref[RWKV_W + GLA_W:, :], preferred_element_type=F32)
    x = _ffn(x, n2_ref[...], wg_ref, wu_ref, wd_ref, act_ref)
    gate = _sigmoid(jnp.dot(_rms(x, np_ref[...]).astype(BF16), pg_ref[...], preferred_element_type=F32))
    x = x + gate * jnp.dot(p_ref[...].astype(BF16), pp_ref[...], preferred_element_type=F32)
    if final:
        x = _rms(x, nf_ref[...])
    o_ref[...] = x


def _post_call(x, yr, yg, ym, p, layer, wo, n2, wg, wu, wd, npn, pg, pp, nf, tm, final):
    n = x.shape[0]
    tok = lambda w: pl.BlockSpec((tm, w), lambda i: (i, 0))
    params = (wo, n2, wg, wu, wd, npn, pg, pp)
    consts = params + (nf,)
    return pl.pallas_call(
        functools.partial(_post_kernel, final=final),
        grid=(n // tm,),
        in_specs=[tok(D_MODEL), tok(RWKV_W), tok(GLA_W), tok(M2_W),
                  pl.BlockSpec((None, tm, PLE_DIM), lambda i: (layer, i, 0))]
                 + [_layer_spec(a, layer) for a in params] + [_const_spec(nf.shape)],
        out_specs=tok(D_MODEL),
        out_shape=jax.ShapeDtypeStruct((n, D_MODEL), F32),
        scratch_shapes=[pltpu.VMEM((tm, D_FF), BF16)],
        compiler_params=pltpu.CompilerParams(dimension_semantics=("parallel",), vmem_limit_bytes=VMEM_LIMIT),
        name="post_outproj_ffn_ple",
    )(x, yr, yg, ym, p, *consts)


def _tri_masks(n):
    ri = lax.broadcasted_iota(jnp.int32, (n, n), 0)
    ci = lax.broadcasted_iota(jnp.int32, (n, n), 1)
    return ri >= ci, ri > ci, (ri == ci).astype(F32)


def _rwkv_kernel(z_ref, sh_ref, s0_ref, mu_ref, w0_ref, w2_ref, a0_ref, a2_ref, g2_ref, kk_ref, ka_ref,
                 rk_ref, lnw_ref, lnb_ref, tri_ref, ones_ref,
                 y_ref, s_ref, zbuf_ref, ybuf_ref, *, bb, chunk, valid):
    c = pl.program_id(1)

    @pl.when(c == 0)
    def _():
        s_ref[...] = s0_ref[...]
        zbuf_ref[:, SUBLANE - 1:SUBLANE, :] = sh_ref[...]

    incl, strict, eye = _tri_masks(chunk)
    tri = tri_ref[...]
    ones = ones_ref[...]
    row = lax.broadcasted_iota(jnp.int32, (chunk, 1), 0)
    mu = mu_ref[...]
    seqs = range(bb)
    zs = []
    for i in seqs:
        z = z_ref[i]
        zbuf_ref[i, SUBLANE:SUBLANE + chunk, :] = z
        prev = zbuf_ref[i, SUBLANE - 1:SUBLANE - 1 + chunk, :]
        zbuf_ref[i, SUBLANE - 1:SUBLANE, :] = z[chunk - 1:chunk, :]
        zs.append(z + mu * (prev - z))
    r = [x[:, 0:RWKV_W] for x in zs]
    k = [x[:, RWKV_W:2 * RWKV_W] for x in zs]
    v = [x[:, 2 * RWKV_W:3 * RWKV_W] for x in zs]
    lora = [x[:, 3 * RWKV_W:] for x in zs]
    w_lin = [_bdot(jnp.tanh(x), w2_ref[...]) for x in lora]
    a_lin = [_bdot(x, a2_ref[...]) for x in lora]
    gate = [_bdot(_sigmoid(x), g2_ref[...]) for x in lora]
    kk = [x * kk_ref[...] for x in k]
    kk_ss = [_rdot3(x * x, ones) for x in kk]
    logw = [-jnp.exp(-_softplus(-(w0_ref[...] + x)) - 0.5) for x in w_lin]
    a = [_sigmoid(a0_ref[...] + x) for x in a_lin]
    kk = [x / jnp.maximum(jnp.sqrt(ss), 1e-12) for x, ss in zip(kk, kk_ss)]
    k = [x * (1.0 + (ai - 1.0) * ka_ref[...]) for x, ai in zip(k, a)]
    ahat = [-x for x in kk]
    bhat = [x * ai for x, ai in zip(kk, a)]
    if valid < chunk:
        vm = row < valid
        mask = lambda xs: [jnp.where(vm, x, 0.0) for x in xs]
        logw, ahat, bhat, k, v = mask(logw), mask(ahat), mask(bhat), mask(k), mask(v)
    g = [_ldot3(tri, x) for x in logw]
    glast = [x[chunk - 1:chunk, :] for x in g]
    e_ng = [jnp.exp(-x) for x in g]
    e_gl = [jnp.exp(gl - x) for gl, x in zip(glast, g)]
    a_l = [ah * jnp.exp(x - lw) for ah, x, lw in zip(ahat, g, logw)]
    r_l = [ri * jnp.exp(x) for ri, x in zip(r, g)]
    b_r = [x * e for x, e in zip(bhat, e_ng)]
    k_r = [x * e for x, e in zip(k, e_ng)]
    k_p = [x * e for x, e in zip(k, e_gl)]
    b_p = [x * e for x, e in zip(bhat, e_gl)]
    dec = [jnp.exp(x) for x in glast]

    probs = [(i, h) for i in seqs for h in range(RWKV_HEADS)]
    hd = lambda xs: [xs[i][:, h * HEAD_DIM:(h + 1) * HEAD_DIM] for i, h in probs]
    a_lh, r_lh, b_rh, k_rh, k_ph, b_ph, vh = hd(a_l), hd(r_l), hd(b_r), hd(k_r), hd(k_p), hd(b_p), hd(v)
    ar = [jnp.concatenate([x, y], axis=0) for x, y in zip(a_lh, r_lh)]
    gb = [_bdot_nt(x, y) for x, y in zip(ar, b_rh)]
    gk = [_bdot_nt(x, y) for x, y in zip(ar, k_rh)]
    a_ab = [jnp.where(strict, x[:chunk], 0.0) for x in gb]
    a_rb = [jnp.where(incl, x[chunk:], 0.0) for x in gb]
    a_ak = [jnp.where(strict, x[:chunk], 0.0) for x in gk]
    a_rk = [jnp.where(incl, x[chunk:], 0.0) for x in gk]
    t = [eye + m for m in a_ab]
    x = [_bdot(m, m) for m in a_ab]
    p = 2
    while p < chunk:
        if 2 * p < chunk:
            xt = [_bdot(jnp.concatenate([xi, ti], axis=0), xi) for xi, ti in zip(x, t)]
            x = [m[:chunk] for m in xt]
            t = [ti + m[chunk:] for ti, m in zip(t, xt)]
        else:
            t = [ti + _bdot(ti, xi) for xi, ti in zip(x, t)]
        p *= 2
    akv = [_bdot(m, vi) for m, vi in zip(a_ak, vh)]
    w_t = [_bdot(ti, m) for ti, m in zip(t, a_lh)]
    u_t = [_bdot(ti, m) for ti, m in zip(t, akv)]
    r_t = [ri + _bdot(m, wi) for ri, m, wi in zip(r_lh, a_rb, w_t)]
    y_t = [_bdot(m, vi) + _bdot(n, ui) for m, vi, n, ui in zip(a_rk, vh, a_rb, u_t)]
    m_s = [_bdot_tn(wi, bi) for wi, bi in zip(w_t, b_ph)]
    q_s = [_bdot_tn(jnp.concatenate([vi, ui], axis=0), jnp.concatenate([ki, bi], axis=0))
           for vi, ui, ki, bi in zip(vh, u_t, k_ph, b_ph)]
    s_old = [s_ref[i, h] for i, h in probs]
    y_h = [_bdot_nt(ri, si) + yi for ri, si, yi in zip(r_t, s_old, y_t)]
    s_m = [_bdot(si, mi) for si, mi in zip(s_old, m_s)]
    for j, (i, h) in enumerate(probs):
        hs = slice(h * HEAD_DIM, (h + 1) * HEAD_DIM)
        ybuf_ref[i, :, hs] = y_h[j]
        s_ref[i, h] = s_old[j] * dec[i][:, hs] + s_m[j] + q_s[j]
    y = [ybuf_ref[i] for i in seqs]
    mean = [_rdot3(x, ones) * (1.0 / HEAD_DIM) for x in y]
    yc = [x - m for x, m in zip(y, mean)]
    var = [_rdot3(x * x, ones) * (1.0 / HEAD_DIM) for x in yc]
    bonus = [_rdot3(ri * ki * rk_ref[...], ones) * vi for ri, ki, vi in zip(r, k, v)]
    for i in seqs:
        yn = yc[i] * lax.rsqrt(var[i] + RWKV_GN_EPS) * lnw_ref[...] + lnb_ref[...]
        y_ref[i] = ((yn + bonus[i]) * gate[i]).astype(y_ref.dtype)


def _state_spec(a, bb, sl):
    nd = a.ndim - 2
    return pl.BlockSpec((None, bb) + a.shape[2:], lambda i, c: (sl, i) + (0,) * nd)


def _rwkv_call(z, shift, s0, sl, params, layer, bb, chunk, valid, out_dtype):
    b, l, _ = z.shape
    grid = (b // bb, l // chunk)
    tri = jnp.tril(jnp.ones((chunk, chunk), F32)).astype(BF16)
    consts = (tri, _HEAD_ONES())
    return pl.pallas_call(
        functools.partial(_rwkv_kernel, bb=bb, chunk=chunk, valid=valid),
        grid=grid,
        in_specs=[pl.BlockSpec((bb, chunk, RWKV_PROJ), lambda i, c: (i, c, 0)),
                  _state_spec(shift, bb, sl), _state_spec(s0, bb, sl)]
                 + [_layer_spec(a, layer) for a in params] + [_const_spec(a.shape) for a in consts],
        out_specs=[pl.BlockSpec((bb, chunk, RWKV_W), lambda i, c: (i, c, 0)),
                   pl.BlockSpec((bb, RWKV_HEADS, HEAD_DIM, HEAD_DIM), lambda i, c: (i, 0, 0, 0))],
        out_shape=[jax.ShapeDtypeStruct((b, l, RWKV_W), out_dtype),
                   jax.ShapeDtypeStruct((b, RWKV_HEADS, HEAD_DIM, HEAD_DIM), F32)],
        scratch_shapes=[pltpu.VMEM((bb, SUBLANE + chunk, RWKV_PROJ), F32),
                        pltpu.VMEM((bb, chunk, RWKV_W), F32)],
        compiler_params=pltpu.CompilerParams(dimension_semantics=("arbitrary", "arbitrary"),
                                             vmem_limit_bytes=VMEM_LIMIT),
        name="rwkv7_mixer",
    )(z, shift, s0, *params, *consts)


def _gla_kernel(z_ref, s0_ref, gw_ref, gb_ref, nw_ref, tri_ref, ones_ref, eye_ref,
                y_ref, s_ref, obuf_ref, *, bb, chunk, valid):
    c = pl.program_id(1)

    @pl.when(c == 0)
    def _():
        s_ref[...] = s0_ref[...]

    incl, _, _ = _tri_masks(chunk)
    tri = tri_ref[...]
    ones = ones_ref[...]
    row = lax.broadcasted_iota(jnp.int32, (chunk, 1), 0)
    seqs = range(bb)
    z = [z_ref[i] for i in seqs]
    q = [x[:, 0:GLA_KEY_W] * (GLA_DK ** -0.5) for x in z]
    k = [x[:, GLA_KEY_W:2 * GLA_KEY_W] for x in z]
    v = [x[:, 2 * GLA_KEY_W:2 * GLA_KEY_W + GLA_W] for x in z]
    og = [x[:, 2 * GLA_KEY_W + GLA_W:2 * GLA_KEY_W + 2 * GLA_W] for x in z]
    lin = [_bdot(x[:, 2 * GLA_KEY_W + 2 * GLA_W:], gw_ref[...]) for x in z]
    log_a = [-_softplus(-(x + gb_ref[...])) * (1.0 / GLA_TAU) for x in lin]
    if valid < chunk:
        vm = row < valid
        log_a = [jnp.where(vm, x, 0.0) for x in log_a]
        k = [jnp.where(vm, x, 0.0) for x in k]
    b = [_ldot3(tri, x) for x in log_a]
    blast = [x[chunk - 1:chunk, :] for x in b]
    qe = [x * jnp.exp(bi) for x, bi in zip(q, b)]
    ke = [x * jnp.exp(-bi) for x, bi in zip(k, b)]
    kl = [x * jnp.exp(bl - bi) for x, bl, bi in zip(k, blast, b)]
    dec = [_nt3(eye_ref[...], jnp.broadcast_to(jnp.exp(bl), (HEAD_DIM, GLA_KEY_W))) for bl in blast]

    probs = [(i, h) for i in seqs for h in range(GLA_HEADS)]
    kd = lambda xs: [xs[i][:, h * GLA_DK:(h + 1) * GLA_DK] for i, h in probs]
    qeh, keh, klh = kd(qe), kd(ke), kd(kl)
    vh = [v[i][:, h * HEAD_DIM:(h + 1) * HEAD_DIM] for i, h in probs]
    att = [jnp.where(incl, _bdot_nt(x, y), 0.0) for x, y in zip(qeh, keh)]
    s_old = [s_ref[i, h] for i, h in probs]
    o_h = [_bdot(ai, vi) + _bdot(qi, si) for ai, vi, qi, si in zip(att, vh, qeh, s_old)]
    kv = [_bdot_tn(ki, vi) for ki, vi in zip(klh, vh)]
    for j, (i, h) in enumerate(probs):
        obuf_ref[i, :, h * HEAD_DIM:(h + 1) * HEAD_DIM] = o_h[j]
        s_ref[i, h] = s_old[j] * dec[i][h * GLA_DK:(h + 1) * GLA_DK, :] + kv[j]
    o = [obuf_ref[i] for i in seqs]
    ms = [_rdot3(x * x, ones) * (1.0 / HEAD_DIM) for x in o]
    for i in seqs:
        y_ref[i] = (o[i] * lax.rsqrt(ms[i] + NORM_EPS) * nw_ref[...] * _silu(og[i])).astype(y_ref.dtype)


def _gla_call(z, s0, sl, params, layer, bb, chunk, valid, out_dtype):
    b, l, _ = z.shape
    tri = jnp.tril(jnp.ones((chunk, chunk), F32)).astype(BF16)
    consts = (tri, _HEAD_ONES(), jnp.eye(LANE, dtype=BF16))
    return pl.pallas_call(
        functools.partial(_gla_kernel, bb=bb, chunk=chunk, valid=valid),
        grid=(b // bb, l // chunk),
        in_specs=[pl.BlockSpec((bb, chunk, GLA_PAD), lambda i, c: (i, c, 0)), _state_spec(s0, bb, sl)]
                 + [_layer_spec(a, layer) for a in params] + [_const_spec(a.shape) for a in consts],
        out_specs=[pl.BlockSpec((bb, chunk, GLA_W), lambda i, c: (i, c, 0)),
                   pl.BlockSpec((bb, GLA_HEADS, GLA_DK, HEAD_DIM), lambda i, c: (i, 0, 0, 0))],
        out_shape=[jax.ShapeDtypeStruct((b, l, GLA_W), out_dtype),
                   jax.ShapeDtypeStruct((b, GLA_HEADS, GLA_DK, HEAD_DIM), F32)],
        scratch_shapes=[pltpu.VMEM((bb, chunk, GLA_W), F32)],
        compiler_params=pltpu.CompilerParams(dimension_semantics=("arbitrary", "arbitrary"),
                                             vmem_limit_bytes=VMEM_LIMIT),
        name="gla_mixer",
    )(z, s0, *params, *consts)


def _ssd_kernel(z_ref, cp_ref, s0_ref, cw_ref, cb_ref, dtb_ref, alog_ref, dsk_ref, nw_ref, tri_ref, eye_ref,
                esel_ref, y_ref, s_ref, xbuf_ref, ybuf_ref, *, bb, chunk, valid):
    c = pl.program_id(1)

    @pl.when(c == 0)
    def _():
        s_ref[...] = s0_ref[...]
        xbuf_ref[:, SUBLANE - (M2_CONV - 1):SUBLANE, :] = cp_ref[...]

    incl, _, _ = _tri_masks(chunk)
    tri = tri_ref[...]
    row = lax.broadcasted_iota(jnp.int32, (chunk, 1), 0)
    lane = lax.broadcasted_iota(jnp.int32, (1, LANE), 1)
    rep = M2_HEADS // M2_GROUPS
    seqs = range(bb)
    zg, xc, dt = [], [], []
    live = lane < M2_HEADS
    if valid < chunk:
        live = jnp.logical_and(live, row < valid)
    for i in seqs:
        z = z_ref[i]
        x = z[:, M2_W:M2_W + M2_CONV_DIM]
        xbuf_ref[i, SUBLANE:SUBLANE + chunk, :] = x
        conv = cb_ref[...] + x * cw_ref[M2_CONV - 1:M2_CONV, :]
        for j in range(1, M2_CONV):
            conv = conv + xbuf_ref[i, SUBLANE - j:SUBLANE - j + chunk, :] * cw_ref[M2_CONV - 1 - j:M2_CONV - j, :]
        xbuf_ref[i, 0:SUBLANE, :] = xbuf_ref[i, chunk:chunk + SUBLANE, :]
        zg.append(z[:, 0:M2_W])
        xc.append(_silu(conv))
        dt.append(jnp.where(live, _softplus(z[:, M2_W + M2_CONV_DIM:] + dtb_ref[...]), 0.0))
    a_neg = -jnp.exp(alog_ref[...])
    cum = [_ldot3(tri, x * a_neg) for x in dt]
    cum_t = [_nt3(eye_ref[0:SUBLANE, :], x) for x in cum]
    dt_b = [_rdot2(x, esel_ref[...]) for x in dt]
    grp = [(i, g) for i in seqs for g in range(M2_GROUPS)]
    bg = [xc[i][:, M2_W + g * M2_STATE:M2_W + (g + 1) * M2_STATE] for i, g in grp]
    cg = [xc[i][:, M2_W + (M2_GROUPS + g) * M2_STATE:M2_W + (M2_GROUPS + g + 1) * M2_STATE] for i, g in grp]
    cb = [_bdot_nt(x, y) for x, y in zip(cg, bg)]
    s_grp = [jnp.concatenate([s_ref[i, g * rep + hh] for hh in range(rep)], axis=0) for i, g in grp]
    y_st = [_bdot_nt(x, s) for x, s in zip(cg, s_grp)]
    probs = [(i, h) for i in seqs for h in range(M2_HEADS)]
    ccb = [jnp.broadcast_to(cum[i][:, h:h + 1], (chunk, LANE)) for i, h in probs]
    low = lane < HEAD_DIM
    cum_b = [jnp.concatenate([jnp.where(low, ccb[i * M2_HEADS + 2 * j], ccb[i * M2_HEADS + 2 * j + 1])
                              for j in range(M2_HEADS // 2)], axis=1) for i in seqs]
    xs = [x[:, 0:M2_W] for x in xc]
    xdt = [x * d for x, d in zip(xs, dt_b)]
    xw = [x * jnp.exp(cb_[chunk - 1:chunk, :] - cb_) for x, cb_ in zip(xdt, cum_b)]
    seg = [jnp.where(incl, jnp.exp(jnp.minimum(ccb[j][:, 0:chunk] - cum_t[i][h:h + 1, :], 0.0)), 0.0)
           for j, (i, h) in enumerate(probs)]
    y_in = [_bdot(cb[i * M2_GROUPS + h // rep] * seg[j], xdt[i][:, h * HEAD_DIM:(h + 1) * HEAD_DIM])
            for j, (i, h) in enumerate(probs)]
    s_in = [_bdot_tn(xw[i][:, g * rep * HEAD_DIM:(g + 1) * rep * HEAD_DIM], bg[j]) for j, (i, g) in enumerate(grp)]
    for j, (i, h) in enumerate(probs):
        ybuf_ref[i, :, h * HEAD_DIM:(h + 1) * HEAD_DIM] = y_in[j]
        gj = i * M2_GROUPS + h // rep
        hh = h % rep
        s_ref[i, h] = (s_grp[gj][hh * HEAD_DIM:(hh + 1) * HEAD_DIM] * jnp.exp(cum[i][chunk - 1:chunk, h:h + 1])
                       + s_in[gj][hh * HEAD_DIM:(hh + 1) * HEAD_DIM])
    for i in seqs:
        y_state = jnp.concatenate([y_st[i * M2_GROUPS + g] for g in range(M2_GROUPS)], axis=1)
        y = ybuf_ref[i] + y_state * jnp.exp(cum_b[i]) + dsk_ref[...] * xs[i]
        y_ref[i] = _rms(y * _silu(zg[i]), nw_ref[...]).astype(y_ref.dtype)


def _ssd_call(z, conv_prev, s0, sl, params, layer, bb, chunk, valid, out_dtype):
    b, l, _ = z.shape
    tri = jnp.tril(jnp.ones((chunk, chunk), F32)).astype(BF16)
    esel = (jnp.arange(LANE)[:, None] == jnp.arange(M2_W)[None, :] // HEAD_DIM).astype(BF16)
    consts = (tri, jnp.eye(LANE, dtype=BF16), esel)
    return pl.pallas_call(
        functools.partial(_ssd_kernel, bb=bb, chunk=chunk, valid=valid),
        grid=(b // bb, l // chunk),
        in_specs=[pl.BlockSpec((bb, chunk, M2_PAD), lambda i, c: (i, c, 0)),
                  _state_spec(conv_prev, bb, sl), _state_spec(s0, bb, sl)]
                 + [_layer_spec(a, layer) for a in params] + [_const_spec(a.shape) for a in consts],
        out_specs=[pl.BlockSpec((bb, chunk, M2_W), lambda i, c: (i, c, 0)),
                   pl.BlockSpec((bb, M2_HEADS, HEAD_DIM, M2_STATE), lambda i, c: (i, 0, 0, 0))],
        out_shape=[jax.ShapeDtypeStruct((b, l, M2_W), out_dtype),
                   jax.ShapeDtypeStruct((b, M2_HEADS, HEAD_DIM, M2_STATE), F32)],
        scratch_shapes=[pltpu.VMEM((bb, SUBLANE + chunk, M2_CONV_DIM), F32),
                        pltpu.VMEM((bb, chunk, M2_W), F32)],
        compiler_params=pltpu.CompilerParams(dimension_semantics=("arbitrary", "arbitrary"),
                                             vmem_limit_bytes=VMEM_LIMIT),
        name="ssd_mixer",
    )(z, conv_prev, s0, *params, *consts)


def _HEAD_ONES():
    idx = jnp.arange(RWKV_W) // HEAD_DIM
    return (idx[:, None] == idx[None, :]).astype(BF16)


ROWS = SUBLANE


def _rows_to_tile(rows):
    rid = lax.broadcasted_iota(jnp.int32, (ROWS, 1), 0)
    out = rows[0]
    for j in range(1, ROWS):
        out = jnp.where(rid == j, rows[j], out)
    return out


def _rwkv_dec_kernel(z_ref, s_ref, sh_ref, cols_ref, w2_ref, a2_ref, g2_ref, ones_ref,
                     y_ref, so_ref, vec_ref, ybuf_ref, post_ref):
    h = pl.program_id(0)
    nt = z_ref.shape[0]
    w = RWKV_W
    ones = ones_ref[...]
    col = lambda j: cols_ref[RWKV_PROJ + j * w:RWKV_PROJ + (j + 1) * w, :]

    @pl.when(h == 0)
    def _():
        mu = cols_ref[0:RWKV_PROJ, :]
        prev = sh_ref[...]
        for t in range(nt):
            z = z_ref[t]
            zs = z + mu * (prev - z)
            prev = z
            r, k, v, lora = zs[0:w], zs[w:2 * w], zs[2 * w:3 * w], zs[3 * w:]
            logw = -jnp.exp(-_softplus(-(col(0) + _bdot(w2_ref[...], jnp.tanh(lora)))) - 0.5)
            a = _sigmoid(col(1) + _bdot(a2_ref[...], lora))
            kk = k * col(2)
            kk = kk / jnp.maximum(jnp.sqrt(_ldot3(ones, kk * kk)), 1e-12)
            k = k * (1.0 + (a - 1.0) * col(3))
            for j, x in enumerate((-kk, jnp.exp(logw), kk * a, k, r, v)):
                vec_ref[t, j] = x
            post_ref[t, 0] = _bdot(g2_ref[...], _sigmoid(lora))
            post_ref[t, 1] = _ldot3(ones, r * k * col(4)) * v

    hs = pl.ds(pl.multiple_of(h * HEAD_DIM, HEAD_DIM), HEAD_DIM)

    def body(v8, carry):
        r0 = pl.multiple_of(h * HEAD_DIM + v8 * ROWS, ROWS)
        vt = [vec_ref[t, 5, pl.ds(r0, ROWS), :] for t in range(nt)]
        ys = [[] for _ in range(nt)]
        for j in range(ROWS):
            s = s_ref[0, v8 * ROWS + j]
            for t in range(nt):
                sa = jnp.sum(s * vec_ref[t, 0, hs, :], axis=0, keepdims=True)
                s = s * vec_ref[t, 1, hs, :] + sa * vec_ref[t, 2, hs, :] + vt[t][j:j + 1, :] * vec_ref[t, 3, hs, :]
                ys[t].append(jnp.sum(s * vec_ref[t, 4, hs, :], axis=0, keepdims=True))
            so_ref[0, v8 * ROWS + j] = s
        for t in range(nt):
            ybuf_ref[t, pl.ds(r0, ROWS), :] = _rows_to_tile(ys[t])
        return carry

    lax.fori_loop(0, HEAD_DIM // ROWS, body, 0)

    @pl.when(h == pl.num_programs(0) - 1)
    def _():
        for t in range(nt):
            y = ybuf_ref[t]
            yc = y - _ldot3(ones, y) * (1.0 / HEAD_DIM)
            var = _ldot3(ones, yc * yc) * (1.0 / HEAD_DIM)
            yn = yc * lax.rsqrt(var + RWKV_GN_EPS) * col(5) + col(6)
            y_ref[t] = (yn + post_ref[t, 1]) * post_ref[t, 0]


def _gla_dec_kernel(z_ref, s_ref, cols_ref, gw_ref, ones_ref, y_ref, so_ref, vec_ref, val_ref, obuf_ref):
    h = pl.program_id(0)
    nt = z_ref.shape[0]

    @pl.when(h == 0)
    def _():
        for t in range(nt):
            z = z_ref[t]
            lin = _bdot(gw_ref[...], z[2 * GLA_KEY_W + 2 * GLA_W:]) + cols_ref[0:GLA_KEY_W, :]
            vec_ref[t, 0] = z[0:GLA_KEY_W] * (GLA_DK ** -0.5)
            vec_ref[t, 1] = z[GLA_KEY_W:2 * GLA_KEY_W]
            vec_ref[t, 2] = jnp.exp(-_softplus(-lin) * (1.0 / GLA_TAU))
            val_ref[t] = z[2 * GLA_KEY_W:2 * GLA_KEY_W + GLA_W]

    vs = pl.ds(pl.multiple_of(h * HEAD_DIM, HEAD_DIM), HEAD_DIM)
    v = [val_ref[t, vs, :] for t in range(nt)]

    def body(k8, acc):
        r0 = pl.multiple_of(h * GLA_DK + k8 * ROWS, ROWS)
        q, k, a = ([vec_ref[t, j, pl.ds(r0, ROWS), :] for t in range(nt)] for j in range(3))
        acc = list(acc)
        for j in range(ROWS):
            s = s_ref[0, k8 * ROWS + j]
            for t in range(nt):
                s = s * a[t][j:j + 1, :] + k[t][j:j + 1, :] * v[t]
                acc[t] = acc[t] + q[t][j:j + 1, :] * s
            so_ref[0, k8 * ROWS + j] = s
        return tuple(acc)

    zero = jnp.zeros((HEAD_DIM, z_ref.shape[2]), F32)
    acc = lax.fori_loop(0, GLA_DK // ROWS, body, (zero,) * nt)
    for t in range(nt):
        obuf_ref[t, vs, :] = acc[t]

    @pl.when(h == pl.num_programs(0) - 1)
    def _():
        for t in range(nt):
            o = obuf_ref[t]
            ms = _ldot3(ones_ref[...], o * o) * (1.0 / HEAD_DIM)
            og = z_ref[t, 2 * GLA_KEY_W + GLA_W:2 * GLA_KEY_W + 2 * GLA_W, :]
            y_ref[t] = o * lax.rsqrt(ms + NORM_EPS) * cols_ref[GLA_KEY_W:, :] * _silu(og)


def _ssd_dec_kernel(z_ref, s_ref, cp_ref, cols_ref, y_ref, so_ref, x_ref, bc_ref, dt_ref, ybuf_ref):
    h = pl.program_id(0)
    nt = z_ref.shape[0]
    cd = M2_CONV_DIM
    off_b = M2_CONV * cd
    off_dt = off_b + cd

    @pl.when(h == 0)
    def _():
        xs = [cp_ref[j] for j in range(M2_CONV - 1)] + [z_ref[t, M2_W:M2_W + cd, :] for t in range(nt)]
        a_neg = -jnp.exp(cols_ref[off_dt + ROWS:off_dt + 2 * ROWS, :])
        for t in range(nt):
            conv = cols_ref[off_b:off_b + cd, :]
            for j in range(M2_CONV):
                conv = conv + xs[t + j] * cols_ref[j * cd:(j + 1) * cd, :]
            xc = _silu(conv)
            x_ref[t] = xc[0:M2_W]
            bc_ref[t] = xc[M2_W:]
            dt = _softplus(z_ref[t, M2_W + cd:M2_W + cd + ROWS, :] + cols_ref[off_dt:off_dt + ROWS, :])
            dt_ref[t, 0] = dt
            dt_ref[t, 1] = jnp.exp(dt * a_neg)

    g = h // (M2_HEADS // M2_GROUPS)
    bs = pl.ds(pl.multiple_of(g * M2_STATE, M2_STATE), M2_STATE)
    cs = pl.ds(pl.multiple_of((M2_GROUPS + g) * M2_STATE, M2_STATE), M2_STATE)
    rid = lax.broadcasted_iota(jnp.int32, (ROWS, 1), 0)
    pick = lambda tile: jnp.sum(jnp.where(rid == h, tile, 0.0), axis=0, keepdims=True)
    dt = [pick(dt_ref[t, 0]) for t in range(nt)]
    da = [pick(dt_ref[t, 1]) for t in range(nt)]

    def body(p8, carry):
        r0 = pl.multiple_of(h * HEAD_DIM + p8 * ROWS, ROWS)
        xt = [x_ref[t, pl.ds(r0, ROWS), :] * dt[t] for t in range(nt)]
        ys = [[] for _ in range(nt)]
        for j in range(ROWS):
            s = s_ref[0, p8 * ROWS + j]
            for t in range(nt):
                s = s * da[t] + xt[t][j:j + 1, :] * bc_ref[t, bs, :]
                ys[t].append(jnp.sum(s * bc_ref[t, cs, :], axis=0, keepdims=True))
            so_ref[0, p8 * ROWS + j] = s
        for t in range(nt):
            ybuf_ref[t, pl.ds(r0, ROWS), :] = _rows_to_tile(ys[t])
        return carry

    lax.fori_loop(0, HEAD_DIM // ROWS, body, 0)

    @pl.when(h == pl.num_programs(0) - 1)
    def _():
        off_d = off_dt + 2 * ROWS
        for t in range(nt):
            y = ybuf_ref[t] + cols_ref[off_d:off_d + M2_W, :] * x_ref[t]
            y = y * _silu(z_ref[t, 0:M2_W, :])
            ms = jnp.mean(y * y, axis=0, keepdims=True)
            y_ref[t] = y * lax.rsqrt(ms + NORM_EPS) * cols_ref[off_d + M2_W:off_d + 2 * M2_W, :]


def _dec_call(kern, name, acts, state, sl, layer_params, consts, layer, y_width, scratch):
    nt, _, nb = acts[0].shape
    heads = state.shape[1]
    whole = lambda a: pl.BlockSpec(a.shape, lambda hd: (0,) * a.ndim, pipeline_mode=pl.Buffered(1))
    blk = (1,) + state.shape[2:]
    return pl.pallas_call(
        kern,
        grid=(heads,),
        in_specs=[whole(a) for a in acts]
                 + [pl.BlockSpec((None,) + blk, lambda hd: (sl, hd, 0, 0, 0))]
                 + [_layer_spec(a, layer) for a in layer_params] + [_const_spec(a.shape) for a in consts],
        out_specs=[pl.BlockSpec((nt, y_width, nb), lambda hd: (0, 0, 0)),
                   pl.BlockSpec(blk, lambda hd: (hd, 0, 0, 0))],
        out_shape=[jax.ShapeDtypeStruct((nt, y_width, nb), F32), jax.ShapeDtypeStruct(state.shape[1:], F32)],
        scratch_shapes=scratch,
        compiler_params=pltpu.CompilerParams(dimension_semantics=("arbitrary",), vmem_limit_bytes=VMEM_LIMIT),
        name=name,
    )(*acts, state, *layer_params, *consts)


def _rows(v):
    return v.reshape(v.shape[0], 1, -1).astype(F32)


def _pad_rows(m, top, total):
    return jnp.zeros((m.shape[0], total, m.shape[2]), F32).at[:, top:top + m.shape[1]].set(m).astype(BF16)


def _pad_lanes(v):
    return jnp.zeros((v.shape[0], 1, LANE), F32).at[:, 0, :v.shape[1]].set(v)


def _stacked_params(ffn1_norm, ffn1_w_gate, ffn1_w_up, ffn1_w_down, mix_norm, w_in, rwkv_mu, rwkv_w0, rwkv_w2,
                    rwkv_a0, rwkv_a2, rwkv_g2, rwkv_k_k, rwkv_k_a, rwkv_r_k, rwkv_ln_w, rwkv_ln_b, gla_gate_w2,
                    gla_gate_b, gla_norm, mamba_conv_w, mamba_conv_b, mamba_dt_bias, mamba_A_log, mamba_D, mamba_norm,
                    w_out, ffn2_norm, ffn2_w_gate, ffn2_w_up, ffn2_w_down, ple_norm, ple_w_gate, ple_w_proj):
    depth = w_in.shape[0]
    w = w_in.astype(BF16)
    g0 = RWKV_PROJ
    m0 = RWKV_PROJ + GLA_PROJ
    qkv_end = 2 * GLA_KEY_W + GLA_W
    zeros = lambda n: jnp.zeros((depth, D_MODEL, n), BF16)
    win = jnp.concatenate([
        w[:, :, :g0 + qkv_end], w[:, :, g0 + qkv_end + GLA_GATE_LORA:m0],
        w[:, :, g0 + qkv_end:g0 + qkv_end + GLA_GATE_LORA], zeros(LANE - GLA_GATE_LORA),
        w[:, :, m0:], zeros(LANE - M2_HEADS)], axis=2)
    pre = (_rows(ffn1_norm), ffn1_w_gate.astype(BF16), ffn1_w_up.astype(BF16), ffn1_w_down.astype(BF16),
           _rows(mix_norm), win)
    rwkv = (_rows(rwkv_mu), _rows(rwkv_w0), _pad_rows(rwkv_w2, 0, RWKV_LORA), _rows(rwkv_a0),
            _pad_rows(rwkv_a2, RWKV_DECAY_LORA, RWKV_LORA),
            _pad_rows(rwkv_g2, RWKV_DECAY_LORA + RWKV_A_LORA, RWKV_LORA),
            _rows(rwkv_k_k), _rows(rwkv_k_a), _rows(rwkv_r_k), _rows(rwkv_ln_w), _rows(rwkv_ln_b))
    gla = (_pad_rows(gla_gate_w2, 0, LANE), _rows(gla_gate_b), _rows(jnp.tile(gla_norm, (1, GLA_HEADS))))
    ssd = (mamba_conv_w.astype(F32), _rows(mamba_conv_b), _pad_lanes(mamba_dt_bias), _pad_lanes(mamba_A_log),
           _rows(jnp.repeat(mamba_D, HEAD_DIM, axis=1)), _rows(mamba_norm))
    post = (w_out.astype(BF16), _rows(ffn2_norm), ffn2_w_gate.astype(BF16), ffn2_w_up.astype(BF16),
            ffn2_w_down.astype(BF16), _rows(ple_norm), ple_w_gate.astype(BF16), ple_w_proj.astype(BF16))
    return pre, rwkv, gla, ssd, post


def _decode_params(nb, rwkv_mu, rwkv_w0, rwkv_w2, rwkv_a0, rwkv_a2, rwkv_g2, rwkv_k_k, rwkv_k_a, rwkv_r_k,
                   rwkv_ln_w, rwkv_ln_b, gla_gate_w2, gla_gate_b, gla_norm, mamba_conv_w, mamba_conv_b,
                   mamba_dt_bias, mamba_A_log, mamba_D, mamba_norm):
    depth = rwkv_mu.shape[0]
    slab = lambda vs: jnp.broadcast_to(jnp.concatenate([v.reshape(depth, -1) for v in vs], axis=1)[:, :, None],
                                       (depth, sum(v[0].size for v in vs), nb)).astype(F32)
    tr = lambda m: jnp.swapaxes(m, 1, 2)
    rwkv = (slab((rwkv_mu, rwkv_w0, rwkv_a0, rwkv_k_k, rwkv_k_a, rwkv_r_k, rwkv_ln_w, rwkv_ln_b)),
            tr(_pad_rows(rwkv_w2, 0, RWKV_LORA)), tr(_pad_rows(rwkv_a2, RWKV_DECAY_LORA, RWKV_LORA)),
            tr(_pad_rows(rwkv_g2, RWKV_DECAY_LORA + RWKV_A_LORA, RWKV_LORA)))
    gla = (slab((gla_gate_b, jnp.tile(gla_norm, (1, GLA_HEADS)))), tr(_pad_rows(gla_gate_w2, 0, LANE)))
    ssd = (slab((mamba_conv_w, mamba_conv_b, mamba_dt_bias, mamba_A_log, jnp.repeat(mamba_D, HEAD_DIM, axis=1),
                 mamba_norm)),)
    return rwkv, gla, ssd


def _sample_layer(x, p, state, params, dparams, layer, final_norm, *, batch, seq, tm, final):
    pre, _, _, _, post = params
    d_rwkv, d_gla, d_ssd = dparams
    shift0, wkv0, gla0, conv0, ssm0 = state
    x1, zr, zg, zm = _pre_call(x, layer, *pre, tm)
    tokens_first = lambda z: jnp.transpose(z.reshape(batch, seq, z.shape[-1]), (1, 2, 0))
    zr_t, zg_t, zm_t = tokens_first(zr), tokens_first(zg), tokens_first(zm)
    zr3 = zr.reshape(batch, seq, RWKV_PROJ)
    xbc = zm.reshape(batch, seq, M2_PAD)[:, :, M2_W:M2_W + M2_CONV_DIM]
    shift1 = zr3[:, -1]
    conv1 = jnp.concatenate([conv0[layer], xbc], axis=1)[:, -(M2_CONV - 1):]
    scr = lambda *shape: pltpu.VMEM(shape, F32)
    yr, wkv1 = _dec_call(_rwkv_dec_kernel, "rwkv7_decode", (zr_t,), wkv0, layer,
                         (jnp.swapaxes(shift0, 1, 2),) + d_rwkv, (_HEAD_ONES(),), layer, RWKV_W,
                         [scr(seq, 6, RWKV_W, batch), scr(seq, RWKV_W, batch), scr(seq, 2, RWKV_W, batch)])
    yg, gla1 = _dec_call(_gla_dec_kernel, "gla_decode", (zg_t,), gla0, layer, d_gla, (_HEAD_ONES(),), layer, GLA_W,
                         [scr(seq, 3, GLA_KEY_W, batch), scr(seq, GLA_W, batch), scr(seq, GLA_W, batch)])
    conv_t = jnp.transpose(conv0, (0, 2, 3, 1))
    ym, ssm1 = _dec_call(_ssd_dec_kernel, "ssd_decode", (zm_t,), ssm0, layer, (conv_t,) + d_ssd, (), layer, M2_W,
                         [scr(seq, M2_W, batch), scr(seq, 2 * M2_GROUPS * M2_STATE, batch),
                          scr(seq, 2, ROWS, batch), scr(seq, M2_W, batch)])
    n = batch * seq
    flat = lambda y: jnp.transpose(y, (2, 0, 1)).reshape(n, y.shape[1])
    x2 = _post_call(x1, flat(yr), flat(yg), flat(ym), p, layer, *post, final_norm.reshape(1, -1), tm, final)
    return x2, (shift1, wkv1, gla1, conv1, ssm1)


def _pad_time(a, total):
    return jnp.pad(a, ((0, 0), (0, total - a.shape[1]), (0, 0)))


def _group_layer(x, p, state, sl, params, layer, final_norm, *, batch, seq, chunks, bb, tm, y_dtype, final):
    pre, rwkv, gla, ssd, post = params
    shift0, wkv0, gla0, conv0, ssm0 = state
    x1, zr, zg, zm = _pre_call(x, layer, *pre, tm)
    zr = zr.reshape(batch, seq, RWKV_PROJ)
    zg = zg.reshape(batch, seq, GLA_PAD)
    zm = zm.reshape(batch, seq, M2_PAD)
    xbc = zm[:, :, M2_W:M2_W + M2_CONV_DIM]
    shift1 = zr[:, -1]
    conv1 = jnp.concatenate([conv0[sl], xbc[:, -(M2_CONV - 1):]], axis=1)[:, -(M2_CONV - 1):]
    c_r, c_g, c_m = chunks
    pad = max(c_r, c_g, c_m, seq)
    if pad != seq:
        zr, zg, zm = _pad_time(zr, pad), _pad_time(zg, pad), _pad_time(zm, pad)
    valid = lambda ch: seq if seq < ch else ch
    yr, wkv1 = _rwkv_call(zr, shift0[:, :, None, :], wkv0, sl, rwkv, layer, bb, c_r, valid(c_r), y_dtype)
    yg, gla1 = _gla_call(zg, gla0, sl, gla, layer, bb, c_g, valid(c_g), y_dtype)
    ym, ssm1 = _ssd_call(zm, conv0, ssm0, sl, ssd, layer, bb, c_m, valid(c_m), y_dtype)
    n = batch * seq
    flat = lambda y: y[:, :seq].reshape(n, y.shape[-1])
    x2 = _post_call(x1, flat(yr), flat(yg), flat(ym), p, layer, *post, final_norm.reshape(1, -1), tm, final)
    return x2, (shift1, wkv1, gla1, conv1, ssm1)


def kernel(x_prompt, x_sample, p_prompt, p_sample, state_rwkv_shift, state_rwkv_wkv, state_gla, state_mamba_conv, state_mamba_ssm, ffn1_norm, ffn1_w_gate, ffn1_w_up, ffn1_w_down, mix_norm, w_in, rwkv_mu, rwkv_w0, rwkv_w2, rwkv_a0, rwkv_a2, rwkv_g2, rwkv_k_k, rwkv_k_a, rwkv_r_k, rwkv_ln_w, rwkv_ln_b, gla_gate_w2, gla_gate_b, gla_norm, mamba_conv_w, mamba_conv_b, mamba_dt_bias, mamba_A_log, mamba_D, mamba_norm, w_out, ffn2_norm, ffn2_w_gate, ffn2_w_up, ffn2_w_down, ple_norm, ple_w_gate, ple_w_proj, final_norm):
    depth = w_in.shape[0]
    nb, seq, _ = x_prompt.shape
    db, dseq, _ = x_sample.shape
    weights = (ffn1_norm, ffn1_w_gate, ffn1_w_up, ffn1_w_down, mix_norm, w_in, rwkv_mu, rwkv_w0, rwkv_w2, rwkv_a0,
               rwkv_a2, rwkv_g2, rwkv_k_k, rwkv_k_a, rwkv_r_k, rwkv_ln_w, rwkv_ln_b, gla_gate_w2, gla_gate_b,
               gla_norm, mamba_conv_w, mamba_conv_b, mamba_dt_bias, mamba_A_log, mamba_D, mamba_norm, w_out,
               ffn2_norm, ffn2_w_gate, ffn2_w_up, ffn2_w_down, ple_norm, ple_w_gate, ple_w_proj)
    xp = x_prompt.reshape(nb * seq, D_MODEL)
    xs = x_sample.reshape(db * dseq, D_MODEL)
    pp = p_prompt.reshape(depth, nb * seq, PLE_DIM)
    ps = p_sample.reshape(depth, db * dseq, PLE_DIM)
    params = _stacked_params(*weights)
    fresh = (jnp.zeros((1, nb, RWKV_PROJ), F32), jnp.zeros((1, nb, RWKV_HEADS, HEAD_DIM, HEAD_DIM), F32),
             jnp.zeros((1, nb, GLA_HEADS, GLA_DK, HEAD_DIM), F32),
             jnp.zeros((1, nb, M2_CONV - 1, M2_CONV_DIM), F32),
             jnp.zeros((1, nb, M2_HEADS, HEAD_DIM, M2_STATE), F32))
    dparams = _decode_params(db, rwkv_mu, rwkv_w0, rwkv_w2, rwkv_a0, rwkv_a2, rwkv_g2, rwkv_k_k, rwkv_k_a, rwkv_r_k,
                             rwkv_ln_w, rwkv_ln_b, gla_gate_w2, gla_gate_b, gla_norm, mamba_conv_w, mamba_conv_b,
                             mamba_dt_bias, mamba_A_log, mamba_D, mamba_norm)
    batch_last = lambda s: jnp.moveaxis(s, 1, -1)
    past = (state_rwkv_shift, batch_last(state_rwkv_wkv), batch_last(state_gla), state_mamba_conv,
            batch_last(state_mamba_ssm))
    p_states, s_states = [], []
    for i in range(depth):
        final = i == depth - 1
        xp, st_p = _group_layer(xp, pp, fresh, 0, params, i, final_norm,
                                batch=nb, seq=seq, chunks=(RWKV_CHUNK, GLA_CHUNK, SSD_CHUNK), bb=8, tm=512,
                                y_dtype=BF16, final=final)
        xs, st_s = _sample_layer(xs, ps, past, params, dparams, i, final_norm,
                                 batch=db, seq=dseq, tm=256, final=final)
        p_states.append(st_p)
        s_states.append(st_s)
    stack = lambda sts, j: jnp.stack([s[j] for s in sts])
    s_out = tuple(jnp.moveaxis(stack(s_states, j), -1, 1) if j in (1, 2, 4) else stack(s_states, j)
                  for j in range(5))
    return ((xp.reshape(nb, seq, D_MODEL), xs.reshape(db, dseq, D_MODEL))
            + tuple(stack(p_states, j) for j in range(5)) + s_out)
```

```python
import functools

import jax
import jax.numpy as jnp
from jax import lax
from jax.experimental import pallas as pl
from jax.experimental.pallas import tpu as pltpu

F32 = jnp.float32
BF16 = jnp.bfloat16

D_MODEL = 1024
D_FF = 2816
PLE_DIM = 256
HEAD_DIM = 64
NORM_EPS = 1e-6

RWKV_W = 256
RWKV_HEADS = 4
RWKV_GN_EPS = 64e-5
RWKV_PROJ = 896
RWKV_LORA = 128
RWKV_DECAY_LORA = 32
RWKV_A_LORA = 32

GLA_W = 256
GLA_HEADS = 4
GLA_DK = 32
GLA_KEY_W = 128
GLA_GATE_LORA = 16
GLA_TAU = 16.0
GLA_PROJ = 784
GLA_PAD = 896

M2_W = 512
M2_HEADS = 8
M2_STATE = 64
M2_GROUPS = 2
M2_CONV = 4
M2_CONV_DIM = 768
M2_PROJ = 1288
M2_PAD = 1408

LANE = 128
SUBLANE = 8
FF_CHUNK = 256
VMEM_LIMIT = 56 * 1024 * 1024

RWKV_CHUNK = 128
GLA_CHUNK = 64
SSD_CHUNK = 128


def _bdot(a, b):
    return jnp.dot(a.astype(BF16), b.astype(BF16), preferred_element_type=F32)


def _bdot_nt(a, b):
    return lax.dot_general(a.astype(BF16), b.astype(BF16), (((1,), (1,)), ((), ())),
                           preferred_element_type=F32)


def _bdot_tn(a, b):
    return lax.dot_general(a.astype(BF16), b.astype(BF16), (((0,), (0,)), ((), ())),
                           preferred_element_type=F32)


def _split3(x):
    hi = x.astype(BF16)
    r1 = x - hi.astype(F32)
    mid = r1.astype(BF16)
    lo = (r1 - mid.astype(F32)).astype(BF16)
    return hi, mid, lo


def _ldot3(mat, x):
    return sum(jnp.dot(mat, p, preferred_element_type=F32) for p in _split3(x))


def _rdot2(x, mat):
    return sum(jnp.dot(p, mat, preferred_element_type=F32) for p in _split3(x)[:2])


def _nt3(mat, x):
    return sum(lax.dot_general(mat, p, (((1,), (1,)), ((), ())), preferred_element_type=F32)
               for p in _split3(x))


def _softplus(x):
    return jnp.maximum(x, 0.0) + jnp.log1p(jnp.exp(-jnp.abs(x)))


def _sigmoid(x):
    return 1.0 / (1.0 + jnp.exp(-x))


def _silu(x):
    return x * _sigmoid(x)


def _rms(x, w):
    return x * lax.rsqrt(jnp.mean(x * x, axis=-1, keepdims=True) + NORM_EPS) * w


def _const_spec(shape):
    nd = len(shape)
    return pl.BlockSpec(shape, lambda *_: (0,) * nd, pipeline_mode=pl.Buffered(1))


def _layer_spec(a, layer):
    nd = a.ndim - 1
    return pl.BlockSpec((None,) + a.shape[1:], lambda *_: (layer,) + (0,) * nd, pipeline_mode=pl.Buffered(1))


def _ffn(x, nw, wg_ref, wu_ref, wd_ref, act_ref):
    xn = _rms(x, nw).astype(BF16)
    for c in range(D_FF // FF_CHUNK):
        sl = slice(c * FF_CHUNK, (c + 1) * FF_CHUNK)
        hg = jnp.dot(xn, wg_ref[:, sl], preferred_element_type=F32)
        hu = jnp.dot(xn, wu_ref[:, sl], preferred_element_type=F32)
        act_ref[:, sl] = (_silu(hg) * hu).astype(BF16)
    return x + 0.5 * jnp.dot(act_ref[...], wd_ref[...], preferred_element_type=F32)


def _pre_kernel(x_ref, n1_ref, wg_ref, wu_ref, wd_ref, nm_ref, win_ref,
                x1_ref, zr_ref, zg_ref, zm_ref, act_ref):
    x1 = _ffn(x_ref[...], n1_ref[...], wg_ref, wu_ref, wd_ref, act_ref)
    x1_ref[...] = x1
    h = _rms(x1, nm_ref[...]).astype(BF16)
    zr_ref[...] = jnp.dot(h, win_ref[:, 0:RWKV_PROJ], preferred_element_type=F32)
    zg_ref[...] = jnp.dot(h, win_ref[:, RWKV_PROJ:RWKV_PROJ + GLA_PAD], preferred_element_type=F32)
    zm_ref[...] = jnp.dot(h, win_ref[:, RWKV_PROJ + GLA_PAD:], preferred_element_type=F32)


def _pre_call(x, layer, n1, wg, wu, wd, nm, win, tm):
    n = x.shape[0]
    tok = lambda w: pl.BlockSpec((tm, w), lambda i: (i, 0))
    return pl.pallas_call(
        _pre_kernel,
        grid=(n // tm,),
        in_specs=[tok(D_MODEL)] + [_layer_spec(a, layer) for a in (n1, wg, wu, wd, nm, win)],
        out_specs=[tok(D_MODEL), tok(RWKV_PROJ), tok(GLA_PAD), tok(M2_PAD)],
        out_shape=[jax.ShapeDtypeStruct((n, D_MODEL), F32), jax.ShapeDtypeStruct((n, RWKV_PROJ), F32),
                   jax.ShapeDtypeStruct((n, GLA_PAD), F32), jax.ShapeDtypeStruct((n, M2_PAD), F32)],
        scratch_shapes=[pltpu.VMEM((tm, D_FF), BF16)],
        compiler_params=pltpu.CompilerParams(dimension_semantics=("parallel",), vmem_limit_bytes=VMEM_LIMIT),
        name="pre_ffn_inproj",
    )(x, n1, wg, wu, wd, nm, win)


def _post_kernel(x_ref, yr_ref, yg_ref, ym_ref, p_ref, wo_ref, n2_ref, wg_ref, wu_ref, wd_ref,
                 np_ref, pg_ref, pp_ref, nf_ref, o_ref, act_ref, *, final):
    x = x_ref[...]
    x = x + jnp.dot(yr_ref[...].astype(BF16), wo_ref[0:RWKV_W, :], preferred_element_type=F32)
    x = x + jnp.dot(yg_ref[...].astype(BF16), wo_ref[RWKV_W:RWKV_W + GLA_W, :], preferred_element_type=F32)
    x = x + jnp.dot(ym_ref[...].astype(BF16), wo_ref[RWKV_W + GLA_W:, :], preferred_element_type=F32)
    x = _ffn(x, n2_ref[...], wg_ref, wu_ref, wd_ref, act_ref)
    gate = _sigmoid(jnp.dot(_rms(x, np_ref[...]).astype(BF16), pg_ref[...], preferred_element_type=F32))
    x = x + gate * jnp.dot(p_ref[...].astype(BF16), pp_ref[...], preferred_element_type=F32)
    if final:
        x = _rms(x, nf_ref[...])
    o_ref[...] = x


def _post_call(x, yr, yg, ym, p, layer, wo, n2, wg, wu, wd, npn, pg, pp, nf, tm, final):
    n = x.shape[0]
    tok = lambda w: pl.BlockSpec((tm, w), lambda i: (i, 0))
    params = (wo, n2, wg, wu, wd, npn, pg, pp)
    consts = params + (nf,)
    return pl.pallas_call(
        functools.partial(_post_kernel, final=final),
        grid=(n // tm,),
        in_specs=[tok(D_MODEL), tok(RWKV_W), tok(GLA_W), tok(M2_W),
                  pl.BlockSpec((None, tm, PLE_DIM), lambda i: (layer, i, 0))]
                 + [_layer_spec(a, layer) for a in params] + [_const_spec(nf.shape)],
        out_specs=tok(D_MODEL),
        out_shape=jax.ShapeDtypeStruct((n, D_MODEL), F32),
        scratch_shapes=[pltpu.VMEM((tm, D_FF), BF16)],
        compiler_params=pltpu.CompilerParams(dimension_semantics=("parallel",), vmem_limit_bytes=VMEM_LIMIT),
        name="post_outproj_ffn_ple",
    )(x, yr, yg, ym, p, *consts)


def _tri_masks(n):
    ri = lax.broadcasted_iota(jnp.int32, (n, n), 0)
    ci = lax.broadcasted_iota(jnp.int32, (n, n), 1)
    return ri >= ci, ri > ci, (ri == ci).astype(F32)


def _rwkv_kernel(z_ref, sh_ref, s0_ref, mu_ref, w0_ref, w2_ref, a0_ref, a2_ref, g2_ref, kk_ref, ka_ref,
                 rk_ref, lnw_ref, lnb_ref, tri_ref, ones_ref,
                 y_ref, s_ref, zbuf_ref, ybuf_ref, *, bb, chunk, valid):
    c = pl.program_id(1)

    @pl.when(c == 0)
    def _():
        s_ref[...] = s0_ref[...]
        zbuf_ref[:, SUBLANE - 1:SUBLANE, :] = sh_ref[...]

    incl, strict, eye = _tri_masks(chunk)
    tri = tri_ref[...]
    ones = ones_ref[...]
    row = lax.broadcasted_iota(jnp.int32, (chunk, 1), 0)
    mu = mu_ref[...]
    seqs = range(bb)
    zs = []
    for i in seqs:
        z = z_ref[i]
        zbuf_ref[i, SUBLANE:SUBLANE + chunk, :] = z
        prev = zbuf_ref[i, SUBLANE - 1:SUBLANE - 1 + chunk, :]
        zbuf_ref[i, SUBLANE - 1:SUBLANE, :] = z[chunk - 1:chunk, :]
        zs.append(z + mu * (prev - z))
    r = [x[:, 0:RWKV_W] for x in zs]
    k = [x[:, RWKV_W:2 * RWKV_W] for x in zs]
    v = [x[:, 2 * RWKV_W:3 * RWKV_W] for x in zs]
    lora = [x[:, 3 * RWKV_W:] for x in zs]
    w_lin = [_bdot(jnp.tanh(x), w2_ref[...]) for x in lora]
    a_lin = [_bdot(x, a2_ref[...]) for x in lora]
    gate = [_bdot(_sigmoid(x), g2_ref[...]) for x in lora]
    kk = [x * kk_ref[...] for x in k]
    kk_ss = [_rdot2(x * x, ones) for x in kk]
    logw = [-jnp.exp(-_softplus(-(w0_ref[...] + x)) - 0.5) for x in w_lin]
    a = [_sigmoid(a0_ref[...] + x) for x in a_lin]
    kk = [x / jnp.maximum(jnp.sqrt(ss), 1e-12) for x, ss in zip(kk, kk_ss)]
    k = [x * (1.0 + (ai - 1.0) * ka_ref[...]) for x, ai in zip(k, a)]
    ahat = [-x for x in kk]
    bhat = [x * ai for x, ai in zip(kk, a)]
    if valid < chunk:
        vm = row < valid
        mask = lambda xs: [jnp.where(vm, x, 0.0) for x in xs]
        logw, ahat, bhat, k, v = mask(logw), mask(ahat), mask(bhat), mask(k), mask(v)
    g = [_ldot3(tri, x) for x in logw]
    glast = [x[chunk - 1:chunk, :] for x in g]
    e_ng = [jnp.exp(-x) for x in g]
    e_gl = [jnp.exp(gl - x) for gl, x in zip(glast, g)]
    a_l = [ah * jnp.exp(x - lw) for ah, x, lw in zip(ahat, g, logw)]
    r_l = [ri * jnp.exp(x) for ri, x in zip(r, g)]
    b_r = [x * e for x, e in zip(bhat, e_ng)]
    k_r = [x * e for x, e in zip(k, e_ng)]
    k_p = [x * e for x, e in zip(k, e_gl)]
    b_p = [x * e for x, e in zip(bhat, e_gl)]
    dec = [jnp.exp(x) for x in glast]

    probs = [(i, h) for i in seqs for h in range(RWKV_HEADS)]
    hd = lambda xs: [xs[i][:, h * HEAD_DIM:(h + 1) * HEAD_DIM] for i, h in probs]
    a_lh, r_lh, b_rh, k_rh, k_ph, b_ph, vh = hd(a_l), hd(r_l), hd(b_r), hd(k_r), hd(k_p), hd(b_p), hd(v)
    ar = [jnp.concatenate([x, y], axis=0) for x, y in zip(a_lh, r_lh)]
    bk = [jnp.concatenate([x, y], axis=0) for x, y in zip(b_rh, k_rh)]
    gm = [_bdot_nt(x, y) for x, y in zip(ar, bk)]
    a_ab = [jnp.where(strict, x[:chunk, :chunk], 0.0) for x in gm]
    a_ak = [jnp.where(strict, x[:chunk, chunk:], 0.0) for x in gm]
    a_rb = [jnp.where(incl, x[chunk:, :chunk], 0.0) for x in gm]
    a_rk = [jnp.where(incl, x[chunk:, chunk:], 0.0) for x in gm]
    t = [eye + m for m in a_ab]
    x = [_bdot(m, m) for m in a_ab]
    p = 2
    while p < chunk:
        if 2 * p < chunk:
            xt = [_bdot(jnp.concatenate([xi, ti], axis=0), xi) for xi, ti in zip(x, t)]
            x = [m[:chunk] for m in xt]
            t = [ti + m[chunk:] for ti, m in zip(t, xt)]
        else:
            t = [ti + _bdot(ti, xi) for xi, ti in zip(x, t)]
        p *= 2
    akv = [_bdot(m, vi) for m, vi in zip(a_ak, vh)]
    w_t = [_bdot(ti, m) for ti, m in zip(t, a_lh)]
    u_t = [_bdot(ti, m) for ti, m in zip(t, akv)]
    r_t = [ri + _bdot(m, wi) for ri, m, wi in zip(r_lh, a_rb, w_t)]
    y_t = [_bdot(m, vi) + _bdot(n, ui) for m, vi, n, ui in zip(a_rk, vh, a_rb, u_t)]
    m_s = [_bdot_tn(wi, bi) for wi, bi in zip(w_t, b_ph)]
    q_s = [_bdot_tn(jnp.concatenate([vi, ui], axis=0), jnp.concatenate([ki, bi], axis=0))
           for vi, ui, ki, bi in zip(vh, u_t, k_ph, b_ph)]
    s_old = [s_ref[i, h] for i, h in probs]
    y_h = [_bdot_nt(ri, si) + yi for ri, si, yi in zip(r_t, s_old, y_t)]
    s_m = [_bdot(si, mi) for si, mi in zip(s_old, m_s)]
    for j, (i, h) in enumerate(probs):
        hs = slice(h * HEAD_DIM, (h + 1) * HEAD_DIM)
        ybuf_ref[i, :, hs] = y_h[j]
        s_ref[i, h] = s_old[j] * dec[i][:, hs] + s_m[j] + q_s[j]
    y = [ybuf_ref[i] for i in seqs]
    mean = [_rdot2(x, ones) * (1.0 / HEAD_DIM) for x in y]
    yc = [x - m for x, m in zip(y, mean)]
    var = [_rdot2(x * x, ones) * (1.0 / HEAD_DIM) for x in yc]
    bonus = [_rdot2(ri * ki * rk_ref[...], ones) * vi for ri, ki, vi in zip(r, k, v)]
    for i in seqs:
        yn = yc[i] * lax.rsqrt(var[i] + RWKV_GN_EPS) * lnw_ref[...] + lnb_ref[...]
        y_ref[i] = ((yn + bonus[i]) * gate[i]).astype(y_ref.dtype)


def _state_spec(a, bb, sl):
    nd = a.ndim - 2
    return pl.BlockSpec((None, bb) + a.shape[2:], lambda i, c: (sl, i) + (0,) * nd)


def _rwkv_call(z, shift, s0, sl, params, layer, bb, chunk, valid, out_dtype):
    b, l, _ = z.shape
    grid = (b // bb, l // chunk)
    tri = jnp.tril(jnp.ones((chunk, chunk), F32)).astype(BF16)
    consts = (tri, _HEAD_ONES())
    return pl.pallas_call(
        functools.partial(_rwkv_kernel, bb=bb, chunk=chunk, valid=valid),
        grid=grid,
        in_specs=[pl.BlockSpec((bb, chunk, RWKV_PROJ), lambda i, c: (i, c, 0)),
                  _state_spec(shift, bb, sl), _state_spec(s0, bb, sl)]
                 + [_layer_spec(a, layer) for a in params] + [_const_spec(a.shape) for a in consts],
        out_specs=[pl.BlockSpec((bb, chunk, RWKV_W), lambda i, c: (i, c, 0)),
                   pl.BlockSpec((bb, RWKV_HEADS, HEAD_DIM, HEAD_DIM), lambda i, c: (i, 0, 0, 0))],
        out_shape=[jax.ShapeDtypeStruct((b, l, RWKV_W), out_dtype),
                   jax.ShapeDtypeStruct((b, RWKV_HEADS, HEAD_DIM, HEAD_DIM), F32)],
        scratch_shapes=[pltpu.VMEM((bb, SUBLANE + chunk, RWKV_PROJ), F32),
                        pltpu.VMEM((bb, chunk, RWKV_W), F32)],
        compiler_params=pltpu.CompilerParams(dimension_semantics=("arbitrary", "arbitrary"),
                                             vmem_limit_bytes=VMEM_LIMIT),
        name="rwkv7_mixer",
    )(z, shift, s0, *params, *consts)


def _gla_kernel(z_ref, s0_ref, gw_ref, gb_ref, nw_ref, tri_ref, ones_ref, eye_ref,
                y_ref, s_ref, obuf_ref, *, bb, chunk, valid):
    c = pl.program_id(1)

    @pl.when(c == 0)
    def _():
        s_ref[...] = s0_ref[...]

    incl, _, _ = _tri_masks(chunk)
    tri = tri_ref[...]
    ones = ones_ref[...]
    row = lax.broadcasted_iota(jnp.int32, (chunk, 1), 0)
    seqs = range(bb)
    z = [z_ref[i] for i in seqs]
    q = [x[:, 0:GLA_KEY_W] * (GLA_DK ** -0.5) for x in z]
    k = [x[:, GLA_KEY_W:2 * GLA_KEY_W] for x in z]
    v = [x[:, 2 * GLA_KEY_W:2 * GLA_KEY_W + GLA_W] for x in z]
    og = [x[:, 2 * GLA_KEY_W + GLA_W:2 * GLA_KEY_W + 2 * GLA_W] for x in z]
    lin = [_bdot(x[:, 2 * GLA_KEY_W + 2 * GLA_W:], gw_ref[...]) for x in z]
    log_a = [-_softplus(-(x + gb_ref[...])) * (1.0 / GLA_TAU) for x in lin]
    if valid < chunk:
        vm = row < valid
        log_a = [jnp.where(vm, x, 0.0) for x in log_a]
        k = [jnp.where(vm, x, 0.0) for x in k]
    b = [_ldot3(tri, x) for x in log_a]
    blast = [x[chunk - 1:chunk, :] for x in b]
    qe = [x * jnp.exp(bi) for x, bi in zip(q, b)]
    ke = [x * jnp.exp(-bi) for x, bi in zip(k, b)]
    kl = [x * jnp.exp(bl - bi) for x, bl, bi in zip(k, blast, b)]
    dec = [_nt3(eye_ref[...], jnp.broadcast_to(jnp.exp(bl), (HEAD_DIM, GLA_KEY_W))) for bl in blast]

    probs = [(i, h) for i in seqs for h in range(GLA_HEADS)]
    kd = lambda xs: [xs[i][:, h * GLA_DK:(h + 1) * GLA_DK] for i, h in probs]
    qeh, keh, klh = kd(qe), kd(ke), kd(kl)
    vh = [v[i][:, h * HEAD_DIM:(h + 1) * HEAD_DIM] for i, h in probs]
    att = [jnp.where(incl, _bdot_nt(x, y), 0.0) for x, y in zip(qeh, keh)]
    s_old = [s_ref[i, h] for i, h in probs]
    o_h = [_bdot(ai, vi) + _bdot(qi, si) for ai, vi, qi, si in zip(att, vh, qeh, s_old)]
    kv = [_bdot_tn(ki, vi) for ki, vi in zip(klh, vh)]
    for j, (i, h) in enumerate(probs):
        obuf_ref[i, :, h * HEAD_DIM:(h + 1) * HEAD_DIM] = o_h[j]
        s_ref[i, h] = s_old[j] * dec[i][h * GLA_DK:(h + 1) * GLA_DK, :] + kv[j]
    o = [obuf_ref[i] for i in seqs]
    ms = [_rdot2(x * x, ones) * (1.0 / HEAD_DIM) for x in o]
    for i in seqs:
        y_ref[i] = (o[i] * lax.rsqrt(ms[i] + NORM_EPS) * nw_ref[...] * _silu(og[i])).astype(y_ref.dtype)


def _gla_call(z, s0, sl, params, layer, bb, chunk, valid, out_dtype):
    b, l, _ = z.shape
    tri = jnp.tril(jnp.ones((chunk, chunk), F32)).astype(BF16)
    consts = (tri, _HEAD_ONES(), jnp.eye(LANE, dtype=BF16))
    return pl.pallas_call(
        functools.partial(_gla_kernel, bb=bb, chunk=chunk, valid=valid),
        grid=(b // bb, l // chunk),
        in_specs=[pl.BlockSpec((bb, chunk, GLA_PAD), lambda i, c: (i, c, 0)), _state_spec(s0, bb, sl)]
                 + [_layer_spec(a, layer) for a in params] + [_const_spec(a.shape) for a in consts],
        out_specs=[pl.BlockSpec((bb, chunk, GLA_W), lambda i, c: (i, c, 0)),
                   pl.BlockSpec((bb, GLA_HEADS, GLA_DK, HEAD_DIM), lambda i, c: (i, 0, 0, 0))],
        out_shape=[jax.ShapeDtypeStruct((b, l, GLA_W), out_dtype),
                   jax.ShapeDtypeStruct((b, GLA_HEADS, GLA_DK, HEAD_DIM), F32)],
        scratch_shapes=[pltpu.VMEM((bb, chunk, GLA_W), F32)],
        compiler_params=pltpu.CompilerParams(dimension_semantics=("arbitrary", "arbitrary"),
                                             vmem_limit_bytes=VMEM_LIMIT),
        name="gla_mixer",
    )(z, s0, *params, *consts)


def _ssd_kernel(z_ref, cp_ref, s0_ref, cw_ref, cb_ref, dtb_ref, alog_ref, dsk_ref, nw_ref, tri_ref, eye_ref,
                esel_ref, y_ref, s_ref, xbuf_ref, ybuf_ref, *, bb, chunk, valid):
    c = pl.program_id(1)

    @pl.when(c == 0)
    def _():
        s_ref[...] = s0_ref[...]
        xbuf_ref[:, SUBLANE - (M2_CONV - 1):SUBLANE, :] = cp_ref[...]

    incl, _, _ = _tri_masks(chunk)
    tri = tri_ref[...]
    row = lax.broadcasted_iota(jnp.int32, (chunk, 1), 0)
    lane = lax.broadcasted_iota(jnp.int32, (1, LANE), 1)
    rep = M2_HEADS // M2_GROUPS
    seqs = range(bb)
    zg, xc, dt = [], [], []
    live = lane < M2_HEADS
    if valid < chunk:
        live = jnp.logical_and(live, row < valid)
    for i in seqs:
        z = z_ref[i]
        x = z[:, M2_W:M2_W + M2_CONV_DIM]
        xbuf_ref[i, SUBLANE:SUBLANE + chunk, :] = x
        conv = cb_ref[...] + x * cw_ref[M2_CONV - 1:M2_CONV, :]
        for j in range(1, M2_CONV):
            conv = conv + xbuf_ref[i, SUBLANE - j:SUBLANE - j + chunk, :] * cw_ref[M2_CONV - 1 - j:M2_CONV - j, :]
        xbuf_ref[i, 0:SUBLANE, :] = xbuf_ref[i, chunk:chunk + SUBLANE, :]
        zg.append(z[:, 0:M2_W])
        xc.append(_silu(conv))
        dt.append(jnp.where(live, _softplus(z[:, M2_W + M2_CONV_DIM:] + dtb_ref[...]), 0.0))
    a_neg = -jnp.exp(alog_ref[...])
    cum = [_ldot3(tri, x * a_neg) for x in dt]
    cum_t = [_nt3(eye_ref[0:SUBLANE, :], x) for x in cum]
    dt_b = [_rdot2(x, esel_ref[...]) for x in dt]
    grp = [(i, g) for i in seqs for g in range(M2_GROUPS)]
    bg = [xc[i][:, M2_W + g * M2_STATE:M2_W + (g + 1) * M2_STATE] for i, g in grp]
    cg = [xc[i][:, M2_W + (M2_GROUPS + g) * M2_STATE:M2_W + (M2_GROUPS + g + 1) * M2_STATE] for i, g in grp]
    cb = [_bdot_nt(x, y) for x, y in zip(cg, bg)]
    s_grp = [jnp.concatenate([s_ref[i, g * rep + hh] for hh in range(rep)], axis=0) for i, g in grp]
    y_st = [_bdot_nt(x, s) for x, s in zip(cg, s_grp)]
    probs = [(i, h) for i in seqs for h in range(M2_HEADS)]
    ccb = [jnp.broadcast_to(cum[i][:, h:h + 1], (chunk, LANE)) for i, h in probs]
    low = lane < HEAD_DIM
    cum_b = [jnp.concatenate([jnp.where(low, ccb[i * M2_HEADS + 2 * j], ccb[i * M2_HEADS + 2 * j + 1])
                              for j in range(M2_HEADS // 2)], axis=1) for i in seqs]
    xs = [x[:, 0:M2_W] for x in xc]
    xdt = [x * d for x, d in zip(xs, dt_b)]
    xw = [x * jnp.exp(cb_[chunk - 1:chunk, :] - cb_) for x, cb_ in zip(xdt, cum_b)]
    seg = [jnp.where(incl, jnp.exp(jnp.minimum(ccb[j][:, 0:chunk] - cum_t[i][h:h + 1, :], 0.0)), 0.0)
           for j, (i, h) in enumerate(probs)]
    y_in = [_bdot(cb[i * M2_GROUPS + h // rep] * seg[j], xdt[i][:, h * HEAD_DIM:(h + 1) * HEAD_DIM])
            for j, (i, h) in enumerate(probs)]
    s_in = [_bdot_tn(xw[i][:, g * rep * HEAD_DIM:(g + 1) * rep * HEAD_DIM], bg[j]) for j, (i, g) in enumerate(grp)]
    for j, (i, h) in enumerate(probs):
        ybuf_ref[i, :, h * HEAD_DIM:(h + 1) * HEAD_DIM] = y_in[j]
        gj = i * M2_GROUPS + h // rep
        hh = h % rep
        s_ref[i, h] = (s_grp[gj][hh * HEAD_DIM:(hh + 1) * HEAD_DIM] * jnp.exp(cum[i][chunk - 1:chunk, h:h + 1])
                       + s_in[gj][hh * HEAD_DIM:(hh + 1) * HEAD_DIM])
    for i in seqs:
        y_state = jnp.concatenate([y_st[i * M2_GROUPS + g] for g in range(M2_GROUPS)], axis=1)
        y = ybuf_ref[i] + y_state * jnp.exp(cum_b[i]) + dsk_ref[...] * xs[i]
        y_ref[i] = _rms(y * _silu(zg[i]), nw_ref[...]).astype(y_ref.dtype)


def _ssd_call(z, conv_prev, s0, sl, params, layer, bb, chunk, valid, out_dtype):
    b, l, _ = z.shape
    tri = jnp.tril(jnp.ones((chunk, chunk), F32)).astype(BF16)
    esel = (jnp.arange(LANE)[:, None] == jnp.arange(M2_W)[None, :] // HEAD_DIM).astype(BF16)
    consts = (tri, jnp.eye(LANE, dtype=BF16), esel)
    return pl.pallas_call(
        functools.partial(_ssd_kernel, bb=bb, chunk=chunk, valid=valid),
        grid=(b // bb, l // chunk),
        in_specs=[pl.BlockSpec((bb, chunk, M2_PAD), lambda i, c: (i, c, 0)),
                  _state_spec(conv_prev, bb, sl), _state_spec(s0, bb, sl)]
                 + [_layer_spec(a, layer) for a in params] + [_const_spec(a.shape) for a in consts],
        out_specs=[pl.BlockSpec((bb, chunk, M2_W), lambda i, c: (i, c, 0)),
                   pl.BlockSpec((bb, M2_HEADS, HEAD_DIM, M2_STATE), lambda i, c: (i, 0, 0, 0))],
        out_shape=[jax.ShapeDtypeStruct((b, l, M2_W), out_dtype),
                   jax.ShapeDtypeStruct((b, M2_HEADS, HEAD_DIM, M2_STATE), F32)],
        scratch_shapes=[pltpu.VMEM((bb, SUBLANE + chunk, M2_CONV_DIM), F32),
                        pltpu.VMEM((bb, chunk, M2_W), F32)],
        compiler_params=pltpu.CompilerParams(dimension_semantics=("arbitrary", "arbitrary"),
                                             vmem_limit_bytes=VMEM_LIMIT),
        name="ssd_mixer",
    )(z, conv_prev, s0, *params, *consts)


def _HEAD_ONES():
    idx = jnp.arange(RWKV_W) // HEAD_DIM
    return (idx[:, None] == idx[None, :]).astype(BF16)


ROWS = SUBLANE


def _rows_to_tile(rows):
    rid = lax.broadcasted_iota(jnp.int32, (ROWS, 1), 0)
    out = rows[0]
    for j in range(1, ROWS):
        out = jnp.where(rid == j, rows[j], out)
    return out


def _rwkv_dec_kernel(z_ref, s_ref, sh_ref, cols_ref, w2_ref, a2_ref, g2_ref, ones_ref,
                     y_ref, so_ref, vec_ref, ybuf_ref, post_ref):
    h = pl.program_id(0)
    nt = z_ref.shape[0]
    w = RWKV_W
    ones = ones_ref[...]
    col = lambda j: cols_ref[RWKV_PROJ + j * w:RWKV_PROJ + (j + 1) * w, :]

    @pl.when(h == 0)
    def _():
        mu = cols_ref[0:RWKV_PROJ, :]
        prev = sh_ref[...]
        for t in range(nt):
            z = z_ref[t]
            zs = z + mu * (prev - z)
            prev = z
            r, k, v, lora = zs[0:w], zs[w:2 * w], zs[2 * w:3 * w], zs[3 * w:]
            logw = -jnp.exp(-_softplus(-(col(0) + _bdot(w2_ref[...], jnp.tanh(lora)))) - 0.5)
            a = _sigmoid(col(1) + _bdot(a2_ref[...], lora))
            kk = k * col(2)
            kk = kk / jnp.maximum(jnp.sqrt(_ldot3(ones, kk * kk)), 1e-12)
            k = k * (1.0 + (a - 1.0) * col(3))
            for j, x in enumerate((-kk, jnp.exp(logw), kk * a, k, r, v)):
                vec_ref[t, j] = x
            post_ref[t, 0] = _bdot(g2_ref[...], _sigmoid(lora))
            post_ref[t, 1] = _ldot3(ones, r * k * col(4)) * v

    hs = pl.ds(pl.multiple_of(h * HEAD_DIM, HEAD_DIM), HEAD_DIM)

    def body(v8, carry):
        r0 = pl.multiple_of(h * HEAD_DIM + v8 * ROWS, ROWS)
        vt = [vec_ref[t, 5, pl.ds(r0, ROWS), :] for t in range(nt)]
        ys = [[] for _ in range(nt)]
        for j in range(ROWS):
            s = s_ref[0, v8 * ROWS + j]
            for t in range(nt):
                sa = jnp.sum(s * vec_ref[t, 0, hs, :], axis=0, keepdims=True)
                s = s * vec_ref[t, 1, hs, :] + sa * vec_ref[t, 2, hs, :] + vt[t][j:j + 1, :] * vec_ref[t, 3, hs, :]
                ys[t].append(jnp.sum(s * vec_ref[t, 4, hs, :], axis=0, keepdims=True))
            so_ref[0, v8 * ROWS + j] = s
        for t in range(nt):
            ybuf_ref[t, pl.ds(r0, ROWS), :] = _rows_to_tile(ys[t])
        return carry

    lax.fori_loop(0, HEAD_DIM // ROWS, body, 0)

    @pl.when(h == pl.num_programs(0) - 1)
    def _():
        for t in range(nt):
            y = ybuf_ref[t]
            yc = y - _ldot3(ones, y) * (1.0 / HEAD_DIM)
            var = _ldot3(ones, yc * yc) * (1.0 / HEAD_DIM)
            yn = yc * lax.rsqrt(var + RWKV_GN_EPS) * col(5) + col(6)
            y_ref[t] = (yn + post_ref[t, 1]) * post_ref[t, 0]


def _gla_dec_kernel(z_ref, s_ref, cols_ref, gw_ref, ones_ref, y_ref, so_ref, vec_ref, val_ref, obuf_ref):
    h = pl.program_id(0)
    nt = z_ref.shape[0]

    @pl.when(h == 0)
    def _():
        for t in range(nt):
            z = z_ref[t]
            lin = _bdot(gw_ref[...], z[2 * GLA_KEY_W + 2 * GLA_W:]) + cols_ref[0:GLA_KEY_W, :]
            vec_ref[t, 0] = z[0:GLA_KEY_W] * (GLA_DK ** -0.5)
            vec_ref[t, 1] = z[GLA_KEY_W:2 * GLA_KEY_W]
            vec_ref[t, 2] = jnp.exp(-_softplus(-lin) * (1.0 / GLA_TAU))
            val_ref[t] = z[2 * GLA_KEY_W:2 * GLA_KEY_W + GLA_W]

    vs = pl.ds(pl.multiple_of(h * HEAD_DIM, HEAD_DIM), HEAD_DIM)
    v = [val_ref[t, vs, :] for t in range(nt)]

    def body(k8, acc):
        r0 = pl.multiple_of(h * GLA_DK + k8 * ROWS, ROWS)
        q, k, a = ([vec_ref[t, j, pl.ds(r0, ROWS), :] for t in range(nt)] for j in range(3))
        acc = list(acc)
        for j in range(ROWS):
            s = s_ref[0, k8 * ROWS + j]
            for t in range(nt):
                s = s * a[t][j:j + 1, :] + k[t][j:j + 1, :] * v[t]
                acc[t] = acc[t] + q[t][j:j + 1, :] * s
            so_ref[0, k8 * ROWS + j] = s
        return tuple(acc)

    zero = jnp.zeros((HEAD_DIM, z_ref.shape[2]), F32)
    acc = lax.fori_loop(0, GLA_DK // ROWS, body, (zero,) * nt)
    for t in range(nt):
        obuf_ref[t, vs, :] = acc[t]

    @pl.when(h == pl.num_programs(0) - 1)
    def _():
        for t in range(nt):
            o = obuf_ref[t]
            ms = _ldot3(ones_ref[...], o * o) * (1.0 / HEAD_DIM)
            og = z_ref[t, 2 * GLA_KEY_W + GLA_W:2 * GLA_KEY_W + 2 * GLA_W, :]
            y_ref[t] = o * lax.rsqrt(ms + NORM_EPS) * cols_ref[GLA_KEY_W:, :] * _silu(og)


def _ssd_dec_kernel(z_ref, s_ref, cp_ref, cols_ref, y_ref, so_ref, x_ref, bc_ref, dt_ref, ybuf_ref):
    h = pl.program_id(0)
    nt = z_ref.shape[0]
    cd = M2_CONV_DIM
    off_b = M2_CONV * cd
    off_dt = off_b + cd

    @pl.when(h == 0)
    def _():
        xs = [cp_ref[j] for j in range(M2_CONV - 1)] + [z_ref[t, M2_W:M2_W + cd, :] for t in range(nt)]
        a_neg = -jnp.exp(cols_ref[off_dt + ROWS:off_dt + 2 * ROWS, :])
        for t in range(nt):
            conv = cols_ref[off_b:off_b + cd, :]
            for j in range(M2_CONV):
                conv = conv + xs[t + j] * cols_ref[j * cd:(j + 1) * cd, :]
            xc = _silu(conv)
            x_ref[t] = xc[0:M2_W]
            bc_ref[t] = xc[M2_W:]
            dt = _softplus(z_ref[t, M2_W + cd:M2_W + cd + ROWS, :] + cols_ref[off_dt:off_dt + ROWS, :])
            dt_ref[t, 0] = dt
            dt_ref[t, 1] = jnp.exp(dt * a_neg)

    g = h // (M2_HEADS // M2_GROUPS)
    bs = pl.ds(pl.multiple_of(g * M2_STATE, M2_STATE), M2_STATE)
    cs = pl.ds(pl.multiple_of((M2_GROUPS + g) * M2_STATE, M2_STATE), M2_STATE)
    rid = lax.broadcasted_iota(jnp.int32, (ROWS, 1), 0)
    pick = lambda tile: jnp.sum(jnp.where(rid == h, tile, 0.0), axis=0, keepdims=True)
    dt = [pick(dt_ref[t, 0]) for t in range(nt)]
    da = [pick(dt_ref[t, 1]) for t in range(nt)]

    def body(p8, carry):
        r0 = pl.multiple_of(h * HEAD_DIM + p8 * ROWS, ROWS)
        xt = [x_ref[t, pl.ds(r0, ROWS), :] * dt[t] for t in range(nt)]
        ys = [[] for _ in range(nt)]
        for j in range(ROWS):
            s = s_ref[0, p8 * ROWS + j]
            for t in range(nt):
                s = s * da[t] + xt[t][j:j + 1, :] * bc_ref[t, bs, :]
                ys[t].append(jnp.sum(s * bc_ref[t, cs, :], axis=0, keepdims=True))
            so_ref[0, p8 * ROWS + j] = s
        for t in range(nt):
            ybuf_ref[t, pl.ds(r0, ROWS), :] = _rows_to_tile(ys[t])
        return carry

    lax.fori_loop(0, HEAD_DIM // ROWS, body, 0)

    @pl.when(h == pl.num_programs(0) - 1)
    def _():
        off_d = off_dt + 2 * ROWS
        for t in range(nt):
            y = ybuf_ref[t] + cols_ref[off_d:off_d + M2_W, :] * x_ref[t]
            y = y * _silu(z_ref[t, 0:M2_W, :])
            ms = jnp.mean(y * y, axis=0, keepdims=True)
            y_ref[t] = y * lax.rsqrt(ms + NORM_EPS) * cols_ref[off_d + M2_W:off_d + 2 * M2_W, :]


def _dec_call(kern, name, acts, state, sl, layer_params, consts, layer, y_width, scratch):
    nt, _, nb = acts[0].shape
    heads = state.shape[1]
    whole = lambda a: pl.BlockSpec(a.shape, lambda hd: (0,) * a.ndim, pipeline_mode=pl.Buffered(1))
    blk = (1,) + state.shape[2:]
    return pl.pallas_call(
        kern,
        grid=(heads,),
        in_specs=[whole(a) for a in acts]
                 + [pl.BlockSpec((None,) + blk, lambda hd: (sl, hd, 0, 0, 0))]
                 + [_layer_spec(a, layer) for a in layer_params] + [_const_spec(a.shape) for a in consts],
        out_specs=[pl.BlockSpec((nt, y_width, nb), lambda hd: (0, 0, 0)),
                   pl.BlockSpec(blk, lambda hd: (hd, 0, 0, 0))],
        out_shape=[jax.ShapeDtypeStruct((nt, y_width, nb), F32), jax.ShapeDtypeStruct(state.shape[1:], F32)],
        scratch_shapes=scratch,
        compiler_params=pltpu.CompilerParams(dimension_semantics=("arbitrary",), vmem_limit_bytes=VMEM_LIMIT),
        name=name,
    )(*acts, state, *layer_params, *consts)


def _rows(v):
    return v.reshape(v.shape[0], 1, -1).astype(F32)


def _pad_rows(m, top, total):
    return jnp.zeros((m.shape[0], total, m.shape[2]), F32).at[:, top:top + m.shape[1]].set(m).astype(BF16)


def _pad_lanes(v):
    return jnp.zeros((v.shape[0], 1, LANE), F32).at[:, 0, :v.shape[1]].set(v)


def _stacked_params(ffn1_norm, ffn1_w_gate, ffn1_w_up, ffn1_w_down, mix_norm, w_in, rwkv_mu, rwkv_w0, rwkv_w2,
                    rwkv_a0, rwkv_a2, rwkv_g2, rwkv_k_k, rwkv_k_a, rwkv_r_k, rwkv_ln_w, rwkv_ln_b, gla_gate_w2,
                    gla_gate_b, gla_norm, mamba_conv_w, mamba_conv_b, mamba_dt_bias, mamba_A_log, mamba_D, mamba_norm,
                    w_out, ffn2_norm, ffn2_w_gate, ffn2_w_up, ffn2_w_down, ple_norm, ple_w_gate, ple_w_proj):
    depth = w_in.shape[0]
    w = w_in.astype(BF16)
    g0 = RWKV_PROJ
    m0 = RWKV_PROJ + GLA_PROJ
    qkv_end = 2 * GLA_KEY_W + GLA_W
    zeros = lambda n: jnp.zeros((depth, D_MODEL, n), BF16)
    win = jnp.concatenate([
        w[:, :, :g0 + qkv_end], w[:, :, g0 + qkv_end + GLA_GATE_LORA:m0],
        w[:, :, g0 + qkv_end:g0 + qkv_end + GLA_GATE_LORA], zeros(LANE - GLA_GATE_LORA),
        w[:, :, m0:], zeros(LANE - M2_HEADS)], axis=2)
    pre = (_rows(ffn1_norm), ffn1_w_gate.astype(BF16), ffn1_w_up.astype(BF16), ffn1_w_down.astype(BF16),
           _rows(mix_norm), win)
    rwkv = (_rows(rwkv_mu), _rows(rwkv_w0), _pad_rows(rwkv_w2, 0, RWKV_LORA), _rows(rwkv_a0),
            _pad_rows(rwkv_a2, RWKV_DECAY_LORA, RWKV_LORA),
            _pad_rows(rwkv_g2, RWKV_DECAY_LORA + RWKV_A_LORA, RWKV_LORA),
            _rows(rwkv_k_k), _rows(rwkv_k_a), _rows(rwkv_r_k), _rows(rwkv_ln_w), _rows(rwkv_ln_b))
    gla = (_pad_rows(gla_gate_w2, 0, LANE), _rows(gla_gate_b), _rows(jnp.tile(gla_norm, (1, GLA_HEADS))))
    ssd = (mamba_conv_w.astype(F32), _rows(mamba_conv_b), _pad_lanes(mamba_dt_bias), _pad_lanes(mamba_A_log),
           _rows(jnp.repeat(mamba_D, HEAD_DIM, axis=1)), _rows(mamba_norm))
    post = (w_out.astype(BF16), _rows(ffn2_norm), ffn2_w_gate.astype(BF16), ffn2_w_up.astype(BF16),
            ffn2_w_down.astype(BF16), _rows(ple_norm), ple_w_gate.astype(BF16), ple_w_proj.astype(BF16))
    return pre, rwkv, gla, ssd, post


def _decode_params(nb, rwkv_mu, rwkv_w0, rwkv_w2, rwkv_a0, rwkv_a2, rwkv_g2, rwkv_k_k, rwkv_k_a, rwkv_r_k,
                   rwkv_ln_w, rwkv_ln_b, gla_gate_w2, gla_gate_b, gla_norm, mamba_conv_w, mamba_conv_b,
                   mamba_dt_bias, mamba_A_log, mamba_D, mamba_norm):
    depth = rwkv_mu.shape[0]
    slab = lambda vs: jnp.broadcast_to(jnp.concatenate([v.reshape(depth, -1) for v in vs], axis=1)[:, :, None],
                                       (depth, sum(v[0].size for v in vs), nb)).astype(F32)
    tr = lambda m: jnp.swapaxes(m, 1, 2)
    rwkv = (slab((rwkv_mu, rwkv_w0, rwkv_a0, rwkv_k_k, rwkv_k_a, rwkv_r_k, rwkv_ln_w, rwkv_ln_b)),
            tr(_pad_rows(rwkv_w2, 0, RWKV_LORA)), tr(_pad_rows(rwkv_a2, RWKV_DECAY_LORA, RWKV_LORA)),
            tr(_pad_rows(rwkv_g2, RWKV_DECAY_LORA + RWKV_A_LORA, RWKV_LORA)))
    gla = (slab((gla_gate_b, jnp.tile(gla_norm, (1, GLA_HEADS)))), tr(_pad_rows(gla_gate_w2, 0, LANE)))
    ssd = (slab((mamba_conv_w, mamba_conv_b, mamba_dt_bias, mamba_A_log, jnp.repeat(mamba_D, HEAD_DIM, axis=1),
                 mamba_norm)),)
    return rwkv, gla, ssd


def _sample_layer(x, p, state, params, dparams, layer, final_norm, *, batch, seq, tm, final):
    pre, _, _, _, post = params
    d_rwkv, d_gla, d_ssd = dparams
    shift0, wkv0, gla0, conv0, ssm0 = state
    x1, zr, zg, zm = _pre_call(x, layer, *pre, tm)
    tokens_first = lambda z: jnp.transpose(z.reshape(batch, seq, z.shape[-1]), (1, 2, 0))
    zr_t, zg_t, zm_t = tokens_first(zr), tokens_first(zg), tokens_first(zm)
    zr3 = zr.reshape(batch, seq, RWKV_PROJ)
    xbc = zm.reshape(batch, seq, M2_PAD)[:, :, M2_W:M2_W + M2_CONV_DIM]
    shift1 = zr3[:, -1]
    conv1 = jnp.concatenate([conv0[layer], xbc], axis=1)[:, -(M2_CONV - 1):]
    scr = lambda *shape: pltpu.VMEM(shape, F32)
    yr, wkv1 = _dec_call(_rwkv_dec_kernel, "rwkv7_decode", (zr_t,), wkv0, layer,
                         (jnp.swapaxes(shift0, 1, 2),) + d_rwkv, (_HEAD_ONES(),), layer, RWKV_W,
                         [scr(seq, 6, RWKV_W, batch), scr(seq, RWKV_W, batch), scr(seq, 2, RWKV_W, batch)])
    yg, gla1 = _dec_call(_gla_dec_kernel, "gla_decode", (zg_t,), gla0, layer, d_gla, (_HEAD_ONES(),), layer, GLA_W,
                         [scr(seq, 3, GLA_KEY_W, batch), scr(seq, GLA_W, batch), scr(seq, GLA_W, batch)])
    conv_t = jnp.transpose(conv0, (0, 2, 3, 1))
    ym, ssm1 = _dec_call(_ssd_dec_kernel, "ssd_decode", (zm_t,), ssm0, layer, (conv_t,) + d_ssd, (), layer, M2_W,
                         [scr(seq, M2_W, batch), scr(seq, 2 * M2_GROUPS * M2_STATE, batch),
                          scr(seq, 2, ROWS, batch), scr(seq, M2_W, batch)])
    n = batch * seq
    flat = lambda y: jnp.transpose(y, (2, 0, 1)).reshape(n, y.shape[1])
    x2 = _post_call(x1, flat(yr), flat(yg), flat(ym), p, layer, *post, final_norm.reshape(1, -1), tm, final)
    return x2, (shift1, wkv1, gla1, conv1, ssm1)


def _pad_time(a, total):
    return jnp.pad(a, ((0, 0), (0, total - a.shape[1]), (0, 0)))


def _group_layer(x, p, state, sl, params, layer, final_norm, *, batch, seq, chunks, bb, tm, tm_post, y_dtype, final):
    pre, rwkv, gla, ssd, post = params
    shift0, wkv0, gla0, conv0, ssm0 = state
    x1, zr, zg, zm = _pre_call(x, layer, *pre, tm)
    zr = zr.reshape(batch, seq, RWKV_PROJ)
    zg = zg.reshape(batch, seq, GLA_PAD)
    zm = zm.reshape(batch, seq, M2_PAD)
    xbc = zm[:, :, M2_W:M2_W + M2_CONV_DIM]
    shift1 = zr[:, -1]
    conv1 = jnp.concatenate([conv0[sl], xbc[:, -(M2_CONV - 1):]], axis=1)[:, -(M2_CONV - 1):]
    c_r, c_g, c_m = chunks
    pad = max(c_r, c_g, c_m, seq)
    if pad != seq:
        zr, zg, zm = _pad_time(zr, pad), _pad_time(zg, pad), _pad_time(zm, pad)
    valid = lambda ch: seq if seq < ch else ch
    yr, wkv1 = _rwkv_call(zr, shift0[:, :, None, :], wkv0, sl, rwkv, layer, bb, c_r, valid(c_r), y_dtype)
    yg, gla1 = _gla_call(zg, gla0, sl, gla, layer, bb, c_g, valid(c_g), y_dtype)
    ym, ssm1 = _ssd_call(zm, conv0, ssm0, sl, ssd, layer, bb, c_m, valid(c_m), y_dtype)
    n = batch * seq
    flat = lambda y: y[:, :seq].reshape(n, y.shape[-1])
    x2 = _post_call(x1, flat(yr), flat(yg), flat(ym), p, layer, *post, final_norm.reshape(1, -1), tm_post, final)
    return x2, (shift1, wkv1, gla1, conv1, ssm1)


def kernel(x_prompt, x_sample, p_prompt, p_sample, state_rwkv_shift, state_rwkv_wkv, state_gla, state_mamba_conv, state_mamba_ssm, ffn1_norm, ffn1_w_gate, ffn1_w_up, ffn1_w_down, mix_norm, w_in, rwkv_mu, rwkv_w0, rwkv_w2, rwkv_a0, rwkv_a2, rwkv_g2, rwkv_k_k, rwkv_k_a, rwkv_r_k, rwkv_ln_w, rwkv_ln_b, gla_gate_w2, gla_gate_b, gla_norm, mamba_conv_w, mamba_conv_b, mamba_dt_bias, mamba_A_log, mamba_D, mamba_norm, w_out, ffn2_norm, ffn2_w_gate, ffn2_w_up, ffn2_w_down, ple_norm, ple_w_gate, ple_w_proj, final_norm):
    depth = w_in.shape[0]
    nb, seq, _ = x_prompt.shape
    db, dseq, _ = x_sample.shape
    weights = (ffn1_norm, ffn1_w_gate, ffn1_w_up, ffn1_w_down, mix_norm, w_in, rwkv_mu, rwkv_w0, rwkv_w2, rwkv_a0,
               rwkv_a2, rwkv_g2, rwkv_k_k, rwkv_k_a, rwkv_r_k, rwkv_ln_w, rwkv_ln_b, gla_gate_w2, gla_gate_b,
               gla_norm, mamba_conv_w, mamba_conv_b, mamba_dt_bias, mamba_A_log, mamba_D, mamba_norm, w_out,
               ffn2_norm, ffn2_w_gate, ffn2_w_up, ffn2_w_down, ple_norm, ple_w_gate, ple_w_proj)
    xp = x_prompt.reshape(nb * seq, D_MODEL)
    xs = x_sample.reshape(db * dseq, D_MODEL)
    pp = p_prompt.reshape(depth, nb * seq, PLE_DIM)
    ps = p_sample.reshape(depth, db * dseq, PLE_DIM)
    params = _stacked_params(*weights)
    fresh = (jnp.zeros((1, nb, RWKV_PROJ), F32), jnp.zeros((1, nb, RWKV_HEADS, HEAD_DIM, HEAD_DIM), F32),
             jnp.zeros((1, nb, GLA_HEADS, GLA_DK, HEAD_DIM), F32),
             jnp.zeros((1, nb, M2_CONV - 1, M2_CONV_DIM), F32),
             jnp.zeros((1, nb, M2_HEADS, HEAD_DIM, M2_STATE), F32))
    dparams = _decode_params(db, rwkv_mu, rwkv_w0, rwkv_w2, rwkv_a0, rwkv_a2, rwkv_g2, rwkv_k_k, rwkv_k_a, rwkv_r_k,
                             rwkv_ln_w, rwkv_ln_b, gla_gate_w2, gla_gate_b, gla_norm, mamba_conv_w, mamba_conv_b,
                             mamba_dt_bias, mamba_A_log, mamba_D, mamba_norm)
    batch_last = lambda s: jnp.moveaxis(s, 1, -1)
    past = (state_rwkv_shift, batch_last(state_rwkv_wkv), batch_last(state_gla), state_mamba_conv,
            batch_last(state_mamba_ssm))
    p_states, s_states = [], []
    for i in range(depth):
        final = i == depth - 1
        xp, st_p = _group_layer(xp, pp, fresh, 0, params, i, final_norm,
                                batch=nb, seq=seq, chunks=(RWKV_CHUNK, GLA_CHUNK, SSD_CHUNK), bb=8, tm=512,
                                tm_post=1024, y_dtype=BF16, final=final)
        xs, st_s = _sample_layer(xs, ps, past, params, dparams, i, final_norm,
                                 batch=db, seq=dseq, tm=512, final=final)
        p_states.append(st_p)
        s_states.append(st_s)
    stack = lambda sts, j: jnp.stack([s[j] for s in sts])
    s_out = tuple(jnp.moveaxis(stack(s_states, j), -1, 1) if j in (1, 2, 4) else stack(s_states, j)
                  for j in range(5))
    return ((xp.reshape(nb, seq, D_MODEL), xs.reshape(db, dseq, D_MODEL))
            + tuple(stack(p_states, j) for j in range(5)) + s_out)
```

```python
import functools

import jax
import jax.numpy as jnp
from jax import lax
from jax.experimental import pallas as pl
from jax.experimental.pallas import tpu as pltpu

F32 = jnp.float32
BF16 = jnp.bfloat16

D_MODEL = 1024
D_FF = 2816
PLE_DIM = 256
HEAD_DIM = 64
NORM_EPS = 1e-6

RWKV_W = 256
RWKV_HEADS = 4
RWKV_GN_EPS = 64e-5
RWKV_PROJ = 896
RWKV_LORA = 128
RWKV_DECAY_LORA = 32
RWKV_A_LORA = 32

GLA_W = 256
GLA_HEADS = 4
GLA_DK = 32
GLA_KEY_W = 128
GLA_GATE_LORA = 16
GLA_TAU = 16.0
GLA_PROJ = 784
GLA_PAD = 896

M2_W = 512
M2_HEADS = 8
M2_STATE = 64
M2_GROUPS = 2
M2_CONV = 4
M2_CONV_DIM = 768
M2_PROJ = 1288
M2_PAD = 1408

LANE = 128
SUBLANE = 8
FF_CHUNK = 256
VMEM_LIMIT = 56 * 1024 * 1024

RWKV_CHUNK = 128
GLA_CHUNK = 64
SSD_CHUNK = 128


def _bdot(a, b):
    return jnp.dot(a.astype(BF16), b.astype(BF16), preferred_element_type=F32)


def _bdot_nt(a, b):
    return lax.dot_general(a.astype(BF16), b.astype(BF16), (((1,), (1,)), ((), ())),
                           preferred_element_type=F32)


def _bdot_tn(a, b):
    return lax.dot_general(a.astype(BF16), b.astype(BF16), (((0,), (0,)), ((), ())),
                           preferred_element_type=F32)


def _split3(x):
    hi = x.astype(BF16)
    r1 = x - hi.astype(F32)
    mid = r1.astype(BF16)
    lo = (r1 - mid.astype(F32)).astype(BF16)
    return hi, mid, lo


def _ldot3(mat, x):
    return sum(jnp.dot(mat, p, preferred_element_type=F32) for p in _split3(x))


def _rdot2(x, mat):
    return sum(jnp.dot(p, mat, preferred_element_type=F32) for p in _split3(x)[:2])


def _nt3(mat, x):
    return sum(lax.dot_general(mat, p, (((1,), (1,)), ((), ())), preferred_element_type=F32)
               for p in _split3(x))


def _softplus(x):
    return jnp.maximum(x, 0.0) + jnp.log1p(jnp.exp(-jnp.abs(x)))


def _sigmoid(x):
    return 1.0 / (1.0 + jnp.exp(-x))


def _silu(x):
    return x * _sigmoid(x)


def _rms(x, w):
    return x * lax.rsqrt(jnp.mean(x * x, axis=-1, keepdims=True) + NORM_EPS) * w


def _const_spec(shape):
    nd = len(shape)
    return pl.BlockSpec(shape, lambda *_: (0,) * nd, pipeline_mode=pl.Buffered(1))


def _layer_spec(a, layer):
    nd = a.ndim - 1
    return pl.BlockSpec((None,) + a.shape[1:], lambda *_: (layer,) + (0,) * nd, pipeline_mode=pl.Buffered(1))


def _ffn(x, nw, wg_ref, wu_ref, wd_ref, act_ref):
    xn = _rms(x, nw).astype(BF16)
    for c in range(D_FF // FF_CHUNK):
        sl = slice(c * FF_CHUNK, (c + 1) * FF_CHUNK)
        hg = jnp.dot(xn, wg_ref[:, sl], preferred_element_type=F32)
        hu = jnp.dot(xn, wu_ref[:, sl], preferred_element_type=F32)
        act_ref[:, sl] = (_silu(hg) * hu).astype(BF16)
    return x + 0.5 * jnp.dot(act_ref[...], wd_ref[...], preferred_element_type=F32)


def _pre_kernel(x_ref, n1_ref, wg_ref, wu_ref, wd_ref, nm_ref, win_ref,
                x1_ref, zr_ref, zg_ref, zm_ref, act_ref, *, steps):
    x1 = _ffn(x_ref[...], n1_ref[...], wg_ref, wu_ref, wd_ref, act_ref)
    x1_ref[...] = x1
    h = _rms(x1, nm_ref[...]).astype(BF16)
    cols = (slice(0, RWKV_PROJ), slice(RWKV_PROJ, RWKV_PROJ + GLA_PAD), slice(RWKV_PROJ + GLA_PAD, None))
    for ref, cs in zip((zr_ref, zg_ref, zm_ref), cols):
        z = jnp.dot(h, win_ref[:, cs], preferred_element_type=F32)
        if steps:
            nb = z.shape[0] // steps
            for t in range(steps):
                ref[t] = z[t * nb:(t + 1) * nb].T
        else:
            ref[...] = z


def _pre_call(x, layer, n1, wg, wu, wd, nm, win, tm, steps=0):
    n = x.shape[0]
    tok = lambda w: pl.BlockSpec((tm, w), lambda i: (i, 0))
    if steps:
        assert n == tm and n % steps == 0
        z_spec = lambda w: pl.BlockSpec((steps, w, n // steps), lambda i: (0, 0, 0))
        z_shape = lambda w: jax.ShapeDtypeStruct((steps, w, n // steps), F32)
    else:
        z_spec = tok
        z_shape = lambda w: jax.ShapeDtypeStruct((n, w), F32)
    return pl.pallas_call(
        functools.partial(_pre_kernel, steps=steps),
        grid=(n // tm,),
        in_specs=[tok(D_MODEL)] + [_layer_spec(a, layer) for a in (n1, wg, wu, wd, nm, win)],
        out_specs=[tok(D_MODEL), z_spec(RWKV_PROJ), z_spec(GLA_PAD), z_spec(M2_PAD)],
        out_shape=[jax.ShapeDtypeStruct((n, D_MODEL), F32), z_shape(RWKV_PROJ), z_shape(GLA_PAD), z_shape(M2_PAD)],
        scratch_shapes=[pltpu.VMEM((tm, D_FF), BF16)],
        compiler_params=pltpu.CompilerParams(dimension_semantics=("parallel",), vmem_limit_bytes=VMEM_LIMIT),
        name="pre_ffn_inproj",
    )(x, n1, wg, wu, wd, nm, win)


def _post_kernel(x_ref, yr_ref, yg_ref, ym_ref, p_ref, wo_ref, n2_ref, wg_ref, wu_ref, wd_ref,
                 np_ref, pg_ref, pp_ref, nf_ref, o_ref, act_ref, *, final, steps):
    rows = lambda ref: (jnp.concatenate([ref[t].T for t in range(steps)], axis=0) if steps else ref[...])
    x = x_ref[...]
    x = x + jnp.dot(rows(yr_ref).astype(BF16), wo_ref[0:RWKV_W, :], preferred_element_type=F32)
    x = x + jnp.dot(rows(yg_ref).astype(BF16), wo_ref[RWKV_W:RWKV_W + GLA_W, :], preferred_element_type=F32)
    x = x + jnp.dot(rows(ym_ref).astype(BF16), wo_ref[RWKV_W + GLA_W:, :], preferred_element_type=F32)
    x = _ffn(x, n2_ref[...], wg_ref, wu_ref, wd_ref, act_ref)
    gate = _sigmoid(jnp.dot(_rms(x, np_ref[...]).astype(BF16), pg_ref[...], preferred_element_type=F32))
    x = x + gate * jnp.dot(p_ref[...].astype(BF16), pp_ref[...], preferred_element_type=F32)
    if final:
        x = _rms(x, nf_ref[...])
    o_ref[...] = x


def _post_call(x, yr, yg, ym, p, layer, wo, n2, wg, wu, wd, npn, pg, pp, nf, tm, final, steps=0):
    n = x.shape[0]
    tok = lambda w: pl.BlockSpec((tm, w), lambda i: (i, 0))
    if steps:
        assert n == tm and n % steps == 0
        y_spec = lambda w: pl.BlockSpec((steps, w, n // steps), lambda i: (0, 0, 0))
    else:
        y_spec = tok
    params = (wo, n2, wg, wu, wd, npn, pg, pp)
    consts = params + (nf,)
    return pl.pallas_call(
        functools.partial(_post_kernel, final=final, steps=steps),
        grid=(n // tm,),
        in_specs=[tok(D_MODEL), y_spec(RWKV_W), y_spec(GLA_W), y_spec(M2_W),
                  pl.BlockSpec((None, tm, PLE_DIM), lambda i: (layer, i, 0))]
                 + [_layer_spec(a, layer) for a in params] + [_const_spec(nf.shape)],
        out_specs=tok(D_MODEL),
        out_shape=jax.ShapeDtypeStruct((n, D_MODEL), F32),
        scratch_shapes=[pltpu.VMEM((tm, D_FF), BF16)],
        compiler_params=pltpu.CompilerParams(dimension_semantics=("parallel",), vmem_limit_bytes=VMEM_LIMIT),
        name="post_outproj_ffn_ple",
    )(x, yr, yg, ym, p, *consts)


def _tri_masks(n):
    ri = lax.broadcasted_iota(jnp.int32, (n, n), 0)
    ci = lax.broadcasted_iota(jnp.int32, (n, n), 1)
    return ri >= ci, ri > ci, (ri == ci).astype(F32)


def _rwkv_kernel(z_ref, sh_ref, s0_ref, mu_ref, w0_ref, w2_ref, a0_ref, a2_ref, g2_ref, kk_ref, ka_ref,
                 rk_ref, lnw_ref, lnb_ref, tri_ref, ones_ref,
                 y_ref, s_ref, zbuf_ref, ybuf_ref, *, bb, chunk, valid):
    c = pl.program_id(1)

    @pl.when(c == 0)
    def _():
        s_ref[...] = s0_ref[...]
        zbuf_ref[:, SUBLANE - 1:SUBLANE, :] = sh_ref[...]

    incl, strict, eye = _tri_masks(chunk)
    tri = tri_ref[...]
    ones = ones_ref[...]
    row = lax.broadcasted_iota(jnp.int32, (chunk, 1), 0)
    mu = mu_ref[...]
    seqs = range(bb)
    zs = []
    for i in seqs:
        z = z_ref[i]
        zbuf_ref[i, SUBLANE:SUBLANE + chunk, :] = z
        prev = zbuf_ref[i, SUBLANE - 1:SUBLANE - 1 + chunk, :]
        zbuf_ref[i, SUBLANE - 1:SUBLANE, :] = z[chunk - 1:chunk, :]
        zs.append(z + mu * (prev - z))
    r = [x[:, 0:RWKV_W] for x in zs]
    k = [x[:, RWKV_W:2 * RWKV_W] for x in zs]
    v = [x[:, 2 * RWKV_W:3 * RWKV_W] for x in zs]
    lora = [x[:, 3 * RWKV_W:] for x in zs]
    w_lin = [_bdot(jnp.tanh(x), w2_ref[...]) for x in lora]
    a_lin = [_bdot(x, a2_ref[...]) for x in lora]
    gate = [_bdot(_sigmoid(x), g2_ref[...]) for x in lora]
    kk = [x * kk_ref[...] for x in k]
    kk_ss = [_rdot2(x * x, ones) for x in kk]
    logw = [-jnp.exp(-_softplus(-(w0_ref[...] + x)) - 0.5) for x in w_lin]
    a = [_sigmoid(a0_ref[...] + x) for x in a_lin]
    kk = [x / jnp.maximum(jnp.sqrt(ss), 1e-12) for x, ss in zip(kk, kk_ss)]
    k = [x * (1.0 + (ai - 1.0) * ka_ref[...]) for x, ai in zip(k, a)]
    ahat = [-x for x in kk]
    bhat = [x * ai for x, ai in zip(kk, a)]
    if valid < chunk:
        vm = row < valid
        mask = lambda xs: [jnp.where(vm, x, 0.0) for x in xs]
        logw, ahat, bhat, k, v = mask(logw), mask(ahat), mask(bhat), mask(k), mask(v)
    g = [_ldot3(tri, x) for x in logw]
    glast = [x[chunk - 1:chunk, :] for x in g]
    e_ng = [jnp.exp(-x) for x in g]
    e_gl = [jnp.exp(gl - x) for gl, x in zip(glast, g)]
    a_l = [ah * jnp.exp(x - lw) for ah, x, lw in zip(ahat, g, logw)]
    r_l = [ri * jnp.exp(x) for ri, x in zip(r, g)]
    b_r = [x * e for x, e in zip(bhat, e_ng)]
    k_r = [x * e for x, e in zip(k, e_ng)]
    k_p = [x * e for x, e in zip(k, e_gl)]
    b_p = [x * e for x, e in zip(bhat, e_gl)]
    dec = [jnp.exp(x) for x in glast]

    probs = [(i, h) for i in seqs for h in range(RWKV_HEADS)]
    hd = lambda xs: [xs[i][:, h * HEAD_DIM:(h + 1) * HEAD_DIM] for i, h in probs]
    a_lh, r_lh, b_rh, k_rh, k_ph, b_ph, vh = hd(a_l), hd(r_l), hd(b_r), hd(k_r), hd(k_p), hd(b_p), hd(v)
    ar = [jnp.concatenate([x, y], axis=0) for x, y in zip(a_lh, r_lh)]
    bk = [jnp.concatenate([x, y], axis=0) for x, y in zip(b_rh, k_rh)]
    gm = [_bdot_nt(x, y) for x, y in zip(ar, bk)]
    a_ab = [jnp.where(strict, x[:chunk, :chunk], 0.0) for x in gm]
    a_ak = [jnp.where(strict, x[:chunk, chunk:], 0.0) for x in gm]
    a_rb = [jnp.where(incl, x[chunk:, :chunk], 0.0) for x in gm]
    a_rk = [jnp.where(incl, x[chunk:, chunk:], 0.0) for x in gm]
    t = [eye + m for m in a_ab]
    x = [_bdot(m, m) for m in a_ab]
    p = 2
    while p < chunk:
        if 2 * p < chunk:
            xt = [_bdot(jnp.concatenate([xi, ti], axis=0), xi) for xi, ti in zip(x, t)]
            x = [m[:chunk] for m in xt]
            t = [ti + m[chunk:] for ti, m in zip(t, xt)]
        else:
            t = [ti + _bdot(ti, xi) for xi, ti in zip(x, t)]
        p *= 2
    akv = [_bdot(m, vi) for m, vi in zip(a_ak, vh)]
    w_t = [_bdot(ti, m) for ti, m in zip(t, a_lh)]
    u_t = [_bdot(ti, m) for ti, m in zip(t, akv)]
    r_t = [ri + _bdot(m, wi) for ri, m, wi in zip(r_lh, a_rb, w_t)]
    y_t = [_bdot(m, vi) + _bdot(n, ui) for m, vi, n, ui in zip(a_rk, vh, a_rb, u_t)]
    m_s = [_bdot_tn(wi, bi) for wi, bi in zip(w_t, b_ph)]
    q_s = [_bdot_tn(jnp.concatenate([vi, ui], axis=0), jnp.concatenate([ki, bi], axis=0))
           for vi, ui, ki, bi in zip(vh, u_t, k_ph, b_ph)]
    s_old = [s_ref[i, h] for i, h in probs]
    y_h = [_bdot_nt(ri, si) + yi for ri, si, yi in zip(r_t, s_old, y_t)]
    s_m = [_bdot(si, mi) for si, mi in zip(s_old, m_s)]
    for j, (i, h) in enumerate(probs):
        hs = slice(h * HEAD_DIM, (h + 1) * HEAD_DIM)
        ybuf_ref[i, :, hs] = y_h[j]
        s_ref[i, h] = s_old[j] * dec[i][:, hs] + s_m[j] + q_s[j]
    y = [ybuf_ref[i] for i in seqs]
    mean = [_rdot2(x, ones) * (1.0 / HEAD_DIM) for x in y]
    yc = [x - m for x, m in zip(y, mean)]
    var = [_rdot2(x * x, ones) * (1.0 / HEAD_DIM) for x in yc]
    bonus = [_rdot2(ri * ki * rk_ref[...], ones) * vi for ri, ki, vi in zip(r, k, v)]
    for i in seqs:
        yn = yc[i] * lax.rsqrt(var[i] + RWKV_GN_EPS) * lnw_ref[...] + lnb_ref[...]
        y_ref[i] = ((yn + bonus[i]) * gate[i]).astype(y_ref.dtype)


def _state_spec(a, bb, sl):
    nd = a.ndim - 2
    return pl.BlockSpec((None, bb) + a.shape[2:], lambda i, c: (sl, i) + (0,) * nd)


def _rwkv_call(z, shift, s0, sl, params, layer, bb, chunk, valid, out_dtype):
    b, l, _ = z.shape
    grid = (b // bb, l // chunk)
    tri = jnp.tril(jnp.ones((chunk, chunk), F32)).astype(BF16)
    consts = (tri, _HEAD_ONES())
    return pl.pallas_call(
        functools.partial(_rwkv_kernel, bb=bb, chunk=chunk, valid=valid),
        grid=grid,
        in_specs=[pl.BlockSpec((bb, chunk, RWKV_PROJ), lambda i, c: (i, c, 0)),
                  _state_spec(shift, bb, sl), _state_spec(s0, bb, sl)]
                 + [_layer_spec(a, layer) for a in params] + [_const_spec(a.shape) for a in consts],
        out_specs=[pl.BlockSpec((bb, chunk, RWKV_W), lambda i, c: (i, c, 0)),
                   pl.BlockSpec((bb, RWKV_HEADS, HEAD_DIM, HEAD_DIM), lambda i, c: (i, 0, 0, 0))],
        out_shape=[jax.ShapeDtypeStruct((b, l, RWKV_W), out_dtype),
                   jax.ShapeDtypeStruct((b, RWKV_HEADS, HEAD_DIM, HEAD_DIM), F32)],
        scratch_shapes=[pltpu.VMEM((bb, SUBLANE + chunk, RWKV_PROJ), F32),
                        pltpu.VMEM((bb, chunk, RWKV_W), F32)],
        compiler_params=pltpu.CompilerParams(dimension_semantics=("arbitrary", "arbitrary"),
                                             vmem_limit_bytes=VMEM_LIMIT),
        name="rwkv7_mixer",
    )(z, shift, s0, *params, *consts)


def _gla_kernel(z_ref, s0_ref, gw_ref, gb_ref, nw_ref, tri_ref, ones_ref, eye_ref,
                y_ref, s_ref, obuf_ref, *, bb, chunk, valid):
    c = pl.program_id(1)

    @pl.when(c == 0)
    def _():
        s_ref[...] = s0_ref[...]

    incl, _, _ = _tri_masks(chunk)
    tri = tri_ref[...]
    ones = ones_ref[...]
    row = lax.broadcasted_iota(jnp.int32, (chunk, 1), 0)
    seqs = range(bb)
    z = [z_ref[i] for i in seqs]
    q = [x[:, 0:GLA_KEY_W] * (GLA_DK ** -0.5) for x in z]
    k = [x[:, GLA_KEY_W:2 * GLA_KEY_W] for x in z]
    v = [x[:, 2 * GLA_KEY_W:2 * GLA_KEY_W + GLA_W] for x in z]
    og = [x[:, 2 * GLA_KEY_W + GLA_W:2 * GLA_KEY_W + 2 * GLA_W] for x in z]
    lin = [_bdot(x[:, 2 * GLA_KEY_W + 2 * GLA_W:], gw_ref[...]) for x in z]
    log_a = [-_softplus(-(x + gb_ref[...])) * (1.0 / GLA_TAU) for x in lin]
    if valid < chunk:
        vm = row < valid
        log_a = [jnp.where(vm, x, 0.0) for x in log_a]
        k = [jnp.where(vm, x, 0.0) for x in k]
    b = [_ldot3(tri, x) for x in log_a]
    blast = [x[chunk - 1:chunk, :] for x in b]
    qe = [x * jnp.exp(bi) for x, bi in zip(q, b)]
    ke = [x * jnp.exp(-bi) for x, bi in zip(k, b)]
    kl = [x * jnp.exp(bl - bi) for x, bl, bi in zip(k, blast, b)]
    dec = [_nt3(eye_ref[...], jnp.broadcast_to(jnp.exp(bl), (HEAD_DIM, GLA_KEY_W))) for bl in blast]

    probs = [(i, h) for i in seqs for h in range(GLA_HEADS)]
    kd = lambda xs: [xs[i][:, h * GLA_DK:(h + 1) * GLA_DK] for i, h in probs]
    qeh, keh, klh = kd(qe), kd(ke), kd(kl)
    vh = [v[i][:, h * HEAD_DIM:(h + 1) * HEAD_DIM] for i, h in probs]
    att = [jnp.where(incl, _bdot_nt(x, y), 0.0) for x, y in zip(qeh, keh)]
    s_old = [s_ref[i, h] for i, h in probs]
    o_h = [_bdot(ai, vi) + _bdot(qi, si) for ai, vi, qi, si in zip(att, vh, qeh, s_old)]
    kv = [_bdot_tn(ki, vi) for ki, vi in zip(klh, vh)]
    for j, (i, h) in enumerate(probs):
        obuf_ref[i, :, h * HEAD_DIM:(h + 1) * HEAD_DIM] = o_h[j]
        s_ref[i, h] = s_old[j] * dec[i][h * GLA_DK:(h + 1) * GLA_DK, :] + kv[j]
    o = [obuf_ref[i] for i in seqs]
    ms = [_rdot2(x * x, ones) * (1.0 / HEAD_DIM) for x in o]
    for i in seqs:
        y_ref[i] = (o[i] * lax.rsqrt(ms[i] + NORM_EPS) * nw_ref[...] * _silu(og[i])).astype(y_ref.dtype)


def _gla_call(z, s0, sl, params, layer, bb, chunk, valid, out_dtype):
    b, l, _ = z.shape
    tri = jnp.tril(jnp.ones((chunk, chunk), F32)).astype(BF16)
    consts = (tri, _HEAD_ONES(), jnp.eye(LANE, dtype=BF16))
    return pl.pallas_call(
        functools.partial(_gla_kernel, bb=bb, chunk=chunk, valid=valid),
        grid=(b // bb, l // chunk),
        in_specs=[pl.BlockSpec((bb, chunk, GLA_PAD), lambda i, c: (i, c, 0)), _state_spec(s0, bb, sl)]
                 + [_layer_spec(a, layer) for a in params] + [_const_spec(a.shape) for a in consts],
        out_specs=[pl.BlockSpec((bb, chunk, GLA_W), lambda i, c: (i, c, 0)),
                   pl.BlockSpec((bb, GLA_HEADS, GLA_DK, HEAD_DIM), lambda i, c: (i, 0, 0, 0))],
        out_shape=[jax.ShapeDtypeStruct((b, l, GLA_W), out_dtype),
                   jax.ShapeDtypeStruct((b, GLA_HEADS, GLA_DK, HEAD_DIM), F32)],
        scratch_shapes=[pltpu.VMEM((bb, chunk, GLA_W), F32)],
        compiler_params=pltpu.CompilerParams(dimension_semantics=("arbitrary", "arbitrary"),
                                             vmem_limit_bytes=VMEM_LIMIT),
        name="gla_mixer",
    )(z, s0, *params, *consts)


def _ssd_kernel(z_ref, cp_ref, s0_ref, cw_ref, cb_ref, dtb_ref, alog_ref, dsk_ref, nw_ref, tri_ref, eye_ref,
                esel_ref, y_ref, s_ref, xbuf_ref, ybuf_ref, *, bb, chunk, valid):
    c = pl.program_id(1)

    @pl.when(c == 0)
    def _():
        s_ref[...] = s0_ref[...]
        xbuf_ref[:, SUBLANE - (M2_CONV - 1):SUBLANE, :] = cp_ref[...]

    incl, _, _ = _tri_masks(chunk)
    tri = tri_ref[...]
    row = lax.broadcasted_iota(jnp.int32, (chunk, 1), 0)
    lane = lax.broadcasted_iota(jnp.int32, (1, LANE), 1)
    rep = M2_HEADS // M2_GROUPS
    seqs = range(bb)
    zg, xc, dt = [], [], []
    live = lane < M2_HEADS
    if valid < chunk:
        live = jnp.logical_and(live, row < valid)
    for i in seqs:
        z = z_ref[i]
        x = z[:, M2_W:M2_W + M2_CONV_DIM]
        xbuf_ref[i, SUBLANE:SUBLANE + chunk, :] = x
        conv = cb_ref[...] + x * cw_ref[M2_CONV - 1:M2_CONV, :]
        for j in range(1, M2_CONV):
            conv = conv + xbuf_ref[i, SUBLANE - j:SUBLANE - j + chunk, :] * cw_ref[M2_CONV - 1 - j:M2_CONV - j, :]
        xbuf_ref[i, 0:SUBLANE, :] = xbuf_ref[i, chunk:chunk + SUBLANE, :]
        zg.append(z[:, 0:M2_W])
        xc.append(_silu(conv))
        dt.append(jnp.where(live, _softplus(z[:, M2_W + M2_CONV_DIM:] + dtb_ref[...]), 0.0))
    a_neg = -jnp.exp(alog_ref[...])
    cum = [_ldot3(tri, x * a_neg) for x in dt]
    cum_t = [_nt3(eye_ref[0:SUBLANE, :], x) for x in cum]
    dt_b = [_rdot2(x, esel_ref[...]) for x in dt]
    grp = [(i, g) for i in seqs for g in range(M2_GROUPS)]
    bg = [xc[i][:, M2_W + g * M2_STATE:M2_W + (g + 1) * M2_STATE] for i, g in grp]
    cg = [xc[i][:, M2_W + (M2_GROUPS + g) * M2_STATE:M2_W + (M2_GROUPS + g + 1) * M2_STATE] for i, g in grp]
    cb = [_bdot_nt(x, y) for x, y in zip(cg, bg)]
    s_grp = [jnp.concatenate([s_ref[i, g * rep + hh] for hh in range(rep)], axis=0) for i, g in grp]
    y_st = [_bdot_nt(x, s) for x, s in zip(cg, s_grp)]
    probs = [(i, h) for i in seqs for h in range(M2_HEADS)]
    ccb = [jnp.broadcast_to(cum[i][:, h:h + 1], (chunk, LANE)) for i, h in probs]
    low = lane < HEAD_DIM
    cum_b = [jnp.concatenate([jnp.where(low, ccb[i * M2_HEADS + 2 * j], ccb[i * M2_HEADS + 2 * j + 1])
                              for j in range(M2_HEADS // 2)], axis=1) for i in seqs]
    xs = [x[:, 0:M2_W] for x in xc]
    xdt = [x * d for x, d in zip(xs, dt_b)]
    xw = [x * jnp.exp(cb_[chunk - 1:chunk, :] - cb_) for x, cb_ in zip(xdt, cum_b)]
    seg = [jnp.where(incl, jnp.exp(jnp.minimum(ccb[j][:, 0:chunk] - cum_t[i][h:h + 1, :], 0.0)), 0.0)
           for j, (i, h) in enumerate(probs)]
    y_in = [_bdot(cb[i * M2_GROUPS + h // rep] * seg[j], xdt[i][:, h * HEAD_DIM:(h + 1) * HEAD_DIM])
            for j, (i, h) in enumerate(probs)]
    s_in = [_bdot_tn(xw[i][:, g * rep * HEAD_DIM:(g + 1) * rep * HEAD_DIM], bg[j]) for j, (i, g) in enumerate(grp)]
    for j, (i, h) in enumerate(probs):
        ybuf_ref[i, :, h * HEAD_DIM:(h + 1) * HEAD_DIM] = y_in[j]
        gj = i * M2_GROUPS + h // rep
        hh = h % rep
        s_ref[i, h] = (s_grp[gj][hh * HEAD_DIM:(hh + 1) * HEAD_DIM] * jnp.exp(cum[i][chunk - 1:chunk, h:h + 1])
                       + s_in[gj][hh * HEAD_DIM:(hh + 1) * HEAD_DIM])
    for i in seqs:
        y_state = jnp.concatenate([y_st[i * M2_GROUPS + g] for g in range(M2_GROUPS)], axis=1)
        y = ybuf_ref[i] + y_state * jnp.exp(cum_b[i]) + dsk_ref[...] * xs[i]
        y_ref[i] = _rms(y * _silu(zg[i]), nw_ref[...]).astype(y_ref.dtype)


def _ssd_call(z, conv_prev, s0, sl, params, layer, bb, chunk, valid, out_dtype):
    b, l, _ = z.shape
    tri = jnp.tril(jnp.ones((chunk, chunk), F32)).astype(BF16)
    esel = (jnp.arange(LANE)[:, None] == jnp.arange(M2_W)[None, :] // HEAD_DIM).astype(BF16)
    consts = (tri, jnp.eye(LANE, dtype=BF16), esel)
    return pl.pallas_call(
        functools.partial(_ssd_kernel, bb=bb, chunk=chunk, valid=valid),
        grid=(b // bb, l // chunk),
        in_specs=[pl.BlockSpec((bb, chunk, M2_PAD), lambda i, c: (i, c, 0)),
                  _state_spec(conv_prev, bb, sl), _state_spec(s0, bb, sl)]
                 + [_layer_spec(a, layer) for a in params] + [_const_spec(a.shape) for a in consts],
        out_specs=[pl.BlockSpec((bb, chunk, M2_W), lambda i, c: (i, c, 0)),
                   pl.BlockSpec((bb, M2_HEADS, HEAD_DIM, M2_STATE), lambda i, c: (i, 0, 0, 0))],
        out_shape=[jax.ShapeDtypeStruct((b, l, M2_W), out_dtype),
                   jax.ShapeDtypeStruct((b, M2_HEADS, HEAD_DIM, M2_STATE), F32)],
        scratch_shapes=[pltpu.VMEM((bb, SUBLANE + chunk, M2_CONV_DIM), F32),
                        pltpu.VMEM((bb, chunk, M2_W), F32)],
        compiler_params=pltpu.CompilerParams(dimension_semantics=("arbitrary", "arbitrary"),
                                             vmem_limit_bytes=VMEM_LIMIT),
        name="ssd_mixer",
    )(z, conv_prev, s0, *params, *consts)


def _HEAD_ONES():
    idx = jnp.arange(RWKV_W) // HEAD_DIM
    return (idx[:, None] == idx[None, :]).astype(BF16)


ROWS = SUBLANE


def _rows_to_tile(rows):
    rid = lax.broadcasted_iota(jnp.int32, (ROWS, 1), 0)
    out = rows[0]
    for j in range(1, ROWS):
        out = jnp.where(rid == j, rows[j], out)
    return out


def _rwkv_dec_kernel(z_ref, s_ref, sh_ref, cols_ref, w2_ref, a2_ref, g2_ref, ones_ref,
                     y_ref, so_ref, vec_ref, ybuf_ref, post_ref):
    h = pl.program_id(0)
    nt = z_ref.shape[0]
    w = RWKV_W
    ones = ones_ref[...]
    col = lambda j: cols_ref[RWKV_PROJ + j * w:RWKV_PROJ + (j + 1) * w, :]

    @pl.when(h == 0)
    def _():
        mu = cols_ref[0:RWKV_PROJ, :]
        prev = sh_ref[...]
        for t in range(nt):
            z = z_ref[t]
            zs = z + mu * (prev - z)
            prev = z
            r, k, v, lora = zs[0:w], zs[w:2 * w], zs[2 * w:3 * w], zs[3 * w:]
            logw = -jnp.exp(-_softplus(-(col(0) + _bdot(w2_ref[...], jnp.tanh(lora)))) - 0.5)
            a = _sigmoid(col(1) + _bdot(a2_ref[...], lora))
            kk = k * col(2)
            kk = kk / jnp.maximum(jnp.sqrt(_ldot3(ones, kk * kk)), 1e-12)
            k = k * (1.0 + (a - 1.0) * col(3))
            for j, x in enumerate((-kk, jnp.exp(logw), kk * a, k, r, v)):
                vec_ref[t, j] = x
            post_ref[t, 0] = _bdot(g2_ref[...], _sigmoid(lora))
            post_ref[t, 1] = _ldot3(ones, r * k * col(4)) * v

    hs = pl.ds(pl.multiple_of(h * HEAD_DIM, HEAD_DIM), HEAD_DIM)

    def body(v8, carry):
        r0 = pl.multiple_of(h * HEAD_DIM + v8 * ROWS, ROWS)
        vt = [vec_ref[t, 5, pl.ds(r0, ROWS), :] for t in range(nt)]
        ys = [[] for _ in range(nt)]
        for j in range(ROWS):
            s = s_ref[0, v8 * ROWS + j]
            for t in range(nt):
                sa = jnp.sum(s * vec_ref[t, 0, hs, :], axis=0, keepdims=True)
                s = s * vec_ref[t, 1, hs, :] + sa * vec_ref[t, 2, hs, :] + vt[t][j:j + 1, :] * vec_ref[t, 3, hs, :]
                ys[t].append(jnp.sum(s * vec_ref[t, 4, hs, :], axis=0, keepdims=True))
            so_ref[0, v8 * ROWS + j] = s
        for t in range(nt):
            ybuf_ref[t, pl.ds(r0, ROWS), :] = _rows_to_tile(ys[t])
        return carry

    lax.fori_loop(0, HEAD_DIM // ROWS, body, 0)

    @pl.when(h == pl.num_programs(0) - 1)
    def _():
        for t in range(nt):
            y = ybuf_ref[t]
            yc = y - _ldot3(ones, y) * (1.0 / HEAD_DIM)
            var = _ldot3(ones, yc * yc) * (1.0 / HEAD_DIM)
            yn = yc * lax.rsqrt(var + RWKV_GN_EPS) * col(5) + col(6)
            y_ref[t] = (yn + post_ref[t, 1]) * post_ref[t, 0]


def _gla_dec_kernel(z_ref, s_ref, cols_ref, gw_ref, ones_ref, y_ref, so_ref, vec_ref, val_ref, obuf_ref):
    h = pl.program_id(0)
    nt = z_ref.shape[0]

    @pl.when(h == 0)
    def _():
        for t in range(nt):
            z = z_ref[t]
            lin = _bdot(gw_ref[...], z[2 * GLA_KEY_W + 2 * GLA_W:]) + cols_ref[0:GLA_KEY_W, :]
            vec_ref[t, 0] = z[0:GLA_KEY_W] * (GLA_DK ** -0.5)
            vec_ref[t, 1] = z[GLA_KEY_W:2 * GLA_KEY_W]
            vec_ref[t, 2] = jnp.exp(-_softplus(-lin) * (1.0 / GLA_TAU))
            val_ref[t] = z[2 * GLA_KEY_W:2 * GLA_KEY_W + GLA_W]

    vs = pl.ds(pl.multiple_of(h * HEAD_DIM, HEAD_DIM), HEAD_DIM)
    v = [val_ref[t, vs, :] for t in range(nt)]

    def body(k8, acc):
        r0 = pl.multiple_of(h * GLA_DK + k8 * ROWS, ROWS)
        q, k, a = ([vec_ref[t, j, pl.ds(r0, ROWS), :] for t in range(nt)] for j in range(3))
        acc = list(acc)
        for j in range(ROWS):
            s = s_ref[0, k8 * ROWS + j]
            for t in range(nt):
                s = s * a[t][j:j + 1, :] + k[t][j:j + 1, :] * v[t]
                acc[t] = acc[t] + q[t][j:j + 1, :] * s
            so_ref[0, k8 * ROWS + j] = s
        return tuple(acc)

    zero = jnp.zeros((HEAD_DIM, z_ref.shape[2]), F32)
    acc = lax.fori_loop(0, GLA_DK // ROWS, body, (zero,) * nt)
    for t in range(nt):
        obuf_ref[t, vs, :] = acc[t]

    @pl.when(h == pl.num_programs(0) - 1)
    def _():
        for t in range(nt):
            o = obuf_ref[t]
            ms = _ldot3(ones_ref[...], o * o) * (1.0 / HEAD_DIM)
            og = z_ref[t, 2 * GLA_KEY_W + GLA_W:2 * GLA_KEY_W + 2 * GLA_W, :]
            y_ref[t] = o * lax.rsqrt(ms + NORM_EPS) * cols_ref[GLA_KEY_W:, :] * _silu(og)


def _ssd_dec_kernel(z_ref, s_ref, cp_ref, cols_ref, y_ref, so_ref, x_ref, bc_ref, dt_ref, ybuf_ref):
    h = pl.program_id(0)
    nt = z_ref.shape[0]
    cd = M2_CONV_DIM
    off_b = M2_CONV * cd
    off_dt = off_b + cd

    @pl.when(h == 0)
    def _():
        xs = [cp_ref[j] for j in range(M2_CONV - 1)] + [z_ref[t, M2_W:M2_W + cd, :] for t in range(nt)]
        a_neg = -jnp.exp(cols_ref[off_dt + ROWS:off_dt + 2 * ROWS, :])
        for t in range(nt):
            conv = cols_ref[off_b:off_b + cd, :]
            for j in range(M2_CONV):
                conv = conv + xs[t + j] * cols_ref[j * cd:(j + 1) * cd, :]
            xc = _silu(conv)
            x_ref[t] = xc[0:M2_W]
            bc_ref[t] = xc[M2_W:]
            dt = _softplus(z_ref[t, M2_W + cd:M2_W + cd + ROWS, :] + cols_ref[off_dt:off_dt + ROWS, :])
            dt_ref[t, 0] = dt
            dt_ref[t, 1] = jnp.exp(dt * a_neg)

    g = h // (M2_HEADS // M2_GROUPS)
    bs = pl.ds(pl.multiple_of(g * M2_STATE, M2_STATE), M2_STATE)
    cs = pl.ds(pl.multiple_of((M2_GROUPS + g) * M2_STATE, M2_STATE), M2_STATE)
    rid = lax.broadcasted_iota(jnp.int32, (ROWS, 1), 0)
    pick = lambda tile: jnp.sum(jnp.where(rid == h, tile, 0.0), axis=0, keepdims=True)
    dt = [pick(dt_ref[t, 0]) for t in range(nt)]
    da = [pick(dt_ref[t, 1]) for t in range(nt)]

    def body(p8, carry):
        r0 = pl.multiple_of(h * HEAD_DIM + p8 * ROWS, ROWS)
        xt = [x_ref[t, pl.ds(r0, ROWS), :] * dt[t] for t in range(nt)]
        ys = [[] for _ in range(nt)]
        for j in range(ROWS):
            s = s_ref[0, p8 * ROWS + j]
            for t in range(nt):
                s = s * da[t] + xt[t][j:j + 1, :] * bc_ref[t, bs, :]
                ys[t].append(jnp.sum(s * bc_ref[t, cs, :], axis=0, keepdims=True))
            so_ref[0, p8 * ROWS + j] = s
        for t in range(nt):
            ybuf_ref[t, pl.ds(r0, ROWS), :] = _rows_to_tile(ys[t])
        return carry

    lax.fori_loop(0, HEAD_DIM // ROWS, body, 0)

    @pl.when(h == pl.num_programs(0) - 1)
    def _():
        off_d = off_dt + 2 * ROWS
        for t in range(nt):
            y = ybuf_ref[t] + cols_ref[off_d:off_d + M2_W, :] * x_ref[t]
            y = y * _silu(z_ref[t, 0:M2_W, :])
            ms = jnp.mean(y * y, axis=0, keepdims=True)
            y_ref[t] = y * lax.rsqrt(ms + NORM_EPS) * cols_ref[off_d + M2_W:off_d + 2 * M2_W, :]


def _dec_call(kern, name, acts, state, sl, layer_params, consts, layer, y_width, scratch):
    nt, _, nb = acts[0].shape
    heads = state.shape[1]
    whole = lambda a: pl.BlockSpec(a.shape, lambda hd: (0,) * a.ndim, pipeline_mode=pl.Buffered(1))
    blk = (1,) + state.shape[2:]
    return pl.pallas_call(
        kern,
        grid=(heads,),
        in_specs=[whole(a) for a in acts]
                 + [pl.BlockSpec((None,) + blk, lambda hd: (sl, hd, 0, 0, 0))]
                 + [_layer_spec(a, layer) for a in layer_params] + [_const_spec(a.shape) for a in consts],
        out_specs=[pl.BlockSpec((nt, y_width, nb), lambda hd: (0, 0, 0)),
                   pl.BlockSpec(blk, lambda hd: (hd, 0, 0, 0))],
        out_shape=[jax.ShapeDtypeStruct((nt, y_width, nb), F32), jax.ShapeDtypeStruct(state.shape[1:], F32)],
        scratch_shapes=scratch,
        compiler_params=pltpu.CompilerParams(dimension_semantics=("arbitrary",), vmem_limit_bytes=VMEM_LIMIT),
        name=name,
    )(*acts, state, *layer_params, *consts)


def _rows(v):
    return v.reshape(v.shape[0], 1, -1).astype(F32)


def _pad_rows(m, top, total):
    return jnp.zeros((m.shape[0], total, m.shape[2]), F32).at[:, top:top + m.shape[1]].set(m).astype(BF16)


def _pad_lanes(v):
    return jnp.zeros((v.shape[0], 1, LANE), F32).at[:, 0, :v.shape[1]].set(v)


def _stacked_params(ffn1_norm, ffn1_w_gate, ffn1_w_up, ffn1_w_down, mix_norm, w_in, rwkv_mu, rwkv_w0, rwkv_w2,
                    rwkv_a0, rwkv_a2, rwkv_g2, rwkv_k_k, rwkv_k_a, rwkv_r_k, rwkv_ln_w, rwkv_ln_b, gla_gate_w2,
                    gla_gate_b, gla_norm, mamba_conv_w, mamba_conv_b, mamba_dt_bias, mamba_A_log, mamba_D, mamba_norm,
                    w_out, ffn2_norm, ffn2_w_gate, ffn2_w_up, ffn2_w_down, ple_norm, ple_w_gate, ple_w_proj):
    depth = w_in.shape[0]
    w = w_in.astype(BF16)
    g0 = RWKV_PROJ
    m0 = RWKV_PROJ + GLA_PROJ
    qkv_end = 2 * GLA_KEY_W + GLA_W
    zeros = lambda n: jnp.zeros((depth, D_MODEL, n), BF16)
    win = jnp.concatenate([
        w[:, :, :g0 + qkv_end], w[:, :, g0 + qkv_end + GLA_GATE_LORA:m0],
        w[:, :, g0 + qkv_end:g0 + qkv_end + GLA_GATE_LORA], zeros(LANE - GLA_GATE_LORA),
        w[:, :, m0:], zeros(LANE - M2_HEADS)], axis=2)
    pre = (_rows(ffn1_norm), ffn1_w_gate.astype(BF16), ffn1_w_up.astype(BF16), ffn1_w_down.astype(BF16),
           _rows(mix_norm), win)
    rwkv = (_rows(rwkv_mu), _rows(rwkv_w0), _pad_rows(rwkv_w2, 0, RWKV_LORA), _rows(rwkv_a0),
            _pad_rows(rwkv_a2, RWKV_DECAY_LORA, RWKV_LORA),
            _pad_rows(rwkv_g2, RWKV_DECAY_LORA + RWKV_A_LORA, RWKV_LORA),
            _rows(rwkv_k_k), _rows(rwkv_k_a), _rows(rwkv_r_k), _rows(rwkv_ln_w), _rows(rwkv_ln_b))
    gla = (_pad_rows(gla_gate_w2, 0, LANE), _rows(gla_gate_b), _rows(jnp.tile(gla_norm, (1, GLA_HEADS))))
    ssd = (mamba_conv_w.astype(F32), _rows(mamba_conv_b), _pad_lanes(mamba_dt_bias), _pad_lanes(mamba_A_log),
           _rows(jnp.repeat(mamba_D, HEAD_DIM, axis=1)), _rows(mamba_norm))
    post = (w_out.astype(BF16), _rows(ffn2_norm), ffn2_w_gate.astype(BF16), ffn2_w_up.astype(BF16),
            ffn2_w_down.astype(BF16), _rows(ple_norm), ple_w_gate.astype(BF16), ple_w_proj.astype(BF16))
    return pre, rwkv, gla, ssd, post


def _decode_params(nb, rwkv_mu, rwkv_w0, rwkv_w2, rwkv_a0, rwkv_a2, rwkv_g2, rwkv_k_k, rwkv_k_a, rwkv_r_k,
                   rwkv_ln_w, rwkv_ln_b, gla_gate_w2, gla_gate_b, gla_norm, mamba_conv_w, mamba_conv_b,
                   mamba_dt_bias, mamba_A_log, mamba_D, mamba_norm):
    depth = rwkv_mu.shape[0]
    slab = lambda vs: jnp.broadcast_to(jnp.concatenate([v.reshape(depth, -1) for v in vs], axis=1)[:, :, None],
                                       (depth, sum(v[0].size for v in vs), nb)).astype(F32)
    tr = lambda m: jnp.swapaxes(m, 1, 2)
    rwkv = (slab((rwkv_mu, rwkv_w0, rwkv_a0, rwkv_k_k, rwkv_k_a, rwkv_r_k, rwkv_ln_w, rwkv_ln_b)),
            tr(_pad_rows(rwkv_w2, 0, RWKV_LORA)), tr(_pad_rows(rwkv_a2, RWKV_DECAY_LORA, RWKV_LORA)),
            tr(_pad_rows(rwkv_g2, RWKV_DECAY_LORA + RWKV_A_LORA, RWKV_LORA)))
    gla = (slab((gla_gate_b, jnp.tile(gla_norm, (1, GLA_HEADS)))), tr(_pad_rows(gla_gate_w2, 0, LANE)))
    ssd = (slab((mamba_conv_w, mamba_conv_b, mamba_dt_bias, mamba_A_log, jnp.repeat(mamba_D, HEAD_DIM, axis=1),
                 mamba_norm)),)
    return rwkv, gla, ssd


def _sample_layer(x, p, state, params, dparams, layer, final_norm, *, batch, seq, final):
    pre, _, _, _, post = params
    d_rwkv, d_gla, d_ssd = dparams
    shift0, wkv0, gla0, conv0, ssm0 = state
    n = batch * seq
    x1, zr_t, zg_t, zm_t = _pre_call(x, layer, *pre, n, steps=seq)
    shift1 = zr_t[seq - 1].T
    xbc = jnp.transpose(zm_t[:, M2_W:M2_W + M2_CONV_DIM, :], (2, 0, 1))
    conv1 = jnp.concatenate([conv0[layer], xbc], axis=1)[:, -(M2_CONV - 1):]
    scr = lambda *shape: pltpu.VMEM(shape, F32)
    yr, wkv1 = _dec_call(_rwkv_dec_kernel, "rwkv7_decode", (zr_t,), wkv0, layer,
                         (jnp.swapaxes(shift0, 1, 2),) + d_rwkv, (_HEAD_ONES(),), layer, RWKV_W,
                         [scr(seq, 6, RWKV_W, batch), scr(seq, RWKV_W, batch), scr(seq, 2, RWKV_W, batch)])
    yg, gla1 = _dec_call(_gla_dec_kernel, "gla_decode", (zg_t,), gla0, layer, d_gla, (_HEAD_ONES(),), layer, GLA_W,
                         [scr(seq, 3, GLA_KEY_W, batch), scr(seq, GLA_W, batch), scr(seq, GLA_W, batch)])
    conv_t = jnp.transpose(conv0, (0, 2, 3, 1))
    ym, ssm1 = _dec_call(_ssd_dec_kernel, "ssd_decode", (zm_t,), ssm0, layer, (conv_t,) + d_ssd, (), layer, M2_W,
                         [scr(seq, M2_W, batch), scr(seq, 2 * M2_GROUPS * M2_STATE, batch),
                          scr(seq, 2, ROWS, batch), scr(seq, M2_W, batch)])
    x2 = _post_call(x1, yr, yg, ym, p, layer, *post, final_norm.reshape(1, -1), n, final, steps=seq)
    return x2, (shift1, wkv1, gla1, conv1, ssm1)


def _pad_time(a, total):
    return jnp.pad(a, ((0, 0), (0, total - a.shape[1]), (0, 0)))


def _group_layer(x, p, state, sl, params, layer, final_norm, *, batch, seq, chunks, bb, tm, tm_post, y_dtype, final):
    pre, rwkv, gla, ssd, post = params
    shift0, wkv0, gla0, conv0, ssm0 = state
    x1, zr, zg, zm = _pre_call(x, layer, *pre, tm)
    zr = zr.reshape(batch, seq, RWKV_PROJ)
    zg = zg.reshape(batch, seq, GLA_PAD)
    zm = zm.reshape(batch, seq, M2_PAD)
    xbc = zm[:, :, M2_W:M2_W + M2_CONV_DIM]
    shift1 = zr[:, -1]
    conv1 = jnp.concatenate([conv0[sl], xbc[:, -(M2_CONV - 1):]], axis=1)[:, -(M2_CONV - 1):]
    c_r, c_g, c_m = chunks
    pad = max(c_r, c_g, c_m, seq)
    if pad != seq:
        zr, zg, zm = _pad_time(zr, pad), _pad_time(zg, pad), _pad_time(zm, pad)
    valid = lambda ch: seq if seq < ch else ch
    yr, wkv1 = _rwkv_call(zr, shift0[:, :, None, :], wkv0, sl, rwkv, layer, bb, c_r, valid(c_r), y_dtype)
    yg, gla1 = _gla_call(zg, gla0, sl, gla, layer, bb, c_g, valid(c_g), y_dtype)
    ym, ssm1 = _ssd_call(zm, conv0, ssm0, sl, ssd, layer, bb, c_m, valid(c_m), y_dtype)
    n = batch * seq
    flat = lambda y: y[:, :seq].reshape(n, y.shape[-1])
    x2 = _post_call(x1, flat(yr), flat(yg), flat(ym), p, layer, *post, final_norm.reshape(1, -1), tm_post, final)
    return x2, (shift1, wkv1, gla1, conv1, ssm1)


def kernel(x_prompt, x_sample, p_prompt, p_sample, state_rwkv_shift, state_rwkv_wkv, state_gla, state_mamba_conv, state_mamba_ssm, ffn1_norm, ffn1_w_gate, ffn1_w_up, ffn1_w_down, mix_norm, w_in, rwkv_mu, rwkv_w0, rwkv_w2, rwkv_a0, rwkv_a2, rwkv_g2, rwkv_k_k, rwkv_k_a, rwkv_r_k, rwkv_ln_w, rwkv_ln_b, gla_gate_w2, gla_gate_b, gla_norm, mamba_conv_w, mamba_conv_b, mamba_dt_bias, mamba_A_log, mamba_D, mamba_norm, w_out, ffn2_norm, ffn2_w_gate, ffn2_w_up, ffn2_w_down, ple_norm, ple_w_gate, ple_w_proj, final_norm):
    depth = w_in.shape[0]
    nb, seq, _ = x_prompt.shape
    db, dseq, _ = x_sample.shape
    weights = (ffn1_norm, ffn1_w_gate, ffn1_w_up, ffn1_w_down, mix_norm, w_in, rwkv_mu, rwkv_w0, rwkv_w2, rwkv_a0,
               rwkv_a2, rwkv_g2, rwkv_k_k, rwkv_k_a, rwkv_r_k, rwkv_ln_w, rwkv_ln_b, gla_gate_w2, gla_gate_b,
               gla_norm, mamba_conv_w, mamba_conv_b, mamba_dt_bias, mamba_A_log, mamba_D, mamba_norm, w_out,
               ffn2_norm, ffn2_w_gate, ffn2_w_up, ffn2_w_down, ple_norm, ple_w_gate, ple_w_proj)
    xp = x_prompt.reshape(nb * seq, D_MODEL)
    xs = jnp.swapaxes(x_sample, 0, 1).reshape(dseq * db, D_MODEL)
    pp = p_prompt.reshape(depth, nb * seq, PLE_DIM)
    ps = jnp.swapaxes(p_sample, 1, 2).reshape(depth, dseq * db, PLE_DIM)
    params = _stacked_params(*weights)
    fresh = (jnp.zeros((1, nb, RWKV_PROJ), F32), jnp.zeros((1, nb, RWKV_HEADS, HEAD_DIM, HEAD_DIM), F32),
             jnp.zeros((1, nb, GLA_HEADS, GLA_DK, HEAD_DIM), F32),
             jnp.zeros((1, nb, M2_CONV - 1, M2_CONV_DIM), F32),
             jnp.zeros((1, nb, M2_HEADS, HEAD_DIM, M2_STATE), F32))
    dparams = _decode_params(db, rwkv_mu, rwkv_w0, rwkv_w2, rwkv_a0, rwkv_a2, rwkv_g2, rwkv_k_k, rwkv_k_a, rwkv_r_k,
                             rwkv_ln_w, rwkv_ln_b, gla_gate_w2, gla_gate_b, gla_norm, mamba_conv_w, mamba_conv_b,
                             mamba_dt_bias, mamba_A_log, mamba_D, mamba_norm)
    batch_last = lambda s: jnp.moveaxis(s, 1, -1)
    past = (state_rwkv_shift, batch_last(state_rwkv_wkv), batch_last(state_gla), state_mamba_conv,
            batch_last(state_mamba_ssm))
    p_states, s_states = [], []
    for i in range(depth):
        final = i == depth - 1
        xp, st_p = _group_layer(xp, pp, fresh, 0, params, i, final_norm,
                                batch=nb, seq=seq, chunks=(RWKV_CHUNK, GLA_CHUNK, SSD_CHUNK), bb=8, tm=512,
                                tm_post=1024, y_dtype=BF16, final=final)
        xs, st_s = _sample_layer(xs, ps, past, params, dparams, i, final_norm, batch=db, seq=dseq, final=final)
        p_states.append(st_p)
        s_states.append(st_s)
    stack = lambda sts, j: jnp.stack([s[j] for s in sts])
    s_out = tuple(jnp.moveaxis(stack(s_states, j), -1, 1) if j in (1, 2, 4) else stack(s_states, j)
                  for j in range(5))
    return ((xp.reshape(nb, seq, D_MODEL), jnp.swapaxes(xs.reshape(dseq, db, D_MODEL), 0, 1))
            + tuple(stack(p_states, j) for j in range(5)) + s_out)
```

```python
import functools

import jax
import jax.numpy as jnp
from jax import lax
from jax.experimental import pallas as pl
from jax.experimental.pallas import tpu as pltpu

F32 = jnp.float32
BF16 = jnp.bfloat16

D_MODEL = 1024
D_FF = 2816
PLE_DIM = 256
HEAD_DIM = 64
NORM_EPS = 1e-6

RWKV_W = 256
RWKV_HEADS = 4
RWKV_GN_EPS = 64e-5
RWKV_PROJ = 896
RWKV_LORA = 128
RWKV_DECAY_LORA = 32
RWKV_A_LORA = 32

GLA_W = 256
GLA_HEADS = 4
GLA_DK = 32
GLA_KEY_W = 128
GLA_GATE_LORA = 16
GLA_TAU = 16.0
GLA_PROJ = 784
GLA_PAD = 896

M2_W = 512
M2_HEADS = 8
M2_STATE = 64
M2_GROUPS = 2
M2_CONV = 4
M2_CONV_DIM = 768
M2_PROJ = 1288
M2_PAD = 1408

LANE = 128
SUBLANE = 8
FF_CHUNK = 256
VMEM_LIMIT = 56 * 1024 * 1024

RWKV_CHUNK = 128
GLA_CHUNK = 64
SSD_CHUNK = 128


def _bdot(a, b):
    return jnp.dot(a.astype(BF16), b.astype(BF16), preferred_element_type=F32)


def _bdot_nt(a, b):
    return lax.dot_general(a.astype(BF16), b.astype(BF16), (((1,), (1,)), ((), ())),
                           preferred_element_type=F32)


def _bdot_tn(a, b):
    return lax.dot_general(a.astype(BF16), b.astype(BF16), (((0,), (0,)), ((), ())),
                           preferred_element_type=F32)


def _split3(x):
    hi = x.astype(BF16)
    r1 = x - hi.astype(F32)
    mid = r1.astype(BF16)
    lo = (r1 - mid.astype(F32)).astype(BF16)
    return hi, mid, lo


def _ldot3(mat, x):
    return sum(jnp.dot(mat, p, preferred_element_type=F32) for p in _split3(x))


def _rdot2(x, mat):
    return sum(jnp.dot(p, mat, preferred_element_type=F32) for p in _split3(x)[:2])


def _nt3(mat, x):
    return sum(lax.dot_general(mat, p, (((1,), (1,)), ((), ())), preferred_element_type=F32)
               for p in _split3(x))


def _softplus(x):
    return jnp.maximum(x, 0.0) + jnp.log1p(jnp.exp(-jnp.abs(x)))


def _sigmoid(x):
    return 1.0 / (1.0 + jnp.exp(-x))


def _silu(x):
    return x * _sigmoid(x)


def _rms(x, w):
    return x * lax.rsqrt(jnp.mean(x * x, axis=-1, keepdims=True) + NORM_EPS) * w


def _const_spec(shape):
    nd = len(shape)
    return pl.BlockSpec(shape, lambda *_: (0,) * nd, pipeline_mode=pl.Buffered(1))


def _layer_spec(a, layer):
    nd = a.ndim - 1
    return pl.BlockSpec((None,) + a.shape[1:], lambda *_: (layer,) + (0,) * nd, pipeline_mode=pl.Buffered(1))


def _ffn(x, nw, wg_ref, wu_ref, wd_ref, act_ref):
    xn = _rms(x, nw).astype(BF16)
    for c in range(D_FF // FF_CHUNK):
        sl = slice(c * FF_CHUNK, (c + 1) * FF_CHUNK)
        hg = jnp.dot(xn, wg_ref[:, sl], preferred_element_type=F32)
        hu = jnp.dot(xn, wu_ref[:, sl], preferred_element_type=F32)
        act_ref[:, sl] = (_silu(hg) * hu).astype(BF16)
    return x + 0.5 * jnp.dot(act_ref[...], wd_ref[...], preferred_element_type=F32)


def _pre_kernel(x_ref, n1_ref, wg_ref, wu_ref, wd_ref, nm_ref, win_ref,
                x1_ref, zr_ref, zg_ref, zm_ref, act_ref, *, steps):
    x1 = _ffn(x_ref[...], n1_ref[...], wg_ref, wu_ref, wd_ref, act_ref)
    x1_ref[...] = x1
    h = _rms(x1, nm_ref[...]).astype(BF16)
    cols = (slice(0, RWKV_PROJ), slice(RWKV_PROJ, RWKV_PROJ + GLA_PAD), slice(RWKV_PROJ + GLA_PAD, None))
    for ref, cs in zip((zr_ref, zg_ref, zm_ref), cols):
        z = jnp.dot(h, win_ref[:, cs], preferred_element_type=F32)
        if steps:
            nb = z.shape[0] // steps
            for t in range(steps):
                ref[t] = z[t * nb:(t + 1) * nb].T
        else:
            ref[...] = z


def _pre_call(x, layer, n1, wg, wu, wd, nm, win, tm, steps=0):
    n = x.shape[0]
    tok = lambda w: pl.BlockSpec((tm, w), lambda i: (i, 0))
    if steps:
        assert n == tm and n % steps == 0
        z_spec = lambda w: pl.BlockSpec((steps, w, n // steps), lambda i: (0, 0, 0))
        z_shape = lambda w: jax.ShapeDtypeStruct((steps, w, n // steps), F32)
    else:
        z_spec = tok
        z_shape = lambda w: jax.ShapeDtypeStruct((n, w), F32)
    return pl.pallas_call(
        functools.partial(_pre_kernel, steps=steps),
        grid=(n // tm,),
        in_specs=[tok(D_MODEL)] + [_layer_spec(a, layer) for a in (n1, wg, wu, wd, nm, win)],
        out_specs=[tok(D_MODEL), z_spec(RWKV_PROJ), z_spec(GLA_PAD), z_spec(M2_PAD)],
        out_shape=[jax.ShapeDtypeStruct((n, D_MODEL), F32), z_shape(RWKV_PROJ), z_shape(GLA_PAD), z_shape(M2_PAD)],
        scratch_shapes=[pltpu.VMEM((tm, D_FF), BF16)],
        compiler_params=pltpu.CompilerParams(dimension_semantics=("parallel",), vmem_limit_bytes=VMEM_LIMIT),
        name="pre_ffn_inproj",
    )(x, n1, wg, wu, wd, nm, win)


def _post_kernel(x_ref, yr_ref, yg_ref, ym_ref, p_ref, wo_ref, n2_ref, wg_ref, wu_ref, wd_ref,
                 np_ref, pg_ref, pp_ref, nf_ref, o_ref, act_ref, *, final, steps):
    rows = lambda ref: (jnp.concatenate([ref[t].T for t in range(steps)], axis=0) if steps else ref[...])
    x = x_ref[...]
    x = x + jnp.dot(rows(yr_ref).astype(BF16), wo_ref[0:RWKV_W, :], preferred_element_type=F32)
    x = x + jnp.dot(rows(yg_ref).astype(BF16), wo_ref[RWKV_W:RWKV_W + GLA_W, :], preferred_element_type=F32)
    x = x + jnp.dot(rows(ym_ref).astype(BF16), wo_ref[RWKV_W + GLA_W:, :], preferred_element_type=F32)
    x = _ffn(x, n2_ref[...], wg_ref, wu_ref, wd_ref, act_ref)
    gate = _sigmoid(jnp.dot(_rms(x, np_ref[...]).astype(BF16), pg_ref[...], preferred_element_type=F32))
    x = x + gate * jnp.dot(p_ref[...].astype(BF16), pp_ref[...], preferred_element_type=F32)
    if final:
        x = _rms(x, nf_ref[...])
    o_ref[...] = x


def _post_call(x, yr, yg, ym, p, layer, wo, n2, wg, wu, wd, npn, pg, pp, nf, tm, final, steps=0):
    n = x.shape[0]
    tok = lambda w: pl.BlockSpec((tm, w), lambda i: (i, 0))
    if steps:
        assert n == tm and n % steps == 0
        y_spec = lambda w: pl.BlockSpec((steps, w, n // steps), lambda i: (0, 0, 0))
    else:
        y_spec = tok
    params = (wo, n2, wg, wu, wd, npn, pg, pp)
    consts = params + (nf,)
    return pl.pallas_call(
        functools.partial(_post_kernel, final=final, steps=steps),
        grid=(n // tm,),
        in_specs=[tok(D_MODEL), y_spec(RWKV_W), y_spec(GLA_W), y_spec(M2_W),
                  pl.BlockSpec((None, tm, PLE_DIM), lambda i: (layer, i, 0))]
                 + [_layer_spec(a, layer) for a in params] + [_const_spec(nf.shape)],
        out_specs=tok(D_MODEL),
        out_shape=jax.ShapeDtypeStruct((n, D_MODEL), F32),
        scratch_shapes=[pltpu.VMEM((tm, D_FF), BF16)],
        compiler_params=pltpu.CompilerParams(dimension_semantics=("parallel",), vmem_limit_bytes=VMEM_LIMIT),
        name="post_outproj_ffn_ple",
    )(x, yr, yg, ym, p, *consts)


def _tri_masks(n):
    ri = lax.broadcasted_iota(jnp.int32, (n, n), 0)
    ci = lax.broadcasted_iota(jnp.int32, (n, n), 1)
    return ri >= ci, ri > ci, (ri == ci).astype(F32)


def _rwkv_kernel(z_ref, sh_ref, s0_ref, mu_ref, w0_ref, w2_ref, a0_ref, a2_ref, g2_ref, kk_ref, ka_ref,
                 rk_ref, lnw_ref, lnb_ref, tri_ref, ones_ref,
                 y_ref, s_ref, zbuf_ref, ybuf_ref, *, bb, chunk, valid):
    c = pl.program_id(1)

    @pl.when(c == 0)
    def _():
        s_ref[...] = s0_ref[...]
        zbuf_ref[:, SUBLANE - 1:SUBLANE, :] = sh_ref[...]

    incl, strict, eye = _tri_masks(chunk)
    tri = tri_ref[...]
    ones = ones_ref[...]
    row = lax.broadcasted_iota(jnp.int32, (chunk, 1), 0)
    mu = mu_ref[...]
    seqs = range(bb)
    zs = []
    for i in seqs:
        z = z_ref[i]
        zbuf_ref[i, SUBLANE:SUBLANE + chunk, :] = z
        prev = zbuf_ref[i, SUBLANE - 1:SUBLANE - 1 + chunk, :]
        zbuf_ref[i, SUBLANE - 1:SUBLANE, :] = z[chunk - 1:chunk, :]
        zs.append(z + mu * (prev - z))
    r = [x[:, 0:RWKV_W] for x in zs]
    k = [x[:, RWKV_W:2 * RWKV_W] for x in zs]
    v = [x[:, 2 * RWKV_W:3 * RWKV_W] for x in zs]
    lora = [x[:, 3 * RWKV_W:] for x in zs]
    w_lin = [_bdot(jnp.tanh(x), w2_ref[...]) for x in lora]
    a_lin = [_bdot(x, a2_ref[...]) for x in lora]
    gate = [_bdot(_sigmoid(x), g2_ref[...]) for x in lora]
    kk = [x * kk_ref[...] for x in k]
    kk_ss = [_rdot2(x * x, ones) for x in kk]
    logw = [-jnp.exp(-_softplus(-(w0_ref[...] + x)) - 0.5) for x in w_lin]
    a = [_sigmoid(a0_ref[...] + x) for x in a_lin]
    kk = [x / jnp.maximum(jnp.sqrt(ss), 1e-12) for x, ss in zip(kk, kk_ss)]
    k = [x * (1.0 + (ai - 1.0) * ka_ref[...]) for x, ai in zip(k, a)]
    ahat = [-x for x in kk]
    bhat = [x * ai for x, ai in zip(kk, a)]
    if valid < chunk:
        vm = row < valid
        mask = lambda xs: [jnp.where(vm, x, 0.0) for x in xs]
        logw, ahat, bhat, k, v = mask(logw), mask(ahat), mask(bhat), mask(k), mask(v)
    g = [_ldot3(tri, x) for x in logw]
    glast = [x[chunk - 1:chunk, :] for x in g]
    e_ng = [jnp.exp(-x) for x in g]
    e_gl = [jnp.exp(gl - x) for gl, x in zip(glast, g)]
    a_l = [ah * jnp.exp(x - lw) for ah, x, lw in zip(ahat, g, logw)]
    r_l = [ri * jnp.exp(x) for ri, x in zip(r, g)]
    b_r = [x * e for x, e in zip(bhat, e_ng)]
    k_r = [x * e for x, e in zip(k, e_ng)]
    k_p = [x * e for x, e in zip(k, e_gl)]
    b_p = [x * e for x, e in zip(bhat, e_gl)]
    dec = [jnp.exp(x) for x in glast]

    probs = [(i, h) for i in seqs for h in range(RWKV_HEADS)]
    hd = lambda xs: [xs[i][:, h * HEAD_DIM:(h + 1) * HEAD_DIM] for i, h in probs]
    a_lh, r_lh, b_rh, k_rh, k_ph, b_ph, vh = hd(a_l), hd(r_l), hd(b_r), hd(k_r), hd(k_p), hd(b_p), hd(v)
    ar = [jnp.concatenate([x, y], axis=0) for x, y in zip(a_lh, r_lh)]
    bk = [jnp.concatenate([x, y], axis=0) for x, y in zip(b_rh, k_rh)]
    gm = [_bdot_nt(x, y) for x, y in zip(ar, bk)]
    a_ab = [jnp.where(strict, x[:chunk, :chunk], 0.0) for x in gm]
    a_ak = [jnp.where(strict, x[:chunk, chunk:], 0.0) for x in gm]
    a_rb = [jnp.where(incl, x[chunk:, :chunk], 0.0) for x in gm]
    a_rk = [jnp.where(incl, x[chunk:, chunk:], 0.0) for x in gm]
    t = [eye + m for m in a_ab]
    x = [_bdot(m, m) for m in a_ab]
    p = 2
    while p < chunk:
        if 2 * p < chunk:
            xt = [_bdot(jnp.concatenate([xi, ti], axis=0), xi) for xi, ti in zip(x, t)]
            x = [m[:chunk] for m in xt]
            t = [ti + m[chunk:] for ti, m in zip(t, xt)]
        else:
            t = [ti + _bdot(ti, xi) for xi, ti in zip(x, t)]
        p *= 2
    akv = [_bdot(m, vi) for m, vi in zip(a_ak, vh)]
    w_t = [_bdot(ti, m) for ti, m in zip(t, a_lh)]
    u_t = [_bdot(ti, m) for ti, m in zip(t, akv)]
    r_t = [ri + _bdot(m, wi) for ri, m, wi in zip(r_lh, a_rb, w_t)]
    y_t = [_bdot(m, vi) + _bdot(n, ui) for m, vi, n, ui in zip(a_rk, vh, a_rb, u_t)]
    m_s = [_bdot_tn(wi, bi) for wi, bi in zip(w_t, b_ph)]
    q_s = [_bdot_tn(jnp.concatenate([vi, ui], axis=0), jnp.concatenate([ki, bi], axis=0))
           for vi, ui, ki, bi in zip(vh, u_t, k_ph, b_ph)]
    s_old = [s_ref[i, h] for i, h in probs]
    y_h = [_bdot_nt(ri, si) + yi for ri, si, yi in zip(r_t, s_old, y_t)]
    s_m = [_bdot(si, mi) for si, mi in zip(s_old, m_s)]
    for j, (i, h) in enumerate(probs):
        hs = slice(h * HEAD_DIM, (h + 1) * HEAD_DIM)
        ybuf_ref[i, :, hs] = y_h[j]
        s_ref[i, h] = s_old[j] * dec[i][:, hs] + s_m[j] + q_s[j]
    y = [ybuf_ref[i] for i in seqs]
    mean = [_rdot2(x, ones) * (1.0 / HEAD_DIM) for x in y]
    yc = [x - m for x, m in zip(y, mean)]
    var = [_rdot2(x * x, ones) * (1.0 / HEAD_DIM) for x in yc]
    bonus = [_rdot2(ri * ki * rk_ref[...], ones) * vi for ri, ki, vi in zip(r, k, v)]
    for i in seqs:
        yn = yc[i] * lax.rsqrt(var[i] + RWKV_GN_EPS) * lnw_ref[...] + lnb_ref[...]
        y_ref[i] = ((yn + bonus[i]) * gate[i]).astype(y_ref.dtype)


def _state_spec(a, bb, sl):
    nd = a.ndim - 2
    return pl.BlockSpec((None, bb) + a.shape[2:], lambda i, c: (sl, i) + (0,) * nd)


def _rwkv_call(z, shift, s0, sl, params, layer, bb, chunk, valid, out_dtype):
    b, l, _ = z.shape
    grid = (b // bb, l // chunk)
    tri = jnp.tril(jnp.ones((chunk, chunk), F32)).astype(BF16)
    consts = (tri, _HEAD_ONES())
    return pl.pallas_call(
        functools.partial(_rwkv_kernel, bb=bb, chunk=chunk, valid=valid),
        grid=grid,
        in_specs=[pl.BlockSpec((bb, chunk, RWKV_PROJ), lambda i, c: (i, c, 0)),
                  _state_spec(shift, bb, sl), _state_spec(s0, bb, sl)]
                 + [_layer_spec(a, layer) for a in params] + [_const_spec(a.shape) for a in consts],
        out_specs=[pl.BlockSpec((bb, chunk, RWKV_W), lambda i, c: (i, c, 0)),
                   pl.BlockSpec((bb, RWKV_HEADS, HEAD_DIM, HEAD_DIM), lambda i, c: (i, 0, 0, 0))],
        out_shape=[jax.ShapeDtypeStruct((b, l, RWKV_W), out_dtype),
                   jax.ShapeDtypeStruct((b, RWKV_HEADS, HEAD_DIM, HEAD_DIM), F32)],
        scratch_shapes=[pltpu.VMEM((bb, SUBLANE + chunk, RWKV_PROJ), F32),
                        pltpu.VMEM((bb, chunk, RWKV_W), F32)],
        compiler_params=pltpu.CompilerParams(dimension_semantics=("arbitrary", "arbitrary"),
                                             vmem_limit_bytes=VMEM_LIMIT),
        name="rwkv7_mixer",
    )(z, shift, s0, *params, *consts)


def _gla_kernel(z_ref, s0_ref, gw_ref, gb_ref, nw_ref, tri_ref, ones_ref, eye_ref,
                y_ref, s_ref, obuf_ref, *, bb, chunk, valid):
    c = pl.program_id(1)

    @pl.when(c == 0)
    def _():
        s_ref[...] = s0_ref[...]

    incl, _, _ = _tri_masks(chunk)
    tri = tri_ref[...]
    ones = ones_ref[...]
    row = lax.broadcasted_iota(jnp.int32, (chunk, 1), 0)
    seqs = range(bb)
    z = [z_ref[i] for i in seqs]
    q = [x[:, 0:GLA_KEY_W] * (GLA_DK ** -0.5) for x in z]
    k = [x[:, GLA_KEY_W:2 * GLA_KEY_W] for x in z]
    v = [x[:, 2 * GLA_KEY_W:2 * GLA_KEY_W + GLA_W] for x in z]
    og = [x[:, 2 * GLA_KEY_W + GLA_W:2 * GLA_KEY_W + 2 * GLA_W] for x in z]
    lin = [_bdot(x[:, 2 * GLA_KEY_W + 2 * GLA_W:], gw_ref[...]) for x in z]
    log_a = [-_softplus(-(x + gb_ref[...])) * (1.0 / GLA_TAU) for x in lin]
    if valid < chunk:
        vm = row < valid
        log_a = [jnp.where(vm, x, 0.0) for x in log_a]
        k = [jnp.where(vm, x, 0.0) for x in k]
    b = [_ldot3(tri, x) for x in log_a]
    blast = [x[chunk - 1:chunk, :] for x in b]
    qe = [x * jnp.exp(bi) for x, bi in zip(q, b)]
    ke = [x * jnp.exp(-bi) for x, bi in zip(k, b)]
    kl = [x * jnp.exp(bl - bi) for x, bl, bi in zip(k, blast, b)]
    dec = [_nt3(eye_ref[...], jnp.broadcast_to(jnp.exp(bl), (HEAD_DIM, GLA_KEY_W))) for bl in blast]

    probs = [(i, h) for i in seqs for h in range(GLA_HEADS)]
    kd = lambda xs: [xs[i][:, h * GLA_DK:(h + 1) * GLA_DK] for i, h in probs]
    qeh, keh, klh = kd(qe), kd(ke), kd(kl)
    vh = [v[i][:, h * HEAD_DIM:(h + 1) * HEAD_DIM] for i, h in probs]
    att = [jnp.where(incl, _bdot_nt(x, y), 0.0) for x, y in zip(qeh, keh)]
    s_old = [s_ref[i, h] for i, h in probs]
    o_h = [_bdot(ai, vi) + _bdot(qi, si) for ai, vi, qi, si in zip(att, vh, qeh, s_old)]
    kv = [_bdot_tn(ki, vi) for ki, vi in zip(klh, vh)]
    for j, (i, h) in enumerate(probs):
        obuf_ref[i, :, h * HEAD_DIM:(h + 1) * HEAD_DIM] = o_h[j]
        s_ref[i, h] = s_old[j] * dec[i][h * GLA_DK:(h + 1) * GLA_DK, :] + kv[j]
    o = [obuf_ref[i] for i in seqs]
    ms = [_rdot2(x * x, ones) * (1.0 / HEAD_DIM) for x in o]
    for i in seqs:
        y_ref[i] = (o[i] * lax.rsqrt(ms[i] + NORM_EPS) * nw_ref[...] * _silu(og[i])).astype(y_ref.dtype)


def _gla_call(z, s0, sl, params, layer, bb, chunk, valid, out_dtype):
    b, l, _ = z.shape
    tri = jnp.tril(jnp.ones((chunk, chunk), F32)).astype(BF16)
    consts = (tri, _HEAD_ONES(), jnp.eye(LANE, dtype=BF16))
    return pl.pallas_call(
        functools.partial(_gla_kernel, bb=bb, chunk=chunk, valid=valid),
        grid=(b // bb, l // chunk),
        in_specs=[pl.BlockSpec((bb, chunk, GLA_PAD), lambda i, c: (i, c, 0)), _state_spec(s0, bb, sl)]
                 + [_layer_spec(a, layer) for a in params] + [_const_spec(a.shape) for a in consts],
        out_specs=[pl.BlockSpec((bb, chunk, GLA_W), lambda i, c: (i, c, 0)),
                   pl.BlockSpec((bb, GLA_HEADS, GLA_DK, HEAD_DIM), lambda i, c: (i, 0, 0, 0))],
        out_shape=[jax.ShapeDtypeStruct((b, l, GLA_W), out_dtype),
                   jax.ShapeDtypeStruct((b, GLA_HEADS, GLA_DK, HEAD_DIM), F32)],
        scratch_shapes=[pltpu.VMEM((bb, chunk, GLA_W), F32)],
        compiler_params=pltpu.CompilerParams(dimension_semantics=("arbitrary", "arbitrary"),
                                             vmem_limit_bytes=VMEM_LIMIT),
        name="gla_mixer",
    )(z, s0, *params, *consts)


def _ssd_kernel(z_ref, cp_ref, s0_ref, cw_ref, cb_ref, dtb_ref, alog_ref, dsk_ref, nw_ref, tri_ref, eye_ref,
                esel_ref, y_ref, s_ref, xbuf_ref, ybuf_ref, *, bb, chunk, valid):
    c = pl.program_id(1)

    @pl.when(c == 0)
    def _():
        s_ref[...] = s0_ref[...]
        xbuf_ref[:, SUBLANE - (M2_CONV - 1):SUBLANE, :] = cp_ref[...]

    incl, _, _ = _tri_masks(chunk)
    tri = tri_ref[...]
    row = lax.broadcasted_iota(jnp.int32, (chunk, 1), 0)
    lane = lax.broadcasted_iota(jnp.int32, (1, LANE), 1)
    rep = M2_HEADS // M2_GROUPS
    seqs = range(bb)
    zg, xc, dt = [], [], []
    live = lane < M2_HEADS
    if valid < chunk:
        live = jnp.logical_and(live, row < valid)
    for i in seqs:
        z = z_ref[i]
        x = z[:, M2_W:M2_W + M2_CONV_DIM]
        xbuf_ref[i, SUBLANE:SUBLANE + chunk, :] = x
        conv = cb_ref[...] + x * cw_ref[M2_CONV - 1:M2_CONV, :]
        for j in range(1, M2_CONV):
            conv = conv + xbuf_ref[i, SUBLANE - j:SUBLANE - j + chunk, :] * cw_ref[M2_CONV - 1 - j:M2_CONV - j, :]
        xbuf_ref[i, 0:SUBLANE, :] = xbuf_ref[i, chunk:chunk + SUBLANE, :]
        zg.append(z[:, 0:M2_W])
        xc.append(_silu(conv))
        dt.append(jnp.where(live, _softplus(z[:, M2_W + M2_CONV_DIM:] + dtb_ref[...]), 0.0))
    a_neg = -jnp.exp(alog_ref[...])
    cum = [_ldot3(tri, x * a_neg) for x in dt]
    cum_t = [_nt3(eye_ref[0:SUBLANE, :], x) for x in cum]
    dt_b = [_rdot2(x, esel_ref[...]) for x in dt]
    grp = [(i, g) for i in seqs for g in range(M2_GROUPS)]
    bg = [xc[i][:, M2_W + g * M2_STATE:M2_W + (g + 1) * M2_STATE] for i, g in grp]
    cg = [xc[i][:, M2_W + (M2_GROUPS + g) * M2_STATE:M2_W + (M2_GROUPS + g + 1) * M2_STATE] for i, g in grp]
    cb = [_bdot_nt(x, y) for x, y in zip(cg, bg)]
    s_grp = [jnp.concatenate([s_ref[i, g * rep + hh] for hh in range(rep)], axis=0) for i, g in grp]
    y_st = [_bdot_nt(x, s) for x, s in zip(cg, s_grp)]
    probs = [(i, h) for i in seqs for h in range(M2_HEADS)]
    ccb = [jnp.broadcast_to(cum[i][:, h:h + 1], (chunk, LANE)) for i, h in probs]
    low = lane < HEAD_DIM
    cum_b = [jnp.concatenate([jnp.where(low, ccb[i * M2_HEADS + 2 * j], ccb[i * M2_HEADS + 2 * j + 1])
                              for j in range(M2_HEADS // 2)], axis=1) for i in seqs]
    xs = [x[:, 0:M2_W] for x in xc]
    xdt = [x * d for x, d in zip(xs, dt_b)]
    xw = [x * jnp.exp(cb_[chunk - 1:chunk, :] - cb_) for x, cb_ in zip(xdt, cum_b)]
    seg = [jnp.where(incl, jnp.exp(jnp.minimum(ccb[j][:, 0:chunk] - cum_t[i][h:h + 1, :], 0.0)), 0.0)
           for j, (i, h) in enumerate(probs)]
    y_in = [_bdot(cb[i * M2_GROUPS + h // rep] * seg[j], xdt[i][:, h * HEAD_DIM:(h + 1) * HEAD_DIM])
            for j, (i, h) in enumerate(probs)]
    s_in = [_bdot_tn(xw[i][:, g * rep * HEAD_DIM:(g + 1) * rep * HEAD_DIM], bg[j]) for j, (i, g) in enumerate(grp)]
    for j, (i, h) in enumerate(probs):
        ybuf_ref[i, :, h * HEAD_DIM:(h + 1) * HEAD_DIM] = y_in[j]
        gj = i * M2_GROUPS + h // rep
        hh = h % rep
        s_ref[i, h] = (s_grp[gj][hh * HEAD_DIM:(hh + 1) * HEAD_DIM] * jnp.exp(cum[i][chunk - 1:chunk, h:h + 1])
                       + s_in[gj][hh * HEAD_DIM:(hh + 1) * HEAD_DIM])
    for i in seqs:
        y_state = jnp.concatenate([y_st[i * M2_GROUPS + g] for g in range(M2_GROUPS)], axis=1)
        y = ybuf_ref[i] + y_state * jnp.exp(cum_b[i]) + dsk_ref[...] * xs[i]
        y_ref[i] = _rms(y * _silu(zg[i]), nw_ref[...]).astype(y_ref.dtype)


def _ssd_call(z, conv_prev, s0, sl, params, layer, bb, chunk, valid, out_dtype):
    b, l, _ = z.shape
    tri = jnp.tril(jnp.ones((chunk, chunk), F32)).astype(BF16)
    esel = (jnp.arange(LANE)[:, None] == jnp.arange(M2_W)[None, :] // HEAD_DIM).astype(BF16)
    consts = (tri, jnp.eye(LANE, dtype=BF16), esel)
    return pl.pallas_call(
        functools.partial(_ssd_kernel, bb=bb, chunk=chunk, valid=valid),
        grid=(b // bb, l // chunk),
        in_specs=[pl.BlockSpec((bb, chunk, M2_PAD), lambda i, c: (i, c, 0)),
                  _state_spec(conv_prev, bb, sl), _state_spec(s0, bb, sl)]
                 + [_layer_spec(a, layer) for a in params] + [_const_spec(a.shape) for a in consts],
        out_specs=[pl.BlockSpec((bb, chunk, M2_W), lambda i, c: (i, c, 0)),
                   pl.BlockSpec((bb, M2_HEADS, HEAD_DIM, M2_STATE), lambda i, c: (i, 0, 0, 0))],
        out_shape=[jax.ShapeDtypeStruct((b, l, M2_W), out_dtype),
                   jax.ShapeDtypeStruct((b, M2_HEADS, HEAD_DIM, M2_STATE), F32)],
        scratch_shapes=[pltpu.VMEM((bb, SUBLANE + chunk, M2_CONV_DIM), F32),
                        pltpu.VMEM((bb, chunk, M2_W), F32)],
        compiler_params=pltpu.CompilerParams(dimension_semantics=("arbitrary", "arbitrary"),
                                             vmem_limit_bytes=VMEM_LIMIT),
        name="ssd_mixer",
    )(z, conv_prev, s0, *params, *consts)


def _HEAD_ONES():
    idx = jnp.arange(RWKV_W) // HEAD_DIM
    return (idx[:, None] == idx[None, :]).astype(BF16)


ROWS = SUBLANE


def _rows_to_tile(rows):
    rid = lax.broadcasted_iota(jnp.int32, (ROWS, 1), 0)
    out = rows[0]
    for j in range(1, ROWS):
        out = jnp.where(rid == j, rows[j], out)
    return out


def _rwkv_dec_kernel(z_ref, s_ref, sh_ref, cols_ref, w2_ref, a2_ref, g2_ref, ones_ref,
                     y_ref, so_ref, vec_ref, ybuf_ref, post_ref):
    h = pl.program_id(0)
    nt = z_ref.shape[0]
    w = RWKV_W
    ones = ones_ref[...]
    col = lambda j: cols_ref[RWKV_PROJ + j * w:RWKV_PROJ + (j + 1) * w, :]

    @pl.when(h == 0)
    def _():
        mu = cols_ref[0:RWKV_PROJ, :]
        prev = sh_ref[...]
        for t in range(nt):
            z = z_ref[t]
            zs = z + mu * (prev - z)
            prev = z
            r, k, v, lora = zs[0:w], zs[w:2 * w], zs[2 * w:3 * w], zs[3 * w:]
            logw = -jnp.exp(-_softplus(-(col(0) + _bdot(w2_ref[...], jnp.tanh(lora)))) - 0.5)
            a = _sigmoid(col(1) + _bdot(a2_ref[...], lora))
            kk = k * col(2)
            kk = kk / jnp.maximum(jnp.sqrt(_ldot3(ones, kk * kk)), 1e-12)
            k = k * (1.0 + (a - 1.0) * col(3))
            for j, x in enumerate((-kk, jnp.exp(logw), kk * a, k, r, v)):
                vec_ref[t, j] = x
            post_ref[t, 0] = _bdot(g2_ref[...], _sigmoid(lora))
            post_ref[t, 1] = _ldot3(ones, r * k * col(4)) * v

    hs = pl.ds(pl.multiple_of(h * HEAD_DIM, HEAD_DIM), HEAD_DIM)

    def body(v8, carry):
        r0 = pl.multiple_of(h * HEAD_DIM + v8 * ROWS, ROWS)
        vt = [vec_ref[t, 5, pl.ds(r0, ROWS), :] for t in range(nt)]
        ys = [[] for _ in range(nt)]
        for j in range(ROWS):
            s = s_ref[0, v8 * ROWS + j]
            for t in range(nt):
                sa = jnp.sum(s * vec_ref[t, 0, hs, :], axis=0, keepdims=True)
                s = s * vec_ref[t, 1, hs, :] + sa * vec_ref[t, 2, hs, :] + vt[t][j:j + 1, :] * vec_ref[t, 3, hs, :]
                ys[t].append(jnp.sum(s * vec_ref[t, 4, hs, :], axis=0, keepdims=True))
            so_ref[0, v8 * ROWS + j] = s
        for t in range(nt):
            ybuf_ref[t, pl.ds(r0, ROWS), :] = _rows_to_tile(ys[t])
        return carry

    lax.fori_loop(0, HEAD_DIM // ROWS, body, 0)

    @pl.when(h == pl.num_programs(0) - 1)
    def _():
        for t in range(nt):
            y = ybuf_ref[t]
            yc = y - _ldot3(ones, y) * (1.0 / HEAD_DIM)
            var = _ldot3(ones, yc * yc) * (1.0 / HEAD_DIM)
            yn = yc * lax.rsqrt(var + RWKV_GN_EPS) * col(5) + col(6)
            y_ref[t] = (yn + post_ref[t, 1]) * post_ref[t, 0]


def _gla_dec_kernel(z_ref, s_ref, cols_ref, gw_ref, ones_ref, y_ref, so_ref, vec_ref, val_ref, obuf_ref):
    h = pl.program_id(0)
    nt = z_ref.shape[0]

    @pl.when(h == 0)
    def _():
        for t in range(nt):
            z = z_ref[t]
            lin = _bdot(gw_ref[...], z[2 * GLA_KEY_W + 2 * GLA_W:]) + cols_ref[0:GLA_KEY_W, :]
            vec_ref[t, 0] = z[0:GLA_KEY_W] * (GLA_DK ** -0.5)
            vec_ref[t, 1] = z[GLA_KEY_W:2 * GLA_KEY_W]
            vec_ref[t, 2] = jnp.exp(-_softplus(-lin) * (1.0 / GLA_TAU))
            val_ref[t] = z[2 * GLA_KEY_W:2 * GLA_KEY_W + GLA_W]

    vs = pl.ds(pl.multiple_of(h * HEAD_DIM, HEAD_DIM), HEAD_DIM)
    v = [val_ref[t, vs, :] for t in range(nt)]

    def body(k8, acc):
        r0 = pl.multiple_of(h * GLA_DK + k8 * ROWS, ROWS)
        q, k, a = ([vec_ref[t, j, pl.ds(r0, ROWS), :] for t in range(nt)] for j in range(3))
        acc = list(acc)
        for j in range(ROWS):
            s = s_ref[0, k8 * ROWS + j]
            for t in range(nt):
                s = s * a[t][j:j + 1, :] + k[t][j:j + 1, :] * v[t]
                acc[t] = acc[t] + q[t][j:j + 1, :] * s
            so_ref[0, k8 * ROWS + j] = s
        return tuple(acc)

    zero = jnp.zeros((HEAD_DIM, z_ref.shape[2]), F32)
    acc = lax.fori_loop(0, GLA_DK // ROWS, body, (zero,) * nt)
    for t in range(nt):
        obuf_ref[t, vs, :] = acc[t]

    @pl.when(h == pl.num_programs(0) - 1)
    def _():
        for t in range(nt):
            o = obuf_ref[t]
            ms = _ldot3(ones_ref[...], o * o) * (1.0 / HEAD_DIM)
            og = z_ref[t, 2 * GLA_KEY_W + GLA_W:2 * GLA_KEY_W + 2 * GLA_W, :]
            y_ref[t] = o * lax.rsqrt(ms + NORM_EPS) * cols_ref[GLA_KEY_W:, :] * _silu(og)


def _ssd_dec_kernel(z_ref, s_ref, cp_ref, cols_ref, y_ref, so_ref, x_ref, bc_ref, dt_ref, ybuf_ref):
    h = pl.program_id(0)
    nt = z_ref.shape[0]
    cd = M2_CONV_DIM
    off_b = M2_CONV * cd
    off_dt = off_b + cd

    @pl.when(h == 0)
    def _():
        xs = [cp_ref[j] for j in range(M2_CONV - 1)] + [z_ref[t, M2_W:M2_W + cd, :] for t in range(nt)]
        a_neg = -jnp.exp(cols_ref[off_dt + ROWS:off_dt + 2 * ROWS, :])
        for t in range(nt):
            conv = cols_ref[off_b:off_b + cd, :]
            for j in range(M2_CONV):
                conv = conv + xs[t + j] * cols_ref[j * cd:(j + 1) * cd, :]
            xc = _silu(conv)
            x_ref[t] = xc[0:M2_W]
            bc_ref[t] = xc[M2_W:]
            dt = _softplus(z_ref[t, M2_W + cd:M2_W + cd + ROWS, :] + cols_ref[off_dt:off_dt + ROWS, :])
            dt_ref[t, 0] = dt
            dt_ref[t, 1] = jnp.exp(dt * a_neg)

    g = h // (M2_HEADS // M2_GROUPS)
    bs = pl.ds(pl.multiple_of(g * M2_STATE, M2_STATE), M2_STATE)
    cs = pl.ds(pl.multiple_of((M2_GROUPS + g) * M2_STATE, M2_STATE), M2_STATE)
    rid = lax.broadcasted_iota(jnp.int32, (ROWS, 1), 0)
    pick = lambda tile: jnp.sum(jnp.where(rid == h, tile, 0.0), axis=0, keepdims=True)
    dt = [pick(dt_ref[t, 0]) for t in range(nt)]
    da = [pick(dt_ref[t, 1]) for t in range(nt)]

    def body(p8, carry):
        r0 = pl.multiple_of(h * HEAD_DIM + p8 * ROWS, ROWS)
        xt = [x_ref[t, pl.ds(r0, ROWS), :] * dt[t] for t in range(nt)]
        ys = [[] for _ in range(nt)]
        for j in range(ROWS):
            s = s_ref[0, p8 * ROWS + j]
            for t in range(nt):
                s = s * da[t] + xt[t][j:j + 1, :] * bc_ref[t, bs, :]
                ys[t].append(jnp.sum(s * bc_ref[t, cs, :], axis=0, keepdims=True))
            so_ref[0, p8 * ROWS + j] = s
        for t in range(nt):
            ybuf_ref[t, pl.ds(r0, ROWS), :] = _rows_to_tile(ys[t])
        return carry

    lax.fori_loop(0, HEAD_DIM // ROWS, body, 0)

    @pl.when(h == pl.num_programs(0) - 1)
    def _():
        off_d = off_dt + 2 * ROWS
        for t in range(nt):
            y = ybuf_ref[t] + cols_ref[off_d:off_d + M2_W, :] * x_ref[t]
            y = y * _silu(z_ref[t, 0:M2_W, :])
            ms = jnp.mean(y * y, axis=0, keepdims=True)
            y_ref[t] = y * lax.rsqrt(ms + NORM_EPS) * cols_ref[off_d + M2_W:off_d + 2 * M2_W, :]


def _dec_call(kern, name, acts, state, sl, layer_params, consts, layer, y_width, scratch, earlier=()):
    nt, _, nb = acts[0].shape
    heads = state.shape[1]
    whole = lambda a: pl.BlockSpec(a.shape, lambda hd: (0,) * a.ndim, pipeline_mode=pl.Buffered(1))
    blk = (1,) + state.shape[2:]
    n_in = len(acts) + 1 + len(layer_params) + len(consts)
    n_old = len(earlier)

    def body(*refs):
        ins, old, (y_ref, so_ref), scr = (refs[:n_in], refs[n_in:n_in + n_old], refs[n_in + n_old:n_in + n_old + 2],
                                          refs[n_in + n_old + 2:])
        for j, ref in enumerate(old):
            so_ref[j] = ref[...]
        kern(*ins, y_ref, so_ref.at[n_old] if n_old else so_ref, *scr)

    if n_old:
        so_spec = pl.BlockSpec((n_old + 1,) + blk, lambda hd: (0, hd, 0, 0, 0))
        so_shape = (n_old + 1,) + state.shape[1:]
    else:
        so_spec = pl.BlockSpec(blk, lambda hd: (hd, 0, 0, 0))
        so_shape = state.shape[1:]
    return pl.pallas_call(
        body,
        grid=(heads,),
        in_specs=[whole(a) for a in acts]
                 + [pl.BlockSpec((None,) + blk, lambda hd: (sl, hd, 0, 0, 0))]
                 + [_layer_spec(a, layer) for a in layer_params] + [_const_spec(a.shape) for a in consts]
                 + [pl.BlockSpec(blk, lambda hd: (hd, 0, 0, 0)) for _ in earlier],
        out_specs=[pl.BlockSpec((nt, y_width, nb), lambda hd: (0, 0, 0)), so_spec],
        out_shape=[jax.ShapeDtypeStruct((nt, y_width, nb), F32), jax.ShapeDtypeStruct(so_shape, F32)],
        scratch_shapes=scratch,
        compiler_params=pltpu.CompilerParams(dimension_semantics=("arbitrary",), vmem_limit_bytes=VMEM_LIMIT),
        name=name,
    )(*acts, state, *layer_params, *consts, *earlier)


def _rows(v):
    return v.reshape(v.shape[0], 1, -1).astype(F32)


def _pad_rows(m, top, total):
    return jnp.zeros((m.shape[0], total, m.shape[2]), F32).at[:, top:top + m.shape[1]].set(m).astype(BF16)


def _pad_lanes(v):
    return jnp.zeros((v.shape[0], 1, LANE), F32).at[:, 0, :v.shape[1]].set(v)


def _stacked_params(ffn1_norm, ffn1_w_gate, ffn1_w_up, ffn1_w_down, mix_norm, w_in, rwkv_mu, rwkv_w0, rwkv_w2,
                    rwkv_a0, rwkv_a2, rwkv_g2, rwkv_k_k, rwkv_k_a, rwkv_r_k, rwkv_ln_w, rwkv_ln_b, gla_gate_w2,
                    gla_gate_b, gla_norm, mamba_conv_w, mamba_conv_b, mamba_dt_bias, mamba_A_log, mamba_D, mamba_norm,
                    w_out, ffn2_norm, ffn2_w_gate, ffn2_w_up, ffn2_w_down, ple_norm, ple_w_gate, ple_w_proj):
    depth = w_in.shape[0]
    w = w_in.astype(BF16)
    g0 = RWKV_PROJ
    m0 = RWKV_PROJ + GLA_PROJ
    qkv_end = 2 * GLA_KEY_W + GLA_W
    zeros = lambda n: jnp.zeros((depth, D_MODEL, n), BF16)
    win = jnp.concatenate([
        w[:, :, :g0 + qkv_end], w[:, :, g0 + qkv_end + GLA_GATE_LORA:m0],
        w[:, :, g0 + qkv_end:g0 + qkv_end + GLA_GATE_LORA], zeros(LANE - GLA_GATE_LORA),
        w[:, :, m0:], zeros(LANE - M2_HEADS)], axis=2)
    pre = (_rows(ffn1_norm), ffn1_w_gate.astype(BF16), ffn1_w_up.astype(BF16), ffn1_w_down.astype(BF16),
           _rows(mix_norm), win)
    rwkv = (_rows(rwkv_mu), _rows(rwkv_w0), _pad_rows(rwkv_w2, 0, RWKV_LORA), _rows(rwkv_a0),
            _pad_rows(rwkv_a2, RWKV_DECAY_LORA, RWKV_LORA),
            _pad_rows(rwkv_g2, RWKV_DECAY_LORA + RWKV_A_LORA, RWKV_LORA),
            _rows(rwkv_k_k), _rows(rwkv_k_a), _rows(rwkv_r_k), _rows(rwkv_ln_w), _rows(rwkv_ln_b))
    gla = (_pad_rows(gla_gate_w2, 0, LANE), _rows(gla_gate_b), _rows(jnp.tile(gla_norm, (1, GLA_HEADS))))
    ssd = (mamba_conv_w.astype(F32), _rows(mamba_conv_b), _pad_lanes(mamba_dt_bias), _pad_lanes(mamba_A_log),
           _rows(jnp.repeat(mamba_D, HEAD_DIM, axis=1)), _rows(mamba_norm))
    post = (w_out.astype(BF16), _rows(ffn2_norm), ffn2_w_gate.astype(BF16), ffn2_w_up.astype(BF16),
            ffn2_w_down.astype(BF16), _rows(ple_norm), ple_w_gate.astype(BF16), ple_w_proj.astype(BF16))
    return pre, rwkv, gla, ssd, post


def _decode_params(nb, rwkv_mu, rwkv_w0, rwkv_w2, rwkv_a0, rwkv_a2, rwkv_g2, rwkv_k_k, rwkv_k_a, rwkv_r_k,
                   rwkv_ln_w, rwkv_ln_b, gla_gate_w2, gla_gate_b, gla_norm, mamba_conv_w, mamba_conv_b,
                   mamba_dt_bias, mamba_A_log, mamba_D, mamba_norm):
    depth = rwkv_mu.shape[0]
    slab = lambda vs: jnp.broadcast_to(jnp.concatenate([v.reshape(depth, -1) for v in vs], axis=1)[:, :, None],
                                       (depth, sum(v[0].size for v in vs), nb)).astype(F32)
    tr = lambda m: jnp.swapaxes(m, 1, 2)
    rwkv = (slab((rwkv_mu, rwkv_w0, rwkv_a0, rwkv_k_k, rwkv_k_a, rwkv_r_k, rwkv_ln_w, rwkv_ln_b)),
            tr(_pad_rows(rwkv_w2, 0, RWKV_LORA)), tr(_pad_rows(rwkv_a2, RWKV_DECAY_LORA, RWKV_LORA)),
            tr(_pad_rows(rwkv_g2, RWKV_DECAY_LORA + RWKV_A_LORA, RWKV_LORA)))
    gla = (slab((gla_gate_b, jnp.tile(gla_norm, (1, GLA_HEADS)))), tr(_pad_rows(gla_gate_w2, 0, LANE)))
    ssd = (slab((mamba_conv_w, mamba_conv_b, mamba_dt_bias, mamba_A_log, jnp.repeat(mamba_D, HEAD_DIM, axis=1),
                 mamba_norm)),)
    return rwkv, gla, ssd


def _sample_layer(x, p, state, params, dparams, layer, final_norm, *, batch, seq, final, earlier):
    pre, _, _, _, post = params
    d_rwkv, d_gla, d_ssd = dparams
    shift0, wkv0, gla0, conv0, ssm0 = state
    old_wkv, old_gla, old_ssm = earlier
    n = batch * seq
    x1, zr_t, zg_t, zm_t = _pre_call(x, layer, *pre, n, steps=seq)
    shift1 = zr_t[seq - 1].T
    xbc = jnp.transpose(zm_t[:, M2_W:M2_W + M2_CONV_DIM, :], (2, 0, 1))
    conv1 = jnp.concatenate([conv0[layer], xbc], axis=1)[:, -(M2_CONV - 1):]
    scr = lambda *shape: pltpu.VMEM(shape, F32)
    yr, wkv1 = _dec_call(_rwkv_dec_kernel, "rwkv7_decode", (zr_t,), wkv0, layer,
                         (jnp.swapaxes(shift0, 1, 2),) + d_rwkv, (_HEAD_ONES(),), layer, RWKV_W,
                         [scr(seq, 6, RWKV_W, batch), scr(seq, RWKV_W, batch), scr(seq, 2, RWKV_W, batch)], old_wkv)
    yg, gla1 = _dec_call(_gla_dec_kernel, "gla_decode", (zg_t,), gla0, layer, d_gla, (_HEAD_ONES(),), layer, GLA_W,
                         [scr(seq, 3, GLA_KEY_W, batch), scr(seq, GLA_W, batch), scr(seq, GLA_W, batch)], old_gla)
    conv_t = jnp.transpose(conv0, (0, 2, 3, 1))
    ym, ssm1 = _dec_call(_ssd_dec_kernel, "ssd_decode", (zm_t,), ssm0, layer, (conv_t,) + d_ssd, (), layer, M2_W,
                         [scr(seq, M2_W, batch), scr(seq, 2 * M2_GROUPS * M2_STATE, batch),
                          scr(seq, 2, ROWS, batch), scr(seq, M2_W, batch)], old_ssm)
    x2 = _post_call(x1, yr, yg, ym, p, layer, *post, final_norm.reshape(1, -1), n, final, steps=seq)
    return x2, (shift1, wkv1, gla1, conv1, ssm1)


def _pad_time(a, total):
    return jnp.pad(a, ((0, 0), (0, total - a.shape[1]), (0, 0)))


def _group_layer(x, p, state, sl, params, layer, final_norm, *, batch, seq, chunks, bb, tm, tm_post, y_dtype, final):
    pre, rwkv, gla, ssd, post = params
    shift0, wkv0, gla0, conv0, ssm0 = state
    x1, zr, zg, zm = _pre_call(x, layer, *pre, tm)
    zr = zr.reshape(batch, seq, RWKV_PROJ)
    zg = zg.reshape(batch, seq, GLA_PAD)
    zm = zm.reshape(batch, seq, M2_PAD)
    xbc = zm[:, :, M2_W:M2_W + M2_CONV_DIM]
    shift1 = zr[:, -1]
    conv1 = jnp.concatenate([conv0[sl], xbc[:, -(M2_CONV - 1):]], axis=1)[:, -(M2_CONV - 1):]
    c_r, c_g, c_m = chunks
    pad = max(c_r, c_g, c_m, seq)
    if pad != seq:
        zr, zg, zm = _pad_time(zr, pad), _pad_time(zg, pad), _pad_time(zm, pad)
    valid = lambda ch: seq if seq < ch else ch
    yr, wkv1 = _rwkv_call(zr, shift0[:, :, None, :], wkv0, sl, rwkv, layer, bb, c_r, valid(c_r), y_dtype)
    yg, gla1 = _gla_call(zg, gla0, sl, gla, layer, bb, c_g, valid(c_g), y_dtype)
    ym, ssm1 = _ssd_call(zm, conv0, ssm0, sl, ssd, layer, bb, c_m, valid(c_m), y_dtype)
    n = batch * seq
    flat = lambda y: y[:, :seq].reshape(n, y.shape[-1])
    x2 = _post_call(x1, flat(yr), flat(yg), flat(ym), p, layer, *post, final_norm.reshape(1, -1), tm_post, final)
    return x2, (shift1, wkv1, gla1, conv1, ssm1)


def kernel(x_prompt, x_sample, p_prompt, p_sample, state_rwkv_shift, state_rwkv_wkv, state_gla, state_mamba_conv, state_mamba_ssm, ffn1_norm, ffn1_w_gate, ffn1_w_up, ffn1_w_down, mix_norm, w_in, rwkv_mu, rwkv_w0, rwkv_w2, rwkv_a0, rwkv_a2, rwkv_g2, rwkv_k_k, rwkv_k_a, rwkv_r_k, rwkv_ln_w, rwkv_ln_b, gla_gate_w2, gla_gate_b, gla_norm, mamba_conv_w, mamba_conv_b, mamba_dt_bias, mamba_A_log, mamba_D, mamba_norm, w_out, ffn2_norm, ffn2_w_gate, ffn2_w_up, ffn2_w_down, ple_norm, ple_w_gate, ple_w_proj, final_norm):
    depth = w_in.shape[0]
    nb, seq, _ = x_prompt.shape
    db, dseq, _ = x_sample.shape
    weights = (ffn1_norm, ffn1_w_gate, ffn1_w_up, ffn1_w_down, mix_norm, w_in, rwkv_mu, rwkv_w0, rwkv_w2, rwkv_a0,
               rwkv_a2, rwkv_g2, rwkv_k_k, rwkv_k_a, rwkv_r_k, rwkv_ln_w, rwkv_ln_b, gla_gate_w2, gla_gate_b,
               gla_norm, mamba_conv_w, mamba_conv_b, mamba_dt_bias, mamba_A_log, mamba_D, mamba_norm, w_out,
               ffn2_norm, ffn2_w_gate, ffn2_w_up, ffn2_w_down, ple_norm, ple_w_gate, ple_w_proj)
    xp = x_prompt.reshape(nb * seq, D_MODEL)
    xs = jnp.swapaxes(x_sample, 0, 1).reshape(dseq * db, D_MODEL)
    pp = p_prompt.reshape(depth, nb * seq, PLE_DIM)
    ps = jnp.swapaxes(p_sample, 1, 2).reshape(depth, dseq * db, PLE_DIM)
    params = _stacked_params(*weights)
    fresh = (jnp.zeros((1, nb, RWKV_PROJ), F32), jnp.zeros((1, nb, RWKV_HEADS, HEAD_DIM, HEAD_DIM), F32),
             jnp.zeros((1, nb, GLA_HEADS, GLA_DK, HEAD_DIM), F32),
             jnp.zeros((1, nb, M2_CONV - 1, M2_CONV_DIM), F32),
             jnp.zeros((1, nb, M2_HEADS, HEAD_DIM, M2_STATE), F32))
    dparams = _decode_params(db, rwkv_mu, rwkv_w0, rwkv_w2, rwkv_a0, rwkv_a2, rwkv_g2, rwkv_k_k, rwkv_k_a, rwkv_r_k,
                             rwkv_ln_w, rwkv_ln_b, gla_gate_w2, gla_gate_b, gla_norm, mamba_conv_w, mamba_conv_b,
                             mamba_dt_bias, mamba_A_log, mamba_D, mamba_norm)
    batch_last = lambda s: jnp.moveaxis(s, 1, -1)
    past = (state_rwkv_shift, batch_last(state_rwkv_wkv), batch_last(state_gla), state_mamba_conv,
            batch_last(state_mamba_ssm))
    p_states, s_states = [], []
    for i in range(depth):
        final = i == depth - 1
        xp, st_p = _group_layer(xp, pp, fresh, 0, params, i, final_norm,
                                batch=nb, seq=seq, chunks=(RWKV_CHUNK, GLA_CHUNK, SSD_CHUNK), bb=8, tm=512,
                                tm_post=1024, y_dtype=BF16, final=final)
        earlier = tuple(tuple(s[j] for s in s_states) if final else () for j in (1, 2, 4))
        xs, st_s = _sample_layer(xs, ps, past, params, dparams, i, final_norm, batch=db, seq=dseq, final=final,
                                 earlier=earlier)
        p_states.append(st_p)
        s_states.append(st_s)
    stack = lambda sts, j: jnp.stack([s[j] for s in sts])
    mats = s_states[-1] if depth > 1 else tuple(s[None] for s in s_states[-1])
    s_out = tuple(jnp.moveaxis(mats[j], -1, 1) if j in (1, 2, 4) else stack(s_states, j) for j in range(5))
    return ((xp.reshape(nb, seq, D_MODEL), jnp.swapaxes(xs.reshape(dseq, db, D_MODEL), 0, 1))
            + tuple(stack(p_states, j) for j in range(5)) + s_out)
```

```python
import functools

import jax
import jax.numpy as jnp
from jax import lax
from jax.experimental import pallas as pl
from jax.experimental.pallas import tpu as pltpu

F32 = jnp.float32
BF16 = jnp.bfloat16

D_MODEL = 1024
D_FF = 2816
PLE_DIM = 256
HEAD_DIM = 64
NORM_EPS = 1e-6

RWKV_W = 256
RWKV_HEADS = 4
RWKV_GN_EPS = 64e-5
RWKV_PROJ = 896
RWKV_LORA = 128
RWKV_DECAY_LORA = 32
RWKV_A_LORA = 32

GLA_W = 256
GLA_HEADS = 4
GLA_DK = 32
GLA_KEY_W = 128
GLA_GATE_LORA = 16
GLA_TAU = 16.0
GLA_PROJ = 784
GLA_PAD = 896

M2_W = 512
M2_HEADS = 8
M2_STATE = 64
M2_GROUPS = 2
M2_CONV = 4
M2_CONV_DIM = 768
M2_PROJ = 1288
M2_PAD = 1408

LANE = 128
SUBLANE = 8
FF_CHUNK = 256
VMEM_LIMIT = 56 * 1024 * 1024

RWKV_CHUNK = 128
GLA_CHUNK = 64
SSD_CHUNK = 128


def _bdot(a, b):
    return jnp.dot(a.astype(BF16), b.astype(BF16), preferred_element_type=F32)


def _bdot_nt(a, b):
    return lax.dot_general(a.astype(BF16), b.astype(BF16), (((1,), (1,)), ((), ())),
                           preferred_element_type=F32)


def _bdot_tn(a, b):
    return lax.dot_general(a.astype(BF16), b.astype(BF16), (((0,), (0,)), ((), ())),
                           preferred_element_type=F32)


def _split3(x):
    hi = x.astype(BF16)
    r1 = x - hi.astype(F32)
    mid = r1.astype(BF16)
    lo = (r1 - mid.astype(F32)).astype(BF16)
    return hi, mid, lo


def _ldot3(mat, x):
    return sum(jnp.dot(mat, p, preferred_element_type=F32) for p in _split3(x))


def _rdot2(x, mat):
    return sum(jnp.dot(p, mat, preferred_element_type=F32) for p in _split3(x)[:2])


def _nt3(mat, x):
    return sum(lax.dot_general(mat, p, (((1,), (1,)), ((), ())), preferred_element_type=F32)
               for p in _split3(x))


def _softplus(x):
    return jnp.maximum(x, 0.0) + jnp.log1p(jnp.exp(-jnp.abs(x)))


def _sigmoid(x):
    return 1.0 / (1.0 + jnp.exp(-x))


def _silu(x):
    return x * _sigmoid(x)


def _rms(x, w):
    return x * lax.rsqrt(jnp.mean(x * x, axis=-1, keepdims=True) + NORM_EPS) * w


def _const_spec(shape):
    nd = len(shape)
    return pl.BlockSpec(shape, lambda *_: (0,) * nd, pipeline_mode=pl.Buffered(1))


def _layer_spec(a, layer):
    nd = a.ndim - 1
    return pl.BlockSpec((None,) + a.shape[1:], lambda *_: (layer,) + (0,) * nd, pipeline_mode=pl.Buffered(1))


def _ffn(x, nw, wg_ref, wu_ref, wd_ref, act_ref):
    xn = _rms(x, nw).astype(BF16)
    for c in range(D_FF // FF_CHUNK):
        sl = slice(c * FF_CHUNK, (c + 1) * FF_CHUNK)
        hg = jnp.dot(xn, wg_ref[:, sl], preferred_element_type=F32)
        hu = jnp.dot(xn, wu_ref[:, sl], preferred_element_type=F32)
        act_ref[:, sl] = (_silu(hg) * hu).astype(BF16)
    return x + 0.5 * jnp.dot(act_ref[...], wd_ref[...], preferred_element_type=F32)


def _pre_kernel(x_ref, n1_ref, wg_ref, wu_ref, wd_ref, nm_ref, win_ref,
                x1_ref, zr_ref, zg_ref, zm_ref, act_ref, *, steps):
    x1 = _ffn(x_ref[...], n1_ref[...], wg_ref, wu_ref, wd_ref, act_ref)
    x1_ref[...] = x1
    h = _rms(x1, nm_ref[...]).astype(BF16)
    cols = (slice(0, RWKV_PROJ), slice(RWKV_PROJ, RWKV_PROJ + GLA_PAD), slice(RWKV_PROJ + GLA_PAD, None))
    for ref, cs in zip((zr_ref, zg_ref, zm_ref), cols):
        z = jnp.dot(h, win_ref[:, cs], preferred_element_type=F32)
        if steps:
            nb = z.shape[0] // steps
            for t in range(steps):
                ref[t] = z[t * nb:(t + 1) * nb].T
        else:
            ref[...] = z


def _pre_call(x, layer, n1, wg, wu, wd, nm, win, tm, steps=0):
    n = x.shape[0]
    tok = lambda w: pl.BlockSpec((tm, w), lambda i: (i, 0))
    if steps:
        assert n == tm and n % steps == 0
        z_spec = lambda w: pl.BlockSpec((steps, w, n // steps), lambda i: (0, 0, 0))
        z_shape = lambda w: jax.ShapeDtypeStruct((steps, w, n // steps), F32)
    else:
        z_spec = tok
        z_shape = lambda w: jax.ShapeDtypeStruct((n, w), F32)
    return pl.pallas_call(
        functools.partial(_pre_kernel, steps=steps),
        grid=(n // tm,),
        in_specs=[tok(D_MODEL)] + [_layer_spec(a, layer) for a in (n1, wg, wu, wd, nm, win)],
        out_specs=[tok(D_MODEL), z_spec(RWKV_PROJ), z_spec(GLA_PAD), z_spec(M2_PAD)],
        out_shape=[jax.ShapeDtypeStruct((n, D_MODEL), F32), z_shape(RWKV_PROJ), z_shape(GLA_PAD), z_shape(M2_PAD)],
        scratch_shapes=[pltpu.VMEM((tm, D_FF), BF16)],
        compiler_params=pltpu.CompilerParams(dimension_semantics=("parallel",), vmem_limit_bytes=VMEM_LIMIT),
        name="pre_ffn_inproj",
    )(x, n1, wg, wu, wd, nm, win)


def _post_kernel(x_ref, yr_ref, yg_ref, ym_ref, p_ref, wo_ref, n2_ref, wg_ref, wu_ref, wd_ref,
                 np_ref, pg_ref, pp_ref, nf_ref, o_ref, act_ref, *, final, steps):
    rows = lambda ref: (jnp.concatenate([ref[t].T for t in range(steps)], axis=0) if steps else ref[...])
    x = x_ref[...]
    x = x + jnp.dot(rows(yr_ref).astype(BF16), wo_ref[0:RWKV_W, :], preferred_element_type=F32)
    x = x + jnp.dot(rows(yg_ref).astype(BF16), wo_ref[RWKV_W:RWKV_W + GLA_W, :], preferred_element_type=F32)
    x = x + jnp.dot(rows(ym_ref).astype(BF16), wo_ref[RWKV_W + GLA_W:, :], preferred_element_type=F32)
    x = _ffn(x, n2_ref[...], wg_ref, wu_ref, wd_ref, act_ref)
    gate = _sigmoid(jnp.dot(_rms(x, np_ref[...]).astype(BF16), pg_ref[...], preferred_element_type=F32))
    x = x + gate * jnp.dot(p_ref[...].astype(BF16), pp_ref[...], preferred_element_type=F32)
    if final:
        x = _rms(x, nf_ref[...])
    o_ref[...] = x


def _post_call(x, yr, yg, ym, p, layer, wo, n2, wg, wu, wd, npn, pg, pp, nf, tm, final, steps=0):
    n = x.shape[0]
    tok = lambda w: pl.BlockSpec((tm, w), lambda i: (i, 0))
    if steps:
        assert n == tm and n % steps == 0
        y_spec = lambda w: pl.BlockSpec((steps, w, n // steps), lambda i: (0, 0, 0))
    else:
        y_spec = tok
    params = (wo, n2, wg, wu, wd, npn, pg, pp)
    consts = params + (nf,)
    return pl.pallas_call(
        functools.partial(_post_kernel, final=final, steps=steps),
        grid=(n // tm,),
        in_specs=[tok(D_MODEL), y_spec(RWKV_W), y_spec(GLA_W), y_spec(M2_W),
                  pl.BlockSpec((None, tm, PLE_DIM), lambda i: (layer, i, 0))]
                 + [_layer_spec(a, layer) for a in params] + [_const_spec(nf.shape)],
        out_specs=tok(D_MODEL),
        out_shape=jax.ShapeDtypeStruct((n, D_MODEL), F32),
        scratch_shapes=[pltpu.VMEM((tm, D_FF), BF16)],
        compiler_params=pltpu.CompilerParams(dimension_semantics=("parallel",), vmem_limit_bytes=VMEM_LIMIT),
        name="post_outproj_ffn_ple",
    )(x, yr, yg, ym, p, *consts)


def _tri_masks(n):
    ri = lax.broadcasted_iota(jnp.int32, (n, n), 0)
    ci = lax.broadcasted_iota(jnp.int32, (n, n), 1)
    return ri >= ci, ri > ci, (ri == ci).astype(F32)


def _rwkv_kernel(z_ref, sh_ref, s0_ref, mu_ref, w0_ref, w2_ref, a0_ref, a2_ref, g2_ref, kk_ref, ka_ref,
                 rk_ref, lnw_ref, lnb_ref, tri_ref, ones_ref,
                 y_ref, s_ref, zbuf_ref, ybuf_ref, *, bb, chunk):
    c = pl.program_id(1)

    @pl.when(c == 0)
    def _():
        s_ref[...] = s0_ref[...]
        zbuf_ref[:, SUBLANE - 1:SUBLANE, :] = sh_ref[...]

    incl, strict, eye = _tri_masks(chunk)
    tri = tri_ref[...]
    ones = ones_ref[...]
    mu = mu_ref[...]
    seqs = range(bb)
    zs = []
    for i in seqs:
        z = z_ref[i]
        zbuf_ref[i, SUBLANE:SUBLANE + chunk, :] = z
        prev = zbuf_ref[i, SUBLANE - 1:SUBLANE - 1 + chunk, :]
        zbuf_ref[i, SUBLANE - 1:SUBLANE, :] = z[chunk - 1:chunk, :]
        zs.append(z + mu * (prev - z))
    r = [x[:, 0:RWKV_W] for x in zs]
    k = [x[:, RWKV_W:2 * RWKV_W] for x in zs]
    v = [x[:, 2 * RWKV_W:3 * RWKV_W] for x in zs]
    lora = [x[:, 3 * RWKV_W:] for x in zs]
    w_lin = [_bdot(jnp.tanh(x), w2_ref[...]) for x in lora]
    a_lin = [_bdot(x, a2_ref[...]) for x in lora]
    gate = [_bdot(_sigmoid(x), g2_ref[...]) for x in lora]
    kk = [x * kk_ref[...] for x in k]
    kk_ss = [_rdot2(x * x, ones) for x in kk]
    logw = [-jnp.exp(-_softplus(-(w0_ref[...] + x)) - 0.5) for x in w_lin]
    a = [_sigmoid(a0_ref[...] + x) for x in a_lin]
    kk = [x / jnp.maximum(jnp.sqrt(ss), 1e-12) for x, ss in zip(kk, kk_ss)]
    k = [x * (1.0 + (ai - 1.0) * ka_ref[...]) for x, ai in zip(k, a)]
    ahat = [-x for x in kk]
    bhat = [x * ai for x, ai in zip(kk, a)]
    g = [_ldot3(tri, x) for x in logw]
    glast = [x[chunk - 1:chunk, :] for x in g]
    e_ng = [jnp.exp(-x) for x in g]
    e_gl = [jnp.exp(gl - x) for gl, x in zip(glast, g)]
    a_l = [ah * jnp.exp(x - lw) for ah, x, lw in zip(ahat, g, logw)]
    r_l = [ri * jnp.exp(x) for ri, x in zip(r, g)]
    b_r = [x * e for x, e in zip(bhat, e_ng)]
    k_r = [x * e for x, e in zip(k, e_ng)]
    k_p = [x * e for x, e in zip(k, e_gl)]
    b_p = [x * e for x, e in zip(bhat, e_gl)]
    dec = [jnp.exp(x) for x in glast]

    probs = [(i, h) for i in seqs for h in range(RWKV_HEADS)]
    hd = lambda xs: [xs[i][:, h * HEAD_DIM:(h + 1) * HEAD_DIM] for i, h in probs]
    a_lh, r_lh, b_rh, k_rh, k_ph, b_ph, vh = hd(a_l), hd(r_l), hd(b_r), hd(k_r), hd(k_p), hd(b_p), hd(v)
    ar = [jnp.concatenate([x, y], axis=0) for x, y in zip(a_lh, r_lh)]
    bk = [jnp.concatenate([x, y], axis=0) for x, y in zip(b_rh, k_rh)]
    gm = [_bdot_nt(x, y) for x, y in zip(ar, bk)]
    a_ab = [jnp.where(strict, x[:chunk, :chunk], 0.0) for x in gm]
    a_ak = [jnp.where(strict, x[:chunk, chunk:], 0.0) for x in gm]
    a_rb = [jnp.where(incl, x[chunk:, :chunk], 0.0) for x in gm]
    a_rk = [jnp.where(incl, x[chunk:, chunk:], 0.0) for x in gm]
    t = [eye + m for m in a_ab]
    x = [_bdot(m, m) for m in a_ab]
    p = 2
    while p < chunk:
        if 2 * p < chunk:
            xt = [_bdot(jnp.concatenate([xi, ti], axis=0), xi) for xi, ti in zip(x, t)]
            x = [m[:chunk] for m in xt]
            t = [ti + m[chunk:] for ti, m in zip(t, xt)]
        else:
            t = [ti + _bdot(ti, xi) for xi, ti in zip(x, t)]
        p *= 2
    akv = [_bdot(m, vi) for m, vi in zip(a_ak, vh)]
    w_t = [_bdot(ti, m) for ti, m in zip(t, a_lh)]
    u_t = [_bdot(ti, m) for ti, m in zip(t, akv)]
    r_t = [ri + _bdot(m, wi) for ri, m, wi in zip(r_lh, a_rb, w_t)]
    y_t = [_bdot(m, vi) + _bdot(n, ui) for m, vi, n, ui in zip(a_rk, vh, a_rb, u_t)]
    m_s = [_bdot_tn(wi, bi) for wi, bi in zip(w_t, b_ph)]
    q_s = [_bdot_tn(jnp.concatenate([vi, ui], axis=0), jnp.concatenate([ki, bi], axis=0))
           for vi, ui, ki, bi in zip(vh, u_t, k_ph, b_ph)]
    s_old = [s_ref[i, h] for i, h in probs]
    y_h = [_bdot_nt(ri, si) + yi for ri, si, yi in zip(r_t, s_old, y_t)]
    s_m = [_bdot(si, mi) for si, mi in zip(s_old, m_s)]
    for j, (i, h) in enumerate(probs):
        hs = slice(h * HEAD_DIM, (h + 1) * HEAD_DIM)
        ybuf_ref[i, :, hs] = y_h[j]
        s_ref[i, h] = s_old[j] * dec[i][:, hs] + s_m[j] + q_s[j]
    y = [ybuf_ref[i] for i in seqs]
    mean = [_rdot2(x, ones) * (1.0 / HEAD_DIM) for x in y]
    yc = [x - m for x, m in zip(y, mean)]
    var = [_rdot2(x * x, ones) * (1.0 / HEAD_DIM) for x in yc]
    bonus = [_rdot2(ri * ki * rk_ref[...], ones) * vi for ri, ki, vi in zip(r, k, v)]
    for i in seqs:
        yn = yc[i] * lax.rsqrt(var[i] + RWKV_GN_EPS) * lnw_ref[...] + lnb_ref[...]
        y_ref[i] = ((yn + bonus[i]) * gate[i]).astype(y_ref.dtype)


def _state_spec(a, bb, sl):
    nd = a.ndim - 2
    return pl.BlockSpec((None, bb) + a.shape[2:], lambda i, c: (sl, i) + (0,) * nd)


def _rwkv_call(z, shift, s0, sl, params, layer, bb, chunk, out_dtype):
    b, l, _ = z.shape
    grid = (b // bb, l // chunk)
    tri = jnp.tril(jnp.ones((chunk, chunk), F32)).astype(BF16)
    consts = (tri, _HEAD_ONES())
    return pl.pallas_call(
        functools.partial(_rwkv_kernel, bb=bb, chunk=chunk),
        grid=grid,
        in_specs=[pl.BlockSpec((bb, chunk, RWKV_PROJ), lambda i, c: (i, c, 0)),
                  _state_spec(shift, bb, sl), _state_spec(s0, bb, sl)]
                 + [_layer_spec(a, layer) for a in params] + [_const_spec(a.shape) for a in consts],
        out_specs=[pl.BlockSpec((bb, chunk, RWKV_W), lambda i, c: (i, c, 0)),
                   pl.BlockSpec((bb, RWKV_HEADS, HEAD_DIM, HEAD_DIM), lambda i, c: (i, 0, 0, 0))],
        out_shape=[jax.ShapeDtypeStruct((b, l, RWKV_W), out_dtype),
                   jax.ShapeDtypeStruct((b, RWKV_HEADS, HEAD_DIM, HEAD_DIM), F32)],
        scratch_shapes=[pltpu.VMEM((bb, SUBLANE + chunk, RWKV_PROJ), F32),
                        pltpu.VMEM((bb, chunk, RWKV_W), F32)],
        compiler_params=pltpu.CompilerParams(dimension_semantics=("arbitrary", "arbitrary"),
                                             vmem_limit_bytes=VMEM_LIMIT),
        name="rwkv7_mixer",
    )(z, shift, s0, *params, *consts)


def _gla_kernel(z_ref, s0_ref, gw_ref, gb_ref, nw_ref, tri_ref, ones_ref, eye_ref,
                y_ref, s_ref, obuf_ref, *, bb, chunk):
    c = pl.program_id(1)

    @pl.when(c == 0)
    def _():
        s_ref[...] = s0_ref[...]

    incl, _, _ = _tri_masks(chunk)
    tri = tri_ref[...]
    ones = ones_ref[...]
    seqs = range(bb)
    z = [z_ref[i] for i in seqs]
    q = [x[:, 0:GLA_KEY_W] * (GLA_DK ** -0.5) for x in z]
    k = [x[:, GLA_KEY_W:2 * GLA_KEY_W] for x in z]
    v = [x[:, 2 * GLA_KEY_W:2 * GLA_KEY_W + GLA_W] for x in z]
    og = [x[:, 2 * GLA_KEY_W + GLA_W:2 * GLA_KEY_W + 2 * GLA_W] for x in z]
    lin = [_bdot(x[:, 2 * GLA_KEY_W + 2 * GLA_W:], gw_ref[...]) for x in z]
    log_a = [-_softplus(-(x + gb_ref[...])) * (1.0 / GLA_TAU) for x in lin]
    b = [_ldot3(tri, x) for x in log_a]
    blast = [x[chunk - 1:chunk, :] for x in b]
    qe = [x * jnp.exp(bi) for x, bi in zip(q, b)]
    ke = [x * jnp.exp(-bi) for x, bi in zip(k, b)]
    kl = [x * jnp.exp(bl - bi) for x, bl, bi in zip(k, blast, b)]
    dec = [_nt3(eye_ref[...], jnp.broadcast_to(jnp.exp(bl), (HEAD_DIM, GLA_KEY_W))) for bl in blast]

    probs = [(i, h) for i in seqs for h in range(GLA_HEADS)]
    kd = lambda xs: [xs[i][:, h * GLA_DK:(h + 1) * GLA_DK] for i, h in probs]
    qeh, keh, klh = kd(qe), kd(ke), kd(kl)
    vh = [v[i][:, h * HEAD_DIM:(h + 1) * HEAD_DIM] for i, h in probs]
    att = [jnp.where(incl, _bdot_nt(x, y), 0.0) for x, y in zip(qeh, keh)]
    s_old = [s_ref[i, h] for i, h in probs]
    o_h = [_bdot(ai, vi) + _bdot(qi, si) for ai, vi, qi, si in zip(att, vh, qeh, s_old)]
    kv = [_bdot_tn(ki, vi) for ki, vi in zip(klh, vh)]
    for j, (i, h) in enumerate(probs):
        obuf_ref[i, :, h * HEAD_DIM:(h + 1) * HEAD_DIM] = o_h[j]
        s_ref[i, h] = s_old[j] * dec[i][h * GLA_DK:(h + 1) * GLA_DK, :] + kv[j]
    o = [obuf_ref[i] for i in seqs]
    ms = [_rdot2(x * x, ones) * (1.0 / HEAD_DIM) for x in o]
    for i in seqs:
        y_ref[i] = (o[i] * lax.rsqrt(ms[i] + NORM_EPS) * nw_ref[...] * _silu(og[i])).astype(y_ref.dtype)


def _gla_call(z, s0, sl, params, layer, bb, chunk, out_dtype):
    b, l, _ = z.shape
    tri = jnp.tril(jnp.ones((chunk, chunk), F32)).astype(BF16)
    consts = (tri, _HEAD_ONES(), jnp.eye(LANE, dtype=BF16))
    return pl.pallas_call(
        functools.partial(_gla_kernel, bb=bb, chunk=chunk),
        grid=(b // bb, l // chunk),
        in_specs=[pl.BlockSpec((bb, chunk, GLA_PAD), lambda i, c: (i, c, 0)), _state_spec(s0, bb, sl)]
                 + [_layer_spec(a, layer) for a in params] + [_const_spec(a.shape) for a in consts],
        out_specs=[pl.BlockSpec((bb, chunk, GLA_W), lambda i, c: (i, c, 0)),
                   pl.BlockSpec((bb, GLA_HEADS, GLA_DK, HEAD_DIM), lambda i, c: (i, 0, 0, 0))],
        out_shape=[jax.ShapeDtypeStruct((b, l, GLA_W), out_dtype),
                   jax.ShapeDtypeStruct((b, GLA_HEADS, GLA_DK, HEAD_DIM), F32)],
        scratch_shapes=[pltpu.VMEM((bb, chunk, GLA_W), F32)],
        compiler_params=pltpu.CompilerParams(dimension_semantics=("arbitrary", "arbitrary"),
                                             vmem_limit_bytes=VMEM_LIMIT),
        name="gla_mixer",
    )(z, s0, *params, *consts)


def _ssd_kernel(z_ref, cp_ref, s0_ref, cw_ref, cb_ref, dtb_ref, alog_ref, dsk_ref, nw_ref, tri_ref, eye_ref,
                esel_ref, y_ref, s_ref, xbuf_ref, ybuf_ref, *, bb, chunk):
    c = pl.program_id(1)

    @pl.when(c == 0)
    def _():
        s_ref[...] = s0_ref[...]
        xbuf_ref[:, SUBLANE - (M2_CONV - 1):SUBLANE, :] = cp_ref[...]

    incl, _, _ = _tri_masks(chunk)
    tri = tri_ref[...]
    lane = lax.broadcasted_iota(jnp.int32, (1, LANE), 1)
    rep = M2_HEADS // M2_GROUPS
    seqs = range(bb)
    zg, xc, dt = [], [], []
    live = lane < M2_HEADS
    for i in seqs:
        z = z_ref[i]
        x = z[:, M2_W:M2_W + M2_CONV_DIM]
        xbuf_ref[i, SUBLANE:SUBLANE + chunk, :] = x
        conv = cb_ref[...] + x * cw_ref[M2_CONV - 1:M2_CONV, :]
        for j in range(1, M2_CONV):
            conv = conv + xbuf_ref[i, SUBLANE - j:SUBLANE - j + chunk, :] * cw_ref[M2_CONV - 1 - j:M2_CONV - j, :]
        xbuf_ref[i, 0:SUBLANE, :] = xbuf_ref[i, chunk:chunk + SUBLANE, :]
        zg.append(z[:, 0:M2_W])
        xc.append(_silu(conv))
        dt.append(jnp.where(live, _softplus(z[:, M2_W + M2_CONV_DIM:] + dtb_ref[...]), 0.0))
    a_neg = -jnp.exp(alog_ref[...])
    cum = [_ldot3(tri, x * a_neg) for x in dt]
    cum_t = [_nt3(eye_ref[0:SUBLANE, :], x) for x in cum]
    dt_b = [_rdot2(x, esel_ref[...]) for x in dt]
    grp = [(i, g) for i in seqs for g in range(M2_GROUPS)]
    bg = [xc[i][:, M2_W + g * M2_STATE:M2_W + (g + 1) * M2_STATE] for i, g in grp]
    cg = [xc[i][:, M2_W + (M2_GROUPS + g) * M2_STATE:M2_W + (M2_GROUPS + g + 1) * M2_STATE] for i, g in grp]
    cb = [_bdot_nt(x, y) for x, y in zip(cg, bg)]
    s_grp = [jnp.concatenate([s_ref[i, g * rep + hh] for hh in range(rep)], axis=0) for i, g in grp]
    y_st = [_bdot_nt(x, s) for x, s in zip(cg, s_grp)]
    probs = [(i, h) for i in seqs for h in range(M2_HEADS)]
    ccb = [jnp.broadcast_to(cum[i][:, h:h + 1], (chunk, LANE)) for i, h in probs]
    low = lane < HEAD_DIM
    cum_b = [jnp.concatenate([jnp.where(low, ccb[i * M2_HEADS + 2 * j], ccb[i * M2_HEADS + 2 * j + 1])
                              for j in range(M2_HEADS // 2)], axis=1) for i in seqs]
    xs = [x[:, 0:M2_W] for x in xc]
    xdt = [x * d for x, d in zip(xs, dt_b)]
    xw = [x * jnp.exp(cb_[chunk - 1:chunk, :] - cb_) for x, cb_ in zip(xdt, cum_b)]
    seg = [jnp.where(incl, jnp.exp(jnp.minimum(ccb[j][:, 0:chunk] - cum_t[i][h:h + 1, :], 0.0)), 0.0)
           for j, (i, h) in enumerate(probs)]
    y_in = [_bdot(cb[i * M2_GROUPS + h // rep] * seg[j], xdt[i][:, h * HEAD_DIM:(h + 1) * HEAD_DIM])
            for j, (i, h) in enumerate(probs)]
    s_in = [_bdot_tn(xw[i][:, g * rep * HEAD_DIM:(g + 1) * rep * HEAD_DIM], bg[j]) for j, (i, g) in enumerate(grp)]
    for j, (i, h) in enumerate(probs):
        ybuf_ref[i, :, h * HEAD_DIM:(h + 1) * HEAD_DIM] = y_in[j]
        gj = i * M2_GROUPS + h // rep
        hh = h % rep
        s_ref[i, h] = (s_grp[gj][hh * HEAD_DIM:(hh + 1) * HEAD_DIM] * jnp.exp(cum[i][chunk - 1:chunk, h:h + 1])
                       + s_in[gj][hh * HEAD_DIM:(hh + 1) * HEAD_DIM])
    for i in seqs:
        y_state = jnp.concatenate([y_st[i * M2_GROUPS + g] for g in range(M2_GROUPS)], axis=1)
        y = ybuf_ref[i] + y_state * jnp.exp(cum_b[i]) + dsk_ref[...] * xs[i]
        y_ref[i] = _rms(y * _silu(zg[i]), nw_ref[...]).astype(y_ref.dtype)


def _ssd_call(z, conv_prev, s0, sl, params, layer, bb, chunk, out_dtype):
    b, l, _ = z.shape
    tri = jnp.tril(jnp.ones((chunk, chunk), F32)).astype(BF16)
    esel = (jnp.arange(LANE)[:, None] == jnp.arange(M2_W)[None, :] // HEAD_DIM).astype(BF16)
    consts = (tri, jnp.eye(LANE, dtype=BF16), esel)
    return pl.pallas_call(
        functools.partial(_ssd_kernel, bb=bb, chunk=chunk),
        grid=(b // bb, l // chunk),
        in_specs=[pl.BlockSpec((bb, chunk, M2_PAD), lambda i, c: (i, c, 0)),
                  _state_spec(conv_prev, bb, sl), _state_spec(s0, bb, sl)]
                 + [_layer_spec(a, layer) for a in params] + [_const_spec(a.shape) for a in consts],
        out_specs=[pl.BlockSpec((bb, chunk, M2_W), lambda i, c: (i, c, 0)),
                   pl.BlockSpec((bb, M2_HEADS, HEAD_DIM, M2_STATE), lambda i, c: (i, 0, 0, 0))],
        out_shape=[jax.ShapeDtypeStruct((b, l, M2_W), out_dtype),
                   jax.ShapeDtypeStruct((b, M2_HEADS, HEAD_DIM, M2_STATE), F32)],
        scratch_shapes=[pltpu.VMEM((bb, SUBLANE + chunk, M2_CONV_DIM), F32),
                        pltpu.VMEM((bb, chunk, M2_W), F32)],
        compiler_params=pltpu.CompilerParams(dimension_semantics=("arbitrary", "arbitrary"),
                                             vmem_limit_bytes=VMEM_LIMIT),
        name="ssd_mixer",
    )(z, conv_prev, s0, *params, *consts)


def _HEAD_ONES():
    idx = jnp.arange(RWKV_W) // HEAD_DIM
    return (idx[:, None] == idx[None, :]).astype(BF16)


ROWS = SUBLANE


def _rows_to_tile(rows):
    rid = lax.broadcasted_iota(jnp.int32, (ROWS, 1), 0)
    out = rows[0]
    for j in range(1, ROWS):
        out = jnp.where(rid == j, rows[j], out)
    return out


def _rwkv_dec_kernel(z_ref, s_ref, sh_ref, cols_ref, w2_ref, a2_ref, g2_ref, ones_ref,
                     y_ref, so_ref, vec_ref, ybuf_ref, post_ref):
    h = pl.program_id(0)
    nt = z_ref.shape[0]
    w = RWKV_W
    ones = ones_ref[...]
    col = lambda j: cols_ref[RWKV_PROJ + j * w:RWKV_PROJ + (j + 1) * w, :]

    @pl.when(h == 0)
    def _():
        mu = cols_ref[0:RWKV_PROJ, :]
        prev = sh_ref[...]
        for t in range(nt):
            z = z_ref[t]
            zs = z + mu * (prev - z)
            prev = z
            r, k, v, lora = zs[0:w], zs[w:2 * w], zs[2 * w:3 * w], zs[3 * w:]
            logw = -jnp.exp(-_softplus(-(col(0) + _bdot(w2_ref[...], jnp.tanh(lora)))) - 0.5)
            a = _sigmoid(col(1) + _bdot(a2_ref[...], lora))
            kk = k * col(2)
            kk = kk / jnp.maximum(jnp.sqrt(_ldot3(ones, kk * kk)), 1e-12)
            k = k * (1.0 + (a - 1.0) * col(3))
            for j, x in enumerate((-kk, jnp.exp(logw), kk * a, k, r, v)):
                vec_ref[t, j] = x
            post_ref[t, 0] = _bdot(g2_ref[...], _sigmoid(lora))
            post_ref[t, 1] = _ldot3(ones, r * k * col(4)) * v

    hs = pl.ds(pl.multiple_of(h * HEAD_DIM, HEAD_DIM), HEAD_DIM)

    def body(v8, carry):
        r0 = pl.multiple_of(h * HEAD_DIM + v8 * ROWS, ROWS)
        vt = [vec_ref[t, 5, pl.ds(r0, ROWS), :] for t in range(nt)]
        ys = [[] for _ in range(nt)]
        for j in range(ROWS):
            s = s_ref[0, v8 * ROWS + j]
            for t in range(nt):
                sa = jnp.sum(s * vec_ref[t, 0, hs, :], axis=0, keepdims=True)
                s = s * vec_ref[t, 1, hs, :] + sa * vec_ref[t, 2, hs, :] + vt[t][j:j + 1, :] * vec_ref[t, 3, hs, :]
                ys[t].append(jnp.sum(s * vec_ref[t, 4, hs, :], axis=0, keepdims=True))
            so_ref[0, v8 * ROWS + j] = s
        for t in range(nt):
            ybuf_ref[t, pl.ds(r0, ROWS), :] = _rows_to_tile(ys[t])
        return carry

    lax.fori_loop(0, HEAD_DIM // ROWS, body, 0)

    @pl.when(h == pl.num_programs(0) - 1)
    def _():
        for t in range(nt):
            y = ybuf_ref[t]
            yc = y - _ldot3(ones, y) * (1.0 / HEAD_DIM)
            var = _ldot3(ones, yc * yc) * (1.0 / HEAD_DIM)
            yn = yc * lax.rsqrt(var + RWKV_GN_EPS) * col(5) + col(6)
            y_ref[t] = (yn + post_ref[t, 1]) * post_ref[t, 0]


def _gla_dec_kernel(z_ref, s_ref, cols_ref, gw_ref, ones_ref, y_ref, so_ref, vec_ref, val_ref, obuf_ref):
    h = pl.program_id(0)
    nt = z_ref.shape[0]

    @pl.when(h == 0)
    def _():
        for t in range(nt):
            z = z_ref[t]
            lin = _bdot(gw_ref[...], z[2 * GLA_KEY_W + 2 * GLA_W:]) + cols_ref[0:GLA_KEY_W, :]
            vec_ref[t, 0] = z[0:GLA_KEY_W] * (GLA_DK ** -0.5)
            vec_ref[t, 1] = z[GLA_KEY_W:2 * GLA_KEY_W]
            vec_ref[t, 2] = jnp.exp(-_softplus(-lin) * (1.0 / GLA_TAU))
            val_ref[t] = z[2 * GLA_KEY_W:2 * GLA_KEY_W + GLA_W]

    vs = pl.ds(pl.multiple_of(h * HEAD_DIM, HEAD_DIM), HEAD_DIM)
    v = [val_ref[t, vs, :] for t in range(nt)]

    def body(k8, acc):
        r0 = pl.multiple_of(h * GLA_DK + k8 * ROWS, ROWS)
        q, k, a = ([vec_ref[t, j, pl.ds(r0, ROWS), :] for t in range(nt)] for j in range(3))
        acc = list(acc)
        for j in range(ROWS):
            s = s_ref[0, k8 * ROWS + j]
            for t in range(nt):
                s = s * a[t][j:j + 1, :] + k[t][j:j + 1, :] * v[t]
                acc[t] = acc[t] + q[t][j:j + 1, :] * s
            so_ref[0, k8 * ROWS + j] = s
        return tuple(acc)

    zero = jnp.zeros((HEAD_DIM, z_ref.shape[2]), F32)
    acc = lax.fori_loop(0, GLA_DK // ROWS, body, (zero,) * nt)
    for t in range(nt):
        obuf_ref[t, vs, :] = acc[t]

    @pl.when(h == pl.num_programs(0) - 1)
    def _():
        for t in range(nt):
            o = obuf_ref[t]
            ms = _ldot3(ones_ref[...], o * o) * (1.0 / HEAD_DIM)
            og = z_ref[t, 2 * GLA_KEY_W + GLA_W:2 * GLA_KEY_W + 2 * GLA_W, :]
            y_ref[t] = o * lax.rsqrt(ms + NORM_EPS) * cols_ref[GLA_KEY_W:, :] * _silu(og)


def _ssd_dec_kernel(z_ref, s_ref, cp_ref, cols_ref, y_ref, so_ref, x_ref, bc_ref, dt_ref, ybuf_ref):
    h = pl.program_id(0)
    nt = z_ref.shape[0]
    cd = M2_CONV_DIM
    off_b = M2_CONV * cd
    off_dt = off_b + cd

    @pl.when(h == 0)
    def _():
        xs = [cp_ref[j] for j in range(M2_CONV - 1)] + [z_ref[t, M2_W:M2_W + cd, :] for t in range(nt)]
        a_neg = -jnp.exp(cols_ref[off_dt + ROWS:off_dt + 2 * ROWS, :])
        for t in range(nt):
            conv = cols_ref[off_b:off_b + cd, :]
            for j in range(M2_CONV):
                conv = conv + xs[t + j] * cols_ref[j * cd:(j + 1) * cd, :]
            xc = _silu(conv)
            x_ref[t] = xc[0:M2_W]
            bc_ref[t] = xc[M2_W:]
            dt = _softplus(z_ref[t, M2_W + cd:M2_W + cd + ROWS, :] + cols_ref[off_dt:off_dt + ROWS, :])
            dt_ref[t, 0] = dt
            dt_ref[t, 1] = jnp.exp(dt * a_neg)

    g = h // (M2_HEADS // M2_GROUPS)
    bs = pl.ds(pl.multiple_of(g * M2_STATE, M2_STATE), M2_STATE)
    cs = pl.ds(pl.multiple_of((M2_GROUPS + g) * M2_STATE, M2_STATE), M2_STATE)
    rid = lax.broadcasted_iota(jnp.int32, (ROWS, 1), 0)
    pick = lambda tile: jnp.sum(jnp.where(rid == h, tile, 0.0), axis=0, keepdims=True)
    dt = [pick(dt_ref[t, 0]) for t in range(nt)]
    da = [pick(dt_ref[t, 1]) for t in range(nt)]

    def body(p8, carry):
        r0 = pl.multiple_of(h * HEAD_DIM + p8 * ROWS, ROWS)
        xt = [x_ref[t, pl.ds(r0, ROWS), :] * dt[t] for t in range(nt)]
        ys = [[] for _ in range(nt)]
        for j in range(ROWS):
            s = s_ref[0, p8 * ROWS + j]
            for t in range(nt):
                s = s * da[t] + xt[t][j:j + 1, :] * bc_ref[t, bs, :]
                ys[t].append(jnp.sum(s * bc_ref[t, cs, :], axis=0, keepdims=True))
            so_ref[0, p8 * ROWS + j] = s
        for t in range(nt):
            ybuf_ref[t, pl.ds(r0, ROWS), :] = _rows_to_tile(ys[t])
        return carry

    lax.fori_loop(0, HEAD_DIM // ROWS, body, 0)

    @pl.when(h == pl.num_programs(0) - 1)
    def _():
        off_d = off_dt + 2 * ROWS
        for t in range(nt):
            y = ybuf_ref[t] + cols_ref[off_d:off_d + M2_W, :] * x_ref[t]
            y = y * _silu(z_ref[t, 0:M2_W, :])
            ms = jnp.mean(y * y, axis=0, keepdims=True)
            y_ref[t] = y * lax.rsqrt(ms + NORM_EPS) * cols_ref[off_d + M2_W:off_d + 2 * M2_W, :]


def _dec_call(kern, name, acts, state, sl, layer_params, consts, layer, y_width, scratch, earlier=()):
    nt, _, nb = acts[0].shape
    heads = state.shape[1]
    whole = lambda a: pl.BlockSpec(a.shape, lambda hd: (0,) * a.ndim, pipeline_mode=pl.Buffered(1))
    blk = (1,) + state.shape[2:]
    n_in = len(acts) + 1 + len(layer_params) + len(consts)
    n_old = len(earlier)

    def body(*refs):
        ins, old, (y_ref, so_ref), scr = (refs[:n_in], refs[n_in:n_in + n_old], refs[n_in + n_old:n_in + n_old + 2],
                                          refs[n_in + n_old + 2:])
        for j, ref in enumerate(old):
            so_ref[j] = ref[...]
        kern(*ins, y_ref, so_ref.at[n_old] if n_old else so_ref, *scr)

    if n_old:
        so_spec = pl.BlockSpec((n_old + 1,) + blk, lambda hd: (0, hd, 0, 0, 0))
        so_shape = (n_old + 1,) + state.shape[1:]
    else:
        so_spec = pl.BlockSpec(blk, lambda hd: (hd, 0, 0, 0))
        so_shape = state.shape[1:]
    return pl.pallas_call(
        body,
        grid=(heads,),
        in_specs=[whole(a) for a in acts]
                 + [pl.BlockSpec((None,) + blk, lambda hd: (sl, hd, 0, 0, 0))]
                 + [_layer_spec(a, layer) for a in layer_params] + [_const_spec(a.shape) for a in consts]
                 + [pl.BlockSpec(blk, lambda hd: (hd, 0, 0, 0)) for _ in earlier],
        out_specs=[pl.BlockSpec((nt, y_width, nb), lambda hd: (0, 0, 0)), so_spec],
        out_shape=[jax.ShapeDtypeStruct((nt, y_width, nb), F32), jax.ShapeDtypeStruct(so_shape, F32)],
        scratch_shapes=scratch,
        compiler_params=pltpu.CompilerParams(dimension_semantics=("arbitrary",), vmem_limit_bytes=VMEM_LIMIT),
        name=name,
    )(*acts, state, *layer_params, *consts, *earlier)


def _rows(v):
    return v.reshape(v.shape[0], 1, -1).astype(F32)


def _pad_rows(m, top, total):
    return jnp.zeros((m.shape[0], total, m.shape[2]), F32).at[:, top:top + m.shape[1]].set(m).astype(BF16)


def _pad_lanes(v):
    return jnp.zeros((v.shape[0], 1, LANE), F32).at[:, 0, :v.shape[1]].set(v)


def _stacked_params(ffn1_norm, ffn1_w_gate, ffn1_w_up, ffn1_w_down, mix_norm, w_in, rwkv_mu, rwkv_w0, rwkv_w2,
                    rwkv_a0, rwkv_a2, rwkv_g2, rwkv_k_k, rwkv_k_a, rwkv_r_k, rwkv_ln_w, rwkv_ln_b, gla_gate_w2,
                    gla_gate_b, gla_norm, mamba_conv_w, mamba_conv_b, mamba_dt_bias, mamba_A_log, mamba_D, mamba_norm,
                    w_out, ffn2_norm, ffn2_w_gate, ffn2_w_up, ffn2_w_down, ple_norm, ple_w_gate, ple_w_proj):
    depth = w_in.shape[0]
    w = w_in.astype(BF16)
    g0 = RWKV_PROJ
    m0 = RWKV_PROJ + GLA_PROJ
    qkv_end = 2 * GLA_KEY_W + GLA_W
    zeros = lambda n: jnp.zeros((depth, D_MODEL, n), BF16)
    win = jnp.concatenate([
        w[:, :, :g0 + qkv_end], w[:, :, g0 + qkv_end + GLA_GATE_LORA:m0],
        w[:, :, g0 + qkv_end:g0 + qkv_end + GLA_GATE_LORA], zeros(LANE - GLA_GATE_LORA),
        w[:, :, m0:], zeros(LANE - M2_HEADS)], axis=2)
    pre = (_rows(ffn1_norm), ffn1_w_gate.astype(BF16), ffn1_w_up.astype(BF16), ffn1_w_down.astype(BF16),
           _rows(mix_norm), win)
    rwkv = (_rows(rwkv_mu), _rows(rwkv_w0), _pad_rows(rwkv_w2, 0, RWKV_LORA), _rows(rwkv_a0),
            _pad_rows(rwkv_a2, RWKV_DECAY_LORA, RWKV_LORA),
            _pad_rows(rwkv_g2, RWKV_DECAY_LORA + RWKV_A_LORA, RWKV_LORA),
            _rows(rwkv_k_k), _rows(rwkv_k_a), _rows(rwkv_r_k), _rows(rwkv_ln_w), _rows(rwkv_ln_b))
    gla = (_pad_rows(gla_gate_w2, 0, LANE), _rows(gla_gate_b), _rows(jnp.tile(gla_norm, (1, GLA_HEADS))))
    ssd = (mamba_conv_w.astype(F32), _rows(mamba_conv_b), _pad_lanes(mamba_dt_bias), _pad_lanes(mamba_A_log),
           _rows(jnp.repeat(mamba_D, HEAD_DIM, axis=1)), _rows(mamba_norm))
    post = (w_out.astype(BF16), _rows(ffn2_norm), ffn2_w_gate.astype(BF16), ffn2_w_up.astype(BF16),
            ffn2_w_down.astype(BF16), _rows(ple_norm), ple_w_gate.astype(BF16), ple_w_proj.astype(BF16))
    return pre, rwkv, gla, ssd, post


def _decode_params(nb, rwkv_mu, rwkv_w0, rwkv_w2, rwkv_a0, rwkv_a2, rwkv_g2, rwkv_k_k, rwkv_k_a, rwkv_r_k,
                   rwkv_ln_w, rwkv_ln_b, gla_gate_w2, gla_gate_b, gla_norm, mamba_conv_w, mamba_conv_b,
                   mamba_dt_bias, mamba_A_log, mamba_D, mamba_norm):
    depth = rwkv_mu.shape[0]
    slab = lambda vs: jnp.broadcast_to(jnp.concatenate([v.reshape(depth, -1) for v in vs], axis=1)[:, :, None],
                                       (depth, sum(v[0].size for v in vs), nb)).astype(F32)
    tr = lambda m: jnp.swapaxes(m, 1, 2)
    rwkv = (slab((rwkv_mu, rwkv_w0, rwkv_a0, rwkv_k_k, rwkv_k_a, rwkv_r_k, rwkv_ln_w, rwkv_ln_b)),
            tr(_pad_rows(rwkv_w2, 0, RWKV_LORA)), tr(_pad_rows(rwkv_a2, RWKV_DECAY_LORA, RWKV_LORA)),
            tr(_pad_rows(rwkv_g2, RWKV_DECAY_LORA + RWKV_A_LORA, RWKV_LORA)))
    gla = (slab((gla_gate_b, jnp.tile(gla_norm, (1, GLA_HEADS)))), tr(_pad_rows(gla_gate_w2, 0, LANE)))
    ssd = (slab((mamba_conv_w, mamba_conv_b, mamba_dt_bias, mamba_A_log, jnp.repeat(mamba_D, HEAD_DIM, axis=1),
                 mamba_norm)),)
    return rwkv, gla, ssd


def _sample_layer(x, p, state, params, dparams, layer, final_norm, *, batch, seq, final, earlier):
    pre, _, _, _, post = params
    d_rwkv, d_gla, d_ssd = dparams
    shift0, wkv0, gla0, conv0, ssm0 = state
    old_wkv, old_gla, old_ssm = earlier
    n = batch * seq
    x1, zr_t, zg_t, zm_t = _pre_call(x, layer, *pre, n, steps=seq)
    shift1 = zr_t[seq - 1].T
    xbc = jnp.transpose(zm_t[:, M2_W:M2_W + M2_CONV_DIM, :], (2, 0, 1))
    conv1 = jnp.concatenate([conv0[layer], xbc], axis=1)[:, -(M2_CONV - 1):]
    scr = lambda *shape: pltpu.VMEM(shape, F32)
    yr, wkv1 = _dec_call(_rwkv_dec_kernel, "rwkv7_decode", (zr_t,), wkv0, layer,
                         (jnp.swapaxes(shift0, 1, 2),) + d_rwkv, (_HEAD_ONES(),), layer, RWKV_W,
                         [scr(seq, 6, RWKV_W, batch), scr(seq, RWKV_W, batch), scr(seq, 2, RWKV_W, batch)], old_wkv)
    yg, gla1 = _dec_call(_gla_dec_kernel, "gla_decode", (zg_t,), gla0, layer, d_gla, (_HEAD_ONES(),), layer, GLA_W,
                         [scr(seq, 3, GLA_KEY_W, batch), scr(seq, GLA_W, batch), scr(seq, GLA_W, batch)], old_gla)
    conv_t = jnp.transpose(conv0, (0, 2, 3, 1))
    ym, ssm1 = _dec_call(_ssd_dec_kernel, "ssd_decode", (zm_t,), ssm0, layer, (conv_t,) + d_ssd, (), layer, M2_W,
                         [scr(seq, M2_W, batch), scr(seq, 2 * M2_GROUPS * M2_STATE, batch),
                          scr(seq, 2, ROWS, batch), scr(seq, M2_W, batch)], old_ssm)
    x2 = _post_call(x1, yr, yg, ym, p, layer, *post, final_norm.reshape(1, -1), n, final, steps=seq)
    return x2, (shift1, wkv1, gla1, conv1, ssm1)


def _prompt_layer(x, p, params, layer, final_norm, *, batch, seq, bb, tm, tm_post, final):
    pre, rwkv, gla, ssd, post = params
    assert seq % RWKV_CHUNK == 0 and seq % GLA_CHUNK == 0 and seq % SSD_CHUNK == 0 and seq >= M2_CONV - 1
    x1, zr, zg, zm = _pre_call(x, layer, *pre, tm)
    zr = zr.reshape(batch, seq, RWKV_PROJ)
    zg = zg.reshape(batch, seq, GLA_PAD)
    zm = zm.reshape(batch, seq, M2_PAD)
    shift1 = zr[:, -1]
    conv1 = zm[:, seq - (M2_CONV - 1):, M2_W:M2_W + M2_CONV_DIM]
    zero = lambda *shape: jnp.zeros((1, batch) + shape, F32)
    yr, wkv1 = _rwkv_call(zr, zero(1, RWKV_PROJ), zero(RWKV_HEADS, HEAD_DIM, HEAD_DIM), 0, rwkv, layer, bb,
                          RWKV_CHUNK, BF16)
    yg, gla1 = _gla_call(zg, zero(GLA_HEADS, GLA_DK, HEAD_DIM), 0, gla, layer, bb, GLA_CHUNK, BF16)
    ym, ssm1 = _ssd_call(zm, zero(M2_CONV - 1, M2_CONV_DIM), zero(M2_HEADS, HEAD_DIM, M2_STATE), 0, ssd, layer, bb,
                         SSD_CHUNK, BF16)
    n = batch * seq
    flat = lambda y: y.reshape(n, y.shape[-1])
    x2 = _post_call(x1, flat(yr), flat(yg), flat(ym), p, layer, *post, final_norm.reshape(1, -1), tm_post, final)
    return x2, (shift1, wkv1, gla1, conv1, ssm1)


def kernel(x_prompt, x_sample, p_prompt, p_sample, state_rwkv_shift, state_rwkv_wkv, state_gla, state_mamba_conv, state_mamba_ssm, ffn1_norm, ffn1_w_gate, ffn1_w_up, ffn1_w_down, mix_norm, w_in, rwkv_mu, rwkv_w0, rwkv_w2, rwkv_a0, rwkv_a2, rwkv_g2, rwkv_k_k, rwkv_k_a, rwkv_r_k, rwkv_ln_w, rwkv_ln_b, gla_gate_w2, gla_gate_b, gla_norm, mamba_conv_w, mamba_conv_b, mamba_dt_bias, mamba_A_log, mamba_D, mamba_norm, w_out, ffn2_norm, ffn2_w_gate, ffn2_w_up, ffn2_w_down, ple_norm, ple_w_gate, ple_w_proj, final_norm):
    depth = w_in.shape[0]
    nb, seq, _ = x_prompt.shape
    db, dseq, _ = x_sample.shape
    weights = (ffn1_norm, ffn1_w_gate, ffn1_w_up, ffn1_w_down, mix_norm, w_in, rwkv_mu, rwkv_w0, rwkv_w2, rwkv_a0,
               rwkv_a2, rwkv_g2, rwkv_k_k, rwkv_k_a, rwkv_r_k, rwkv_ln_w, rwkv_ln_b, gla_gate_w2, gla_gate_b,
               gla_norm, mamba_conv_w, mamba_conv_b, mamba_dt_bias, mamba_A_log, mamba_D, mamba_norm, w_out,
               ffn2_norm, ffn2_w_gate, ffn2_w_up, ffn2_w_down, ple_norm, ple_w_gate, ple_w_proj)
    xp = x_prompt.reshape(nb * seq, D_MODEL)
    xs = jnp.swapaxes(x_sample, 0, 1).reshape(dseq * db, D_MODEL)
    pp = p_prompt.reshape(depth, nb * seq, PLE_DIM)
    ps = jnp.swapaxes(p_sample, 1, 2).reshape(depth, dseq * db, PLE_DIM)
    params = _stacked_params(*weights)
    dparams = _decode_params(db, rwkv_mu, rwkv_w0, rwkv_w2, rwkv_a0, rwkv_a2, rwkv_g2, rwkv_k_k, rwkv_k_a, rwkv_r_k,
                             rwkv_ln_w, rwkv_ln_b, gla_gate_w2, gla_gate_b, gla_norm, mamba_conv_w, mamba_conv_b,
                             mamba_dt_bias, mamba_A_log, mamba_D, mamba_norm)
    batch_last = lambda s: jnp.moveaxis(s, 1, -1)
    past = (state_rwkv_shift, batch_last(state_rwkv_wkv), batch_last(state_gla), state_mamba_conv,
            batch_last(state_mamba_ssm))
    p_states, s_states = [], []
    for i in range(depth):
        final = i == depth - 1
        xp, st_p = _prompt_layer(xp, pp, params, i, final_norm, batch=nb, seq=seq, bb=nb, tm=512, tm_post=1024,
                                 final=final)
        earlier = tuple(tuple(s[j] for s in s_states) if final else () for j in (1, 2, 4))
        xs, st_s = _sample_layer(xs, ps, past, params, dparams, i, final_norm, batch=db, seq=dseq, final=final,
                                 earlier=earlier)
        p_states.append(st_p)
        s_states.append(st_s)
    stack = lambda sts, j: jnp.stack([s[j] for s in sts])
    mats = s_states[-1] if depth > 1 else tuple(s[None] for s in s_states[-1])
    s_out = tuple(jnp.moveaxis(mats[j], -1, 1) if j in (1, 2, 4) else stack(s_states, j) for j in range(5))
    return ((xp.reshape(nb, seq, D_MODEL), jnp.swapaxes(xs.reshape(dseq, db, D_MODEL), 0, 1))
            + tuple(stack(p_states, j) for j in range(5)) + s_out)
```

```python
import functools

import jax
import jax.numpy as jnp
from jax import lax
from jax.experimental import pallas as pl
from jax.experimental.pallas import tpu as pltpu

F32 = jnp.float32
BF16 = jnp.bfloat16

D_MODEL = 1024
D_FF = 2816
PLE_DIM = 256
HEAD_DIM = 64
NORM_EPS = 1e-6

RWKV_W = 256
RWKV_HEADS = 4
RWKV_GN_EPS = 64e-5
RWKV_PROJ = 896
RWKV_LORA = 128
RWKV_DECAY_LORA = 32
RWKV_A_LORA = 32
RWKV_DECAY_SCALE = 0.6065306597126334
RWKV_KEY_NORM_FLOOR = 1e-24

GLA_W = 256
GLA_HEADS = 4
GLA_DK = 32
GLA_KEY_W = 128
GLA_GATE_LORA = 16
GLA_TAU = 16.0
GLA_PROJ = 784
GLA_PAD = 896

M2_W = 512
M2_HEADS = 8
M2_STATE = 64
M2_GROUPS = 2
M2_CONV = 4
M2_CONV_DIM = 768
M2_PROJ = 1288
M2_PAD = 1408

LANE = 128
SUBLANE = 8
FF_CHUNK = 256
VMEM_LIMIT = 56 * 1024 * 1024

RWKV_CHUNK = 128
GLA_CHUNK = 64
SSD_CHUNK = 128


def _bdot(a, b):
    return jnp.dot(a.astype(BF16), b.astype(BF16), preferred_element_type=F32)


def _bdot_nt(a, b):
    return lax.dot_general(a.astype(BF16), b.astype(BF16), (((1,), (1,)), ((), ())),
                           preferred_element_type=F32)


def _bdot_tn(a, b):
    return lax.dot_general(a.astype(BF16), b.astype(BF16), (((0,), (0,)), ((), ())),
                           preferred_element_type=F32)


def _split3(x):
    hi = x.astype(BF16)
    r1 = x - hi.astype(F32)
    mid = r1.astype(BF16)
    lo = (r1 - mid.astype(F32)).astype(BF16)
    return hi, mid, lo


def _ldot3(mat, x):
    return sum(jnp.dot(mat, p, preferred_element_type=F32) for p in _split3(x))


def _rdot2(x, mat):
    return sum(jnp.dot(p, mat, preferred_element_type=F32) for p in _split3(x)[:2])


def _nt3(mat, x):
    return sum(lax.dot_general(mat, p, (((1,), (1,)), ((), ())), preferred_element_type=F32)
               for p in _split3(x))


def _softplus(x):
    return jnp.maximum(x, 0.0) + jnp.log1p(jnp.exp(-jnp.abs(x)))


def _sigmoid(x):
    return 0.5 * jnp.tanh(0.5 * x) + 0.5


def _silu(x):
    return x * _sigmoid(x)


def _rms(x, w):
    return x * lax.rsqrt(jnp.mean(x * x, axis=-1, keepdims=True) + NORM_EPS) * w


def _const_spec(shape):
    nd = len(shape)
    return pl.BlockSpec(shape, lambda *_: (0,) * nd, pipeline_mode=pl.Buffered(1))


def _layer_spec(a, layer):
    nd = a.ndim - 1
    return pl.BlockSpec((None,) + a.shape[1:], lambda *_: (layer,) + (0,) * nd, pipeline_mode=pl.Buffered(1))


def _ffn(x, nw, wg_ref, wu_ref, wd_ref, act_ref):
    xn = _rms(x, nw).astype(BF16)
    for c in range(D_FF // FF_CHUNK):
        sl = slice(c * FF_CHUNK, (c + 1) * FF_CHUNK)
        hg = jnp.dot(xn, wg_ref[:, sl], preferred_element_type=F32)
        hu = jnp.dot(xn, wu_ref[:, sl], preferred_element_type=F32)
        act_ref[:, sl] = (_silu(hg) * hu).astype(BF16)
    return x + 0.5 * jnp.dot(act_ref[...], wd_ref[...], preferred_element_type=F32)


def _pre_kernel(x_ref, n1_ref, wg_ref, wu_ref, wd_ref, nm_ref, win_ref,
                x1_ref, zr_ref, zg_ref, zm_ref, act_ref, *, steps):
    x1 = _ffn(x_ref[...], n1_ref[...], wg_ref, wu_ref, wd_ref, act_ref)
    x1_ref[...] = x1
    h = _rms(x1, nm_ref[...]).astype(BF16)
    cols = (slice(0, RWKV_PROJ), slice(RWKV_PROJ, RWKV_PROJ + GLA_PAD), slice(RWKV_PROJ + GLA_PAD, None))
    for ref, cs in zip((zr_ref, zg_ref, zm_ref), cols):
        z = jnp.dot(h, win_ref[:, cs], preferred_element_type=F32)
        if steps:
            nb = z.shape[0] // steps
            for t in range(steps):
                ref[t] = z[t * nb:(t + 1) * nb].T
        else:
            ref[...] = z


def _pre_call(x, layer, n1, wg, wu, wd, nm, win, tm, steps=0):
    n = x.shape[0]
    tok = lambda w: pl.BlockSpec((tm, w), lambda i: (i, 0))
    if steps:
        assert n == tm and n % steps == 0
        z_spec = lambda w: pl.BlockSpec((steps, w, n // steps), lambda i: (0, 0, 0))
        z_shape = lambda w: jax.ShapeDtypeStruct((steps, w, n // steps), F32)
    else:
        z_spec = tok
        z_shape = lambda w: jax.ShapeDtypeStruct((n, w), F32)
    return pl.pallas_call(
        functools.partial(_pre_kernel, steps=steps),
        grid=(n // tm,),
        in_specs=[tok(D_MODEL)] + [_layer_spec(a, layer) for a in (n1, wg, wu, wd, nm, win)],
        out_specs=[tok(D_MODEL), z_spec(RWKV_PROJ), z_spec(GLA_PAD), z_spec(M2_PAD)],
        out_shape=[jax.ShapeDtypeStruct((n, D_MODEL), F32), z_shape(RWKV_PROJ), z_shape(GLA_PAD), z_shape(M2_PAD)],
        scratch_shapes=[pltpu.VMEM((tm, D_FF), BF16)],
        compiler_params=pltpu.CompilerParams(dimension_semantics=("parallel",), vmem_limit_bytes=VMEM_LIMIT),
        name="pre_ffn_inproj",
    )(x, n1, wg, wu, wd, nm, win)


def _post_kernel(x_ref, yr_ref, yg_ref, ym_ref, p_ref, wo_ref, n2_ref, wg_ref, wu_ref, wd_ref,
                 np_ref, pg_ref, pp_ref, nf_ref, o_ref, act_ref, *, final, steps):
    rows = lambda ref: (jnp.concatenate([ref[t].T for t in range(steps)], axis=0) if steps else ref[...])
    x = x_ref[...]
    x = x + jnp.dot(rows(yr_ref).astype(BF16), wo_ref[0:RWKV_W, :], preferred_element_type=F32)
    x = x + jnp.dot(rows(yg_ref).astype(BF16), wo_ref[RWKV_W:RWKV_W + GLA_W, :], preferred_element_type=F32)
    x = x + jnp.dot(rows(ym_ref).astype(BF16), wo_ref[RWKV_W + GLA_W:, :], preferred_element_type=F32)
    x = _ffn(x, n2_ref[...], wg_ref, wu_ref, wd_ref, act_ref)
    gate = _sigmoid(jnp.dot(_rms(x, np_ref[...]).astype(BF16), pg_ref[...], preferred_element_type=F32))
    x = x + gate * jnp.dot(p_ref[...].astype(BF16), pp_ref[...], preferred_element_type=F32)
    if final:
        x = _rms(x, nf_ref[...])
    o_ref[...] = x


def _post_call(x, yr, yg, ym, p, layer, wo, n2, wg, wu, wd, npn, pg, pp, nf, tm, final, steps=0):
    n = x.shape[0]
    tok = lambda w: pl.BlockSpec((tm, w), lambda i: (i, 0))
    if steps:
        assert n == tm and n % steps == 0
        y_spec = lambda w: pl.BlockSpec((steps, w, n // steps), lambda i: (0, 0, 0))
    else:
        y_spec = tok
    params = (wo, n2, wg, wu, wd, npn, pg, pp)
    consts = params + (nf,)
    return pl.pallas_call(
        functools.partial(_post_kernel, final=final, steps=steps),
        grid=(n // tm,),
        in_specs=[tok(D_MODEL), y_spec(RWKV_W), y_spec(GLA_W), y_spec(M2_W),
                  pl.BlockSpec((None, tm, PLE_DIM), lambda i: (layer, i, 0))]
                 + [_layer_spec(a, layer) for a in params] + [_const_spec(nf.shape)],
        out_specs=tok(D_MODEL),
        out_shape=jax.ShapeDtypeStruct((n, D_MODEL), F32),
        scratch_shapes=[pltpu.VMEM((tm, D_FF), BF16)],
        compiler_params=pltpu.CompilerParams(dimension_semantics=("parallel",), vmem_limit_bytes=VMEM_LIMIT),
        name="post_outproj_ffn_ple",
    )(x, yr, yg, ym, p, *consts)


def _tri_masks(n):
    ri = lax.broadcasted_iota(jnp.int32, (n, n), 0)
    ci = lax.broadcasted_iota(jnp.int32, (n, n), 1)
    return ri >= ci, ri > ci, (ri == ci).astype(F32)


def _rwkv_kernel(z_ref, sh_ref, s0_ref, mu_ref, w0_ref, w2_ref, a0_ref, a2_ref, g2_ref, kk_ref, ka_ref,
                 rk_ref, lnw_ref, lnb_ref, tri_ref, ones_ref,
                 y_ref, s_ref, zbuf_ref, ybuf_ref, *, bb, chunk):
    c = pl.program_id(1)

    @pl.when(c == 0)
    def _():
        s_ref[...] = s0_ref[...]
        zbuf_ref[:, SUBLANE - 1:SUBLANE, :] = sh_ref[...]

    incl, strict, eye = _tri_masks(chunk)
    tri = tri_ref[...]
    ones = ones_ref[...]
    mu = mu_ref[...]
    seqs = range(bb)
    zs = []
    for i in seqs:
        z = z_ref[i]
        zbuf_ref[i, SUBLANE:SUBLANE + chunk, :] = z
        prev = zbuf_ref[i, SUBLANE - 1:SUBLANE - 1 + chunk, :]
        zbuf_ref[i, SUBLANE - 1:SUBLANE, :] = z[chunk - 1:chunk, :]
        zs.append(z + mu * (prev - z))
    r = [x[:, 0:RWKV_W] for x in zs]
    k = [x[:, RWKV_W:2 * RWKV_W] for x in zs]
    v = [x[:, 2 * RWKV_W:3 * RWKV_W] for x in zs]
    lora = [x[:, 3 * RWKV_W:] for x in zs]
    w_lin = [_bdot(jnp.tanh(x), w2_ref[...]) for x in lora]
    a_lin = [_bdot(x, a2_ref[...]) for x in lora]
    gate = [_bdot(_sigmoid(x), g2_ref[...]) for x in lora]
    kk = [x * kk_ref[...] for x in k]
    kk_ss = [_rdot2(x * x, ones) for x in kk]
    logw = [-RWKV_DECAY_SCALE * _sigmoid(w0_ref[...] + x) for x in w_lin]
    a = [_sigmoid(a0_ref[...] + x) for x in a_lin]
    kk = [x * lax.rsqrt(jnp.maximum(ss, RWKV_KEY_NORM_FLOOR)) for x, ss in zip(kk, kk_ss)]
    k = [x * (1.0 + (ai - 1.0) * ka_ref[...]) for x, ai in zip(k, a)]
    ahat = [-x for x in kk]
    bhat = [x * ai for x, ai in zip(kk, a)]
    g = [_ldot3(tri, x) for x in logw]
    glast = [x[chunk - 1:chunk, :] for x in g]
    e_ng = [jnp.exp(-x) for x in g]
    e_gl = [jnp.exp(gl - x) for gl, x in zip(glast, g)]
    a_l = [ah * jnp.exp(x - lw) for ah, x, lw in zip(ahat, g, logw)]
    r_l = [ri * jnp.exp(x) for ri, x in zip(r, g)]
    b_r = [x * e for x, e in zip(bhat, e_ng)]
    k_r = [x * e for x, e in zip(k, e_ng)]
    k_p = [x * e for x, e in zip(k, e_gl)]
    b_p = [x * e for x, e in zip(bhat, e_gl)]
    dec = [jnp.exp(x) for x in glast]

    probs = [(i, h) for i in seqs for h in range(RWKV_HEADS)]
    hd = lambda xs: [xs[i][:, h * HEAD_DIM:(h + 1) * HEAD_DIM] for i, h in probs]
    a_lh, r_lh, b_rh, k_rh, k_ph, b_ph, vh = hd(a_l), hd(r_l), hd(b_r), hd(k_r), hd(k_p), hd(b_p), hd(v)
    ar = [jnp.concatenate([x, y], axis=0) for x, y in zip(a_lh, r_lh)]
    bk = [jnp.concatenate([x, y], axis=0) for x, y in zip(b_rh, k_rh)]
    gm = [_bdot_nt(x, y) for x, y in zip(ar, bk)]
    a_ab = [jnp.where(strict, x[:chunk, :chunk], 0.0) for x in gm]
    a_ak = [jnp.where(strict, x[:chunk, chunk:], 0.0) for x in gm]
    a_rb = [jnp.where(incl, x[chunk:, :chunk], 0.0) for x in gm]
    a_rk = [jnp.where(incl, x[chunk:, chunk:], 0.0) for x in gm]
    t = [eye + m for m in a_ab]
    x = [_bdot(m, m) for m in a_ab]
    p = 2
    while p < chunk:
        if 2 * p < chunk:
            xt = [_bdot(jnp.concatenate([xi, ti], axis=0), xi) for xi, ti in zip(x, t)]
            x = [m[:chunk] for m in xt]
            t = [ti + m[chunk:] for ti, m in zip(t, xt)]
        else:
            t = [ti + _bdot(ti, xi) for xi, ti in zip(x, t)]
        p *= 2
    akv = [_bdot(m, vi) for m, vi in zip(a_ak, vh)]
    w_t = [_bdot(ti, m) for ti, m in zip(t, a_lh)]
    u_t = [_bdot(ti, m) for ti, m in zip(t, akv)]
    r_t = [ri + _bdot(m, wi) for ri, m, wi in zip(r_lh, a_rb, w_t)]
    y_t = [_bdot(m, vi) + _bdot(n, ui) for m, vi, n, ui in zip(a_rk, vh, a_rb, u_t)]
    m_s = [_bdot_tn(wi, bi) for wi, bi in zip(w_t, b_ph)]
    q_s = [_bdot_tn(jnp.concatenate([vi, ui], axis=0), jnp.concatenate([ki, bi], axis=0))
           for vi, ui, ki, bi in zip(vh, u_t, k_ph, b_ph)]
    s_old = [s_ref[i, h] for i, h in probs]
    y_h = [_bdot_nt(ri, si) + yi for ri, si, yi in zip(r_t, s_old, y_t)]
    s_m = [_bdot(si, mi) for si, mi in zip(s_old, m_s)]
    for j, (i, h) in enumerate(probs):
        hs = slice(h * HEAD_DIM, (h + 1) * HEAD_DIM)
        ybuf_ref[i, :, hs] = y_h[j]
        s_ref[i, h] = s_old[j] * dec[i][:, hs] + s_m[j] + q_s[j]
    y = [ybuf_ref[i] for i in seqs]
    mean = [_rdot2(x, ones) * (1.0 / HEAD_DIM) for x in y]
    yc = [x - m for x, m in zip(y, mean)]
    var = [_rdot2(x * x, ones) * (1.0 / HEAD_DIM) for x in yc]
    bonus = [_rdot2(ri * ki * rk_ref[...], ones) * vi for ri, ki, vi in zip(r, k, v)]
    for i in seqs:
        yn = yc[i] * lax.rsqrt(var[i] + RWKV_GN_EPS) * lnw_ref[...] + lnb_ref[...]
        y_ref[i] = ((yn + bonus[i]) * gate[i]).astype(y_ref.dtype)


def _state_spec(a, bb, sl):
    nd = a.ndim - 2
    return pl.BlockSpec((None, bb) + a.shape[2:], lambda i, c: (sl, i) + (0,) * nd)


def _rwkv_call(z, shift, s0, sl, params, layer, bb, chunk, out_dtype):
    b, l, _ = z.shape
    grid = (b // bb, l // chunk)
    tri = jnp.tril(jnp.ones((chunk, chunk), F32)).astype(BF16)
    consts = (tri, _HEAD_ONES())
    return pl.pallas_call(
        functools.partial(_rwkv_kernel, bb=bb, chunk=chunk),
        grid=grid,
        in_specs=[pl.BlockSpec((bb, chunk, RWKV_PROJ), lambda i, c: (i, c, 0)),
                  _state_spec(shift, bb, sl), _state_spec(s0, bb, sl)]
                 + [_layer_spec(a, layer) for a in params] + [_const_spec(a.shape) for a in consts],
        out_specs=[pl.BlockSpec((bb, chunk, RWKV_W), lambda i, c: (i, c, 0)),
                   pl.BlockSpec((bb, RWKV_HEADS, HEAD_DIM, HEAD_DIM), lambda i, c: (i, 0, 0, 0))],
        out_shape=[jax.ShapeDtypeStruct((b, l, RWKV_W), out_dtype),
                   jax.ShapeDtypeStruct((b, RWKV_HEADS, HEAD_DIM, HEAD_DIM), F32)],
        scratch_shapes=[pltpu.VMEM((bb, SUBLANE + chunk, RWKV_PROJ), F32),
                        pltpu.VMEM((bb, chunk, RWKV_W), F32)],
        compiler_params=pltpu.CompilerParams(dimension_semantics=("arbitrary", "arbitrary"),
                                             vmem_limit_bytes=VMEM_LIMIT),
        name="rwkv7_mixer",
    )(z, shift, s0, *params, *consts)


def _gla_kernel(z_ref, s0_ref, gw_ref, gb_ref, nw_ref, tri_ref, ones_ref, eye_ref,
                y_ref, s_ref, obuf_ref, *, bb, chunk):
    c = pl.program_id(1)

    @pl.when(c == 0)
    def _():
        s_ref[...] = s0_ref[...]

    incl, _, _ = _tri_masks(chunk)
    tri = tri_ref[...]
    ones = ones_ref[...]
    seqs = range(bb)
    z = [z_ref[i] for i in seqs]
    q = [x[:, 0:GLA_KEY_W] * (GLA_DK ** -0.5) for x in z]
    k = [x[:, GLA_KEY_W:2 * GLA_KEY_W] for x in z]
    v = [x[:, 2 * GLA_KEY_W:2 * GLA_KEY_W + GLA_W] for x in z]
    og = [x[:, 2 * GLA_KEY_W + GLA_W:2 * GLA_KEY_W + 2 * GLA_W] for x in z]
    lin = [_bdot(x[:, 2 * GLA_KEY_W + 2 * GLA_W:], gw_ref[...]) for x in z]
    log_a = [-_softplus(-(x + gb_ref[...])) * (1.0 / GLA_TAU) for x in lin]
    b = [_ldot3(tri, x) for x in log_a]
    blast = [x[chunk - 1:chunk, :] for x in b]
    qe = [x * jnp.exp(bi) for x, bi in zip(q, b)]
    ke = [x * jnp.exp(-bi) for x, bi in zip(k, b)]
    kl = [x * jnp.exp(bl - bi) for x, bl, bi in zip(k, blast, b)]
    dec = [_nt3(eye_ref[...], jnp.broadcast_to(jnp.exp(bl), (HEAD_DIM, GLA_KEY_W))) for bl in blast]

    probs = [(i, h) for i in seqs for h in range(GLA_HEADS)]
    kd = lambda xs: [xs[i][:, h * GLA_DK:(h + 1) * GLA_DK] for i, h in probs]
    qeh, keh, klh = kd(qe), kd(ke), kd(kl)
    vh = [v[i][:, h * HEAD_DIM:(h + 1) * HEAD_DIM] for i, h in probs]
    att = [jnp.where(incl, _bdot_nt(x, y), 0.0) for x, y in zip(qeh, keh)]
    s_old = [s_ref[i, h] for i, h in probs]
    o_h = [_bdot(ai, vi) + _bdot(qi, si) for ai, vi, qi, si in zip(att, vh, qeh, s_old)]
    kv = [_bdot_tn(ki, vi) for ki, vi in zip(klh, vh)]
    for j, (i, h) in enumerate(probs):
        obuf_ref[i, :, h * HEAD_DIM:(h + 1) * HEAD_DIM] = o_h[j]
        s_ref[i, h] = s_old[j] * dec[i][h * GLA_DK:(h + 1) * GLA_DK, :] + kv[j]
    o = [obuf_ref[i] for i in seqs]
    ms = [_rdot2(x * x, ones) * (1.0 / HEAD_DIM) for x in o]
    for i in seqs:
        y_ref[i] = (o[i] * lax.rsqrt(ms[i] + NORM_EPS) * nw_ref[...] * _silu(og[i])).astype(y_ref.dtype)


def _gla_call(z, s0, sl, params, layer, bb, chunk, out_dtype):
    b, l, _ = z.shape
    tri = jnp.tril(jnp.ones((chunk, chunk), F32)).astype(BF16)
    consts = (tri, _HEAD_ONES(), jnp.eye(LANE, dtype=BF16))
    return pl.pallas_call(
        functools.partial(_gla_kernel, bb=bb, chunk=chunk),
        grid=(b // bb, l // chunk),
        in_specs=[pl.BlockSpec((bb, chunk, GLA_PAD), lambda i, c: (i, c, 0)), _state_spec(s0, bb, sl)]
                 + [_layer_spec(a, layer) for a in params] + [_const_spec(a.shape) for a in consts],
        out_specs=[pl.BlockSpec((bb, chunk, GLA_W), lambda i, c: (i, c, 0)),
                   pl.BlockSpec((bb, GLA_HEADS, GLA_DK, HEAD_DIM), lambda i, c: (i, 0, 0, 0))],
        out_shape=[jax.ShapeDtypeStruct((b, l, GLA_W), out_dtype),
                   jax.ShapeDtypeStruct((b, GLA_HEADS, GLA_DK, HEAD_DIM), F32)],
        scratch_shapes=[pltpu.VMEM((bb, chunk, GLA_W), F32)],
        compiler_params=pltpu.CompilerParams(dimension_semantics=("arbitrary", "arbitrary"),
                                             vmem_limit_bytes=VMEM_LIMIT),
        name="gla_mixer",
    )(z, s0, *params, *consts)


def _ssd_kernel(z_ref, cp_ref, s0_ref, cw_ref, cb_ref, dtb_ref, alog_ref, dsk_ref, nw_ref, tri_ref, eye_ref,
                esel_ref, y_ref, s_ref, xbuf_ref, ybuf_ref, *, bb, chunk):
    c = pl.program_id(1)

    @pl.when(c == 0)
    def _():
        s_ref[...] = s0_ref[...]
        xbuf_ref[:, SUBLANE - (M2_CONV - 1):SUBLANE, :] = cp_ref[...]

    incl, _, _ = _tri_masks(chunk)
    tri = tri_ref[...]
    lane = lax.broadcasted_iota(jnp.int32, (1, LANE), 1)
    rep = M2_HEADS // M2_GROUPS
    seqs = range(bb)
    zg, xc, dt = [], [], []
    live = lane < M2_HEADS
    for i in seqs:
        z = z_ref[i]
        x = z[:, M2_W:M2_W + M2_CONV_DIM]
        xbuf_ref[i, SUBLANE:SUBLANE + chunk, :] = x
        conv = cb_ref[...] + x * cw_ref[M2_CONV - 1:M2_CONV, :]
        for j in range(1, M2_CONV):
            conv = conv + xbuf_ref[i, SUBLANE - j:SUBLANE - j + chunk, :] * cw_ref[M2_CONV - 1 - j:M2_CONV - j, :]
        xbuf_ref[i, 0:SUBLANE, :] = xbuf_ref[i, chunk:chunk + SUBLANE, :]
        zg.append(z[:, 0:M2_W])
        xc.append(_silu(conv))
        dt.append(jnp.where(live, _softplus(z[:, M2_W + M2_CONV_DIM:] + dtb_ref[...]), 0.0))
    a_neg = -jnp.exp(alog_ref[...])
    cum = [_ldot3(tri, x * a_neg) for x in dt]
    cum_t = [_nt3(eye_ref[0:SUBLANE, :], x) for x in cum]
    dt_b = [_rdot2(x, esel_ref[...]) for x in dt]
    grp = [(i, g) for i in seqs for g in range(M2_GROUPS)]
    bg = [xc[i][:, M2_W + g * M2_STATE:M2_W + (g + 1) * M2_STATE] for i, g in grp]
    cg = [xc[i][:, M2_W + (M2_GROUPS + g) * M2_STATE:M2_W + (M2_GROUPS + g + 1) * M2_STATE] for i, g in grp]
    cb = [_bdot_nt(x, y) for x, y in zip(cg, bg)]
    s_grp = [jnp.concatenate([s_ref[i, g * rep + hh] for hh in range(rep)], axis=0) for i, g in grp]
    y_st = [_bdot_nt(x, s) for x, s in zip(cg, s_grp)]
    probs = [(i, h) for i in seqs for h in range(M2_HEADS)]
    ccb = [jnp.broadcast_to(cum[i][:, h:h + 1], (chunk, LANE)) for i, h in probs]
    low = lane < HEAD_DIM
    cum_b = [jnp.concatenate([jnp.where(low, ccb[i * M2_HEADS + 2 * j], ccb[i * M2_HEADS + 2 * j + 1])
                              for j in range(M2_HEADS // 2)], axis=1) for i in seqs]
    xs = [x[:, 0:M2_W] for x in xc]
    xdt = [x * d for x, d in zip(xs, dt_b)]
    xw = [x * jnp.exp(cb_[chunk - 1:chunk, :] - cb_) for x, cb_ in zip(xdt, cum_b)]
    seg = [jnp.where(incl, jnp.exp(jnp.minimum(ccb[j][:, 0:chunk] - cum_t[i][h:h + 1, :], 0.0)), 0.0)
           for j, (i, h) in enumerate(probs)]
    y_in = [_bdot(cb[i * M2_GROUPS + h // rep] * seg[j], xdt[i][:, h * HEAD_DIM:(h + 1) * HEAD_DIM])
            for j, (i, h) in enumerate(probs)]
    s_in = [_bdot_tn(xw[i][:, g * rep * HEAD_DIM:(g + 1) * rep * HEAD_DIM], bg[j]) for j, (i, g) in enumerate(grp)]
    for j, (i, h) in enumerate(probs):
        ybuf_ref[i, :, h * HEAD_DIM:(h + 1) * HEAD_DIM] = y_in[j]
        gj = i * M2_GROUPS + h // rep
        hh = h % rep
        s_ref[i, h] = (s_grp[gj][hh * HEAD_DIM:(hh + 1) * HEAD_DIM] * jnp.exp(cum[i][chunk - 1:chunk, h:h + 1])
                       + s_in[gj][hh * HEAD_DIM:(hh + 1) * HEAD_DIM])
    for i in seqs:
        y_state = jnp.concatenate([y_st[i * M2_GROUPS + g] for g in range(M2_GROUPS)], axis=1)
        y = ybuf_ref[i] + y_state * jnp.exp(cum_b[i]) + dsk_ref[...] * xs[i]
        y_ref[i] = _rms(y * _silu(zg[i]), nw_ref[...]).astype(y_ref.dtype)


def _ssd_call(z, conv_prev, s0, sl, params, layer, bb, chunk, out_dtype):
    b, l, _ = z.shape
    tri = jnp.tril(jnp.ones((chunk, chunk), F32)).astype(BF16)
    esel = (jnp.arange(LANE)[:, None] == jnp.arange(M2_W)[None, :] // HEAD_DIM).astype(BF16)
    consts = (tri, jnp.eye(LANE, dtype=BF16), esel)
    return pl.pallas_call(
        functools.partial(_ssd_kernel, bb=bb, chunk=chunk),
        grid=(b // bb, l // chunk),
        in_specs=[pl.BlockSpec((bb, chunk, M2_PAD), lambda i, c: (i, c, 0)),
                  _state_spec(conv_prev, bb, sl), _state_spec(s0, bb, sl)]
                 + [_layer_spec(a, layer) for a in params] + [_const_spec(a.shape) for a in consts],
        out_specs=[pl.BlockSpec((bb, chunk, M2_W), lambda i, c: (i, c, 0)),
                   pl.BlockSpec((bb, M2_HEADS, HEAD_DIM, M2_STATE), lambda i, c: (i, 0, 0, 0))],
        out_shape=[jax.ShapeDtypeStruct((b, l, M2_W), out_dtype),
                   jax.ShapeDtypeStruct((b, M2_HEADS, HEAD_DIM, M2_STATE), F32)],
        scratch_shapes=[pltpu.VMEM((bb, SUBLANE + chunk, M2_CONV_DIM), F32),
                        pltpu.VMEM((bb, chunk, M2_W), F32)],
        compiler_params=pltpu.CompilerParams(dimension_semantics=("arbitrary", "arbitrary"),
                                             vmem_limit_bytes=VMEM_LIMIT),
        name="ssd_mixer",
    )(z, conv_prev, s0, *params, *consts)


def _HEAD_ONES():
    idx = jnp.arange(RWKV_W) // HEAD_DIM
    return (idx[:, None] == idx[None, :]).astype(BF16)


ROWS = SUBLANE


def _rows_to_tile(rows):
    rid = lax.broadcasted_iota(jnp.int32, (ROWS, 1), 0)
    out = rows[0]
    for j in range(1, ROWS):
        out = jnp.where(rid == j, rows[j], out)
    return out


def _rwkv_dec_kernel(z_ref, s_ref, sh_ref, cols_ref, w2_ref, a2_ref, g2_ref, ones_ref,
                     y_ref, so_ref, vec_ref, ybuf_ref, post_ref):
    h = pl.program_id(0)
    nt = z_ref.shape[0]
    w = RWKV_W
    ones = ones_ref[...]
    col = lambda j: cols_ref[RWKV_PROJ + j * w:RWKV_PROJ + (j + 1) * w, :]

    @pl.when(h == 0)
    def _():
        mu = cols_ref[0:RWKV_PROJ, :]
        prev = sh_ref[...]
        for t in range(nt):
            z = z_ref[t]
            zs = z + mu * (prev - z)
            prev = z
            r, k, v, lora = zs[0:w], zs[w:2 * w], zs[2 * w:3 * w], zs[3 * w:]
            logw = -RWKV_DECAY_SCALE * _sigmoid(col(0) + _bdot(w2_ref[...], jnp.tanh(lora)))
            a = _sigmoid(col(1) + _bdot(a2_ref[...], lora))
            kk = k * col(2)
            kk = kk * lax.rsqrt(jnp.maximum(_ldot3(ones, kk * kk), RWKV_KEY_NORM_FLOOR))
            k = k * (1.0 + (a - 1.0) * col(3))
            for j, x in enumerate((-kk, jnp.exp(logw), kk * a, k, r, v)):
                vec_ref[t, j] = x
            post_ref[t, 0] = _bdot(g2_ref[...], _sigmoid(lora))
            post_ref[t, 1] = _ldot3(ones, r * k * col(4)) * v

    hs = pl.ds(pl.multiple_of(h * HEAD_DIM, HEAD_DIM), HEAD_DIM)

    def body(v8, carry):
        r0 = pl.multiple_of(h * HEAD_DIM + v8 * ROWS, ROWS)
        vt = [vec_ref[t, 5, pl.ds(r0, ROWS), :] for t in range(nt)]
        ys = [[] for _ in range(nt)]
        for j in range(ROWS):
            s = s_ref[0, v8 * ROWS + j]
            for t in range(nt):
                sa = jnp.sum(s * vec_ref[t, 0, hs, :], axis=0, keepdims=True)
                s = s * vec_ref[t, 1, hs, :] + sa * vec_ref[t, 2, hs, :] + vt[t][j:j + 1, :] * vec_ref[t, 3, hs, :]
                ys[t].append(jnp.sum(s * vec_ref[t, 4, hs, :], axis=0, keepdims=True))
            so_ref[0, v8 * ROWS + j] = s
        for t in range(nt):
            ybuf_ref[t, pl.ds(r0, ROWS), :] = _rows_to_tile(ys[t])
        return carry

    lax.fori_loop(0, HEAD_DIM // ROWS, body, 0)

    @pl.when(h == pl.num_programs(0) - 1)
    def _():
        for t in range(nt):
            y = ybuf_ref[t]
            yc = y - _ldot3(ones, y) * (1.0 / HEAD_DIM)
            var = _ldot3(ones, yc * yc) * (1.0 / HEAD_DIM)
            yn = yc * lax.rsqrt(var + RWKV_GN_EPS) * col(5) + col(6)
            y_ref[t] = (yn + post_ref[t, 1]) * post_ref[t, 0]


def _gla_dec_kernel(z_ref, s_ref, cols_ref, gw_ref, ones_ref, y_ref, so_ref, vec_ref, val_ref, obuf_ref):
    h = pl.program_id(0)
    nt = z_ref.shape[0]

    @pl.when(h == 0)
    def _():
        for t in range(nt):
            z = z_ref[t]
            lin = _bdot(gw_ref[...], z[2 * GLA_KEY_W + 2 * GLA_W:]) + cols_ref[0:GLA_KEY_W, :]
            vec_ref[t, 0] = z[0:GLA_KEY_W] * (GLA_DK ** -0.5)
            vec_ref[t, 1] = z[GLA_KEY_W:2 * GLA_KEY_W]
            vec_ref[t, 2] = jnp.exp(-_softplus(-lin) * (1.0 / GLA_TAU))
            val_ref[t] = z[2 * GLA_KEY_W:2 * GLA_KEY_W + GLA_W]

    vs = pl.ds(pl.multiple_of(h * HEAD_DIM, HEAD_DIM), HEAD_DIM)
    v = [val_ref[t, vs, :] for t in range(nt)]

    def body(k8, acc):
        r0 = pl.multiple_of(h * GLA_DK + k8 * ROWS, ROWS)
        q, k, a = ([vec_ref[t, j, pl.ds(r0, ROWS), :] for t in range(nt)] for j in range(3))
        acc = list(acc)
        for j in range(ROWS):
            s = s_ref[0, k8 * ROWS + j]
            for t in range(nt):
                s = s * a[t][j:j + 1, :] + k[t][j:j + 1, :] * v[t]
                acc[t] = acc[t] + q[t][j:j + 1, :] * s
            so_ref[0, k8 * ROWS + j] = s
        return tuple(acc)

    zero = jnp.zeros((HEAD_DIM, z_ref.shape[2]), F32)
    acc = lax.fori_loop(0, GLA_DK // ROWS, body, (zero,) * nt)
    for t in range(nt):
        obuf_ref[t, vs, :] = acc[t]

    @pl.when(h == pl.num_programs(0) - 1)
    def _():
        for t in range(nt):
            o = obuf_ref[t]
            ms = _ldot3(ones_ref[...], o * o) * (1.0 / HEAD_DIM)
            og = z_ref[t, 2 * GLA_KEY_W + GLA_W:2 * GLA_KEY_W + 2 * GLA_W, :]
            y_ref[t] = o * lax.rsqrt(ms + NORM_EPS) * cols_ref[GLA_KEY_W:, :] * _silu(og)


def _ssd_dec_kernel(z_ref, s_ref, cp_ref, cols_ref, y_ref, so_ref, x_ref, bc_ref, dt_ref, ybuf_ref):
    h = pl.program_id(0)
    nt = z_ref.shape[0]
    cd = M2_CONV_DIM
    off_b = M2_CONV * cd
    off_dt = off_b + cd

    @pl.when(h == 0)
    def _():
        xs = [cp_ref[j] for j in range(M2_CONV - 1)] + [z_ref[t, M2_W:M2_W + cd, :] for t in range(nt)]
        a_neg = -jnp.exp(cols_ref[off_dt + ROWS:off_dt + 2 * ROWS, :])
        for t in range(nt):
            conv = cols_ref[off_b:off_b + cd, :]
            for j in range(M2_CONV):
                conv = conv + xs[t + j] * cols_ref[j * cd:(j + 1) * cd, :]
            xc = _silu(conv)
            x_ref[t] = xc[0:M2_W]
            bc_ref[t] = xc[M2_W:]
            dt = _softplus(z_ref[t, M2_W + cd:M2_W + cd + ROWS, :] + cols_ref[off_dt:off_dt + ROWS, :])
            dt_ref[t, 0] = dt
            dt_ref[t, 1] = jnp.exp(dt * a_neg)

    g = h // (M2_HEADS // M2_GROUPS)
    bs = pl.ds(pl.multiple_of(g * M2_STATE, M2_STATE), M2_STATE)
    cs = pl.ds(pl.multiple_of((M2_GROUPS + g) * M2_STATE, M2_STATE), M2_STATE)
    rid = lax.broadcasted_iota(jnp.int32, (ROWS, 1), 0)
    pick = lambda tile: jnp.sum(jnp.where(rid == h, tile, 0.0), axis=0, keepdims=True)
    dt = [pick(dt_ref[t, 0]) for t in range(nt)]
    da = [pick(dt_ref[t, 1]) for t in range(nt)]

    def body(p8, carry):
        r0 = pl.multiple_of(h * HEAD_DIM + p8 * ROWS, ROWS)
        xt = [x_ref[t, pl.ds(r0, ROWS), :] * dt[t] for t in range(nt)]
        ys = [[] for _ in range(nt)]
        for j in range(ROWS):
            s = s_ref[0, p8 * ROWS + j]
            for t in range(nt):
                s = s * da[t] + xt[t][j:j + 1, :] * bc_ref[t, bs, :]
                ys[t].append(jnp.sum(s * bc_ref[t, cs, :], axis=0, keepdims=True))
            so_ref[0, p8 * ROWS + j] = s
        for t in range(nt):
            ybuf_ref[t, pl.ds(r0, ROWS), :] = _rows_to_tile(ys[t])
        return carry

    lax.fori_loop(0, HEAD_DIM // ROWS, body, 0)

    @pl.when(h == pl.num_programs(0) - 1)
    def _():
        off_d = off_dt + 2 * ROWS
        for t in range(nt):
            y = ybuf_ref[t] + cols_ref[off_d:off_d + M2_W, :] * x_ref[t]
            y = y * _silu(z_ref[t, 0:M2_W, :])
            ms = jnp.mean(y * y, axis=0, keepdims=True)
            y_ref[t] = y * lax.rsqrt(ms + NORM_EPS) * cols_ref[off_d + M2_W:off_d + 2 * M2_W, :]


def _dec_call(kern, name, acts, state, sl, layer_params, consts, layer, y_width, scratch, earlier=()):
    nt, _, nb = acts[0].shape
    heads = state.shape[1]
    whole = lambda a: pl.BlockSpec(a.shape, lambda hd: (0,) * a.ndim, pipeline_mode=pl.Buffered(1))
    blk = (1,) + state.shape[2:]
    n_in = len(acts) + 1 + len(layer_params) + len(consts)
    n_old = len(earlier)

    def body(*refs):
        ins, old, (y_ref, so_ref), scr = (refs[:n_in], refs[n_in:n_in + n_old], refs[n_in + n_old:n_in + n_old + 2],
                                          refs[n_in + n_old + 2:])
        for j, ref in enumerate(old):
            so_ref[j] = ref[...]
        kern(*ins, y_ref, so_ref.at[n_old] if n_old else so_ref, *scr)

    if n_old:
        so_spec = pl.BlockSpec((n_old + 1,) + blk, lambda hd: (0, hd, 0, 0, 0))
        so_shape = (n_old + 1,) + state.shape[1:]
    else:
        so_spec = pl.BlockSpec(blk, lambda hd: (hd, 0, 0, 0))
        so_shape = state.shape[1:]
    return pl.pallas_call(
        body,
        grid=(heads,),
        in_specs=[whole(a) for a in acts]
                 + [pl.BlockSpec((None,) + blk, lambda hd: (sl, hd, 0, 0, 0))]
                 + [_layer_spec(a, layer) for a in layer_params] + [_const_spec(a.shape) for a in consts]
                 + [pl.BlockSpec(blk, lambda hd: (hd, 0, 0, 0)) for _ in earlier],
        out_specs=[pl.BlockSpec((nt, y_width, nb), lambda hd: (0, 0, 0)), so_spec],
        out_shape=[jax.ShapeDtypeStruct((nt, y_width, nb), F32), jax.ShapeDtypeStruct(so_shape, F32)],
        scratch_shapes=scratch,
        compiler_params=pltpu.CompilerParams(dimension_semantics=("arbitrary",), vmem_limit_bytes=VMEM_LIMIT),
        name=name,
    )(*acts, state, *layer_params, *consts, *earlier)


def _rows(v):
    return v.reshape(v.shape[0], 1, -1).astype(F32)


def _pad_rows(m, top, total):
    return jnp.zeros((m.shape[0], total, m.shape[2]), F32).at[:, top:top + m.shape[1]].set(m).astype(BF16)


def _pad_lanes(v):
    return jnp.zeros((v.shape[0], 1, LANE), F32).at[:, 0, :v.shape[1]].set(v)


def _stacked_params(ffn1_norm, ffn1_w_gate, ffn1_w_up, ffn1_w_down, mix_norm, w_in, rwkv_mu, rwkv_w0, rwkv_w2,
                    rwkv_a0, rwkv_a2, rwkv_g2, rwkv_k_k, rwkv_k_a, rwkv_r_k, rwkv_ln_w, rwkv_ln_b, gla_gate_w2,
                    gla_gate_b, gla_norm, mamba_conv_w, mamba_conv_b, mamba_dt_bias, mamba_A_log, mamba_D, mamba_norm,
                    w_out, ffn2_norm, ffn2_w_gate, ffn2_w_up, ffn2_w_down, ple_norm, ple_w_gate, ple_w_proj):
    depth = w_in.shape[0]
    w = w_in.astype(BF16)
    g0 = RWKV_PROJ
    m0 = RWKV_PROJ + GLA_PROJ
    qkv_end = 2 * GLA_KEY_W + GLA_W
    zeros = lambda n: jnp.zeros((depth, D_MODEL, n), BF16)
    win = jnp.concatenate([
        w[:, :, :g0 + qkv_end], w[:, :, g0 + qkv_end + GLA_GATE_LORA:m0],
        w[:, :, g0 + qkv_end:g0 + qkv_end + GLA_GATE_LORA], zeros(LANE - GLA_GATE_LORA),
        w[:, :, m0:], zeros(LANE - M2_HEADS)], axis=2)
    pre = (_rows(ffn1_norm), ffn1_w_gate.astype(BF16), ffn1_w_up.astype(BF16), ffn1_w_down.astype(BF16),
           _rows(mix_norm), win)
    rwkv = (_rows(rwkv_mu), _rows(rwkv_w0), _pad_rows(rwkv_w2, 0, RWKV_LORA), _rows(rwkv_a0),
            _pad_rows(rwkv_a2, RWKV_DECAY_LORA, RWKV_LORA),
            _pad_rows(rwkv_g2, RWKV_DECAY_LORA + RWKV_A_LORA, RWKV_LORA),
            _rows(rwkv_k_k), _rows(rwkv_k_a), _rows(rwkv_r_k), _rows(rwkv_ln_w), _rows(rwkv_ln_b))
    gla = (_pad_rows(gla_gate_w2, 0, LANE), _rows(gla_gate_b), _rows(jnp.tile(gla_norm, (1, GLA_HEADS))))
    ssd = (mamba_conv_w.astype(F32), _rows(mamba_conv_b), _pad_lanes(mamba_dt_bias), _pad_lanes(mamba_A_log),
           _rows(jnp.repeat(mamba_D, HEAD_DIM, axis=1)), _rows(mamba_norm))
    post = (w_out.astype(BF16), _rows(ffn2_norm), ffn2_w_gate.astype(BF16), ffn2_w_up.astype(BF16),
            ffn2_w_down.astype(BF16), _rows(ple_norm), ple_w_gate.astype(BF16), ple_w_proj.astype(BF16))
    return pre, rwkv, gla, ssd, post


def _decode_params(nb, rwkv_mu, rwkv_w0, rwkv_w2, rwkv_a0, rwkv_a2, rwkv_g2, rwkv_k_k, rwkv_k_a, rwkv_r_k,
                   rwkv_ln_w, rwkv_ln_b, gla_gate_w2, gla_gate_b, gla_norm, mamba_conv_w, mamba_conv_b,
                   mamba_dt_bias, mamba_A_log, mamba_D, mamba_norm):
    depth = rwkv_mu.shape[0]
    slab = lambda vs: jnp.broadcast_to(jnp.concatenate([v.reshape(depth, -1) for v in vs], axis=1)[:, :, None],
                                       (depth, sum(v[0].size for v in vs), nb)).astype(F32)
    tr = lambda m: jnp.swapaxes(m, 1, 2)
    rwkv = (slab((rwkv_mu, rwkv_w0, rwkv_a0, rwkv_k_k, rwkv_k_a, rwkv_r_k, rwkv_ln_w, rwkv_ln_b)),
            tr(_pad_rows(rwkv_w2, 0, RWKV_LORA)), tr(_pad_rows(rwkv_a2, RWKV_DECAY_LORA, RWKV_LORA)),
            tr(_pad_rows(rwkv_g2, RWKV_DECAY_LORA + RWKV_A_LORA, RWKV_LORA)))
    gla = (slab((gla_gate_b, jnp.tile(gla_norm, (1, GLA_HEADS)))), tr(_pad_rows(gla_gate_w2, 0, LANE)))
    ssd = (slab((mamba_conv_w, mamba_conv_b, mamba_dt_bias, mamba_A_log, jnp.repeat(mamba_D, HEAD_DIM, axis=1),
                 mamba_norm)),)
    return rwkv, gla, ssd


def _sample_layer(x, p, state, params, dparams, layer, final_norm, *, batch, seq, final, earlier):
    pre, _, _, _, post = params
    d_rwkv, d_gla, d_ssd = dparams
    shift0, wkv0, gla0, conv0, ssm0 = state
    old_wkv, old_gla, old_ssm = earlier
    n = batch * seq
    x1, zr_t, zg_t, zm_t = _pre_call(x, layer, *pre, n, steps=seq)
    shift1 = zr_t[seq - 1].T
    xbc = jnp.transpose(zm_t[:, M2_W:M2_W + M2_CONV_DIM, :], (2, 0, 1))
    conv1 = jnp.concatenate([conv0[layer], xbc], axis=1)[:, -(M2_CONV - 1):]
    scr = lambda *shape: pltpu.VMEM(shape, F32)
    yr, wkv1 = _dec_call(_rwkv_dec_kernel, "rwkv7_decode", (zr_t,), wkv0, layer,
                         (jnp.swapaxes(shift0, 1, 2),) + d_rwkv, (_HEAD_ONES(),), layer, RWKV_W,
                         [scr(seq, 6, RWKV_W, batch), scr(seq, RWKV_W, batch), scr(seq, 2, RWKV_W, batch)], old_wkv)
    yg, gla1 = _dec_call(_gla_dec_kernel, "gla_decode", (zg_t,), gla0, layer, d_gla, (_HEAD_ONES(),), layer, GLA_W,
                         [scr(seq, 3, GLA_KEY_W, batch), scr(seq, GLA_W, batch), scr(seq, GLA_W, batch)], old_gla)
    conv_t = jnp.transpose(conv0, (0, 2, 3, 1))
    ym, ssm1 = _dec_call(_ssd_dec_kernel, "ssd_decode", (zm_t,), ssm0, layer, (conv_t,) + d_ssd, (), layer, M2_W,
                         [scr(seq, M2_W, batch), scr(seq, 2 * M2_GROUPS * M2_STATE, batch),
                          scr(seq, 2, ROWS, batch), scr(seq, M2_W, batch)], old_ssm)
    x2 = _post_call(x1, yr, yg, ym, p, layer, *post, final_norm.reshape(1, -1), n, final, steps=seq)
    return x2, (shift1, wkv1, gla1, conv1, ssm1)


def _prompt_layer(x, p, params, layer, final_norm, *, batch, seq, bb, tm, tm_post, final):
    pre, rwkv, gla, ssd, post = params
    assert seq % RWKV_CHUNK == 0 and seq % GLA_CHUNK == 0 and seq % SSD_CHUNK == 0 and seq >= M2_CONV - 1
    x1, zr, zg, zm = _pre_call(x, layer, *pre, tm)
    zr = zr.reshape(batch, seq, RWKV_PROJ)
    zg = zg.reshape(batch, seq, GLA_PAD)
    zm = zm.reshape(batch, seq, M2_PAD)
    shift1 = zr[:, -1]
    conv1 = zm[:, seq - (M2_CONV - 1):, M2_W:M2_W + M2_CONV_DIM]
    zero = lambda *shape: jnp.zeros((1, batch) + shape, F32)
    yr, wkv1 = _rwkv_call(zr, zero(1, RWKV_PROJ), zero(RWKV_HEADS, HEAD_DIM, HEAD_DIM), 0, rwkv, layer, bb,
                          RWKV_CHUNK, BF16)
    yg, gla1 = _gla_call(zg, zero(GLA_HEADS, GLA_DK, HEAD_DIM), 0, gla, layer, bb, GLA_CHUNK, BF16)
    ym, ssm1 = _ssd_call(zm, zero(M2_CONV - 1, M2_CONV_DIM), zero(M2_HEADS, HEAD_DIM, M2_STATE), 0, ssd, layer, bb,
                         SSD_CHUNK, BF16)
    n = batch * seq
    flat = lambda y: y.reshape(n, y.shape[-1])
    x2 = _post_call(x1, flat(yr), flat(yg), flat(ym), p, layer, *post, final_norm.reshape(1, -1), tm_post, final)
    return x2, (shift1, wkv1, gla1, conv1, ssm1)


def kernel(x_prompt, x_sample, p_prompt, p_sample, state_rwkv_shift, state_rwkv_wkv, state_gla, state_mamba_conv, state_mamba_ssm, ffn1_norm, ffn1_w_gate, ffn1_w_up, ffn1_w_down, mix_norm, w_in, rwkv_mu, rwkv_w0, rwkv_w2, rwkv_a0, rwkv_a2, rwkv_g2, rwkv_k_k, rwkv_k_a, rwkv_r_k, rwkv_ln_w, rwkv_ln_b, gla_gate_w2, gla_gate_b, gla_norm, mamba_conv_w, mamba_conv_b, mamba_dt_bias, mamba_A_log, mamba_D, mamba_norm, w_out, ffn2_norm, ffn2_w_gate, ffn2_w_up, ffn2_w_down, ple_norm, ple_w_gate, ple_w_proj, final_norm):
    depth = w_in.shape[0]
    nb, seq, _ = x_prompt.shape
    db, dseq, _ = x_sample.shape
    weights = (ffn1_norm, ffn1_w_gate, ffn1_w_up, ffn1_w_down, mix_norm, w_in, rwkv_mu, rwkv_w0, rwkv_w2, rwkv_a0,
               rwkv_a2, rwkv_g2, rwkv_k_k, rwkv_k_a, rwkv_r_k, rwkv_ln_w, rwkv_ln_b, gla_gate_w2, gla_gate_b,
               gla_norm, mamba_conv_w, mamba_conv_b, mamba_dt_bias, mamba_A_log, mamba_D, mamba_norm, w_out,
               ffn2_norm, ffn2_w_gate, ffn2_w_up, ffn2_w_down, ple_norm, ple_w_gate, ple_w_proj)
    xp = x_prompt.reshape(nb * seq, D_MODEL)
    xs = jnp.swapaxes(x_sample, 0, 1).reshape(dseq * db, D_MODEL)
    pp = p_prompt.reshape(depth, nb * seq, PLE_DIM)
    ps = jnp.swapaxes(p_sample, 1, 2).reshape(depth, dseq * db, PLE_DIM)
    params = _stacked_params(*weights)
    dparams = _decode_params(db, rwkv_mu, rwkv_w0, rwkv_w2, rwkv_a0, rwkv_a2, rwkv_g2, rwkv_k_k, rwkv_k_a, rwkv_r_k,
                             rwkv_ln_w, rwkv_ln_b, gla_gate_w2, gla_gate_b, gla_norm, mamba_conv_w, mamba_conv_b,
                             mamba_dt_bias, mamba_A_log, mamba_D, mamba_norm)
    batch_last = lambda s: jnp.moveaxis(s, 1, -1)
    past = (state_rwkv_shift, batch_last(state_rwkv_wkv), batch_last(state_gla), state_mamba_conv,
            batch_last(state_mamba_ssm))
    p_states, s_states = [], []
    for i in range(depth):
        final = i == depth - 1
        xp, st_p = _prompt_layer(xp, pp, params, i, final_norm, batch=nb, seq=seq, bb=nb, tm=512, tm_post=1024,
                                 final=final)
        earlier = tuple(tuple(s[j] for s in s_states) if final else () for j in (1, 2, 4))
        xs, st_s = _sample_layer(xs, ps, past, params, dparams, i, final_norm, batch=db, seq=dseq, final=final,
                                 earlier=earlier)
        p_states.append(st_p)
        s_states.append(st_s)
    stack = lambda sts, j: jnp.stack([s[j] for s in sts])
    mats = s_states[-1] if depth > 1 else tuple(s[None] for s in s_states[-1])
    s_out = tuple(jnp.moveaxis(mats[j], -1, 1) if j in (1, 2, 4) else stack(s_states, j) for j in range(5))
    return ((xp.reshape(nb, seq, D_MODEL), jnp.swapaxes(xs.reshape(dseq, db, D_MODEL), 0, 1))
            + tuple(stack(p_states, j) for j in range(5)) + s_out)
```

```python
import functools

import jax
import jax.numpy as jnp
from jax import lax
from jax.experimental import pallas as pl
from jax.experimental.pallas import tpu as pltpu

F32 = jnp.float32
BF16 = jnp.bfloat16

D_MODEL = 1024
D_FF = 2816
PLE_DIM = 256
HEAD_DIM = 64
NORM_EPS = 1e-6

RWKV_W = 256
RWKV_HEADS = 4
RWKV_GN_EPS = 64e-5
RWKV_PROJ = 896
RWKV_LORA = 128
RWKV_DECAY_LORA = 32
RWKV_A_LORA = 32
RWKV_DECAY_SCALE = 0.6065306597126334
RWKV_KEY_NORM_FLOOR = 1e-24

GLA_W = 256
GLA_HEADS = 4
GLA_DK = 32
GLA_KEY_W = 128
GLA_GATE_LORA = 16
GLA_TAU = 16.0
GLA_PROJ = 784
GLA_PAD = 896

M2_W = 512
M2_HEADS = 8
M2_STATE = 64
M2_GROUPS = 2
M2_CONV = 4
M2_CONV_DIM = 768
M2_PROJ = 1288
M2_PAD = 1408

LANE = 128
SUBLANE = 8
FF_CHUNK = 256
VMEM_LIMIT = 56 * 1024 * 1024

RWKV_CHUNK = 128
GLA_CHUNK = 64
GLA_SUB = 4
SSD_CHUNK = 128


def _bdot(a, b):
    return jnp.dot(a.astype(BF16), b.astype(BF16), preferred_element_type=F32)


def _bdot_nt(a, b):
    return lax.dot_general(a.astype(BF16), b.astype(BF16), (((1,), (1,)), ((), ())),
                           preferred_element_type=F32)


def _bdot_tn(a, b):
    return lax.dot_general(a.astype(BF16), b.astype(BF16), (((0,), (0,)), ((), ())),
                           preferred_element_type=F32)


def _split3(x):
    hi = x.astype(BF16)
    r1 = x - hi.astype(F32)
    mid = r1.astype(BF16)
    lo = (r1 - mid.astype(F32)).astype(BF16)
    return hi, mid, lo


def _ldot3(mat, x):
    return sum(jnp.dot(mat, p, preferred_element_type=F32) for p in _split3(x))


def _rdot2(x, mat):
    return sum(jnp.dot(p, mat, preferred_element_type=F32) for p in _split3(x)[:2])


def _nt3(mat, x):
    return sum(lax.dot_general(mat, p, (((1,), (1,)), ((), ())), preferred_element_type=F32)
               for p in _split3(x))


def _softplus(x):
    return jnp.maximum(x, 0.0) + jnp.log1p(jnp.exp(-jnp.abs(x)))


def _sigmoid(x):
    return 0.5 * jnp.tanh(0.5 * x) + 0.5


def _silu(x):
    return x * _sigmoid(x)


def _rms(x, w):
    return x * lax.rsqrt(jnp.mean(x * x, axis=-1, keepdims=True) + NORM_EPS) * w


def _const_spec(shape):
    nd = len(shape)
    return pl.BlockSpec(shape, lambda *_: (0,) * nd, pipeline_mode=pl.Buffered(1))


def _layer_spec(a, layer):
    nd = a.ndim - 1
    return pl.BlockSpec((None,) + a.shape[1:], lambda *_: (layer,) + (0,) * nd, pipeline_mode=pl.Buffered(1))


def _ffn(x, nw, wg_ref, wu_ref, wd_ref, act_ref):
    xn = _rms(x, nw).astype(BF16)
    for c in range(D_FF // FF_CHUNK):
        sl = slice(c * FF_CHUNK, (c + 1) * FF_CHUNK)
        hg = jnp.dot(xn, wg_ref[:, sl], preferred_element_type=F32)
        hu = jnp.dot(xn, wu_ref[:, sl], preferred_element_type=F32)
        act_ref[:, sl] = (_silu(hg) * hu).astype(BF16)
    return x + 0.5 * jnp.dot(act_ref[...], wd_ref[...], preferred_element_type=F32)


def _pre_kernel(x_ref, n1_ref, wg_ref, wu_ref, wd_ref, nm_ref, win_ref,
                x1_ref, zr_ref, zg_ref, zm_ref, act_ref, *, steps):
    x1 = _ffn(x_ref[...], n1_ref[...], wg_ref, wu_ref, wd_ref, act_ref)
    x1_ref[...] = x1
    h = _rms(x1, nm_ref[...]).astype(BF16)
    cols = (slice(0, RWKV_PROJ), slice(RWKV_PROJ, RWKV_PROJ + GLA_PAD), slice(RWKV_PROJ + GLA_PAD, None))
    for ref, cs in zip((zr_ref, zg_ref, zm_ref), cols):
        z = jnp.dot(h, win_ref[:, cs], preferred_element_type=F32)
        if steps:
            nb = z.shape[0] // steps
            for t in range(steps):
                ref[t] = z[t * nb:(t + 1) * nb].T
        else:
            ref[...] = z


def _pre_call(x, layer, n1, wg, wu, wd, nm, win, tm, steps=0):
    n = x.shape[0]
    tok = lambda w: pl.BlockSpec((tm, w), lambda i: (i, 0))
    if steps:
        assert n == tm and n % steps == 0
        z_spec = lambda w: pl.BlockSpec((steps, w, n // steps), lambda i: (0, 0, 0))
        z_shape = lambda w: jax.ShapeDtypeStruct((steps, w, n // steps), F32)
    else:
        z_spec = tok
        z_shape = lambda w: jax.ShapeDtypeStruct((n, w), F32)
    return pl.pallas_call(
        functools.partial(_pre_kernel, steps=steps),
        grid=(n // tm,),
        in_specs=[tok(D_MODEL)] + [_layer_spec(a, layer) for a in (n1, wg, wu, wd, nm, win)],
        out_specs=[tok(D_MODEL), z_spec(RWKV_PROJ), z_spec(GLA_PAD), z_spec(M2_PAD)],
        out_shape=[jax.ShapeDtypeStruct((n, D_MODEL), F32), z_shape(RWKV_PROJ), z_shape(GLA_PAD), z_shape(M2_PAD)],
        scratch_shapes=[pltpu.VMEM((tm, D_FF), BF16)],
        compiler_params=pltpu.CompilerParams(dimension_semantics=("parallel",), vmem_limit_bytes=VMEM_LIMIT),
        name="pre_ffn_inproj",
    )(x, n1, wg, wu, wd, nm, win)


def _post_kernel(x_ref, yr_ref, yg_ref, ym_ref, p_ref, wo_ref, n2_ref, wg_ref, wu_ref, wd_ref,
                 np_ref, pg_ref, pp_ref, nf_ref, o_ref, act_ref, *, final, steps):
    rows = lambda ref: (jnp.concatenate([ref[t].T for t in range(steps)], axis=0) if steps else ref[...])
    x = x_ref[...]
    x = x + jnp.dot(rows(yr_ref).astype(BF16), wo_ref[0:RWKV_W, :], preferred_element_type=F32)
    x = x + jnp.dot(rows(yg_ref).astype(BF16), wo_ref[RWKV_W:RWKV_W + GLA_W, :], preferred_element_type=F32)
    x = x + jnp.dot(rows(ym_ref).astype(BF16), wo_ref[RWKV_W + GLA_W:, :], preferred_element_type=F32)
    x = _ffn(x, n2_ref[...], wg_ref, wu_ref, wd_ref, act_ref)
    gate = _sigmoid(jnp.dot(_rms(x, np_ref[...]).astype(BF16), pg_ref[...], preferred_element_type=F32))
    x = x + gate * jnp.dot(p_ref[...].astype(BF16), pp_ref[...], preferred_element_type=F32)
    if final:
        x = _rms(x, nf_ref[...])
    o_ref[...] = x


def _post_call(x, yr, yg, ym, p, layer, wo, n2, wg, wu, wd, npn, pg, pp, nf, tm, final, steps=0):
    n = x.shape[0]
    tok = lambda w: pl.BlockSpec((tm, w), lambda i: (i, 0))
    if steps:
        assert n == tm and n % steps == 0
        y_spec = lambda w: pl.BlockSpec((steps, w, n // steps), lambda i: (0, 0, 0))
    else:
        y_spec = tok
    params = (wo, n2, wg, wu, wd, npn, pg, pp)
    consts = params + (nf,)
    return pl.pallas_call(
        functools.partial(_post_kernel, final=final, steps=steps),
        grid=(n // tm,),
        in_specs=[tok(D_MODEL), y_spec(RWKV_W), y_spec(GLA_W), y_spec(M2_W),
                  pl.BlockSpec((None, tm, PLE_DIM), lambda i: (layer, i, 0))]
                 + [_layer_spec(a, layer) for a in params] + [_const_spec(nf.shape)],
        out_specs=tok(D_MODEL),
        out_shape=jax.ShapeDtypeStruct((n, D_MODEL), F32),
        scratch_shapes=[pltpu.VMEM((tm, D_FF), BF16)],
        compiler_params=pltpu.CompilerParams(dimension_semantics=("parallel",), vmem_limit_bytes=VMEM_LIMIT),
        name="post_outproj_ffn_ple",
    )(x, yr, yg, ym, p, *consts)


def _tri_masks(n):
    ri = lax.broadcasted_iota(jnp.int32, (n, n), 0)
    ci = lax.broadcasted_iota(jnp.int32, (n, n), 1)
    return ri >= ci, ri > ci, (ri == ci).astype(F32)


def _rwkv_kernel(z_ref, sh_ref, s0_ref, mu_ref, w0_ref, w2_ref, a0_ref, a2_ref, g2_ref, kk_ref, ka_ref,
                 rk_ref, lnw_ref, lnb_ref, tri_ref, ones_ref,
                 y_ref, s_ref, zbuf_ref, ybuf_ref, *, bb, chunk):
    c = pl.program_id(1)

    @pl.when(c == 0)
    def _():
        s_ref[...] = s0_ref[...]
        zbuf_ref[:, SUBLANE - 1:SUBLANE, :] = sh_ref[...]

    incl, strict, eye = _tri_masks(chunk)
    tri = tri_ref[...]
    ones = ones_ref[...]
    mu = mu_ref[...]
    seqs = range(bb)
    zs = []
    for i in seqs:
        z = z_ref[i]
        zbuf_ref[i, SUBLANE:SUBLANE + chunk, :] = z
        prev = zbuf_ref[i, SUBLANE - 1:SUBLANE - 1 + chunk, :]
        zbuf_ref[i, SUBLANE - 1:SUBLANE, :] = z[chunk - 1:chunk, :]
        zs.append(z + mu * (prev - z))
    r = [x[:, 0:RWKV_W] for x in zs]
    k = [x[:, RWKV_W:2 * RWKV_W] for x in zs]
    v = [x[:, 2 * RWKV_W:3 * RWKV_W] for x in zs]
    lora = [x[:, 3 * RWKV_W:] for x in zs]
    w_lin = [_bdot(jnp.tanh(x), w2_ref[...]) for x in lora]
    a_lin = [_bdot(x, a2_ref[...]) for x in lora]
    gate = [_bdot(_sigmoid(x), g2_ref[...]) for x in lora]
    kk = [x * kk_ref[...] for x in k]
    kk_ss = [_rdot2(x * x, ones) for x in kk]
    logw = [-RWKV_DECAY_SCALE * _sigmoid(w0_ref[...] + x) for x in w_lin]
    a = [_sigmoid(a0_ref[...] + x) for x in a_lin]
    kk = [x * lax.rsqrt(jnp.maximum(ss, RWKV_KEY_NORM_FLOOR)) for x, ss in zip(kk, kk_ss)]
    k = [x * (1.0 + (ai - 1.0) * ka_ref[...]) for x, ai in zip(k, a)]
    ahat = [-x for x in kk]
    bhat = [x * ai for x, ai in zip(kk, a)]
    g = [_ldot3(tri, x) for x in logw]
    glast = [x[chunk - 1:chunk, :] for x in g]
    e_ng = [jnp.exp(-x) for x in g]
    e_gl = [jnp.exp(gl - x) for gl, x in zip(glast, g)]
    a_l = [ah * jnp.exp(x - lw) for ah, x, lw in zip(ahat, g, logw)]
    r_l = [ri * jnp.exp(x) for ri, x in zip(r, g)]
    b_r = [x * e for x, e in zip(bhat, e_ng)]
    k_r = [x * e for x, e in zip(k, e_ng)]
    k_p = [x * e for x, e in zip(k, e_gl)]
    b_p = [x * e for x, e in zip(bhat, e_gl)]
    dec = [jnp.exp(x) for x in glast]

    probs = [(i, h) for i in seqs for h in range(RWKV_HEADS)]
    hd = lambda xs: [xs[i][:, h * HEAD_DIM:(h + 1) * HEAD_DIM] for i, h in probs]
    a_lh, r_lh, b_rh, k_rh, k_ph, b_ph, vh = hd(a_l), hd(r_l), hd(b_r), hd(k_r), hd(k_p), hd(b_p), hd(v)
    ar = [jnp.concatenate([x, y], axis=0) for x, y in zip(a_lh, r_lh)]
    bk = [jnp.concatenate([x, y], axis=0) for x, y in zip(b_rh, k_rh)]
    gm = [_bdot_nt(x, y) for x, y in zip(ar, bk)]
    a_ab = [jnp.where(strict, x[:chunk, :chunk], 0.0) for x in gm]
    a_ak = [jnp.where(strict, x[:chunk, chunk:], 0.0) for x in gm]
    a_rb = [jnp.where(incl, x[chunk:, :chunk], 0.0) for x in gm]
    a_rk = [jnp.where(incl, x[chunk:, chunk:], 0.0) for x in gm]
    t = [eye + m for m in a_ab]
    x = [_bdot(m, m) for m in a_ab]
    p = 2
    while p < chunk:
        if 2 * p < chunk:
            xt = [_bdot(jnp.concatenate([xi, ti], axis=0), xi) for xi, ti in zip(x, t)]
            x = [m[:chunk] for m in xt]
            t = [ti + m[chunk:] for ti, m in zip(t, xt)]
        else:
            t = [ti + _bdot(ti, xi) for xi, ti in zip(x, t)]
        p *= 2
    akv = [_bdot(m, vi) for m, vi in zip(a_ak, vh)]
    w_t = [_bdot(ti, m) for ti, m in zip(t, a_lh)]
    u_t = [_bdot(ti, m) for ti, m in zip(t, akv)]
    r_t = [ri + _bdot(m, wi) for ri, m, wi in zip(r_lh, a_rb, w_t)]
    y_t = [_bdot(m, vi) + _bdot(n, ui) for m, vi, n, ui in zip(a_rk, vh, a_rb, u_t)]
    m_s = [_bdot_tn(wi, bi) for wi, bi in zip(w_t, b_ph)]
    q_s = [_bdot_tn(jnp.concatenate([vi, ui], axis=0), jnp.concatenate([ki, bi], axis=0))
           for vi, ui, ki, bi in zip(vh, u_t, k_ph, b_ph)]
    s_old = [s_ref[i, h] for i, h in probs]
    y_h = [_bdot_nt(ri, si) + yi for ri, si, yi in zip(r_t, s_old, y_t)]
    s_m = [_bdot(si, mi) for si, mi in zip(s_old, m_s)]
    for j, (i, h) in enumerate(probs):
        hs = slice(h * HEAD_DIM, (h + 1) * HEAD_DIM)
        ybuf_ref[i, :, hs] = y_h[j]
        s_ref[i, h] = s_old[j] * dec[i][:, hs] + s_m[j] + q_s[j]
    y = [ybuf_ref[i] for i in seqs]
    mean = [_rdot2(x, ones) * (1.0 / HEAD_DIM) for x in y]
    yc = [x - m for x, m in zip(y, mean)]
    var = [_rdot2(x * x, ones) * (1.0 / HEAD_DIM) for x in yc]
    bonus = [_rdot2(ri * ki * rk_ref[...], ones) * vi for ri, ki, vi in zip(r, k, v)]
    for i in seqs:
        yn = yc[i] * lax.rsqrt(var[i] + RWKV_GN_EPS) * lnw_ref[...] + lnb_ref[...]
        y_ref[i] = ((yn + bonus[i]) * gate[i]).astype(y_ref.dtype)


def _state_spec(a, bb, sl):
    nd = a.ndim - 2
    return pl.BlockSpec((None, bb) + a.shape[2:], lambda i, c: (sl, i) + (0,) * nd)


def _rwkv_call(z, shift, s0, sl, params, layer, bb, chunk, out_dtype):
    b, l, _ = z.shape
    grid = (b // bb, l // chunk)
    tri = jnp.tril(jnp.ones((chunk, chunk), F32)).astype(BF16)
    consts = (tri, _HEAD_ONES())
    return pl.pallas_call(
        functools.partial(_rwkv_kernel, bb=bb, chunk=chunk),
        grid=grid,
        in_specs=[pl.BlockSpec((bb, chunk, RWKV_PROJ), lambda i, c: (i, c, 0)),
                  _state_spec(shift, bb, sl), _state_spec(s0, bb, sl)]
                 + [_layer_spec(a, layer) for a in params] + [_const_spec(a.shape) for a in consts],
        out_specs=[pl.BlockSpec((bb, chunk, RWKV_W), lambda i, c: (i, c, 0)),
                   pl.BlockSpec((bb, RWKV_HEADS, HEAD_DIM, HEAD_DIM), lambda i, c: (i, 0, 0, 0))],
        out_shape=[jax.ShapeDtypeStruct((b, l, RWKV_W), out_dtype),
                   jax.ShapeDtypeStruct((b, RWKV_HEADS, HEAD_DIM, HEAD_DIM), F32)],
        scratch_shapes=[pltpu.VMEM((bb, SUBLANE + chunk, RWKV_PROJ), F32),
                        pltpu.VMEM((bb, chunk, RWKV_W), F32)],
        compiler_params=pltpu.CompilerParams(dimension_semantics=("arbitrary", "arbitrary"),
                                             vmem_limit_bytes=VMEM_LIMIT),
        name="rwkv7_mixer",
    )(z, shift, s0, *params, *consts)


def _gla_kernel(z_ref, s0_ref, gw_ref, gb_ref, nw_ref, tri_ref, ones_ref, eye_ref,
                y_ref, s_ref, obuf_ref, *, bb, chunk, sub):
    c = pl.program_id(1)

    @pl.when(c == 0)
    def _():
        s_ref[...] = s0_ref[...]

    incl, _, _ = _tri_masks(chunk)
    tri = tri_ref[...]
    ones = ones_ref[...]
    units = [(i, u) for u in range(sub) for i in range(bb)]
    seqs = range(len(units))
    z = [z_ref[i, u * chunk:(u + 1) * chunk, :] for i, u in units]
    q = [x[:, 0:GLA_KEY_W] * (GLA_DK ** -0.5) for x in z]
    k = [x[:, GLA_KEY_W:2 * GLA_KEY_W] for x in z]
    v = [x[:, 2 * GLA_KEY_W:2 * GLA_KEY_W + GLA_W] for x in z]
    og = [x[:, 2 * GLA_KEY_W + GLA_W:2 * GLA_KEY_W + 2 * GLA_W] for x in z]
    lin = [_bdot(x[:, 2 * GLA_KEY_W + 2 * GLA_W:], gw_ref[...]) for x in z]
    log_a = [-_softplus(-(x + gb_ref[...])) * (1.0 / GLA_TAU) for x in lin]
    b = [_ldot3(tri, x) for x in log_a]
    blast = [x[chunk - 1:chunk, :] for x in b]
    qe = [x * jnp.exp(bi) for x, bi in zip(q, b)]
    ke = [x * jnp.exp(-bi) for x, bi in zip(k, b)]
    kl = [x * jnp.exp(bl - bi) for x, bl, bi in zip(k, blast, b)]
    dec = [_nt3(eye_ref[...], jnp.broadcast_to(jnp.exp(bl), (HEAD_DIM, GLA_KEY_W))) for bl in blast]

    probs = [(i, h) for i in seqs for h in range(GLA_HEADS)]
    kd = lambda xs: [xs[i][:, h * GLA_DK:(h + 1) * GLA_DK] for i, h in probs]
    qeh, keh, klh = kd(qe), kd(ke), kd(kl)
    vh = [v[i][:, h * HEAD_DIM:(h + 1) * HEAD_DIM] for i, h in probs]
    att = [jnp.where(incl, _bdot_nt(x, y), 0.0) for x, y in zip(qeh, keh)]
    intra = [_bdot(ai, vi) for ai, vi in zip(att, vh)]
    kv = [_bdot_tn(ki, vi) for ki, vi in zip(klh, vh)]
    state = {(i, h): s_ref[i, h] for i in range(bb) for h in range(GLA_HEADS)}
    for u in range(sub):
        todo = [p for p, (j, h) in enumerate(probs) if units[j][1] == u]
        inter = [_bdot(qeh[p], state[units[probs[p][0]][0], probs[p][1]]) for p in todo]
        for p, x in zip(todo, inter):
            j, h = probs[p]
            i = units[j][0]
            obuf_ref[i, u * chunk:(u + 1) * chunk, h * HEAD_DIM:(h + 1) * HEAD_DIM] = intra[p] + x
            state[i, h] = state[i, h] * dec[j][h * GLA_DK:(h + 1) * GLA_DK, :] + kv[p]
    for (i, h), s in state.items():
        s_ref[i, h] = s
    rows = [slice(u * chunk, (u + 1) * chunk) for _, u in units]
    o = [obuf_ref[i, r, :] for (i, _), r in zip(units, rows)]
    ms = [_rdot2(x * x, ones) * (1.0 / HEAD_DIM) for x in o]
    for j, ((i, _), r) in enumerate(zip(units, rows)):
        y_ref[i, r, :] = (o[j] * lax.rsqrt(ms[j] + NORM_EPS) * nw_ref[...] * _silu(og[j])).astype(y_ref.dtype)


def _gla_call(z, s0, sl, params, layer, bb, chunk, sub, out_dtype):
    b, l, _ = z.shape
    step = chunk * sub
    tri = jnp.tril(jnp.ones((chunk, chunk), F32)).astype(BF16)
    consts = (tri, _HEAD_ONES(), jnp.eye(LANE, dtype=BF16))
    return pl.pallas_call(
        functools.partial(_gla_kernel, bb=bb, chunk=chunk, sub=sub),
        grid=(b // bb, l // step),
        in_specs=[pl.BlockSpec((bb, step, GLA_PAD), lambda i, c: (i, c, 0)), _state_spec(s0, bb, sl)]
                 + [_layer_spec(a, layer) for a in params] + [_const_spec(a.shape) for a in consts],
        out_specs=[pl.BlockSpec((bb, step, GLA_W), lambda i, c: (i, c, 0)),
                   pl.BlockSpec((bb, GLA_HEADS, GLA_DK, HEAD_DIM), lambda i, c: (i, 0, 0, 0))],
        out_shape=[jax.ShapeDtypeStruct((b, l, GLA_W), out_dtype),
                   jax.ShapeDtypeStruct((b, GLA_HEADS, GLA_DK, HEAD_DIM), F32)],
        scratch_shapes=[pltpu.VMEM((bb, step, GLA_W), F32)],
        compiler_params=pltpu.CompilerParams(dimension_semantics=("arbitrary", "arbitrary"),
                                             vmem_limit_bytes=VMEM_LIMIT),
        name="gla_mixer",
    )(z, s0, *params, *consts)


def _ssd_kernel(z_ref, cp_ref, s0_ref, cw_ref, cb_ref, dtb_ref, alog_ref, dsk_ref, nw_ref, tri_ref, eye_ref,
                esel_ref, y_ref, s_ref, xbuf_ref, ybuf_ref, *, bb, chunk):
    c = pl.program_id(1)

    @pl.when(c == 0)
    def _():
        s_ref[...] = s0_ref[...]
        xbuf_ref[:, SUBLANE - (M2_CONV - 1):SUBLANE, :] = cp_ref[...]

    incl, _, _ = _tri_masks(chunk)
    tri = tri_ref[...]
    lane = lax.broadcasted_iota(jnp.int32, (1, LANE), 1)
    rep = M2_HEADS // M2_GROUPS
    seqs = range(bb)
    zg, xc, dt = [], [], []
    live = lane < M2_HEADS
    for i in seqs:
        z = z_ref[i]
        x = z[:, M2_W:M2_W + M2_CONV_DIM]
        xbuf_ref[i, SUBLANE:SUBLANE + chunk, :] = x
        conv = cb_ref[...] + x * cw_ref[M2_CONV - 1:M2_CONV, :]
        for j in range(1, M2_CONV):
            conv = conv + xbuf_ref[i, SUBLANE - j:SUBLANE - j + chunk, :] * cw_ref[M2_CONV - 1 - j:M2_CONV - j, :]
        xbuf_ref[i, 0:SUBLANE, :] = xbuf_ref[i, chunk:chunk + SUBLANE, :]
        zg.append(z[:, 0:M2_W])
        xc.append(_silu(conv))
        dt.append(jnp.where(live, _softplus(z[:, M2_W + M2_CONV_DIM:] + dtb_ref[...]), 0.0))
    a_neg = -jnp.exp(alog_ref[...])
    cum = [_ldot3(tri, x * a_neg) for x in dt]
    cum_t = [_nt3(eye_ref[0:SUBLANE, :], x) for x in cum]
    dt_b = [_rdot2(x, esel_ref[...]) for x in dt]
    grp = [(i, g) for i in seqs for g in range(M2_GROUPS)]
    bg = [xc[i][:, M2_W + g * M2_STATE:M2_W + (g + 1) * M2_STATE] for i, g in grp]
    cg = [xc[i][:, M2_W + (M2_GROUPS + g) * M2_STATE:M2_W + (M2_GROUPS + g + 1) * M2_STATE] for i, g in grp]
    cb = [_bdot_nt(x, y) for x, y in zip(cg, bg)]
    s_grp = [jnp.concatenate([s_ref[i, g * rep + hh] for hh in range(rep)], axis=0) for i, g in grp]
    y_st = [_bdot_nt(x, s) for x, s in zip(cg, s_grp)]
    probs = [(i, h) for i in seqs for h in range(M2_HEADS)]
    ccb = [jnp.broadcast_to(cum[i][:, h:h + 1], (chunk, LANE)) for i, h in probs]
    low = lane < HEAD_DIM
    cum_b = [jnp.concatenate([jnp.where(low, ccb[i * M2_HEADS + 2 * j], ccb[i * M2_HEADS + 2 * j + 1])
                              for j in range(M2_HEADS // 2)], axis=1) for i in seqs]
    xs = [x[:, 0:M2_W] for x in xc]
    xdt = [x * d for x, d in zip(xs, dt_b)]
    xw = [x * jnp.exp(cb_[chunk - 1:chunk, :] - cb_) for x, cb_ in zip(xdt, cum_b)]
    seg = [jnp.where(incl, jnp.exp(jnp.minimum(ccb[j][:, 0:chunk] - cum_t[i][h:h + 1, :], 0.0)), 0.0)
           for j, (i, h) in enumerate(probs)]
    y_in = [_bdot(cb[i * M2_GROUPS + h // rep] * seg[j], xdt[i][:, h * HEAD_DIM:(h + 1) * HEAD_DIM])
            for j, (i, h) in enumerate(probs)]
    s_in = [_bdot_tn(xw[i][:, g * rep * HEAD_DIM:(g + 1) * rep * HEAD_DIM], bg[j]) for j, (i, g) in enumerate(grp)]
    for j, (i, h) in enumerate(probs):
        ybuf_ref[i, :, h * HEAD_DIM:(h + 1) * HEAD_DIM] = y_in[j]
        gj = i * M2_GROUPS + h // rep
        hh = h % rep
        s_ref[i, h] = (s_grp[gj][hh * HEAD_DIM:(hh + 1) * HEAD_DIM] * jnp.exp(cum[i][chunk - 1:chunk, h:h + 1])
                       + s_in[gj][hh * HEAD_DIM:(hh + 1) * HEAD_DIM])
    for i in seqs:
        y_state = jnp.concatenate([y_st[i * M2_GROUPS + g] for g in range(M2_GROUPS)], axis=1)
        y = ybuf_ref[i] + y_state * jnp.exp(cum_b[i]) + dsk_ref[...] * xs[i]
        y_ref[i] = _rms(y * _silu(zg[i]), nw_ref[...]).astype(y_ref.dtype)


def _ssd_call(z, conv_prev, s0, sl, params, layer, bb, chunk, out_dtype):
    b, l, _ = z.shape
    tri = jnp.tril(jnp.ones((chunk, chunk), F32)).astype(BF16)
    esel = (jnp.arange(LANE)[:, None] == jnp.arange(M2_W)[None, :] // HEAD_DIM).astype(BF16)
    consts = (tri, jnp.eye(LANE, dtype=BF16), esel)
    return pl.pallas_call(
        functools.partial(_ssd_kernel, bb=bb, chunk=chunk),
        grid=(b // bb, l // chunk),
        in_specs=[pl.BlockSpec((bb, chunk, M2_PAD), lambda i, c: (i, c, 0)),
                  _state_spec(conv_prev, bb, sl), _state_spec(s0, bb, sl)]
                 + [_layer_spec(a, layer) for a in params] + [_const_spec(a.shape) for a in consts],
        out_specs=[pl.BlockSpec((bb, chunk, M2_W), lambda i, c: (i, c, 0)),
                   pl.BlockSpec((bb, M2_HEADS, HEAD_DIM, M2_STATE), lambda i, c: (i, 0, 0, 0))],
        out_shape=[jax.ShapeDtypeStruct((b, l, M2_W), out_dtype),
                   jax.ShapeDtypeStruct((b, M2_HEADS, HEAD_DIM, M2_STATE), F32)],
        scratch_shapes=[pltpu.VMEM((bb, SUBLANE + chunk, M2_CONV_DIM), F32),
                        pltpu.VMEM((bb, chunk, M2_W), F32)],
        compiler_params=pltpu.CompilerParams(dimension_semantics=("arbitrary", "arbitrary"),
                                             vmem_limit_bytes=VMEM_LIMIT),
        name="ssd_mixer",
    )(z, conv_prev, s0, *params, *consts)


def _HEAD_ONES():
    idx = jnp.arange(RWKV_W) // HEAD_DIM
    return (idx[:, None] == idx[None, :]).astype(BF16)


ROWS = SUBLANE


def _rows_to_tile(rows):
    rid = lax.broadcasted_iota(jnp.int32, (ROWS, 1), 0)
    out = rows[0]
    for j in range(1, ROWS):
        out = jnp.where(rid == j, rows[j], out)
    return out


def _rwkv_dec_kernel(z_ref, s_ref, sh_ref, cols_ref, w2_ref, a2_ref, g2_ref, ones_ref,
                     y_ref, so_ref, vec_ref, ybuf_ref, post_ref):
    h = pl.program_id(0)
    nt = z_ref.shape[0]
    w = RWKV_W
    ones = ones_ref[...]
    col = lambda j: cols_ref[RWKV_PROJ + j * w:RWKV_PROJ + (j + 1) * w, :]

    @pl.when(h == 0)
    def _():
        mu = cols_ref[0:RWKV_PROJ, :]
        prev = sh_ref[...]
        for t in range(nt):
            z = z_ref[t]
            zs = z + mu * (prev - z)
            prev = z
            r, k, v, lora = zs[0:w], zs[w:2 * w], zs[2 * w:3 * w], zs[3 * w:]
            logw = -RWKV_DECAY_SCALE * _sigmoid(col(0) + _bdot(w2_ref[...], jnp.tanh(lora)))
            a = _sigmoid(col(1) + _bdot(a2_ref[...], lora))
            kk = k * col(2)
            kk = kk * lax.rsqrt(jnp.maximum(_ldot3(ones, kk * kk), RWKV_KEY_NORM_FLOOR))
            k = k * (1.0 + (a - 1.0) * col(3))
            for j, x in enumerate((-kk, jnp.exp(logw), kk * a, k, r, v)):
                vec_ref[t, j] = x
            post_ref[t, 0] = _bdot(g2_ref[...], _sigmoid(lora))
            post_ref[t, 1] = _ldot3(ones, r * k * col(4)) * v

    hs = pl.ds(pl.multiple_of(h * HEAD_DIM, HEAD_DIM), HEAD_DIM)

    def body(v8, carry):
        r0 = pl.multiple_of(h * HEAD_DIM + v8 * ROWS, ROWS)
        vt = [vec_ref[t, 5, pl.ds(r0, ROWS), :] for t in range(nt)]
        ys = [[] for _ in range(nt)]
        for j in range(ROWS):
            s = s_ref[0, v8 * ROWS + j]
            for t in range(nt):
                sa = jnp.sum(s * vec_ref[t, 0, hs, :], axis=0, keepdims=True)
                s = s * vec_ref[t, 1, hs, :] + sa * vec_ref[t, 2, hs, :] + vt[t][j:j + 1, :] * vec_ref[t, 3, hs, :]
                ys[t].append(jnp.sum(s * vec_ref[t, 4, hs, :], axis=0, keepdims=True))
            so_ref[0, v8 * ROWS + j] = s
        for t in range(nt):
            ybuf_ref[t, pl.ds(r0, ROWS), :] = _rows_to_tile(ys[t])
        return carry

    lax.fori_loop(0, HEAD_DIM // ROWS, body, 0)

    @pl.when(h == pl.num_programs(0) - 1)
    def _():
        for t in range(nt):
            y = ybuf_ref[t]
            yc = y - _ldot3(ones, y) * (1.0 / HEAD_DIM)
            var = _ldot3(ones, yc * yc) * (1.0 / HEAD_DIM)
            yn = yc * lax.rsqrt(var + RWKV_GN_EPS) * col(5) + col(6)
            y_ref[t] = (yn + post_ref[t, 1]) * post_ref[t, 0]


def _gla_dec_kernel(z_ref, s_ref, cols_ref, gw_ref, ones_ref, y_ref, so_ref, vec_ref, val_ref, obuf_ref):
    h = pl.program_id(0)
    nt = z_ref.shape[0]

    @pl.when(h == 0)
    def _():
        for t in range(nt):
            z = z_ref[t]
            lin = _bdot(gw_ref[...], z[2 * GLA_KEY_W + 2 * GLA_W:]) + cols_ref[0:GLA_KEY_W, :]
            vec_ref[t, 0] = z[0:GLA_KEY_W] * (GLA_DK ** -0.5)
            vec_ref[t, 1] = z[GLA_KEY_W:2 * GLA_KEY_W]
            vec_ref[t, 2] = jnp.exp(-_softplus(-lin) * (1.0 / GLA_TAU))
            val_ref[t] = z[2 * GLA_KEY_W:2 * GLA_KEY_W + GLA_W]

    vs = pl.ds(pl.multiple_of(h * HEAD_DIM, HEAD_DIM), HEAD_DIM)
    v = [val_ref[t, vs, :] for t in range(nt)]

    def body(k8, acc):
        r0 = pl.multiple_of(h * GLA_DK + k8 * ROWS, ROWS)
        q, k, a = ([vec_ref[t, j, pl.ds(r0, ROWS), :] for t in range(nt)] for j in range(3))
        acc = list(acc)
        for j in range(ROWS):
            s = s_ref[0, k8 * ROWS + j]
            for t in range(nt):
                s = s * a[t][j:j + 1, :] + k[t][j:j + 1, :] * v[t]
                acc[t] = acc[t] + q[t][j:j + 1, :] * s
            so_ref[0, k8 * ROWS + j] = s
        return tuple(acc)

    zero = jnp.zeros((HEAD_DIM, z_ref.shape[2]), F32)
    acc = lax.fori_loop(0, GLA_DK // ROWS, body, (zero,) * nt)
    for t in range(nt):
        obuf_ref[t, vs, :] = acc[t]

    @pl.when(h == pl.num_programs(0) - 1)
    def _():
        for t in range(nt):
            o = obuf_ref[t]
            ms = _ldot3(ones_ref[...], o * o) * (1.0 / HEAD_DIM)
            og = z_ref[t, 2 * GLA_KEY_W + GLA_W:2 * GLA_KEY_W + 2 * GLA_W, :]
            y_ref[t] = o * lax.rsqrt(ms + NORM_EPS) * cols_ref[GLA_KEY_W:, :] * _silu(og)


def _ssd_dec_kernel(z_ref, s_ref, cp_ref, cols_ref, y_ref, so_ref, x_ref, bc_ref, dt_ref, ybuf_ref):
    h = pl.program_id(0)
    nt = z_ref.shape[0]
    cd = M2_CONV_DIM
    off_b = M2_CONV * cd
    off_dt = off_b + cd

    @pl.when(h == 0)
    def _():
        xs = [cp_ref[j] for j in range(M2_CONV - 1)] + [z_ref[t, M2_W:M2_W + cd, :] for t in range(nt)]
        a_neg = -jnp.exp(cols_ref[off_dt + ROWS:off_dt + 2 * ROWS, :])
        for t in range(nt):
            conv = cols_ref[off_b:off_b + cd, :]
            for j in range(M2_CONV):
                conv = conv + xs[t + j] * cols_ref[j * cd:(j + 1) * cd, :]
            xc = _silu(conv)
            x_ref[t] = xc[0:M2_W]
            bc_ref[t] = xc[M2_W:]
            dt = _softplus(z_ref[t, M2_W + cd:M2_W + cd + ROWS, :] + cols_ref[off_dt:off_dt + ROWS, :])
            dt_ref[t, 0] = dt
            dt_ref[t, 1] = jnp.exp(dt * a_neg)

    g = h // (M2_HEADS // M2_GROUPS)
    bs = pl.ds(pl.multiple_of(g * M2_STATE, M2_STATE), M2_STATE)
    cs = pl.ds(pl.multiple_of((M2_GROUPS + g) * M2_STATE, M2_STATE), M2_STATE)
    rid = lax.broadcasted_iota(jnp.int32, (ROWS, 1), 0)
    pick = lambda tile: jnp.sum(jnp.where(rid == h, tile, 0.0), axis=0, keepdims=True)
    dt = [pick(dt_ref[t, 0]) for t in range(nt)]
    da = [pick(dt_ref[t, 1]) for t in range(nt)]

    def body(p8, carry):
        r0 = pl.multiple_of(h * HEAD_DIM + p8 * ROWS, ROWS)
        xt = [x_ref[t, pl.ds(r0, ROWS), :] * dt[t] for t in range(nt)]
        ys = [[] for _ in range(nt)]
        for j in range(ROWS):
            s = s_ref[0, p8 * ROWS + j]
            for t in range(nt):
                s = s * da[t] + xt[t][j:j + 1, :] * bc_ref[t, bs, :]
                ys[t].append(jnp.sum(s * bc_ref[t, cs, :], axis=0, keepdims=True))
            so_ref[0, p8 * ROWS + j] = s
        for t in range(nt):
            ybuf_ref[t, pl.ds(r0, ROWS), :] = _rows_to_tile(ys[t])
        return carry

    lax.fori_loop(0, HEAD_DIM // ROWS, body, 0)

    @pl.when(h == pl.num_programs(0) - 1)
    def _():
        off_d = off_dt + 2 * ROWS
        for t in range(nt):
            y = ybuf_ref[t] + cols_ref[off_d:off_d + M2_W, :] * x_ref[t]
            y = y * _silu(z_ref[t, 0:M2_W, :])
            ms = jnp.mean(y * y, axis=0, keepdims=True)
            y_ref[t] = y * lax.rsqrt(ms + NORM_EPS) * cols_ref[off_d + M2_W:off_d + 2 * M2_W, :]


def _dec_call(kern, name, acts, state, sl, layer_params, consts, layer, y_width, scratch, earlier=()):
    nt, _, nb = acts[0].shape
    heads = state.shape[1]
    whole = lambda a: pl.BlockSpec(a.shape, lambda hd: (0,) * a.ndim, pipeline_mode=pl.Buffered(1))
    blk = (1,) + state.shape[2:]
    n_in = len(acts) + 1 + len(layer_params) + len(consts)
    n_old = len(earlier)

    def body(*refs):
        ins, old, (y_ref, so_ref), scr = (refs[:n_in], refs[n_in:n_in + n_old], refs[n_in + n_old:n_in + n_old + 2],
                                          refs[n_in + n_old + 2:])
        for j, ref in enumerate(old):
            so_ref[j] = ref[...]
        kern(*ins, y_ref, so_ref.at[n_old] if n_old else so_ref, *scr)

    if n_old:
        so_spec = pl.BlockSpec((n_old + 1,) + blk, lambda hd: (0, hd, 0, 0, 0))
        so_shape = (n_old + 1,) + state.shape[1:]
    else:
        so_spec = pl.BlockSpec(blk, lambda hd: (hd, 0, 0, 0))
        so_shape = state.shape[1:]
    return pl.pallas_call(
        body,
        grid=(heads,),
        in_specs=[whole(a) for a in acts]
                 + [pl.BlockSpec((None,) + blk, lambda hd: (sl, hd, 0, 0, 0))]
                 + [_layer_spec(a, layer) for a in layer_params] + [_const_spec(a.shape) for a in consts]
                 + [pl.BlockSpec(blk, lambda hd: (hd, 0, 0, 0)) for _ in earlier],
        out_specs=[pl.BlockSpec((nt, y_width, nb), lambda hd: (0, 0, 0)), so_spec],
        out_shape=[jax.ShapeDtypeStruct((nt, y_width, nb), F32), jax.ShapeDtypeStruct(so_shape, F32)],
        scratch_shapes=scratch,
        compiler_params=pltpu.CompilerParams(dimension_semantics=("arbitrary",), vmem_limit_bytes=VMEM_LIMIT),
        name=name,
    )(*acts, state, *layer_params, *consts, *earlier)


def _rows(v):
    return v.reshape(v.shape[0], 1, -1).astype(F32)


def _pad_rows(m, top, total):
    return jnp.zeros((m.shape[0], total, m.shape[2]), F32).at[:, top:top + m.shape[1]].set(m).astype(BF16)


def _pad_lanes(v):
    return jnp.zeros((v.shape[0], 1, LANE), F32).at[:, 0, :v.shape[1]].set(v)


def _stacked_params(ffn1_norm, ffn1_w_gate, ffn1_w_up, ffn1_w_down, mix_norm, w_in, rwkv_mu, rwkv_w0, rwkv_w2,
                    rwkv_a0, rwkv_a2, rwkv_g2, rwkv_k_k, rwkv_k_a, rwkv_r_k, rwkv_ln_w, rwkv_ln_b, gla_gate_w2,
                    gla_gate_b, gla_norm, mamba_conv_w, mamba_conv_b, mamba_dt_bias, mamba_A_log, mamba_D, mamba_norm,
                    w_out, ffn2_norm, ffn2_w_gate, ffn2_w_up, ffn2_w_down, ple_norm, ple_w_gate, ple_w_proj):
    depth = w_in.shape[0]
    w = w_in.astype(BF16)
    g0 = RWKV_PROJ
    m0 = RWKV_PROJ + GLA_PROJ
    qkv_end = 2 * GLA_KEY_W + GLA_W
    zeros = lambda n: jnp.zeros((depth, D_MODEL, n), BF16)
    win = jnp.concatenate([
        w[:, :, :g0 + qkv_end], w[:, :, g0 + qkv_end + GLA_GATE_LORA:m0],
        w[:, :, g0 + qkv_end:g0 + qkv_end + GLA_GATE_LORA], zeros(LANE - GLA_GATE_LORA),
        w[:, :, m0:], zeros(LANE - M2_HEADS)], axis=2)
    pre = (_rows(ffn1_norm), ffn1_w_gate.astype(BF16), ffn1_w_up.astype(BF16), ffn1_w_down.astype(BF16),
           _rows(mix_norm), win)
    rwkv = (_rows(rwkv_mu), _rows(rwkv_w0), _pad_rows(rwkv_w2, 0, RWKV_LORA), _rows(rwkv_a0),
            _pad_rows(rwkv_a2, RWKV_DECAY_LORA, RWKV_LORA),
            _pad_rows(rwkv_g2, RWKV_DECAY_LORA + RWKV_A_LORA, RWKV_LORA),
            _rows(rwkv_k_k), _rows(rwkv_k_a), _rows(rwkv_r_k), _rows(rwkv_ln_w), _rows(rwkv_ln_b))
    gla = (_pad_rows(gla_gate_w2, 0, LANE), _rows(gla_gate_b), _rows(jnp.tile(gla_norm, (1, GLA_HEADS))))
    ssd = (mamba_conv_w.astype(F32), _rows(mamba_conv_b), _pad_lanes(mamba_dt_bias), _pad_lanes(mamba_A_log),
           _rows(jnp.repeat(mamba_D, HEAD_DIM, axis=1)), _rows(mamba_norm))
    post = (w_out.astype(BF16), _rows(ffn2_norm), ffn2_w_gate.astype(BF16), ffn2_w_up.astype(BF16),
            ffn2_w_down.astype(BF16), _rows(ple_norm), ple_w_gate.astype(BF16), ple_w_proj.astype(BF16))
    return pre, rwkv, gla, ssd, post


def _decode_params(nb, rwkv_mu, rwkv_w0, rwkv_w2, rwkv_a0, rwkv_a2, rwkv_g2, rwkv_k_k, rwkv_k_a, rwkv_r_k,
                   rwkv_ln_w, rwkv_ln_b, gla_gate_w2, gla_gate_b, gla_norm, mamba_conv_w, mamba_conv_b,
                   mamba_dt_bias, mamba_A_log, mamba_D, mamba_norm):
    depth = rwkv_mu.shape[0]
    slab = lambda vs: jnp.broadcast_to(jnp.concatenate([v.reshape(depth, -1) for v in vs], axis=1)[:, :, None],
                                       (depth, sum(v[0].size for v in vs), nb)).astype(F32)
    tr = lambda m: jnp.swapaxes(m, 1, 2)
    rwkv = (slab((rwkv_mu, rwkv_w0, rwkv_a0, rwkv_k_k, rwkv_k_a, rwkv_r_k, rwkv_ln_w, rwkv_ln_b)),
            tr(_pad_rows(rwkv_w2, 0, RWKV_LORA)), tr(_pad_rows(rwkv_a2, RWKV_DECAY_LORA, RWKV_LORA)),
            tr(_pad_rows(rwkv_g2, RWKV_DECAY_LORA + RWKV_A_LORA, RWKV_LORA)))
    gla = (slab((gla_gate_b, jnp.tile(gla_norm, (1, GLA_HEADS)))), tr(_pad_rows(gla_gate_w2, 0, LANE)))
    ssd = (slab((mamba_conv_w, mamba_conv_b, mamba_dt_bias, mamba_A_log, jnp.repeat(mamba_D, HEAD_DIM, axis=1),
                 mamba_norm)),)
    return rwkv, gla, ssd


def _sample_layer(x, p, state, params, dparams, layer, final_norm, *, batch, seq, final, earlier):
    pre, _, _, _, post = params
    d_rwkv, d_gla, d_ssd = dparams
    shift0, wkv0, gla0, conv0, ssm0 = state
    old_wkv, old_gla, old_ssm = earlier
    n = batch * seq
    x1, zr_t, zg_t, zm_t = _pre_call(x, layer, *pre, n, steps=seq)
    shift1 = zr_t[seq - 1].T
    xbc = jnp.transpose(zm_t[:, M2_W:M2_W + M2_CONV_DIM, :], (2, 0, 1))
    conv1 = jnp.concatenate([conv0[layer], xbc], axis=1)[:, -(M2_CONV - 1):]
    scr = lambda *shape: pltpu.VMEM(shape, F32)
    yr, wkv1 = _dec_call(_rwkv_dec_kernel, "rwkv7_decode", (zr_t,), wkv0, layer,
                         (jnp.swapaxes(shift0, 1, 2),) + d_rwkv, (_HEAD_ONES(),), layer, RWKV_W,
                         [scr(seq, 6, RWKV_W, batch), scr(seq, RWKV_W, batch), scr(seq, 2, RWKV_W, batch)], old_wkv)
    yg, gla1 = _dec_call(_gla_dec_kernel, "gla_decode", (zg_t,), gla0, layer, d_gla, (_HEAD_ONES(),), layer, GLA_W,
                         [scr(seq, 3, GLA_KEY_W, batch), scr(seq, GLA_W, batch), scr(seq, GLA_W, batch)], old_gla)
    conv_t = jnp.transpose(conv0, (0, 2, 3, 1))
    ym, ssm1 = _dec_call(_ssd_dec_kernel, "ssd_decode", (zm_t,), ssm0, layer, (conv_t,) + d_ssd, (), layer, M2_W,
                         [scr(seq, M2_W, batch), scr(seq, 2 * M2_GROUPS * M2_STATE, batch),
                          scr(seq, 2, ROWS, batch), scr(seq, M2_W, batch)], old_ssm)
    x2 = _post_call(x1, yr, yg, ym, p, layer, *post, final_norm.reshape(1, -1), n, final, steps=seq)
    return x2, (shift1, wkv1, gla1, conv1, ssm1)


def _prompt_layer(x, p, params, layer, final_norm, *, batch, seq, bb, tm, tm_post, final):
    pre, rwkv, gla, ssd, post = params
    assert seq % RWKV_CHUNK == 0 and seq % (GLA_CHUNK * GLA_SUB) == 0 and seq % SSD_CHUNK == 0 and seq >= M2_CONV - 1
    x1, zr, zg, zm = _pre_call(x, layer, *pre, tm)
    zr = zr.reshape(batch, seq, RWKV_PROJ)
    zg = zg.reshape(batch, seq, GLA_PAD)
    zm = zm.reshape(batch, seq, M2_PAD)
    shift1 = zr[:, -1]
    conv1 = zm[:, seq - (M2_CONV - 1):, M2_W:M2_W + M2_CONV_DIM]
    zero = lambda *shape: jnp.zeros((1, batch) + shape, F32)
    yr, wkv1 = _rwkv_call(zr, zero(1, RWKV_PROJ), zero(RWKV_HEADS, HEAD_DIM, HEAD_DIM), 0, rwkv, layer, bb,
                          RWKV_CHUNK, BF16)
    yg, gla1 = _gla_call(zg, zero(GLA_HEADS, GLA_DK, HEAD_DIM), 0, gla, layer, bb, GLA_CHUNK, GLA_SUB, BF16)
    ym, ssm1 = _ssd_call(zm, zero(M2_CONV - 1, M2_CONV_DIM), zero(M2_HEADS, HEAD_DIM, M2_STATE), 0, ssd, layer, bb,
                         SSD_CHUNK, BF16)
    n = batch * seq
    flat = lambda y: y.reshape(n, y.shape[-1])
    x2 = _post_call(x1, flat(yr), flat(yg), flat(ym), p, layer, *post, final_norm.reshape(1, -1), tm_post, final)
    return x2, (shift1, wkv1, gla1, conv1, ssm1)


def kernel(x_prompt, x_sample, p_prompt, p_sample, state_rwkv_shift, state_rwkv_wkv, state_gla, state_mamba_conv, state_mamba_ssm, ffn1_norm, ffn1_w_gate, ffn1_w_up, ffn1_w_down, mix_norm, w_in, rwkv_mu, rwkv_w0, rwkv_w2, rwkv_a0, rwkv_a2, rwkv_g2, rwkv_k_k, rwkv_k_a, rwkv_r_k, rwkv_ln_w, rwkv_ln_b, gla_gate_w2, gla_gate_b, gla_norm, mamba_conv_w, mamba_conv_b, mamba_dt_bias, mamba_A_log, mamba_D, mamba_norm, w_out, ffn2_norm, ffn2_w_gate, ffn2_w_up, ffn2_w_down, ple_norm, ple_w_gate, ple_w_proj, final_norm):
    depth = w_in.shape[0]
    nb, seq, _ = x_prompt.shape
    db, dseq, _ = x_sample.shape
    weights = (ffn1_norm, ffn1_w_gate, ffn1_w_up, ffn1_w_down, mix_norm, w_in, rwkv_mu, rwkv_w0, rwkv_w2, rwkv_a0,
               rwkv_a2, rwkv_g2, rwkv_k_k, rwkv_k_a, rwkv_r_k, rwkv_ln_w, rwkv_ln_b, gla_gate_w2, gla_gate_b,
               gla_norm, mamba_conv_w, mamba_conv_b, mamba_dt_bias, mamba_A_log, mamba_D, mamba_norm, w_out,
               ffn2_norm, ffn2_w_gate, ffn2_w_up, ffn2_w_down, ple_norm, ple_w_gate, ple_w_proj)
    xp = x_prompt.reshape(nb * seq, D_MODEL)
    xs = jnp.swapaxes(x_sample, 0, 1).reshape(dseq * db, D_MODEL)
    pp = p_prompt.reshape(depth, nb * seq, PLE_DIM)
    ps = jnp.swapaxes(p_sample, 1, 2).reshape(depth, dseq * db, PLE_DIM)
    params = _stacked_params(*weights)
    dparams = _decode_params(db, rwkv_mu, rwkv_w0, rwkv_w2, rwkv_a0, rwkv_a2, rwkv_g2, rwkv_k_k, rwkv_k_a, rwkv_r_k,
                             rwkv_ln_w, rwkv_ln_b, gla_gate_w2, gla_gate_b, gla_norm, mamba_conv_w, mamba_conv_b,
                             mamba_dt_bias, mamba_A_log, mamba_D, mamba_norm)
    batch_last = lambda s: jnp.moveaxis(s, 1, -1)
    past = (state_rwkv_shift, batch_last(state_rwkv_wkv), batch_last(state_gla), state_mamba_conv,
            batch_last(state_mamba_ssm))
    p_states, s_states = [], []
    for i in range(depth):
        final = i == depth - 1
        xp, st_p = _prompt_layer(xp, pp, params, i, final_norm, batch=nb, seq=seq, bb=nb, tm=512, tm_post=1024,
                                 final=final)
        earlier = tuple(tuple(s[j] for s in s_states) if final else () for j in (1, 2, 4))
        xs, st_s = _sample_layer(xs, ps, past, params, dparams, i, final_norm, batch=db, seq=dseq, final=final,
                                 earlier=earlier)
        p_states.append(st_p)
        s_states.append(st_s)
    stack = lambda sts, j: jnp.stack([s[j] for s in sts])
    mats = s_states[-1] if depth > 1 else tuple(s[None] for s in s_states[-1])
    s_out = tuple(jnp.moveaxis(mats[j], -1, 1) if j in (1, 2, 4) else stack(s_states, j) for j in range(5))
    return ((xp.reshape(nb, seq, D_MODEL), jnp.swapaxes(xs.reshape(dseq, db, D_MODEL), 0, 1))
            + tuple(stack(p_states, j) for j in range(5)) + s_out)
```

```python
import functools

import jax
import jax.numpy as jnp
from jax import lax
from jax.experimental import pallas as pl
from jax.experimental.pallas import tpu as pltpu

F32 = jnp.float32
BF16 = jnp.bfloat16

D_MODEL = 1024
D_FF = 2816
PLE_DIM = 256
HEAD_DIM = 64
NORM_EPS = 1e-6

RWKV_W = 256
RWKV_HEADS = 4
RWKV_GN_EPS = 64e-5
RWKV_PROJ = 896
RWKV_LORA = 128
RWKV_DECAY_LORA = 32
RWKV_A_LORA = 32
RWKV_DECAY_SCALE = 0.6065306597126334
RWKV_KEY_NORM_FLOOR = 1e-24

GLA_W = 256
GLA_HEADS = 4
GLA_DK = 32
GLA_KEY_W = 128
GLA_GATE_LORA = 16
GLA_TAU = 16.0
GLA_PROJ = 784
GLA_PAD = 896

M2_W = 512
M2_HEADS = 8
M2_STATE = 64
M2_GROUPS = 2
M2_CONV = 4
M2_CONV_DIM = 768
M2_PROJ = 1288
M2_PAD = 1408

LANE = 128
SUBLANE = 8
FF_CHUNK = 256
VMEM_LIMIT = 56 * 1024 * 1024

RWKV_CHUNK = 128
GLA_CHUNK = 64
GLA_SUB = 4
SSD_CHUNK = 128


def _bdot(a, b):
    return jnp.dot(a.astype(BF16), b.astype(BF16), preferred_element_type=F32)


def _bdot_nt(a, b):
    return lax.dot_general(a.astype(BF16), b.astype(BF16), (((1,), (1,)), ((), ())),
                           preferred_element_type=F32)


def _bdot_tn(a, b):
    return lax.dot_general(a.astype(BF16), b.astype(BF16), (((0,), (0,)), ((), ())),
                           preferred_element_type=F32)


def _split3(x):
    hi = x.astype(BF16)
    r1 = x - hi.astype(F32)
    mid = r1.astype(BF16)
    lo = (r1 - mid.astype(F32)).astype(BF16)
    return hi, mid, lo


def _ldot3(mat, x):
    return sum(jnp.dot(mat, p, preferred_element_type=F32) for p in _split3(x))


def _rdot2(x, mat):
    return sum(jnp.dot(p, mat, preferred_element_type=F32) for p in _split3(x)[:2])


def _nt3(mat, x):
    return sum(lax.dot_general(mat, p, (((1,), (1,)), ((), ())), preferred_element_type=F32)
               for p in _split3(x))


def _softplus(x):
    return jnp.maximum(x, 0.0) + jnp.log1p(jnp.exp(-jnp.abs(x)))


def _sigmoid(x):
    return 0.5 * jnp.tanh(0.5 * x) + 0.5


def _silu(x):
    return x * _sigmoid(x)


def _rms(x, w):
    return x * lax.rsqrt(jnp.mean(x * x, axis=-1, keepdims=True) + NORM_EPS) * w


def _const_spec(shape):
    nd = len(shape)
    return pl.BlockSpec(shape, lambda *_: (0,) * nd, pipeline_mode=pl.Buffered(1))


def _layer_spec(a, layer):
    nd = a.ndim - 1
    return pl.BlockSpec((None,) + a.shape[1:], lambda *_: (layer,) + (0,) * nd, pipeline_mode=pl.Buffered(1))


def _ffn(x, nw, wg_ref, wu_ref, wd_ref, act_ref):
    xn = _rms(x, nw).astype(BF16)
    for c in range(D_FF // FF_CHUNK):
        sl = slice(c * FF_CHUNK, (c + 1) * FF_CHUNK)
        hg = jnp.dot(xn, wg_ref[:, sl], preferred_element_type=F32)
        hu = jnp.dot(xn, wu_ref[:, sl], preferred_element_type=F32)
        act_ref[:, sl] = (_silu(hg) * hu).astype(BF16)
    return x + 0.5 * jnp.dot(act_ref[...], wd_ref[...], preferred_element_type=F32)


def _pre_kernel(x_ref, n1_ref, wg_ref, wu_ref, wd_ref, nm_ref, win_ref,
                x1_ref, zr_ref, zg_ref, zm_ref, act_ref, *, steps):
    x1 = _ffn(x_ref[...], n1_ref[...], wg_ref, wu_ref, wd_ref, act_ref)
    x1_ref[...] = x1
    h = _rms(x1, nm_ref[...]).astype(BF16)
    cols = (slice(0, RWKV_PROJ), slice(RWKV_PROJ, RWKV_PROJ + GLA_PAD), slice(RWKV_PROJ + GLA_PAD, None))
    for ref, cs in zip((zr_ref, zg_ref, zm_ref), cols):
        z = jnp.dot(h, win_ref[:, cs], preferred_element_type=F32)
        if steps:
            nb = z.shape[0] // steps
            for t in range(steps):
                ref[t] = z[t * nb:(t + 1) * nb].T
        else:
            ref[...] = z


def _pre_call(x, layer, n1, wg, wu, wd, nm, win, tm, steps=0):
    n = x.shape[0]
    tok = lambda w: pl.BlockSpec((tm, w), lambda i: (i, 0))
    if steps:
        assert n == tm and n % steps == 0
        z_spec = lambda w: pl.BlockSpec((steps, w, n // steps), lambda i: (0, 0, 0))
        z_shape = lambda w: jax.ShapeDtypeStruct((steps, w, n // steps), F32)
    else:
        z_spec = tok
        z_shape = lambda w: jax.ShapeDtypeStruct((n, w), F32)
    return pl.pallas_call(
        functools.partial(_pre_kernel, steps=steps),
        grid=(n // tm,),
        in_specs=[tok(D_MODEL)] + [_layer_spec(a, layer) for a in (n1, wg, wu, wd, nm, win)],
        out_specs=[tok(D_MODEL), z_spec(RWKV_PROJ), z_spec(GLA_PAD), z_spec(M2_PAD)],
        out_shape=[jax.ShapeDtypeStruct((n, D_MODEL), F32), z_shape(RWKV_PROJ), z_shape(GLA_PAD), z_shape(M2_PAD)],
        scratch_shapes=[pltpu.VMEM((tm, D_FF), BF16)],
        compiler_params=pltpu.CompilerParams(dimension_semantics=("parallel",), vmem_limit_bytes=VMEM_LIMIT),
        name="pre_ffn_inproj",
    )(x, n1, wg, wu, wd, nm, win)


def _post_kernel(x_ref, yr_ref, yg_ref, ym_ref, p_ref, wo_ref, n2_ref, wg_ref, wu_ref, wd_ref,
                 np_ref, pg_ref, pp_ref, nf_ref, o_ref, act_ref, *, final, steps):
    rows = lambda ref: (jnp.concatenate([ref[t].T for t in range(steps)], axis=0) if steps else ref[...])
    x = x_ref[...]
    x = x + jnp.dot(rows(yr_ref).astype(BF16), wo_ref[0:RWKV_W, :], preferred_element_type=F32)
    x = x + jnp.dot(rows(yg_ref).astype(BF16), wo_ref[RWKV_W:RWKV_W + GLA_W, :], preferred_element_type=F32)
    x = x + jnp.dot(rows(ym_ref).astype(BF16), wo_ref[RWKV_W + GLA_W:, :], preferred_element_type=F32)
    x = _ffn(x, n2_ref[...], wg_ref, wu_ref, wd_ref, act_ref)
    gate = _sigmoid(jnp.dot(_rms(x, np_ref[...]).astype(BF16), pg_ref[...], preferred_element_type=F32))
    x = x + gate * jnp.dot(p_ref[...].astype(BF16), pp_ref[...], preferred_element_type=F32)
    if final:
        x = _rms(x, nf_ref[...])
    o_ref[...] = x


def _post_call(x, yr, yg, ym, p, layer, wo, n2, wg, wu, wd, npn, pg, pp, nf, tm, final, steps=0):
    n = x.shape[0]
    tok = lambda w: pl.BlockSpec((tm, w), lambda i: (i, 0))
    if steps:
        assert n == tm and n % steps == 0
        y_spec = lambda w: pl.BlockSpec((steps, w, n // steps), lambda i: (0, 0, 0))
    else:
        y_spec = tok
    params = (wo, n2, wg, wu, wd, npn, pg, pp)
    consts = params + (nf,)
    return pl.pallas_call(
        functools.partial(_post_kernel, final=final, steps=steps),
        grid=(n // tm,),
        in_specs=[tok(D_MODEL), y_spec(RWKV_W), y_spec(GLA_W), y_spec(M2_W),
                  pl.BlockSpec((None, tm, PLE_DIM), lambda i: (layer, i, 0))]
                 + [_layer_spec(a, layer) for a in params] + [_const_spec(nf.shape)],
        out_specs=tok(D_MODEL),
        out_shape=jax.ShapeDtypeStruct((n, D_MODEL), F32),
        scratch_shapes=[pltpu.VMEM((tm, D_FF), BF16)],
        compiler_params=pltpu.CompilerParams(dimension_semantics=("parallel",), vmem_limit_bytes=VMEM_LIMIT),
        name="post_outproj_ffn_ple",
    )(x, yr, yg, ym, p, *consts)


def _tri_masks(n):
    ri = lax.broadcasted_iota(jnp.int32, (n, n), 0)
    ci = lax.broadcasted_iota(jnp.int32, (n, n), 1)
    return ri >= ci, ri > ci, (ri == ci).astype(F32)


def _rwkv_kernel(z_ref, sh_ref, s0_ref, mu_ref, w0_ref, w2_ref, a0_ref, a2_ref, g2_ref, kk_ref, ka_ref,
                 rk_ref, lnw_ref, lnb_ref, tri_ref, ones_ref,
                 y_ref, s_ref, zbuf_ref, ybuf_ref, *, bb, chunk):
    c = pl.program_id(1)

    @pl.when(c == 0)
    def _():
        s_ref[...] = s0_ref[...]
        zbuf_ref[:, SUBLANE - 1:SUBLANE, :] = sh_ref[...]

    incl, strict, eye = _tri_masks(chunk)
    tri = tri_ref[...]
    ones = ones_ref[...]
    mu = mu_ref[...]
    seqs = range(bb)
    zs = []
    first = lax.broadcasted_iota(jnp.int32, (chunk, 1), 0) == 0
    for i in seqs:
        z = z_ref[i]
        prev = jnp.where(first, zbuf_ref[i, SUBLANE - 1:SUBLANE, :], pltpu.roll(z, 1, 0))
        zbuf_ref[i, SUBLANE - 1:SUBLANE, :] = z[chunk - 1:chunk, :]
        zs.append(z + mu * (prev - z))
    r = [x[:, 0:RWKV_W] for x in zs]
    k = [x[:, RWKV_W:2 * RWKV_W] for x in zs]
    v = [x[:, 2 * RWKV_W:3 * RWKV_W] for x in zs]
    lora = [x[:, 3 * RWKV_W:] for x in zs]
    w_lin = [_bdot(jnp.tanh(x), w2_ref[...]) for x in lora]
    a_lin = [_bdot(x, a2_ref[...]) for x in lora]
    gate = [_bdot(_sigmoid(x), g2_ref[...]) for x in lora]
    kk = [x * kk_ref[...] for x in k]
    kk_ss = [_rdot2(x * x, ones) for x in kk]
    logw = [-RWKV_DECAY_SCALE * _sigmoid(w0_ref[...] + x) for x in w_lin]
    a = [_sigmoid(a0_ref[...] + x) for x in a_lin]
    kk = [x * lax.rsqrt(jnp.maximum(ss, RWKV_KEY_NORM_FLOOR)) for x, ss in zip(kk, kk_ss)]
    k = [x * (1.0 + (ai - 1.0) * ka_ref[...]) for x, ai in zip(k, a)]
    ahat = [-x for x in kk]
    bhat = [x * ai for x, ai in zip(kk, a)]
    g = [_ldot3(tri, x) for x in logw]
    glast = [x[chunk - 1:chunk, :] for x in g]
    e_ng = [jnp.exp(-x) for x in g]
    e_gl = [jnp.exp(gl - x) for gl, x in zip(glast, g)]
    a_l = [ah * jnp.exp(x - lw) for ah, x, lw in zip(ahat, g, logw)]
    r_l = [ri * jnp.exp(x) for ri, x in zip(r, g)]
    b_r = [x * e for x, e in zip(bhat, e_ng)]
    k_r = [x * e for x, e in zip(k, e_ng)]
    k_p = [x * e for x, e in zip(k, e_gl)]
    b_p = [x * e for x, e in zip(bhat, e_gl)]
    dec = [jnp.exp(x) for x in glast]

    probs = [(i, h) for i in seqs for h in range(RWKV_HEADS)]
    hd = lambda xs: [xs[i][:, h * HEAD_DIM:(h + 1) * HEAD_DIM] for i, h in probs]
    a_lh, r_lh, b_rh, k_rh, k_ph, b_ph, vh = hd(a_l), hd(r_l), hd(b_r), hd(k_r), hd(k_p), hd(b_p), hd(v)
    ar = [jnp.concatenate([x, y], axis=0) for x, y in zip(a_lh, r_lh)]
    bk = [jnp.concatenate([x, y], axis=0) for x, y in zip(b_rh, k_rh)]
    gm = [_bdot_nt(x, y) for x, y in zip(ar, bk)]
    a_ab = [jnp.where(strict, x[:chunk, :chunk], 0.0) for x in gm]
    a_ak = [jnp.where(strict, x[:chunk, chunk:], 0.0) for x in gm]
    a_rb = [jnp.where(incl, x[chunk:, :chunk], 0.0) for x in gm]
    a_rk = [jnp.where(incl, x[chunk:, chunk:], 0.0) for x in gm]
    t = [eye + m for m in a_ab]
    x = [_bdot(m, m) for m in a_ab]
    p = 2
    while p < chunk:
        if 2 * p < chunk:
            xt = [_bdot(jnp.concatenate([xi, ti], axis=0), xi) for xi, ti in zip(x, t)]
            x = [m[:chunk] for m in xt]
            t = [ti + m[chunk:] for ti, m in zip(t, xt)]
        else:
            t = [ti + _bdot(ti, xi) for xi, ti in zip(x, t)]
        p *= 2
    akv = [_bdot(m, vi) for m, vi in zip(a_ak, vh)]
    w_t = [_bdot(ti, m) for ti, m in zip(t, a_lh)]
    u_t = [_bdot(ti, m) for ti, m in zip(t, akv)]
    r_t = [ri + _bdot(m, wi) for ri, m, wi in zip(r_lh, a_rb, w_t)]
    y_t = [_bdot(m, vi) + _bdot(n, ui) for m, vi, n, ui in zip(a_rk, vh, a_rb, u_t)]
    m_s = [_bdot_tn(wi, bi) for wi, bi in zip(w_t, b_ph)]
    q_s = [_bdot_tn(jnp.concatenate([vi, ui], axis=0), jnp.concatenate([ki, bi], axis=0))
           for vi, ui, ki, bi in zip(vh, u_t, k_ph, b_ph)]
    s_old = [s_ref[i, h] for i, h in probs]
    y_h = [_bdot_nt(ri, si) + yi for ri, si, yi in zip(r_t, s_old, y_t)]
    s_m = [_bdot(si, mi) for si, mi in zip(s_old, m_s)]
    for j, (i, h) in enumerate(probs):
        hs = slice(h * HEAD_DIM, (h + 1) * HEAD_DIM)
        ybuf_ref[i, :, hs] = y_h[j]
        s_ref[i, h] = s_old[j] * dec[i][:, hs] + s_m[j] + q_s[j]
    y = [ybuf_ref[i] for i in seqs]
    mean = [_rdot2(x, ones) * (1.0 / HEAD_DIM) for x in y]
    yc = [x - m for x, m in zip(y, mean)]
    var = [_rdot2(x * x, ones) * (1.0 / HEAD_DIM) for x in yc]
    bonus = [_rdot2(ri * ki * rk_ref[...], ones) * vi for ri, ki, vi in zip(r, k, v)]
    for i in seqs:
        yn = yc[i] * lax.rsqrt(var[i] + RWKV_GN_EPS) * lnw_ref[...] + lnb_ref[...]
        y_ref[i] = ((yn + bonus[i]) * gate[i]).astype(y_ref.dtype)


def _state_spec(a, bb, sl):
    nd = a.ndim - 2
    return pl.BlockSpec((None, bb) + a.shape[2:], lambda i, c: (sl, i) + (0,) * nd)


def _rwkv_call(z, shift, s0, sl, params, layer, bb, chunk, out_dtype):
    b, l, _ = z.shape
    grid = (b // bb, l // chunk)
    tri = jnp.tril(jnp.ones((chunk, chunk), F32)).astype(BF16)
    consts = (tri, _HEAD_ONES())
    return pl.pallas_call(
        functools.partial(_rwkv_kernel, bb=bb, chunk=chunk),
        grid=grid,
        in_specs=[pl.BlockSpec((bb, chunk, RWKV_PROJ), lambda i, c: (i, c, 0)),
                  _state_spec(shift, bb, sl), _state_spec(s0, bb, sl)]
                 + [_layer_spec(a, layer) for a in params] + [_const_spec(a.shape) for a in consts],
        out_specs=[pl.BlockSpec((bb, chunk, RWKV_W), lambda i, c: (i, c, 0)),
                   pl.BlockSpec((bb, RWKV_HEADS, HEAD_DIM, HEAD_DIM), lambda i, c: (i, 0, 0, 0))],
        out_shape=[jax.ShapeDtypeStruct((b, l, RWKV_W), out_dtype),
                   jax.ShapeDtypeStruct((b, RWKV_HEADS, HEAD_DIM, HEAD_DIM), F32)],
        scratch_shapes=[pltpu.VMEM((bb, SUBLANE + chunk, RWKV_PROJ), F32),
                        pltpu.VMEM((bb, chunk, RWKV_W), F32)],
        compiler_params=pltpu.CompilerParams(dimension_semantics=("arbitrary", "arbitrary"),
                                             vmem_limit_bytes=VMEM_LIMIT),
        name="rwkv7_mixer",
    )(z, shift, s0, *params, *consts)


def _gla_kernel(z_ref, s0_ref, gw_ref, gb_ref, nw_ref, tri_ref, ones_ref, eye_ref,
                y_ref, s_ref, obuf_ref, *, bb, chunk, sub):
    c = pl.program_id(1)

    @pl.when(c == 0)
    def _():
        s_ref[...] = s0_ref[...]

    incl, _, _ = _tri_masks(chunk)
    tri = tri_ref[...]
    ones = ones_ref[...]
    units = [(i, u) for u in range(sub) for i in range(bb)]
    seqs = range(len(units))
    z = [z_ref[i, u * chunk:(u + 1) * chunk, :] for i, u in units]
    q = [x[:, 0:GLA_KEY_W] * (GLA_DK ** -0.5) for x in z]
    k = [x[:, GLA_KEY_W:2 * GLA_KEY_W] for x in z]
    v = [x[:, 2 * GLA_KEY_W:2 * GLA_KEY_W + GLA_W] for x in z]
    og = [x[:, 2 * GLA_KEY_W + GLA_W:2 * GLA_KEY_W + 2 * GLA_W] for x in z]
    lin = [_bdot(x[:, 2 * GLA_KEY_W + 2 * GLA_W:], gw_ref[...]) for x in z]
    log_a = [-_softplus(-(x + gb_ref[...])) * (1.0 / GLA_TAU) for x in lin]
    b = [_ldot3(tri, x) for x in log_a]
    blast = [x[chunk - 1:chunk, :] for x in b]
    qe = [x * jnp.exp(bi) for x, bi in zip(q, b)]
    ke = [x * jnp.exp(-bi) for x, bi in zip(k, b)]
    kl = [x * jnp.exp(bl - bi) for x, bl, bi in zip(k, blast, b)]
    dec = [_nt3(eye_ref[...], jnp.broadcast_to(jnp.exp(bl), (HEAD_DIM, GLA_KEY_W))) for bl in blast]

    probs = [(i, h) for i in seqs for h in range(GLA_HEADS)]
    kd = lambda xs: [xs[i][:, h * GLA_DK:(h + 1) * GLA_DK] for i, h in probs]
    qeh, keh, klh = kd(qe), kd(ke), kd(kl)
    vh = [v[i][:, h * HEAD_DIM:(h + 1) * HEAD_DIM] for i, h in probs]
    att = [jnp.where(incl, _bdot_nt(x, y), 0.0) for x, y in zip(qeh, keh)]
    intra = [_bdot(ai, vi) for ai, vi in zip(att, vh)]
    kv = [_bdot_tn(ki, vi) for ki, vi in zip(klh, vh)]
    state = {(i, h): s_ref[i, h] for i in range(bb) for h in range(GLA_HEADS)}
    for u in range(sub):
        todo = [p for p, (j, h) in enumerate(probs) if units[j][1] == u]
        inter = [_bdot(qeh[p], state[units[probs[p][0]][0], probs[p][1]]) for p in todo]
        for p, x in zip(todo, inter):
            j, h = probs[p]
            i = units[j][0]
            obuf_ref[i, u * chunk:(u + 1) * chunk, h * HEAD_DIM:(h + 1) * HEAD_DIM] = intra[p] + x
            state[i, h] = state[i, h] * dec[j][h * GLA_DK:(h + 1) * GLA_DK, :] + kv[p]
    for (i, h), s in state.items():
        s_ref[i, h] = s
    rows = [slice(u * chunk, (u + 1) * chunk) for _, u in units]
    o = [obuf_ref[i, r, :] for (i, _), r in zip(units, rows)]
    ms = [_rdot2(x * x, ones) * (1.0 / HEAD_DIM) for x in o]
    for j, ((i, _), r) in enumerate(zip(units, rows)):
        y_ref[i, r, :] = (o[j] * lax.rsqrt(ms[j] + NORM_EPS) * nw_ref[...] * _silu(og[j])).astype(y_ref.dtype)


def _gla_call(z, s0, sl, params, layer, bb, chunk, sub, out_dtype):
    b, l, _ = z.shape
    step = chunk * sub
    tri = jnp.tril(jnp.ones((chunk, chunk), F32)).astype(BF16)
    consts = (tri, _HEAD_ONES(), jnp.eye(LANE, dtype=BF16))
    return pl.pallas_call(
        functools.partial(_gla_kernel, bb=bb, chunk=chunk, sub=sub),
        grid=(b // bb, l // step),
        in_specs=[pl.BlockSpec((bb, step, GLA_PAD), lambda i, c: (i, c, 0)), _state_spec(s0, bb, sl)]
                 + [_layer_spec(a, layer) for a in params] + [_const_spec(a.shape) for a in consts],
        out_specs=[pl.BlockSpec((bb, step, GLA_W), lambda i, c: (i, c, 0)),
                   pl.BlockSpec((bb, GLA_HEADS, GLA_DK, HEAD_DIM), lambda i, c: (i, 0, 0, 0))],
        out_shape=[jax.ShapeDtypeStruct((b, l, GLA_W), out_dtype),
                   jax.ShapeDtypeStruct((b, GLA_HEADS, GLA_DK, HEAD_DIM), F32)],
        scratch_shapes=[pltpu.VMEM((bb, step, GLA_W), F32)],
        compiler_params=pltpu.CompilerParams(dimension_semantics=("arbitrary", "arbitrary"),
                                             vmem_limit_bytes=VMEM_LIMIT),
        name="gla_mixer",
    )(z, s0, *params, *consts)


def _ssd_kernel(z_ref, cp_ref, s0_ref, cw_ref, cb_ref, dtb_ref, alog_ref, dsk_ref, nw_ref, tri_ref, eye_ref,
                esel_ref, y_ref, s_ref, xbuf_ref, ybuf_ref, *, bb, chunk):
    c = pl.program_id(1)

    @pl.when(c == 0)
    def _():
        s_ref[...] = s0_ref[...]
        xbuf_ref[:, SUBLANE - (M2_CONV - 1):SUBLANE, :] = cp_ref[...]

    incl, _, _ = _tri_masks(chunk)
    tri = tri_ref[...]
    lane = lax.broadcasted_iota(jnp.int32, (1, LANE), 1)
    rep = M2_HEADS // M2_GROUPS
    seqs = range(bb)
    zg, xc, dt = [], [], []
    live = lane < M2_HEADS
    for i in seqs:
        z = z_ref[i]
        x = z[:, M2_W:M2_W + M2_CONV_DIM]
        xbuf_ref[i, SUBLANE:SUBLANE + chunk, :] = x
        conv = cb_ref[...] + x * cw_ref[M2_CONV - 1:M2_CONV, :]
        for j in range(1, M2_CONV):
            conv = conv + xbuf_ref[i, SUBLANE - j:SUBLANE - j + chunk, :] * cw_ref[M2_CONV - 1 - j:M2_CONV - j, :]
        xbuf_ref[i, 0:SUBLANE, :] = xbuf_ref[i, chunk:chunk + SUBLANE, :]
        zg.append(z[:, 0:M2_W])
        xc.append(_silu(conv))
        dt.append(jnp.where(live, _softplus(z[:, M2_W + M2_CONV_DIM:] + dtb_ref[...]), 0.0))
    a_neg = -jnp.exp(alog_ref[...])
    cum = [_ldot3(tri, x * a_neg) for x in dt]
    cum_t = [_nt3(eye_ref[0:SUBLANE, :], x) for x in cum]
    dt_b = [_rdot2(x, esel_ref[...]) for x in dt]
    grp = [(i, g) for i in seqs for g in range(M2_GROUPS)]
    bg = [xc[i][:, M2_W + g * M2_STATE:M2_W + (g + 1) * M2_STATE] for i, g in grp]
    cg = [xc[i][:, M2_W + (M2_GROUPS + g) * M2_STATE:M2_W + (M2_GROUPS + g + 1) * M2_STATE] for i, g in grp]
    cb = [_bdot_nt(x, y) for x, y in zip(cg, bg)]
    s_grp = [jnp.concatenate([s_ref[i, g * rep + hh] for hh in range(rep)], axis=0) for i, g in grp]
    y_st = [_bdot_nt(x, s) for x, s in zip(cg, s_grp)]
    probs = [(i, h) for i in seqs for h in range(M2_HEADS)]
    ccb = [jnp.broadcast_to(cum[i][:, h:h + 1], (chunk, LANE)) for i, h in probs]
    low = lane < HEAD_DIM
    cum_b = [jnp.concatenate([jnp.where(low, ccb[i * M2_HEADS + 2 * j], ccb[i * M2_HEADS + 2 * j + 1])
                              for j in range(M2_HEADS // 2)], axis=1) for i in seqs]
    xs = [x[:, 0:M2_W] for x in xc]
    xdt = [x * d for x, d in zip(xs, dt_b)]
    xw = [x * jnp.exp(cb_[chunk - 1:chunk, :] - cb_) for x, cb_ in zip(xdt, cum_b)]
    seg = [jnp.where(incl, jnp.exp(jnp.minimum(ccb[j][:, 0:chunk] - cum_t[i][h:h + 1, :], 0.0)), 0.0)
           for j, (i, h) in enumerate(probs)]
    y_in = [_bdot(cb[i * M2_GROUPS + h // rep] * seg[j], xdt[i][:, h * HEAD_DIM:(h + 1) * HEAD_DIM])
            for j, (i, h) in enumerate(probs)]
    s_in = [_bdot_tn(xw[i][:, g * rep * HEAD_DIM:(g + 1) * rep * HEAD_DIM], bg[j]) for j, (i, g) in enumerate(grp)]
    for j, (i, h) in enumerate(probs):
        ybuf_ref[i, :, h * HEAD_DIM:(h + 1) * HEAD_DIM] = y_in[j]
        gj = i * M2_GROUPS + h // rep
        hh = h % rep
        s_ref[i, h] = (s_grp[gj][hh * HEAD_DIM:(hh + 1) * HEAD_DIM] * jnp.exp(cum[i][chunk - 1:chunk, h:h + 1])
                       + s_in[gj][hh * HEAD_DIM:(hh + 1) * HEAD_DIM])
    for i in seqs:
        y_state = jnp.concatenate([y_st[i * M2_GROUPS + g] for g in range(M2_GROUPS)], axis=1)
        y = ybuf_ref[i] + y_state * jnp.exp(cum_b[i]) + dsk_ref[...] * xs[i]
        y_ref[i] = _rms(y * _silu(zg[i]), nw_ref[...]).astype(y_ref.dtype)


def _ssd_call(z, conv_prev, s0, sl, params, layer, bb, chunk, out_dtype):
    b, l, _ = z.shape
    tri = jnp.tril(jnp.ones((chunk, chunk), F32)).astype(BF16)
    esel = (jnp.arange(LANE)[:, None] == jnp.arange(M2_W)[None, :] // HEAD_DIM).astype(BF16)
    consts = (tri, jnp.eye(LANE, dtype=BF16), esel)
    return pl.pallas_call(
        functools.partial(_ssd_kernel, bb=bb, chunk=chunk),
        grid=(b // bb, l // chunk),
        in_specs=[pl.BlockSpec((bb, chunk, M2_PAD), lambda i, c: (i, c, 0)),
                  _state_spec(conv_prev, bb, sl), _state_spec(s0, bb, sl)]
                 + [_layer_spec(a, layer) for a in params] + [_const_spec(a.shape) for a in consts],
        out_specs=[pl.BlockSpec((bb, chunk, M2_W), lambda i, c: (i, c, 0)),
                   pl.BlockSpec((bb, M2_HEADS, HEAD_DIM, M2_STATE), lambda i, c: (i, 0, 0, 0))],
        out_shape=[jax.ShapeDtypeStruct((b, l, M2_W), out_dtype),
                   jax.ShapeDtypeStruct((b, M2_HEADS, HEAD_DIM, M2_STATE), F32)],
        scratch_shapes=[pltpu.VMEM((bb, SUBLANE + chunk, M2_CONV_DIM), F32),
                        pltpu.VMEM((bb, chunk, M2_W), F32)],
        compiler_params=pltpu.CompilerParams(dimension_semantics=("arbitrary", "arbitrary"),
                                             vmem_limit_bytes=VMEM_LIMIT),
        name="ssd_mixer",
    )(z, conv_prev, s0, *params, *consts)


def _HEAD_ONES():
    idx = jnp.arange(RWKV_W) // HEAD_DIM
    return (idx[:, None] == idx[None, :]).astype(BF16)


ROWS = SUBLANE


def _rows_to_tile(rows):
    rid = lax.broadcasted_iota(jnp.int32, (ROWS, 1), 0)
    out = rows[0]
    for j in range(1, ROWS):
        out = jnp.where(rid == j, rows[j], out)
    return out


def _rwkv_dec_kernel(z_ref, s_ref, sh_ref, cols_ref, w2_ref, a2_ref, g2_ref, ones_ref,
                     y_ref, so_ref, vec_ref, ybuf_ref, post_ref):
    h = pl.program_id(0)
    nt = z_ref.shape[0]
    w = RWKV_W
    ones = ones_ref[...]
    col = lambda j: cols_ref[RWKV_PROJ + j * w:RWKV_PROJ + (j + 1) * w, :]

    @pl.when(h == 0)
    def _():
        mu = cols_ref[0:RWKV_PROJ, :]
        prev = sh_ref[...]
        for t in range(nt):
            z = z_ref[t]
            zs = z + mu * (prev - z)
            prev = z
            r, k, v, lora = zs[0:w], zs[w:2 * w], zs[2 * w:3 * w], zs[3 * w:]
            logw = -RWKV_DECAY_SCALE * _sigmoid(col(0) + _bdot(w2_ref[...], jnp.tanh(lora)))
            a = _sigmoid(col(1) + _bdot(a2_ref[...], lora))
            kk = k * col(2)
            kk = kk * lax.rsqrt(jnp.maximum(_ldot3(ones, kk * kk), RWKV_KEY_NORM_FLOOR))
            k = k * (1.0 + (a - 1.0) * col(3))
            for j, x in enumerate((-kk, jnp.exp(logw), kk * a, k, r, v)):
                vec_ref[t, j] = x
            post_ref[t, 0] = _bdot(g2_ref[...], _sigmoid(lora))
            post_ref[t, 1] = _ldot3(ones, r * k * col(4)) * v

    hs = pl.ds(pl.multiple_of(h * HEAD_DIM, HEAD_DIM), HEAD_DIM)

    def body(v8, carry):
        r0 = pl.multiple_of(h * HEAD_DIM + v8 * ROWS, ROWS)
        vt = [vec_ref[t, 5, pl.ds(r0, ROWS), :] for t in range(nt)]
        ys = [[] for _ in range(nt)]
        for j in range(ROWS):
            s = s_ref[0, v8 * ROWS + j]
            for t in range(nt):
                sa = jnp.sum(s * vec_ref[t, 0, hs, :], axis=0, keepdims=True)
                s = s * vec_ref[t, 1, hs, :] + sa * vec_ref[t, 2, hs, :] + vt[t][j:j + 1, :] * vec_ref[t, 3, hs, :]
                ys[t].append(jnp.sum(s * vec_ref[t, 4, hs, :], axis=0, keepdims=True))
            so_ref[0, v8 * ROWS + j] = s
        for t in range(nt):
            ybuf_ref[t, pl.ds(r0, ROWS), :] = _rows_to_tile(ys[t])
        return carry

    lax.fori_loop(0, HEAD_DIM // ROWS, body, 0)

    @pl.when(h == pl.num_programs(0) - 1)
    def _():
        for t in range(nt):
            y = ybuf_ref[t]
            yc = y - _ldot3(ones, y) * (1.0 / HEAD_DIM)
            var = _ldot3(ones, yc * yc) * (1.0 / HEAD_DIM)
            yn = yc * lax.rsqrt(var + RWKV_GN_EPS) * col(5) + col(6)
            y_ref[t] = (yn + post_ref[t, 1]) * post_ref[t, 0]


def _gla_dec_kernel(z_ref, s_ref, cols_ref, gw_ref, ones_ref, y_ref, so_ref, vec_ref, val_ref, obuf_ref):
    h = pl.program_id(0)
    nt = z_ref.shape[0]

    @pl.when(h == 0)
    def _():
        for t in range(nt):
            z = z_ref[t]
            lin = _bdot(gw_ref[...], z[2 * GLA_KEY_W + 2 * GLA_W:]) + cols_ref[0:GLA_KEY_W, :]
            vec_ref[t, 0] = z[0:GLA_KEY_W] * (GLA_DK ** -0.5)
            vec_ref[t, 1] = z[GLA_KEY_W:2 * GLA_KEY_W]
            vec_ref[t, 2] = jnp.exp(-_softplus(-lin) * (1.0 / GLA_TAU))
            val_ref[t] = z[2 * GLA_KEY_W:2 * GLA_KEY_W + GLA_W]

    vs = pl.ds(pl.multiple_of(h * HEAD_DIM, HEAD_DIM), HEAD_DIM)
    v = [val_ref[t, vs, :] for t in range(nt)]

    def body(k8, acc):
        r0 = pl.multiple_of(h * GLA_DK + k8 * ROWS, ROWS)
        q, k, a = ([vec_ref[t, j, pl.ds(r0, ROWS), :] for t in range(nt)] for j in range(3))
        acc = list(acc)
        for j in range(ROWS):
            s = s_ref[0, k8 * ROWS + j]
            for t in range(nt):
                s = s * a[t][j:j + 1, :] + k[t][j:j + 1, :] * v[t]
                acc[t] = acc[t] + q[t][j:j + 1, :] * s
            so_ref[0, k8 * ROWS + j] = s
        return tuple(acc)

    zero = jnp.zeros((HEAD_DIM, z_ref.shape[2]), F32)
    acc = lax.fori_loop(0, GLA_DK // ROWS, body, (zero,) * nt)
    for t in range(nt):
        obuf_ref[t, vs, :] = acc[t]

    @pl.when(h == pl.num_programs(0) - 1)
    def _():
        for t in range(nt):
            o = obuf_ref[t]
            ms = _ldot3(ones_ref[...], o * o) * (1.0 / HEAD_DIM)
            og = z_ref[t, 2 * GLA_KEY_W + GLA_W:2 * GLA_KEY_W + 2 * GLA_W, :]
            y_ref[t] = o * lax.rsqrt(ms + NORM_EPS) * cols_ref[GLA_KEY_W:, :] * _silu(og)


def _ssd_dec_kernel(z_ref, s_ref, cp_ref, cols_ref, y_ref, so_ref, x_ref, bc_ref, dt_ref, ybuf_ref):
    h = pl.program_id(0)
    nt = z_ref.shape[0]
    cd = M2_CONV_DIM
    off_b = M2_CONV * cd
    off_dt = off_b + cd

    @pl.when(h == 0)
    def _():
        xs = [cp_ref[j] for j in range(M2_CONV - 1)] + [z_ref[t, M2_W:M2_W + cd, :] for t in range(nt)]
        a_neg = -jnp.exp(cols_ref[off_dt + ROWS:off_dt + 2 * ROWS, :])
        for t in range(nt):
            conv = cols_ref[off_b:off_b + cd, :]
            for j in range(M2_CONV):
                conv = conv + xs[t + j] * cols_ref[j * cd:(j + 1) * cd, :]
            xc = _silu(conv)
            x_ref[t] = xc[0:M2_W]
            bc_ref[t] = xc[M2_W:]
            dt = _softplus(z_ref[t, M2_W + cd:M2_W + cd + ROWS, :] + cols_ref[off_dt:off_dt + ROWS, :])
            dt_ref[t, 0] = dt
            dt_ref[t, 1] = jnp.exp(dt * a_neg)

    g = h // (M2_HEADS // M2_GROUPS)
    bs = pl.ds(pl.multiple_of(g * M2_STATE, M2_STATE), M2_STATE)
    cs = pl.ds(pl.multiple_of((M2_GROUPS + g) * M2_STATE, M2_STATE), M2_STATE)
    rid = lax.broadcasted_iota(jnp.int32, (ROWS, 1), 0)
    pick = lambda tile: jnp.sum(jnp.where(rid == h, tile, 0.0), axis=0, keepdims=True)
    dt = [pick(dt_ref[t, 0]) for t in range(nt)]
    da = [pick(dt_ref[t, 1]) for t in range(nt)]

    def body(p8, carry):
        r0 = pl.multiple_of(h * HEAD_DIM + p8 * ROWS, ROWS)
        xt = [x_ref[t, pl.ds(r0, ROWS), :] * dt[t] for t in range(nt)]
        ys = [[] for _ in range(nt)]
        for j in range(ROWS):
            s = s_ref[0, p8 * ROWS + j]
            for t in range(nt):
                s = s * da[t] + xt[t][j:j + 1, :] * bc_ref[t, bs, :]
                ys[t].append(jnp.sum(s * bc_ref[t, cs, :], axis=0, keepdims=True))
            so_ref[0, p8 * ROWS + j] = s
        for t in range(nt):
            ybuf_ref[t, pl.ds(r0, ROWS), :] = _rows_to_tile(ys[t])
        return carry

    lax.fori_loop(0, HEAD_DIM // ROWS, body, 0)

    @pl.when(h == pl.num_programs(0) - 1)
    def _():
        off_d = off_dt + 2 * ROWS
        for t in range(nt):
            y = ybuf_ref[t] + cols_ref[off_d:off_d + M2_W, :] * x_ref[t]
            y = y * _silu(z_ref[t, 0:M2_W, :])
            ms = jnp.mean(y * y, axis=0, keepdims=True)
            y_ref[t] = y * lax.rsqrt(ms + NORM_EPS) * cols_ref[off_d + M2_W:off_d + 2 * M2_W, :]


def _dec_call(kern, name, acts, state, sl, layer_params, consts, layer, y_width, scratch, earlier=()):
    nt, _, nb = acts[0].shape
    heads = state.shape[1]
    whole = lambda a: pl.BlockSpec(a.shape, lambda hd: (0,) * a.ndim, pipeline_mode=pl.Buffered(1))
    blk = (1,) + state.shape[2:]
    n_in = len(acts) + 1 + len(layer_params) + len(consts)
    n_old = len(earlier)

    def body(*refs):
        ins, old, (y_ref, so_ref), scr = (refs[:n_in], refs[n_in:n_in + n_old], refs[n_in + n_old:n_in + n_old + 2],
                                          refs[n_in + n_old + 2:])
        for j, ref in enumerate(old):
            so_ref[j] = ref[...]
        kern(*ins, y_ref, so_ref.at[n_old] if n_old else so_ref, *scr)

    if n_old:
        so_spec = pl.BlockSpec((n_old + 1,) + blk, lambda hd: (0, hd, 0, 0, 0))
        so_shape = (n_old + 1,) + state.shape[1:]
    else:
        so_spec = pl.BlockSpec(blk, lambda hd: (hd, 0, 0, 0))
        so_shape = state.shape[1:]
    return pl.pallas_call(
        body,
        grid=(heads,),
        in_specs=[whole(a) for a in acts]
                 + [pl.BlockSpec((None,) + blk, lambda hd: (sl, hd, 0, 0, 0))]
                 + [_layer_spec(a, layer) for a in layer_params] + [_const_spec(a.shape) for a in consts]
                 + [pl.BlockSpec(blk, lambda hd: (hd, 0, 0, 0)) for _ in earlier],
        out_specs=[pl.BlockSpec((nt, y_width, nb), lambda hd: (0, 0, 0)), so_spec],
        out_shape=[jax.ShapeDtypeStruct((nt, y_width, nb), F32), jax.ShapeDtypeStruct(so_shape, F32)],
        scratch_shapes=scratch,
        compiler_params=pltpu.CompilerParams(dimension_semantics=("arbitrary",), vmem_limit_bytes=VMEM_LIMIT),
        name=name,
    )(*acts, state, *layer_params, *consts, *earlier)


def _rows(v):
    return v.reshape(v.shape[0], 1, -1).astype(F32)


def _pad_rows(m, top, total):
    return jnp.zeros((m.shape[0], total, m.shape[2]), F32).at[:, top:top + m.shape[1]].set(m).astype(BF16)


def _pad_lanes(v):
    return jnp.zeros((v.shape[0], 1, LANE), F32).at[:, 0, :v.shape[1]].set(v)


def _stacked_params(ffn1_norm, ffn1_w_gate, ffn1_w_up, ffn1_w_down, mix_norm, w_in, rwkv_mu, rwkv_w0, rwkv_w2,
                    rwkv_a0, rwkv_a2, rwkv_g2, rwkv_k_k, rwkv_k_a, rwkv_r_k, rwkv_ln_w, rwkv_ln_b, gla_gate_w2,
                    gla_gate_b, gla_norm, mamba_conv_w, mamba_conv_b, mamba_dt_bias, mamba_A_log, mamba_D, mamba_norm,
                    w_out, ffn2_norm, ffn2_w_gate, ffn2_w_up, ffn2_w_down, ple_norm, ple_w_gate, ple_w_proj):
    depth = w_in.shape[0]
    w = w_in.astype(BF16)
    g0 = RWKV_PROJ
    m0 = RWKV_PROJ + GLA_PROJ
    qkv_end = 2 * GLA_KEY_W + GLA_W
    zeros = lambda n: jnp.zeros((depth, D_MODEL, n), BF16)
    win = jnp.concatenate([
        w[:, :, :g0 + qkv_end], w[:, :, g0 + qkv_end + GLA_GATE_LORA:m0],
        w[:, :, g0 + qkv_end:g0 + qkv_end + GLA_GATE_LORA], zeros(LANE - GLA_GATE_LORA),
        w[:, :, m0:], zeros(LANE - M2_HEADS)], axis=2)
    pre = (_rows(ffn1_norm), ffn1_w_gate.astype(BF16), ffn1_w_up.astype(BF16), ffn1_w_down.astype(BF16),
           _rows(mix_norm), win)
    rwkv = (_rows(rwkv_mu), _rows(rwkv_w0), _pad_rows(rwkv_w2, 0, RWKV_LORA), _rows(rwkv_a0),
            _pad_rows(rwkv_a2, RWKV_DECAY_LORA, RWKV_LORA),
            _pad_rows(rwkv_g2, RWKV_DECAY_LORA + RWKV_A_LORA, RWKV_LORA),
            _rows(rwkv_k_k), _rows(rwkv_k_a), _rows(rwkv_r_k), _rows(rwkv_ln_w), _rows(rwkv_ln_b))
    gla = (_pad_rows(gla_gate_w2, 0, LANE), _rows(gla_gate_b), _rows(jnp.tile(gla_norm, (1, GLA_HEADS))))
    ssd = (mamba_conv_w.astype(F32), _rows(mamba_conv_b), _pad_lanes(mamba_dt_bias), _pad_lanes(mamba_A_log),
           _rows(jnp.repeat(mamba_D, HEAD_DIM, axis=1)), _rows(mamba_norm))
    post = (w_out.astype(BF16), _rows(ffn2_norm), ffn2_w_gate.astype(BF16), ffn2_w_up.astype(BF16),
            ffn2_w_down.astype(BF16), _rows(ple_norm), ple_w_gate.astype(BF16), ple_w_proj.astype(BF16))
    return pre, rwkv, gla, ssd, post


def _decode_params(nb, rwkv_mu, rwkv_w0, rwkv_w2, rwkv_a0, rwkv_a2, rwkv_g2, rwkv_k_k, rwkv_k_a, rwkv_r_k,
                   rwkv_ln_w, rwkv_ln_b, gla_gate_w2, gla_gate_b, gla_norm, mamba_conv_w, mamba_conv_b,
                   mamba_dt_bias, mamba_A_log, mamba_D, mamba_norm):
    depth = rwkv_mu.shape[0]
    slab = lambda vs: jnp.broadcast_to(jnp.concatenate([v.reshape(depth, -1) for v in vs], axis=1)[:, :, None],
                                       (depth, sum(v[0].size for v in vs), nb)).astype(F32)
    tr = lambda m: jnp.swapaxes(m, 1, 2)
    rwkv = (slab((rwkv_mu, rwkv_w0, rwkv_a0, rwkv_k_k, rwkv_k_a, rwkv_r_k, rwkv_ln_w, rwkv_ln_b)),
            tr(_pad_rows(rwkv_w2, 0, RWKV_LORA)), tr(_pad_rows(rwkv_a2, RWKV_DECAY_LORA, RWKV_LORA)),
            tr(_pad_rows(rwkv_g2, RWKV_DECAY_LORA + RWKV_A_LORA, RWKV_LORA)))
    gla = (slab((gla_gate_b, jnp.tile(gla_norm, (1, GLA_HEADS)))), tr(_pad_rows(gla_gate_w2, 0, LANE)))
    ssd = (slab((mamba_conv_w, mamba_conv_b, mamba_dt_bias, mamba_A_log, jnp.repeat(mamba_D, HEAD_DIM, axis=1),
                 mamba_norm)),)
    return rwkv, gla, ssd


def _sample_layer(x, p, state, params, dparams, layer, final_norm, *, batch, seq, final, earlier):
    pre, _, _, _, post = params
    d_rwkv, d_gla, d_ssd = dparams
    shift0, wkv0, gla0, conv0, ssm0 = state
    old_wkv, old_gla, old_ssm = earlier
    n = batch * seq
    x1, zr_t, zg_t, zm_t = _pre_call(x, layer, *pre, n, steps=seq)
    shift1 = zr_t[seq - 1].T
    xbc = jnp.transpose(zm_t[:, M2_W:M2_W + M2_CONV_DIM, :], (2, 0, 1))
    conv1 = jnp.concatenate([conv0[layer], xbc], axis=1)[:, -(M2_CONV - 1):]
    scr = lambda *shape: pltpu.VMEM(shape, F32)
    yr, wkv1 = _dec_call(_rwkv_dec_kernel, "rwkv7_decode", (zr_t,), wkv0, layer,
                         (jnp.swapaxes(shift0, 1, 2),) + d_rwkv, (_HEAD_ONES(),), layer, RWKV_W,
                         [scr(seq, 6, RWKV_W, batch), scr(seq, RWKV_W, batch), scr(seq, 2, RWKV_W, batch)], old_wkv)
    yg, gla1 = _dec_call(_gla_dec_kernel, "gla_decode", (zg_t,), gla0, layer, d_gla, (_HEAD_ONES(),), layer, GLA_W,
                         [scr(seq, 3, GLA_KEY_W, batch), scr(seq, GLA_W, batch), scr(seq, GLA_W, batch)], old_gla)
    conv_t = jnp.transpose(conv0, (0, 2, 3, 1))
    ym, ssm1 = _dec_call(_ssd_dec_kernel, "ssd_decode", (zm_t,), ssm0, layer, (conv_t,) + d_ssd, (), layer, M2_W,
                         [scr(seq, M2_W, batch), scr(seq, 2 * M2_GROUPS * M2_STATE, batch),
                          scr(seq, 2, ROWS, batch), scr(seq, M2_W, batch)], old_ssm)
    x2 = _post_call(x1, yr, yg, ym, p, layer, *post, final_norm.reshape(1, -1), n, final, steps=seq)
    return x2, (shift1, wkv1, gla1, conv1, ssm1)


def _prompt_layer(x, p, params, layer, final_norm, *, batch, seq, bb, tm, tm_post, final):
    pre, rwkv, gla, ssd, post = params
    assert seq % RWKV_CHUNK == 0 and seq % (GLA_CHUNK * GLA_SUB) == 0 and seq % SSD_CHUNK == 0 and seq >= M2_CONV - 1
    x1, zr, zg, zm = _pre_call(x, layer, *pre, tm)
    zr = zr.reshape(batch, seq, RWKV_PROJ)
    zg = zg.reshape(batch, seq, GLA_PAD)
    zm = zm.reshape(batch, seq, M2_PAD)
    shift1 = zr[:, -1]
    conv1 = zm[:, seq - (M2_CONV - 1):, M2_W:M2_W + M2_CONV_DIM]
    zero = lambda *shape: jnp.zeros((1, batch) + shape, F32)
    yr, wkv1 = _rwkv_call(zr, zero(1, RWKV_PROJ), zero(RWKV_HEADS, HEAD_DIM, HEAD_DIM), 0, rwkv, layer, bb,
                          RWKV_CHUNK, BF16)
    yg, gla1 = _gla_call(zg, zero(GLA_HEADS, GLA_DK, HEAD_DIM), 0, gla, layer, bb, GLA_CHUNK, GLA_SUB, BF16)
    ym, ssm1 = _ssd_call(zm, zero(M2_CONV - 1, M2_CONV_DIM), zero(M2_HEADS, HEAD_DIM, M2_STATE), 0, ssd, layer, bb,
                         SSD_CHUNK, BF16)
    n = batch * seq
    flat = lambda y: y.reshape(n, y.shape[-1])
    x2 = _post_call(x1, flat(yr), flat(yg), flat(ym), p, layer, *post, final_norm.reshape(1, -1), tm_post, final)
    return x2, (shift1, wkv1, gla1, conv1, ssm1)


def kernel(x_prompt, x_sample, p_prompt, p_sample, state_rwkv_shift, state_rwkv_wkv, state_gla, state_mamba_conv, state_mamba_ssm, ffn1_norm, ffn1_w_gate, ffn1_w_up, ffn1_w_down, mix_norm, w_in, rwkv_mu, rwkv_w0, rwkv_w2, rwkv_a0, rwkv_a2, rwkv_g2, rwkv_k_k, rwkv_k_a, rwkv_r_k, rwkv_ln_w, rwkv_ln_b, gla_gate_w2, gla_gate_b, gla_norm, mamba_conv_w, mamba_conv_b, mamba_dt_bias, mamba_A_log, mamba_D, mamba_norm, w_out, ffn2_norm, ffn2_w_gate, ffn2_w_up, ffn2_w_down, ple_norm, ple_w_gate, ple_w_proj, final_norm):
    depth = w_in.shape[0]
    nb, seq, _ = x_prompt.shape
    db, dseq, _ = x_sample.shape
    weights = (ffn1_norm, ffn1_w_gate, ffn1_w_up, ffn1_w_down, mix_norm, w_in, rwkv_mu, rwkv_w0, rwkv_w2, rwkv_a0,
               rwkv_a2, rwkv_g2, rwkv_k_k, rwkv_k_a, rwkv_r_k, rwkv_ln_w, rwkv_ln_b, gla_gate_w2, gla_gate_b,
               gla_norm, mamba_conv_w, mamba_conv_b, mamba_dt_bias, mamba_A_log, mamba_D, mamba_norm, w_out,
               ffn2_norm, ffn2_w_gate, ffn2_w_up, ffn2_w_down, ple_norm, ple_w_gate, ple_w_proj)
    xp = x_prompt.reshape(nb * seq, D_MODEL)
    xs = jnp.swapaxes(x_sample, 0, 1).reshape(dseq * db, D_MODEL)
    pp = p_prompt.reshape(depth, nb * seq, PLE_DIM)
    ps = jnp.swapaxes(p_sample, 1, 2).reshape(depth, dseq * db, PLE_DIM)
    params = _stacked_params(*weights)
    dparams = _decode_params(db, rwkv_mu, rwkv_w0, rwkv_w2, rwkv_a0, rwkv_a2, rwkv_g2, rwkv_k_k, rwkv_k_a, rwkv_r_k,
                             rwkv_ln_w, rwkv_ln_b, gla_gate_w2, gla_gate_b, gla_norm, mamba_conv_w, mamba_conv_b,
                             mamba_dt_bias, mamba_A_log, mamba_D, mamba_norm)
    batch_last = lambda s: jnp.moveaxis(s, 1, -1)
    past = (state_rwkv_shift, batch_last(state_rwkv_wkv), batch_last(state_gla), state_mamba_conv,
            batch_last(state_mamba_ssm))
    p_states, s_states = [], []
    for i in range(depth):
        final = i == depth - 1
        xp, st_p = _prompt_layer(xp, pp, params, i, final_norm, batch=nb, seq=seq, bb=nb, tm=512, tm_post=1024,
                                 final=final)
        earlier = tuple(tuple(s[j] for s in s_states) if final else () for j in (1, 2, 4))
        xs, st_s = _sample_layer(xs, ps, past, params, dparams, i, final_norm, batch=db, seq=dseq, final=final,
                                 earlier=earlier)
        p_states.append(st_p)
        s_states.append(st_s)
    stack = lambda sts, j: jnp.stack([s[j] for s in sts])
    mats = s_states[-1] if depth > 1 else tuple(s[None] for s in s_states[-1])
    s_out = tuple(jnp.moveaxis(mats[j], -1, 1) if j in (1, 2, 4) else stack(s_states, j) for j in range(5))
    return ((xp.reshape(nb, seq, D_MODEL), jnp.swapaxes(xs.reshape(dseq, db, D_MODEL), 0, 1))
            + tuple(stack(p_states, j) for j in range(5)) + s_out)
```
